```python
import jax, jax.numpy as jnp
from jax import lax
import numpy as np

D_MODEL = 1024
BATCH = 4
SEQ = 4096
DEPTH = 1

PLE_DIM = 256
MLA_HEADS = 8
MLA_NOPE_DIM = 64
MLA_ROPE_DIM = 32
MLA_V_DIM = 64
Q_LORA_RANK = 384
KV_LORA_RANK = 256
SB_HEADS = 8
SB_HEAD_DIM = 64
MLA_WIDTH = MLA_HEADS * MLA_V_DIM
SB_WIDTH = SB_HEADS * SB_HEAD_DIM
MIX_WIDTH = MLA_WIDTH + SB_WIDTH
IN_SPLITS = (Q_LORA_RANK,
             Q_LORA_RANK + KV_LORA_RANK,
             Q_LORA_RANK + KV_LORA_RANK + MLA_ROPE_DIM,
             Q_LORA_RANK + KV_LORA_RANK + MLA_ROPE_DIM + SB_WIDTH,
             Q_LORA_RANK + KV_LORA_RANK + MLA_ROPE_DIM + 2 * SB_WIDTH)
IN_COLS = Q_LORA_RANK + KV_LORA_RANK + MLA_ROPE_DIM + 3 * SB_WIDTH
ROPE_THETA = 10000.0
Q_BLOCK = 128
N_EXPERTS = 32
TOP_K = 4
D_FF = 1024
SWIGLU_LIMIT = 7.0
SWIGLU_ALPHA = 1.702
EXPERT_BLOCK = 128
RMS_EPS = 1e-6
MAX_POS_OFFSET = 4096

kernel_name = "hybrid_mla_stickbreak_moe_ple"


def rmsnorm(x, g):
    xf = x.astype(jnp.float32)
    y = xf * lax.rsqrt(jnp.mean(xf * xf, axis=-1, keepdims=True) + RMS_EPS)
    return (y * g.astype(jnp.float32)).astype(x.dtype)


def rope_tables(positions):
    inv_freq = ROPE_THETA ** (-jnp.arange(0, MLA_ROPE_DIM, 2, dtype=jnp.float32) / MLA_ROPE_DIM)
    ang = positions.astype(jnp.float32)[..., None] * inv_freq
    return jnp.cos(ang), jnp.sin(ang)


def apply_rope(x, cos, sin):
    xf = x.astype(jnp.float32)
    x1, x2 = jnp.split(xf, 2, axis=-1)
    return jnp.concatenate([x1 * cos - x2 * sin, x2 * cos + x1 * sin], axis=-1).astype(x.dtype)


def to_blocks(a):
    b, h, s, d = a.shape
    return jnp.moveaxis(a.reshape(b, h, s // Q_BLOCK, Q_BLOCK, d), 2, 0)


def from_blocks(o):
    n, b, h, qb, d = o.shape
    return jnp.moveaxis(o, 0, 2).reshape(b, h, n * qb, d).transpose(0, 2, 1, 3).reshape(b, n * qb, h * d)


def mixer_heads(q_nope, q_rope, k_nope, k_rope, v_mla, q_sb, k_sb, v_sb):
    seq = k_nope.shape[2]
    n_blk = seq // Q_BLOCK
    key_pos = jnp.arange(seq)
    mla_scale = (MLA_NOPE_DIM + MLA_ROPE_DIM) ** -0.5
    sb_scale = SB_HEAD_DIM ** -0.5

    def block(args):
        bi, qn, qr, qs = args
        q_pos = bi * Q_BLOCK + jnp.arange(Q_BLOCK)
        s = (jnp.einsum('bhqd,bhkd->bhqk', qn, k_nope, preferred_element_type=jnp.float32)
             + jnp.einsum('bhqr,bkr->bhqk', qr, k_rope, preferred_element_type=jnp.float32)) * mla_scale
        causal = key_pos[None, :] <= q_pos[:, None]
        w = jax.nn.softmax(jnp.where(causal, s, -jnp.inf), axis=-1)
        o_m = jnp.einsum('bhqk,bhkd->bhqd', w.astype(v_mla.dtype), v_mla)
        z = jnp.einsum('bhqd,bhkd->bhqk', qs, k_sb, preferred_element_type=jnp.float32) * sb_scale
        strict = key_pos[None, :] < q_pos[:, None]
        log_not = jnp.where(strict, jax.nn.log_sigmoid(-z), 0.0)
        after = lax.cumsum(log_not, axis=3, reverse=True) - log_not
        log_a = jnp.where(strict, jax.nn.log_sigmoid(z) + after, -jnp.inf)
        o_s = jnp.einsum('bhqk,bhkd->bhqd', jnp.exp(log_a).astype(v_sb.dtype), v_sb)
        return o_m, o_s

    o_m, o_s = lax.map(block, (jnp.arange(n_blk), to_blocks(q_nope), to_blocks(q_rope), to_blocks(q_sb)))
    return from_blocks(o_m), from_blocks(o_s)


def moe(u, w_router, b_router, w_gu, b_gu, w_dn, b_dn):
    b, s, d = u.shape
    t = b * s
    xt = u.reshape(t, d)
    logits = jnp.dot(xt, w_router, preferred_element_type=jnp.float32) + b_router.astype(jnp.float32)
    top_logit, top_e = lax.top_k(logits, TOP_K)
    gate = jax.nn.softmax(top_logit, axis=-1)
    n_pairs = t * TOP_K
    flat_e = top_e.reshape(-1)
    flat_tok = jnp.arange(n_pairs, dtype=jnp.int32) // TOP_K
    flat_g = gate.reshape(-1)
    order = jnp.argsort(flat_e)
    sorted_e = flat_e[order]
    counts = jnp.bincount(flat_e, length=N_EXPERTS)
    padded = (counts + EXPERT_BLOCK - 1) // EXPERT_BLOCK * EXPERT_BLOCK
    pad_end = jnp.cumsum(padded)
    pad_start = pad_end - padded
    start = jnp.cumsum(counts) - counts
    dest = pad_start[sorted_e] + jnp.arange(n_pairs) - start[sorted_e]
    n_blocks = -(-n_pairs // EXPERT_BLOCK) + N_EXPERTS
    n_rows = n_blocks * EXPERT_BLOCK
    row_tok = jnp.full((n_rows,), t, jnp.int32).at[dest].set(flat_tok[order])
    row_gate = jnp.zeros((n_rows,), jnp.float32).at[dest].set(flat_g[order])
    block_e = jnp.minimum(jnp.searchsorted(pad_end, jnp.arange(n_blocks) * EXPERT_BLOCK, side='right'),
                          N_EXPERTS - 1)
    x_pad = jnp.concatenate([xt, jnp.zeros((1, d), xt.dtype)], axis=0)
    xb = x_pad[row_tok].reshape(n_blocks, EXPERT_BLOCK, d)

    def expert_block(args):
        e, xs = args
        hh = jnp.dot(xs, w_gu[e]) + b_gu[e]
        glu, lin = jnp.split(hh, 2, axis=-1)
        glu = jnp.minimum(glu, SWIGLU_LIMIT)
        lin = jnp.clip(lin, -SWIGLU_LIMIT, SWIGLU_LIMIT)
        act = glu * jax.nn.sigmoid(SWIGLU_ALPHA * glu) * (lin + 1.0)
        return jnp.dot(act, w_dn[e]) + b_dn[e]

    yb = lax.map(expert_block, (block_e, xb))
    yr = yb.reshape(n_rows, d) * row_gate[:, None].astype(yb.dtype)
    y = jax.ops.segment_sum(yr, row_tok, num_segments=t + 1)[:t]
    return y.reshape(b, s, d)


def setup_inputs(seed: int = 0) -> dict:
    key = jax.random.key(seed)
    ks = jax.random.split(key, 24)

    def nrm(k, shape, scale):
        return jax.random.normal(k, shape, jnp.float32) * scale

    def gain(k, shape):
        return 1.0 + 0.05 * jax.random.normal(k, shape, jnp.float32)

    offsets = jax.random.randint(ks[2], (BATCH, 1), 0, MAX_POS_OFFSET, dtype=jnp.int32)
    positions = offsets + jnp.arange(SEQ, dtype=jnp.int32)[None, :]
    return {
        'x': nrm(ks[0], (BATCH, SEQ, D_MODEL), 1.0),
        'p': nrm(ks[1], (DEPTH, BATCH, SEQ, PLE_DIM), 1.0),
        'positions': positions,
        'w_in': nrm(ks[3], (DEPTH, D_MODEL, IN_COLS), D_MODEL ** -0.5),
        'g_attn': gain(ks[4], (DEPTH, D_MODEL)),
        'g_cq': gain(ks[5], (DEPTH, Q_LORA_RANK)),
        'w_uq': nrm(ks[6], (DEPTH, Q_LORA_RANK, MLA_HEADS * (MLA_NOPE_DIM + MLA_ROPE_DIM)), Q_LORA_RANK ** -0.5),
        'g_ckv': gain(ks[7], (DEPTH, KV_LORA_RANK)),
        'w_ukv': nrm(ks[8], (DEPTH, KV_LORA_RANK, MLA_HEADS * (MLA_NOPE_DIM + MLA_V_DIM)), KV_LORA_RANK ** -0.5),
        'g_out_mla': gain(ks[9], (DEPTH, MLA_WIDTH)),
        'g_out_sb': gain(ks[10], (DEPTH, SB_WIDTH)),
        'w_o': nrm(ks[11], (DEPTH, MIX_WIDTH, D_MODEL), MIX_WIDTH ** -0.5),
        'g_moe': gain(ks[12], (DEPTH, D_MODEL)),
        'w_router': nrm(ks[13], (DEPTH, D_MODEL, N_EXPERTS), D_MODEL ** -0.5),
        'b_router': nrm(ks[14], (DEPTH, N_EXPERTS), 0.01),
        'w_gu': nrm(ks[15], (DEPTH, N_EXPERTS, D_MODEL, 2 * D_FF), D_MODEL ** -0.5),
        'b_gu': nrm(ks[16], (DEPTH, N_EXPERTS, 2 * D_FF), 0.01),
        'w_dn': nrm(ks[17], (DEPTH, N_EXPERTS, D_FF, D_MODEL), D_FF ** -0.5),
        'b_dn': nrm(ks[18], (DEPTH, N_EXPERTS, D_MODEL), 0.01),
        'g_ple': gain(ks[19], (DEPTH, D_MODEL)),
        'w_ple_gate': nrm(ks[20], (DEPTH, D_MODEL, D_MODEL), D_MODEL ** -0.5),
        'w_ple_proj': nrm(ks[21], (DEPTH, PLE_DIM, D_MODEL), PLE_DIM ** -0.5),
        'g_final': gain(ks[22], (D_MODEL,)),
    }


def reference(x, p, positions, w_in, g_attn, g_cq, w_uq, g_ckv, w_ukv, g_out_mla, g_out_sb, w_o,
              g_moe, w_router, b_router, w_gu, b_gu, w_dn, b_dn, g_ple, w_ple_gate, w_ple_proj, g_final):
    b, s, _ = x.shape
    cos, sin = rope_tables(positions)

    def heads_first(a):
        return jnp.swapaxes(a, 1, 2)

    h = x
    for i in range(DEPTH):
        u = rmsnorm(h, g_attn[i])
        c_q, c_kv, k_r, q_s, k_s, v_s = jnp.split(u @ w_in[i], IN_SPLITS, axis=-1)
        q = (rmsnorm(c_q, g_cq[i]) @ w_uq[i]).reshape(b, s, MLA_HEADS, MLA_NOPE_DIM + MLA_ROPE_DIM)
        kv = (rmsnorm(c_kv, g_ckv[i]) @ w_ukv[i]).reshape(b, s, MLA_HEADS, MLA_NOPE_DIM + MLA_V_DIM)
        q_nope = q[..., :MLA_NOPE_DIM]
        q_rope = apply_rope(q[..., MLA_NOPE_DIM:], cos[:, :, None, :], sin[:, :, None, :])
        k_nope = kv[..., :MLA_NOPE_DIM]
        v_m = kv[..., MLA_NOPE_DIM:]
        k_rope = apply_rope(k_r, cos, sin)
        sb_shape = (b, s, SB_HEADS, SB_HEAD_DIM)
        o_m, o_s = mixer_heads(heads_first(q_nope), heads_first(q_rope), heads_first(k_nope), k_rope,
                               heads_first(v_m), heads_first(q_s.reshape(sb_shape)),
                               heads_first(k_s.reshape(sb_shape)), heads_first(v_s.reshape(sb_shape)))
        mixed = jnp.concatenate([rmsnorm(o_m, g_out_mla[i]), rmsnorm(o_s, g_out_sb[i])], axis=-1)
        h = h + mixed @ w_o[i]
        h = h + moe(rmsnorm(h, g_moe[i]), w_router[i], b_router[i], w_gu[i], b_gu[i], w_dn[i], b_dn[i])
        u = rmsnorm(h, g_ple[i])
        h = h + jax.nn.sigmoid(u @ w_ple_gate[i]) * (p[i] @ w_ple_proj[i])
    return rmsnorm(h, g_final)
```

```python
import functools

import jax
import jax.numpy as jnp
from jax import lax
from jax.experimental import pallas as pl
from jax.experimental.pallas import tpu as pltpu

D_MODEL = 1024
PLE_DIM = 256
MLA_HEADS = 8
MLA_NOPE_DIM = 64
MLA_ROPE_DIM = 32
MLA_V_DIM = 64
Q_LORA_RANK = 384
KV_LORA_RANK = 256
SB_HEADS = 8
SB_HEAD_DIM = 64
SB_WIDTH = SB_HEADS * SB_HEAD_DIM
ROPE_THETA = 10000.0
N_EXPERTS = 32
TOP_K = 4
D_FF = 1024
SWIGLU_LIMIT = 7.0
SWIGLU_ALPHA = 1.702
RMS_EPS = 1e-6

LANES = 128
HEAD_PAIRS = MLA_HEADS // 2
PROJ_ROWS = 256
MLA_TILE = 256
SB_TILE = 128
EXPERT_ROWS = 256
FINAL_ROWS = 128
VMEM_LIMIT = 56 * 1024 * 1024
SB_LOG_FLOOR = -105.0

_F32 = jnp.float32
_BF16 = jnp.bfloat16


def _rms(x, g):
    return x * lax.rsqrt(jnp.mean(x * x, axis=-1, keepdims=True) + RMS_EPS) * g


def _dot(a, b):
    return jnp.dot(a, b, preferred_element_type=_F32)


def _dot_nt(a, b):
    return lax.dot_general(a, b, (((1,), (1,)), ((), ())), preferred_element_type=_F32)


def _proj_kernel(pos_ref, x_ref, g_attn_ref, w1_ref, g_cq_ref, wq_ref, wqr_ref, g_ckv_ref, wk_ref, wv_ref,
                 invf_ref, qm_ref, km_ref, vm_ref, qs_ref, ks_ref, vs_ref):
    u = _rms(x_ref[...], g_attn_ref[...]).astype(_BF16)
    y = _dot(u, w1_ref[...])
    c_q = y[:, :Q_LORA_RANK]
    c_kv = y[:, Q_LORA_RANK:Q_LORA_RANK + KV_LORA_RANK]
    o = Q_LORA_RANK + KV_LORA_RANK
    k_r = y[:, o:o + LANES]
    k_r_rot = y[:, o + LANES:o + 2 * LANES]
    o += 2 * LANES
    q_s = y[:, o:o + SB_WIDTH]
    k_s = y[:, o + SB_WIDTH:o + 2 * SB_WIDTH]
    v_s = y[:, o + 2 * SB_WIDTH:o + 3 * SB_WIDTH]

    ang = pos_ref[...].astype(_F32) * invf_ref[...]
    cos = jnp.cos(ang)
    sin = jnp.sin(ang)

    cq_n = _rms(c_q, g_cq_ref[...]).astype(_BF16)
    q = _dot(cq_n, wq_ref[...])
    q_rot = _dot(cq_n, wqr_ref[...])
    ckv_n = _rms(c_kv, g_ckv_ref[...]).astype(_BF16)
    k = _dot(ckv_n, wk_ref[...])
    vm_ref[...] = _dot(ckv_n, wv_ref[...]).astype(_BF16)
    k_rope = k_r * cos + k_r_rot * sin

    lane = lax.broadcasted_iota(jnp.int32, (1, LANES), 1)
    low = lane < SB_HEAD_DIM
    for h in range(MLA_HEADS):
        sl = slice(h * LANES, (h + 1) * LANES)
        qm_ref[:, sl] = (q[:, sl] * cos + q_rot[:, sl] * sin).astype(_BF16)
        km_ref[:, sl] = (k[:, sl] + k_rope).astype(_BF16)
    qs_ref[...] = (q_s * (SB_HEAD_DIM ** -0.5)).astype(_BF16)
    for hp in range(HEAD_PAIRS):
        sl = slice(hp * LANES, (hp + 1) * LANES)
        for half, keep in ((0, low), (1, jnp.logical_not(low))):
            dst = slice((2 * hp + half) * LANES, (2 * hp + half + 1) * LANES)
            ks_ref[:, dst] = jnp.where(keep, k_s[:, sl], 0.0).astype(_BF16)
            vs_ref[:, dst] = jnp.where(keep, v_s[:, sl], 0.0).astype(_BF16)


def _proj(pos, x2, g_attn, w1, g_cq, wq, wqr, g_ckv, wk, wv, invf):
    t = x2.shape[0]
    rows = PROJ_ROWS
    wide = MLA_HEADS * LANES

    def full(a):
        return pl.BlockSpec(a.shape, lambda i: (0,) * a.ndim)

    def tok(n):
        return pl.BlockSpec((rows, n), lambda i: (i, 0))

    outs = [jax.ShapeDtypeStruct((t, n), _BF16) for n in (wide, wide, wide, SB_WIDTH, wide, wide)]
    return pl.pallas_call(
        _proj_kernel,
        grid=(t // rows,),
        in_specs=[tok(1), tok(D_MODEL), full(g_attn), full(w1), full(g_cq), full(wq), full(wqr), full(g_ckv),
                  full(wk), full(wv), full(invf)],
        out_specs=[tok(wide), tok(wide), tok(wide), tok(SB_WIDTH), tok(wide), tok(wide)],
        out_shape=outs,
        compiler_params=pltpu.CompilerParams(dimension_semantics=("parallel",), vmem_limit_bytes=VMEM_LIMIT),
        name="proj",
    )(pos, x2, g_attn, w1, g_cq, wq, wqr, g_ckv, wk, wv, invf)


def _mla_kernel(q_ref, k_ref, v_ref, o_ref):
    tile = MLA_TILE
    qi = pl.program_id(2)
    scale = (MLA_NOPE_DIM + MLA_ROPE_DIM) ** -0.5
    row = lax.broadcasted_iota(jnp.int32, (tile, tile), 0)
    col = lax.broadcasted_iota(jnp.int32, (tile, tile), 1)
    out = jnp.zeros((tile, LANES), _F32)
    for h in range(2):
        sl = slice(h * LANES, (h + 1) * LANES)
        q = q_ref[0, :, sl]

        def step(kt, carry, diagonal):
            m, l, acc = carry
            ks = pl.ds(pl.multiple_of(kt * tile, tile), tile)
            s = _dot_nt(q, k_ref[0, ks, sl]) * scale
            if diagonal:
                s = jnp.where(col <= row, s, -jnp.inf)
            m_new = jnp.maximum(m, jnp.max(s, axis=-1, keepdims=True))
            alpha = jnp.exp(m - m_new)
            p = jnp.exp(s - m_new)
            l = alpha * l + jnp.sum(p, axis=-1, keepdims=True)
            acc = alpha * acc + _dot(p.astype(_BF16), v_ref[0, ks, sl])
            return m_new, l, acc

        init = (jnp.full((tile, 1), -jnp.inf, _F32), jnp.zeros((tile, 1), _F32), jnp.zeros((tile, LANES), _F32))
        carry = lax.fori_loop(0, qi, functools.partial(step, diagonal=False), init)
        _, l, acc = step(qi, carry, True)
        out = out + acc / l
    o_ref[0] = out


def _mla(qm, km, vm):
    b, s, _ = qm.shape
    tile = MLA_TILE
    return pl.pallas_call(
        _mla_kernel,
        grid=(b, HEAD_PAIRS, s // tile),
        in_specs=[pl.BlockSpec((1, tile, 2 * LANES), lambda bi, hp, qi: (bi, qi, hp)),
                  pl.BlockSpec((1, s, 2 * LANES), lambda bi, hp, qi: (bi, 0, hp)),
                  pl.BlockSpec((1, s, 2 * LANES), lambda bi, hp, qi: (bi, 0, hp))],
        out_specs=pl.BlockSpec((1, tile, LANES), lambda bi, hp, qi: (bi, qi, hp)),
        out_shape=jax.ShapeDtypeStruct((b, s, HEAD_PAIRS * LANES), _F32),
        compiler_params=pltpu.CompilerParams(dimension_semantics=("parallel", "parallel", "arbitrary"),
                                             vmem_limit_bytes=VMEM_LIMIT),
        name="mla",
    )(qm, km, vm)


def _sb_kernel(q_ref, k_ref, v_ref, tri_ref, o_ref):
    tile = SB_TILE
    qi = pl.program_id(2)
    row = lax.broadcasted_iota(jnp.int32, (tile, tile), 0)
    col = lax.broadcasted_iota(jnp.int32, (tile, tile), 1)
    strict = col < row
    q = q_ref[0]
    tri = tri_ref[...]
    out = jnp.zeros((tile, LANES), _F32)
    for h in range(2):
        sl = slice(h * LANES, (h + 1) * LANES)

        def step(kt, rem, acc, diagonal):
            ks = pl.ds(pl.multiple_of(kt * tile, tile), tile)
            z = _dot_nt(q, k_ref[0, ks, sl])
            sp = jnp.maximum(z, 0.0) + jnp.log1p(jnp.exp(-jnp.abs(z)))
            log_not = -sp
            if diagonal:
                log_not = jnp.where(strict, log_not, 0.0)
            hi = log_not.astype(_BF16)
            r1 = log_not - hi.astype(_F32)
            mid = r1.astype(_BF16)
            lo = (r1 - mid.astype(_F32)).astype(_BF16)
            incl = _dot(hi, tri) + _dot(mid, tri) + _dot(lo, tri)
            log_a = (z - sp) + (rem + incl - log_not)
            p = jnp.exp(log_a)
            if diagonal:
                p = jnp.where(strict, p, 0.0)
            acc = acc + _dot(p.astype(_BF16), v_ref[0, ks, sl])
            rem = rem + incl[:, :1]
            return rem, acc

        rem, acc = step(qi, jnp.zeros((tile, 1), _F32), jnp.zeros((tile, LANES), _F32), True)

        def cond(state):
            kt, rem_max, _, _ = state
            return jnp.logical_and(kt >= 0, rem_max > SB_LOG_FLOOR)

        def body(state):
            kt, _, rem, acc = state
            rem, acc = step(kt, rem, acc, False)
            return kt - 1, jnp.max(rem), rem, acc

        _, _, _, acc = lax.while_loop(cond, body, (qi - 1, jnp.max(rem), rem, acc))
        out = out + acc
    o_ref[0] = out


def _sb(qs, ks, vs, tri):
    b, s, _ = qs.shape
    tile = SB_TILE
    return pl.pallas_call(
        _sb_kernel,
        grid=(b, HEAD_PAIRS, s // tile),
        in_specs=[pl.BlockSpec((1, tile, LANES), lambda bi, hp, qi: (bi, qi, hp)),
                  pl.BlockSpec((1, s, 2 * LANES), lambda bi, hp, qi: (bi, 0, hp)),
                  pl.BlockSpec((1, s, 2 * LANES), lambda bi, hp, qi: (bi, 0, hp)),
                  pl.BlockSpec((tile, tile), lambda bi, hp, qi: (0, 0))],
        out_specs=pl.BlockSpec((1, tile, LANES), lambda bi, hp, qi: (bi, qi, hp)),
        out_shape=jax.ShapeDtypeStruct((b, s, HEAD_PAIRS * LANES), _F32),
        compiler_params=pltpu.CompilerParams(dimension_semantics=("parallel", "parallel", "arbitrary"),
                                             vmem_limit_bytes=VMEM_LIMIT),
        name="sb",
    )(qs, ks, vs, tri)


def _post_kernel(om_ref, os_ref, x_ref, g_om_ref, g_os_ref, wo_ref, g_moe_ref, wr_ref, br_ref, ltri_ref,
                 h1_ref, u2_ref, route_ref, counts_ref, seen_ref):
    rows = PROJ_ROWS

    @pl.when(pl.program_id(0) == 0)
    def _():
        seen_ref[...] = jnp.zeros_like(seen_ref)

    mixed = jnp.concatenate([_rms(om_ref[...], g_om_ref[...]), _rms(os_ref[...], g_os_ref[...])], axis=-1)
    h1 = x_ref[...] + _dot(mixed.astype(_BF16), wo_ref[...])
    h1_ref[...] = h1
    u2 = _rms(h1, g_moe_ref[...])
    u2_ref[...] = u2
    logits = jnp.dot(u2, wr_ref[...], preferred_element_type=_F32, precision=lax.Precision.HIGHEST) + br_ref[...]

    lane = lax.broadcasted_iota(jnp.int32, (rows, N_EXPERTS), 1).astype(_F32)
    work = logits
    ids, tops = [], []
    onehot = jnp.zeros((rows, N_EXPERTS), _F32)
    for _ in range(TOP_K):
        top = jnp.max(work, axis=-1, keepdims=True)
        idx = jnp.min(jnp.where(work == top, lane, float(N_EXPERTS)), axis=-1, keepdims=True)
        hit = lane == idx
        onehot = jnp.where(hit, 1.0, onehot)
        work = jnp.where(hit, -jnp.inf, work)
        ids.append(idx)
        tops.append(top)
    exps = [jnp.exp(tp - tops[0]) for tp in tops]
    denom = exps[0] + exps[1] + exps[2] + exps[3]
    gates = [e / denom for e in exps]

    before = seen_ref[...] + _dot(ltri_ref[...], onehot.astype(_BF16))
    ranks = [jnp.sum(jnp.where(lane == idx, before, 0.0), axis=-1, keepdims=True) for idx in ids]
    seen_ref[...] = seen_ref[...] + jnp.sum(onehot, axis=0, keepdims=True)
    counts_ref[...] = seen_ref[...]

    out_lane = lax.broadcasted_iota(jnp.int32, (rows, LANES), 1)
    route = jnp.zeros((rows, LANES), _F32)
    for j, val in enumerate(ids + gates + ranks):
        route = jnp.where(out_lane == j, val, route)
    route_ref[...] = route


def _post(om, os_, x2, g_om, g_os, wo, g_moe, wr, br, ltri):
    t = x2.shape[0]
    rows = PROJ_ROWS

    def full(a):
        return pl.BlockSpec(a.shape, lambda i: (0,) * a.ndim)

    def tok(n):
        return pl.BlockSpec((rows, n), lambda i: (i, 0))

    return pl.pallas_call(
        _post_kernel,
        grid=(t // rows,),
        in_specs=[tok(om.shape[1]), tok(os_.shape[1]), tok(D_MODEL), full(g_om), full(g_os), full(wo), full(g_moe),
                  full(wr), full(br), full(ltri)],
        out_specs=[tok(D_MODEL), tok(D_MODEL), tok(LANES), pl.BlockSpec((1, N_EXPERTS), lambda i: (0, 0))],
        out_shape=[jax.ShapeDtypeStruct((t, D_MODEL), _F32), jax.ShapeDtypeStruct((t, D_MODEL), _F32),
                   jax.ShapeDtypeStruct((t, LANES), _F32), jax.ShapeDtypeStruct((1, N_EXPERTS), _F32)],
        scratch_shapes=[pltpu.VMEM((1, N_EXPERTS), _F32)],
        compiler_params=pltpu.CompilerParams(dimension_semantics=("arbitrary",), vmem_limit_bytes=VMEM_LIMIT),
        name="post",
    )(om, os_, x2, g_om, g_os, wo, g_moe, wr, br, ltri)


def _row_copy(src_hbm, row, dst, dst_row, sem):
    return pltpu.make_async_copy(src_hbm.at[pl.ds(row, 1)], dst.at[pl.ds(dst_row, 1)], sem)


def _expert_kernel(be_ref, rt_hbm, u_hbm, wgu_ref, bgu_ref, wdn_ref, bdn_ref, ys_ref,
                   wgu_bf, wdn_bf, xbuf, idx, sem_x, sem_i):
    rows = EXPERT_ROWS
    i = pl.program_id(0)
    nb = pl.num_programs(0)
    slot = lax.rem(i, 2)

    def idx_copy(block, s):
        return pltpu.make_async_copy(rt_hbm.at[block], idx.at[s], sem_i.at[s])

    def gather(s):
        def one(r, c):
            _row_copy(u_hbm, idx[s, r], xbuf.at[s], r, sem_x.at[s]).start()
            return c
        lax.fori_loop(0, rows, one, 0, unroll=8)

    @pl.when(i == 0)
    def _():
        idx_copy(0, 0).start()
        idx_copy(0, 0).wait()
        gather(0)

        @pl.when(nb > 1)
        def _():
            idx_copy(1, 1).start()

    @pl.when(i + 1 < nb)
    def _():
        idx_copy(i + 1, 1 - slot).wait()
        gather(1 - slot)

        @pl.when(i + 2 < nb)
        def _():
            idx_copy(i + 2, slot).start()

    e = be_ref[i]
    prev = be_ref[jnp.maximum(i - 1, 0)]

    @pl.when(jnp.logical_or(i == 0, e != prev))
    def _():
        chunk = 128

        def cast(c, carry):
            rs = pl.ds(pl.multiple_of(c * chunk, chunk), chunk)
            wgu_bf[rs, :] = wgu_ref[0, rs, :].astype(_BF16)
            wdn_bf[rs, :] = wdn_ref[0, rs, :].astype(_BF16)
            return carry
        lax.fori_loop(0, D_MODEL // chunk, cast, 0)

    pltpu.make_async_copy(u_hbm.at[pl.ds(0, rows)], xbuf.at[slot], sem_x.at[slot]).wait()
    x = xbuf[slot].astype(_BF16)
    hh = _dot(x, wgu_bf[...]) + bgu_ref[0]
    glu = jnp.minimum(hh[:, :D_FF], SWIGLU_LIMIT)
    lin = jnp.clip(hh[:, D_FF:], -SWIGLU_LIMIT, SWIGLU_LIMIT)
    act = glu * jax.nn.sigmoid(SWIGLU_ALPHA * glu) * (lin + 1.0)
    ys_ref[...] = _dot(act.astype(_BF16), wdn_bf[...]) + bdn_ref[0]


def _experts(block_e, row_tok, u2, w_gu, b_gu, w_dn, b_dn):
    nb, rows = row_tok.shape
    grid_spec = pltpu.PrefetchScalarGridSpec(
        num_scalar_prefetch=1,
        grid=(nb,),
        in_specs=[pl.BlockSpec(memory_space=pl.ANY), pl.BlockSpec(memory_space=pl.ANY),
                  pl.BlockSpec((1, D_MODEL, 2 * D_FF), lambda i, be: (be[i], 0, 0)),
                  pl.BlockSpec((1, 1, 2 * D_FF), lambda i, be: (be[i], 0, 0)),
                  pl.BlockSpec((1, D_FF, D_MODEL), lambda i, be: (be[i], 0, 0)),
                  pl.BlockSpec((1, 1, D_MODEL), lambda i, be: (be[i], 0, 0))],
        out_specs=pl.BlockSpec((rows, D_MODEL), lambda i, be: (i, 0)),
        scratch_shapes=[pltpu.VMEM((D_MODEL, 2 * D_FF), _BF16), pltpu.VMEM((D_FF, D_MODEL), _BF16),
                        pltpu.VMEM((2, rows, D_MODEL), _F32), pltpu.SMEM((2, rows), jnp.int32),
                        pltpu.SemaphoreType.DMA((2,)), pltpu.SemaphoreType.DMA((2,))],
    )
    return pl.pallas_call(
        _expert_kernel,
        grid_spec=grid_spec,
        out_shape=jax.ShapeDtypeStruct((nb * rows, D_MODEL), _F32),
        compiler_params=pltpu.CompilerParams(dimension_semantics=("arbitrary",), vmem_limit_bytes=VMEM_LIMIT),
        name="experts",
    )(block_e, row_tok, u2, w_gu, b_gu.reshape(N_EXPERTS, 1, 2 * D_FF), w_dn, b_dn.reshape(N_EXPERTS, 1, D_MODEL))


def _final_kernel(dest_hbm, ys_hbm, h1_ref, route_ref, p_ref, g_ple_ref, wpg_ref, wpp_ref, g_fin_ref, out_ref,
                  ybuf, idx, sem_y, sem_i, *, last_layer):
    rows = FINAL_ROWS
    i = pl.program_id(0)
    nt = pl.num_programs(0)
    slot = lax.rem(i, 2)

    def idx_copy(block, s):
        return pltpu.make_async_copy(dest_hbm.at[block], idx.at[s], sem_i.at[s])

    def gather(s):
        def one(r, c):
            _row_copy(ys_hbm, idx[s, r], ybuf.at[s], r, sem_y.at[s]).start()
            return c
        lax.fori_loop(0, TOP_K * rows, one, 0, unroll=8)

    @pl.when(i == 0)
    def _():
        idx_copy(0, 0).start()
        idx_copy(0, 0).wait()
        gather(0)

        @pl.when(nt > 1)
        def _():
            idx_copy(1, 1).start()

    @pl.when(i + 1 < nt)
    def _():
        idx_copy(i + 1, 1 - slot).wait()
        gather(1 - slot)

        @pl.when(i + 2 < nt)
        def _():
            idx_copy(i + 2, slot).start()

    pltpu.make_async_copy(ys_hbm.at[pl.ds(0, TOP_K * rows)], ybuf.at[slot], sem_y.at[slot]).wait()
    route = route_ref[...]
    y = jnp.zeros((rows, D_MODEL), _F32)
    for k in range(TOP_K):
        y = y + ybuf[slot, k * rows:(k + 1) * rows, :] * route[:, TOP_K + k:TOP_K + k + 1]
    h2 = h1_ref[...] + y
    u3 = _rms(h2, g_ple_ref[...]).astype(_BF16)
    gate = jax.nn.sigmoid(_dot(u3, wpg_ref[...]))
    h3 = h2 + gate * _dot(p_ref[...].astype(_BF16), wpp_ref[...])
    out_ref[...] = _rms(h3, g_fin_ref[...]) if last_layer else h3


def _final(dest_k, ys, h1, route, p2, g_ple, wpg, wpp, g_fin, last_layer):
    t = h1.shape[0]
    rows = FINAL_ROWS

    def full(a):
        return pl.BlockSpec(a.shape, lambda i: (0,) * a.ndim)

    def tok(n):
        return pl.BlockSpec((rows, n), lambda i: (i, 0))

    return pl.pallas_call(
        functools.partial(_final_kernel, last_layer=last_layer),
        grid=(t // rows,),
        in_specs=[pl.BlockSpec(memory_space=pl.ANY), pl.BlockSpec(memory_space=pl.ANY), tok(D_MODEL), tok(LANES),
                  tok(PLE_DIM), full(g_ple), full(wpg), full(wpp), full(g_fin)],
        out_specs=tok(D_MODEL),
        out_shape=jax.ShapeDtypeStruct((t, D_MODEL), _F32),
        scratch_shapes=[pltpu.VMEM((2, TOP_K * rows, D_MODEL), _F32), pltpu.SMEM((2, TOP_K * rows), jnp.int32),
                        pltpu.SemaphoreType.DMA((2,)), pltpu.SemaphoreType.DMA((2,))],
        compiler_params=pltpu.CompilerParams(dimension_semantics=("arbitrary",), vmem_limit_bytes=VMEM_LIMIT),
        name="final",
    )(dest_k, ys, h1, route, p2, g_ple, wpg, wpp, g_fin)


def _rope_pad(w, rot):
    half = MLA_ROPE_DIM // 2
    body = jnp.concatenate([-w[:, half:], w[:, :half]], axis=1) if rot else w
    z = jnp.zeros((w.shape[0], MLA_NOPE_DIM), w.dtype)
    return jnp.concatenate([z, body, jnp.zeros((w.shape[0], LANES - MLA_NOPE_DIM - MLA_ROPE_DIM), w.dtype)], axis=1)


def _layer_weights(w_in, w_uq, w_ukv):
    o = Q_LORA_RANK + KV_LORA_RANK
    w_kr = w_in[:, o:o + MLA_ROPE_DIM]
    w1 = jnp.concatenate([w_in[:, :o], _rope_pad(w_kr, False), _rope_pad(w_kr, True), w_in[:, o + MLA_ROPE_DIM:]],
                         axis=1).astype(_BF16)
    uq = w_uq.reshape(Q_LORA_RANK, MLA_HEADS, MLA_NOPE_DIM + MLA_ROPE_DIM)
    zq = jnp.zeros((Q_LORA_RANK, MLA_HEADS, LANES - MLA_NOPE_DIM - MLA_ROPE_DIM), w_uq.dtype)
    wq = jnp.concatenate([uq, zq], axis=2).reshape(Q_LORA_RANK, MLA_HEADS * LANES).astype(_BF16)
    rope = uq[:, :, MLA_NOPE_DIM:]
    half = MLA_ROPE_DIM // 2
    rot = jnp.concatenate([jnp.zeros_like(uq[:, :, :MLA_NOPE_DIM]), -rope[:, :, half:], rope[:, :, :half], zq], axis=2)
    wqr = rot.reshape(Q_LORA_RANK, MLA_HEADS * LANES).astype(_BF16)
    ukv = w_ukv.reshape(KV_LORA_RANK, MLA_HEADS, MLA_NOPE_DIM + MLA_V_DIM)
    zk = jnp.zeros((KV_LORA_RANK, MLA_HEADS, LANES - MLA_NOPE_DIM), w_ukv.dtype)
    wk = jnp.concatenate([ukv[:, :, :MLA_NOPE_DIM], zk], axis=2).reshape(KV_LORA_RANK, MLA_HEADS * LANES).astype(_BF16)
    v = ukv[:, :, MLA_NOPE_DIM:].reshape(KV_LORA_RANK, HEAD_PAIRS, 2, MLA_V_DIM)
    zv = jnp.zeros((KV_LORA_RANK, HEAD_PAIRS, MLA_V_DIM), w_ukv.dtype)
    wv = jnp.stack([jnp.concatenate([v[:, :, 0], zv], axis=2), jnp.concatenate([zv, v[:, :, 1]], axis=2)], axis=2)
    wv = wv.reshape(KV_LORA_RANK, MLA_HEADS * LANES).astype(_BF16)
    return w1, wq, wqr, wk, wv


def kernel(x, p, positions, w_in, g_attn, g_cq, w_uq, g_ckv, w_ukv, g_out_mla, g_out_sb, w_o, g_moe, w_router,
           b_router, w_gu, b_gu, w_dn, b_dn, g_ple, w_ple_gate, w_ple_proj, g_final):
    b, s, d = x.shape
    t = b * s
    depth = w_in.shape[0]
    assert d == D_MODEL and s % MLA_TILE == 0 and t % PROJ_ROWS == 0 and (t * TOP_K) % EXPERT_ROWS == 0

    freq = ROPE_THETA ** (-jnp.arange(0, MLA_ROPE_DIM, 2, dtype=_F32) / MLA_ROPE_DIM)
    invf = jnp.concatenate([jnp.zeros((MLA_NOPE_DIM,), _F32), freq, freq,
                            jnp.zeros((LANES - MLA_NOPE_DIM - MLA_ROPE_DIM,), _F32)]).reshape(1, LANES)
    pos = positions.reshape(t, 1)
    idx = jnp.arange(SB_TILE)
    tri = (idx[:, None] >= idx[None, :]).astype(_BF16)
    idx = jnp.arange(PROJ_ROWS)
    ltri = (idx[None, :] < idx[:, None]).astype(_BF16)
    n_pairs = t * TOP_K
    n_blocks = n_pairs // EXPERT_ROWS + N_EXPERTS

    h = x.reshape(t, d)
    for i in range(depth):
        w1, wq, wqr, wk, wv = _layer_weights(w_in[i], w_uq[i], w_ukv[i])
        qm, km, vm, qs, ks, vs = _proj(pos, h, g_attn[i].reshape(1, d), w1, g_cq[i].reshape(1, -1), wq, wqr,
                                       g_ckv[i].reshape(1, -1), wk, wv, invf)

        def seq(a):
            return a.reshape(b, s, a.shape[1])

        om = _mla(seq(qm), seq(km), seq(vm)).reshape(t, -1)
        os_ = _sb(seq(qs), seq(ks), seq(vs), tri).reshape(t, -1)
        h1, u2, route, counts = _post(om, os_, h, g_out_mla[i].reshape(1, -1), g_out_sb[i].reshape(1, -1),
                                      w_o[i].astype(_BF16), g_moe[i].reshape(1, d), w_router[i],
                                      b_router[i].reshape(1, -1), ltri)

        counts = counts.reshape(-1).astype(jnp.int32)
        padded = (counts + EXPERT_ROWS - 1) // EXPERT_ROWS * EXPERT_ROWS
        pad_end = jnp.cumsum(padded)
        pad_start = pad_end - padded
        top_e = route[:, :TOP_K].astype(jnp.int32)
        rank = route[:, 2 * TOP_K:3 * TOP_K].astype(jnp.int32)
        dest = pad_start[top_e] + rank
        row_tok = jnp.zeros((n_blocks * EXPERT_ROWS,), jnp.int32).at[dest.reshape(-1)].set(
            jnp.arange(n_pairs, dtype=jnp.int32) // TOP_K)
        block_e = jnp.minimum(jnp.searchsorted(pad_end, jnp.arange(n_blocks, dtype=jnp.int32) * EXPERT_ROWS,
                                               side='right'), N_EXPERTS - 1).astype(jnp.int32)
        ys = _experts(block_e, row_tok.reshape(n_blocks, EXPERT_ROWS), u2, w_gu[i], b_gu[i], w_dn[i], b_dn[i])
        dest_k = dest.reshape(t // FINAL_ROWS, FINAL_ROWS, TOP_K).transpose(0, 2, 1).reshape(t // FINAL_ROWS, -1)
        h = _final(dest_k, ys, h1, route, p[i].reshape(t, -1), g_ple[i].reshape(1, d), w_ple_gate[i].astype(_BF16),
                   w_ple_proj[i].astype(_BF16), g_final.reshape(1, d), i == depth - 1)
    return h.reshape(b, s, d)
```

```python
import functools

import jax
import jax.numpy as jnp
from jax import lax
from jax.experimental import pallas as pl
from jax.experimental.pallas import tpu as pltpu

D_MODEL = 1024
PLE_DIM = 256
MLA_HEADS = 8
MLA_NOPE_DIM = 64
MLA_ROPE_DIM = 32
MLA_V_DIM = 64
Q_LORA_RANK = 384
KV_LORA_RANK = 256
SB_HEADS = 8
SB_HEAD_DIM = 64
SB_WIDTH = SB_HEADS * SB_HEAD_DIM
ROPE_THETA = 10000.0
N_EXPERTS = 32
TOP_K = 4
D_FF = 1024
SWIGLU_LIMIT = 7.0
SWIGLU_ALPHA = 1.702
RMS_EPS = 1e-6

LANES = 128
HEAD_PAIRS = MLA_HEADS // 2
PROJ_ROWS = 256
GROUP_HEADS = 4
MLA_Q_TILE = 256
MLA_K_TILE = 512
SB_TILE = 256
EXPERT_ROWS = 256
FINAL_ROWS = 128
VMEM_LIMIT = 56 * 1024 * 1024
SB_LOG_FLOOR = -105.0

_F32 = jnp.float32
_BF16 = jnp.bfloat16


def _rms(x, g):
    return x * lax.rsqrt(jnp.mean(x * x, axis=-1, keepdims=True) + RMS_EPS) * g


def _dot(a, b):
    return jnp.dot(a, b, preferred_element_type=_F32)


def _dot_nt(a, b):
    return lax.dot_general(a, b, (((1,), (1,)), ((), ())), preferred_element_type=_F32)


def _proj_kernel(pos_ref, x_ref, g_attn_ref, w1_ref, g_cq_ref, wq_ref, wqr_ref, g_ckv_ref, wk_ref, wv_ref,
                 invf_ref, qm_ref, km_ref, vm_ref, qs_ref, ks_ref, vs_ref):
    u = _rms(x_ref[...], g_attn_ref[...]).astype(_BF16)
    y = _dot(u, w1_ref[...])
    c_q = y[:, :Q_LORA_RANK]
    c_kv = y[:, Q_LORA_RANK:Q_LORA_RANK + KV_LORA_RANK]
    o = Q_LORA_RANK + KV_LORA_RANK
    k_r = y[:, o:o + LANES]
    k_r_rot = y[:, o + LANES:o + 2 * LANES]
    o += 2 * LANES
    q_s = y[:, o:o + SB_WIDTH]
    k_s = y[:, o + SB_WIDTH:o + 2 * SB_WIDTH]
    v_s = y[:, o + 2 * SB_WIDTH:o + 3 * SB_WIDTH]

    ang = pos_ref[...].astype(_F32) * invf_ref[...]
    cos = jnp.cos(ang)
    sin = jnp.sin(ang)

    cq_n = _rms(c_q, g_cq_ref[...]).astype(_BF16)
    q = _dot(cq_n, wq_ref[...])
    q_rot = _dot(cq_n, wqr_ref[...])
    ckv_n = _rms(c_kv, g_ckv_ref[...]).astype(_BF16)
    k = _dot(ckv_n, wk_ref[...])
    vm_ref[...] = _dot(ckv_n, wv_ref[...]).astype(_BF16)
    k_rope = k_r * cos + k_r_rot * sin

    lane = lax.broadcasted_iota(jnp.int32, (1, LANES), 1)
    low = lane < SB_HEAD_DIM
    for h in range(MLA_HEADS):
        sl = slice(h * LANES, (h + 1) * LANES)
        qm_ref[:, sl] = (q[:, sl] * cos + q_rot[:, sl] * sin).astype(_BF16)
        km_ref[:, sl] = (k[:, sl] + k_rope).astype(_BF16)
    qs_ref[...] = (q_s * (SB_HEAD_DIM ** -0.5)).astype(_BF16)
    for hp in range(HEAD_PAIRS):
        sl = slice(hp * LANES, (hp + 1) * LANES)
        for half, keep in ((0, low), (1, jnp.logical_not(low))):
            dst = slice((2 * hp + half) * LANES, (2 * hp + half + 1) * LANES)
            ks_ref[:, dst] = jnp.where(keep, k_s[:, sl], 0.0).astype(_BF16)
            vs_ref[:, dst] = jnp.where(keep, v_s[:, sl], 0.0).astype(_BF16)


def _proj(pos, x2, g_attn, w1, g_cq, wq, wqr, g_ckv, wk, wv, invf):
    t = x2.shape[0]
    rows = PROJ_ROWS
    wide = MLA_HEADS * LANES

    def full(a):
        return pl.BlockSpec(a.shape, lambda i: (0,) * a.ndim)

    def tok(n):
        return pl.BlockSpec((rows, n), lambda i: (i, 0))

    outs = [jax.ShapeDtypeStruct((t, n), _BF16) for n in (wide, wide, wide, SB_WIDTH, wide, wide)]
    return pl.pallas_call(
        _proj_kernel,
        grid=(t // rows,),
        in_specs=[tok(1), tok(D_MODEL), full(g_attn), full(w1), full(g_cq), full(wq), full(wqr), full(g_ckv),
                  full(wk), full(wv), full(invf)],
        out_specs=[tok(wide), tok(wide), tok(wide), tok(SB_WIDTH), tok(wide), tok(wide)],
        out_shape=outs,
        compiler_params=pltpu.CompilerParams(dimension_semantics=("parallel",), vmem_limit_bytes=VMEM_LIMIT),
        name="proj",
    )(pos, x2, g_attn, w1, g_cq, wq, wqr, g_ckv, wk, wv, invf)


def _mla_kernel(q_ref, k_ref, v_ref, o_ref):
    tq, tk = MLA_Q_TILE, MLA_K_TILE
    qi = pl.program_id(2)
    scale = (MLA_NOPE_DIM + MLA_ROPE_DIM) ** -0.5
    row = lax.broadcasted_iota(jnp.int32, (tq, tk), 0)
    col = lax.broadcasted_iota(jnp.int32, (tq, tk), 1)

    def step(kt, carry, masked):
        ks = pl.ds(pl.multiple_of(kt * tk, tk), tk)
        new = []
        for h in range(GROUP_HEADS):
            m, l, acc = carry[h]
            sl = slice(h * LANES, (h + 1) * LANES)
            s = _dot_nt(q_ref[0, :, sl], k_ref[0, ks, sl]) * scale
            if masked:
                s = jnp.where(kt * tk + col <= qi * tq + row, s, -jnp.inf)
            m_new = jnp.maximum(m, jnp.max(s, axis=-1, keepdims=True))
            alpha = jnp.exp(m - m_new)
            p = jnp.exp(s - m_new)
            l = alpha * l + jnp.sum(p, axis=-1, keepdims=True)
            acc = alpha * acc + _dot(p.astype(_BF16), v_ref[0, ks, sl])
            new.append((m_new, l, acc))
        return tuple(new)

    init = tuple((jnp.full((tq, 1), -jnp.inf, _F32), jnp.zeros((tq, 1), _F32), jnp.zeros((tq, LANES), _F32))
                 for _ in range(GROUP_HEADS))
    last = (qi * tq) // tk
    carry = lax.fori_loop(0, last, functools.partial(step, masked=False), init)
    carry = step(last, carry, True)
    for hp in range(GROUP_HEADS // 2):
        (_, l0, a0), (_, l1, a1) = carry[2 * hp], carry[2 * hp + 1]
        o_ref[0, :, hp * LANES:(hp + 1) * LANES] = a0 / l0 + a1 / l1


def _mla(qm, km, vm):
    b, s, _ = qm.shape
    tq = MLA_Q_TILE
    gw = GROUP_HEADS * LANES
    return pl.pallas_call(
        _mla_kernel,
        grid=(b, MLA_HEADS // GROUP_HEADS, s // tq),
        in_specs=[pl.BlockSpec((1, tq, gw), lambda bi, g, qi: (bi, qi, g)),
                  pl.BlockSpec((1, s, gw), lambda bi, g, qi: (bi, 0, g)),
                  pl.BlockSpec((1, s, gw), lambda bi, g, qi: (bi, 0, g))],
        out_specs=pl.BlockSpec((1, tq, gw // 2), lambda bi, g, qi: (bi, qi, g)),
        out_shape=jax.ShapeDtypeStruct((b, s, HEAD_PAIRS * LANES), _F32),
        compiler_params=pltpu.CompilerParams(dimension_semantics=("parallel", "parallel", "arbitrary"),
                                             vmem_limit_bytes=VMEM_LIMIT),
        name="mla",
    )(qm, km, vm)


def _sb_kernel(q_ref, k_ref, v_ref, tri_ref, o_ref):
    tile = SB_TILE
    qi = pl.program_id(2)
    row = lax.broadcasted_iota(jnp.int32, (tile, tile), 0)
    col = lax.broadcasted_iota(jnp.int32, (tile, tile), 1)
    strict = col < row

    def step(kt, rems, accs, diagonal):
        ks = pl.ds(pl.multiple_of(kt * tile, tile), tile)
        tri = tri_ref[...]
        new_rems, new_accs = [], []
        for h in range(GROUP_HEADS):
            sl = slice(h * LANES, (h + 1) * LANES)
            q = q_ref[0, :, (h // 2) * LANES:(h // 2 + 1) * LANES]
            z = _dot_nt(q, k_ref[0, ks, sl])
            sp = jnp.maximum(z, 0.0) + jnp.log1p(jnp.exp(-jnp.abs(z)))
            log_not = -sp
            if diagonal:
                log_not = jnp.where(strict, log_not, 0.0)
            hi = log_not.astype(_BF16)
            r1 = log_not - hi.astype(_F32)
            mid = r1.astype(_BF16)
            lo = (r1 - mid.astype(_F32)).astype(_BF16)
            incl = _dot(hi, tri) + _dot(mid, tri) + _dot(lo, tri)
            log_a = (z - sp) + (rems[h] + incl - log_not)
            p = jnp.exp(log_a)
            if diagonal:
                p = jnp.where(strict, p, 0.0)
            new_accs.append(accs[h] + _dot(p.astype(_BF16), v_ref[0, ks, sl]))
            new_rems.append(rems[h] + incl[:, :1])
        return tuple(new_rems), tuple(new_accs)

    def rem_max(rems):
        return functools.reduce(jnp.maximum, [jnp.max(r) for r in rems])

    zeros = tuple(jnp.zeros((tile, 1), _F32) for _ in range(GROUP_HEADS))
    rems, accs = step(qi, zeros, tuple(jnp.zeros((tile, LANES), _F32) for _ in range(GROUP_HEADS)), True)

    def cond(state):
        kt, worst, _, _ = state
        return jnp.logical_and(kt >= 0, worst > SB_LOG_FLOOR)

    def body(state):
        kt, _, rems, accs = state
        rems, accs = step(kt, rems, accs, False)
        return kt - 1, rem_max(rems), rems, accs

    _, _, _, accs = lax.while_loop(cond, body, (qi - 1, rem_max(rems), rems, accs))
    for hp in range(GROUP_HEADS // 2):
        o_ref[0, :, hp * LANES:(hp + 1) * LANES] = accs[2 * hp] + accs[2 * hp + 1]


def _sb(qs, ks, vs, tri):
    b, s, _ = qs.shape
    tile = SB_TILE
    gw = GROUP_HEADS * LANES
    return pl.pallas_call(
        _sb_kernel,
        grid=(b, SB_HEADS // GROUP_HEADS, s // tile),
        in_specs=[pl.BlockSpec((1, tile, gw // 2), lambda bi, g, qi: (bi, qi, g)),
                  pl.BlockSpec((1, s, gw), lambda bi, g, qi: (bi, 0, g)),
                  pl.BlockSpec((1, s, gw), lambda bi, g, qi: (bi, 0, g)),
                  pl.BlockSpec((tile, tile), lambda bi, g, qi: (0, 0))],
        out_specs=pl.BlockSpec((1, tile, gw // 2), lambda bi, g, qi: (bi, qi, g)),
        out_shape=jax.ShapeDtypeStruct((b, s, HEAD_PAIRS * LANES), _F32),
        compiler_params=pltpu.CompilerParams(dimension_semantics=("parallel", "parallel", "arbitrary"),
                                             vmem_limit_bytes=VMEM_LIMIT),
        name="sb",
    )(qs, ks, vs, tri)


def _post_kernel(om_ref, os_ref, x_ref, g_om_ref, g_os_ref, wo_ref, g_moe_ref, wr_ref, br_ref, ltri_ref,
                 h1_ref, u2_ref, route_ref, counts_ref, seen_ref):
    rows = PROJ_ROWS

    @pl.when(pl.program_id(0) == 0)
    def _():
        seen_ref[...] = jnp.zeros_like(seen_ref)

    mixed = jnp.concatenate([_rms(om_ref[...], g_om_ref[...]), _rms(os_ref[...], g_os_ref[...])], axis=-1)
    h1 = x_ref[...] + _dot(mixed.astype(_BF16), wo_ref[...])
    h1_ref[...] = h1
    u2 = _rms(h1, g_moe_ref[...])
    u2_ref[...] = u2
    logits = jnp.dot(u2, wr_ref[...], preferred_element_type=_F32, precision=lax.Precision.HIGHEST) + br_ref[...]

    lane = lax.broadcasted_iota(jnp.int32, (rows, N_EXPERTS), 1).astype(_F32)
    work = logits
    ids, tops = [], []
    onehot = jnp.zeros((rows, N_EXPERTS), _F32)
    for _ in range(TOP_K):
        top = jnp.max(work, axis=-1, keepdims=True)
        idx = jnp.min(jnp.where(work == top, lane, float(N_EXPERTS)), axis=-1, keepdims=True)
        hit = lane == idx
        onehot = jnp.where(hit, 1.0, onehot)
        work = jnp.where(hit, -jnp.inf, work)
        ids.append(idx)
        tops.append(top)
    exps = [jnp.exp(tp - tops[0]) for tp in tops]
    denom = exps[0] + exps[1] + exps[2] + exps[3]
    gates = [e / denom for e in exps]

    before = seen_ref[...] + _dot(ltri_ref[...], onehot.astype(_BF16))
    ranks = [jnp.sum(jnp.where(lane == idx, before, 0.0), axis=-1, keepdims=True) for idx in ids]
    seen_ref[...] = seen_ref[...] + jnp.sum(onehot, axis=0, keepdims=True)
    counts_ref[...] = seen_ref[...]

    out_lane = lax.broadcasted_iota(jnp.int32, (rows, LANES), 1)
    route = jnp.zeros((rows, LANES), _F32)
    for j, val in enumerate(ids + gates + ranks):
        route = jnp.where(out_lane == j, val, route)
    route_ref[...] = route


def _post(om, os_, x2, g_om, g_os, wo, g_moe, wr, br, ltri):
    t = x2.shape[0]
    rows = PROJ_ROWS

    def full(a):
        return pl.BlockSpec(a.shape, lambda i: (0,) * a.ndim)

    def tok(n):
        return pl.BlockSpec((rows, n), lambda i: (i, 0))

    return pl.pallas_call(
        _post_kernel,
        grid=(t // rows,),
        in_specs=[tok(om.shape[1]), tok(os_.shape[1]), tok(D_MODEL), full(g_om), full(g_os), full(wo), full(g_moe),
                  full(wr), full(br), full(ltri)],
        out_specs=[tok(D_MODEL), tok(D_MODEL), tok(LANES), pl.BlockSpec((1, N_EXPERTS), lambda i: (0, 0))],
        out_shape=[jax.ShapeDtypeStruct((t, D_MODEL), _F32), jax.ShapeDtypeStruct((t, D_MODEL), _F32),
                   jax.ShapeDtypeStruct((t, LANES), _F32), jax.ShapeDtypeStruct((1, N_EXPERTS), _F32)],
        scratch_shapes=[pltpu.VMEM((1, N_EXPERTS), _F32)],
        compiler_params=pltpu.CompilerParams(dimension_semantics=("arbitrary",), vmem_limit_bytes=VMEM_LIMIT),
        name="post",
    )(om, os_, x2, g_om, g_os, wo, g_moe, wr, br, ltri)


def _row_copy(src_hbm, row, dst, dst_row, sem):
    return pltpu.make_async_copy(src_hbm.at[pl.ds(row, 1)], dst.at[pl.ds(dst_row, 1)], sem)


def _expert_kernel(be_ref, rt_hbm, u_hbm, wgu_ref, bgu_ref, wdn_ref, bdn_ref, ys_ref,
                   wgu_bf, wdn_bf, xbuf, idx, sem_x, sem_i):
    rows = EXPERT_ROWS
    i = pl.program_id(0)
    nb = pl.num_programs(0)
    slot = lax.rem(i, 2)

    def idx_copy(block, s):
        return pltpu.make_async_copy(rt_hbm.at[block], idx.at[s], sem_i.at[s])

    def gather(s):
        def one(r, c):
            _row_copy(u_hbm, idx[s, r], xbuf.at[s], r, sem_x.at[s]).start()
            return c
        lax.fori_loop(0, rows, one, 0, unroll=8)

    @pl.when(i == 0)
    def _():
        idx_copy(0, 0).start()
        idx_copy(0, 0).wait()
        gather(0)

        @pl.when(nb > 1)
        def _():
            idx_copy(1, 1).start()

    @pl.when(i + 1 < nb)
    def _():
        idx_copy(i + 1, 1 - slot).wait()
        gather(1 - slot)

        @pl.when(i + 2 < nb)
        def _():
            idx_copy(i + 2, slot).start()

    e = be_ref[i]
    prev = be_ref[jnp.maximum(i - 1, 0)]

    @pl.when(jnp.logical_or(i == 0, e != prev))
    def _():
        chunk = 128

        def cast(c, carry):
            rs = pl.ds(pl.multiple_of(c * chunk, chunk), chunk)
            wgu_bf[rs, :] = wgu_ref[0, rs, :].astype(_BF16)
            wdn_bf[rs, :] = wdn_ref[0, rs, :].astype(_BF16)
            return carry
        lax.fori_loop(0, D_MODEL // chunk, cast, 0)

    pltpu.make_async_copy(u_hbm.at[pl.ds(0, rows)], xbuf.at[slot], sem_x.at[slot]).wait()
    x = xbuf[slot].astype(_BF16)
    hh = _dot(x, wgu_bf[...]) + bgu_ref[0]
    glu = jnp.minimum(hh[:, :D_FF], SWIGLU_LIMIT)
    lin = jnp.clip(hh[:, D_FF:], -SWIGLU_LIMIT, SWIGLU_LIMIT)
    act = glu * jax.nn.sigmoid(SWIGLU_ALPHA * glu) * (lin + 1.0)
    ys_ref[...] = _dot(act.astype(_BF16), wdn_bf[...]) + bdn_ref[0]


def _experts(block_e, row_tok, u2, w_gu, b_gu, w_dn, b_dn):
    nb, rows = row_tok.shape
    grid_spec = pltpu.PrefetchScalarGridSpec(
        num_scalar_prefetch=1,
        grid=(nb,),
        in_specs=[pl.BlockSpec(memory_space=pl.ANY), pl.BlockSpec(memory_space=pl.ANY),
                  pl.BlockSpec((1, D_MODEL, 2 * D_FF), lambda i, be: (be[i], 0, 0)),
                  pl.BlockSpec((1, 1, 2 * D_FF), lambda i, be: (be[i], 0, 0)),
                  pl.BlockSpec((1, D_FF, D_MODEL), lambda i, be: (be[i], 0, 0)),
                  pl.BlockSpec((1, 1, D_MODEL), lambda i, be: (be[i], 0, 0))],
        out_specs=pl.BlockSpec((rows, D_MODEL), lambda i, be: (i, 0)),
        scratch_shapes=[pltpu.VMEM((D_MODEL, 2 * D_FF), _BF16), pltpu.VMEM((D_FF, D_MODEL), _BF16),
                        pltpu.VMEM((2, rows, D_MODEL), _F32), pltpu.SMEM((2, rows), jnp.int32),
                        pltpu.SemaphoreType.DMA((2,)), pltpu.SemaphoreType.DMA((2,))],
    )
    return pl.pallas_call(
        _expert_kernel,
        grid_spec=grid_spec,
        out_shape=jax.ShapeDtypeStruct((nb * rows, D_MODEL), _F32),
        compiler_params=pltpu.CompilerParams(dimension_semantics=("arbitrary",), vmem_limit_bytes=VMEM_LIMIT),
        name="experts",
    )(block_e, row_tok, u2, w_gu, b_gu.reshape(N_EXPERTS, 1, 2 * D_FF), w_dn, b_dn.reshape(N_EXPERTS, 1, D_MODEL))


def _final_kernel(dest_hbm, ys_hbm, h1_ref, route_ref, p_ref, g_ple_ref, wpg_ref, wpp_ref, g_fin_ref, out_ref,
                  ybuf, idx, sem_y, sem_i, *, last_layer):
    rows = FINAL_ROWS
    i = pl.program_id(0)
    nt = pl.num_programs(0)
    slot = lax.rem(i, 2)

    def idx_copy(block, s):
        return pltpu.make_async_copy(dest_hbm.at[block], idx.at[s], sem_i.at[s])

    def gather(s):
        def one(r, c):
            _row_copy(ys_hbm, idx[s, r], ybuf.at[s], r, sem_y.at[s]).start()
            return c
        lax.fori_loop(0, TOP_K * rows, one, 0, unroll=8)

    @pl.when(i == 0)
    def _():
        idx_copy(0, 0).start()
        idx_copy(0, 0).wait()
        gather(0)

        @pl.when(nt > 1)
        def _():
            idx_copy(1, 1).start()

    @pl.when(i + 1 < nt)
    def _():
        idx_copy(i + 1, 1 - slot).wait()
        gather(1 - slot)

        @pl.when(i + 2 < nt)
        def _():
            idx_copy(i + 2, slot).start()

    pltpu.make_async_copy(ys_hbm.at[pl.ds(0, TOP_K * rows)], ybuf.at[slot], sem_y.at[slot]).wait()
    route = route_ref[...]
    y = jnp.zeros((rows, D_MODEL), _F32)
    for k in range(TOP_K):
        y = y + ybuf[slot, k * rows:(k + 1) * rows, :] * route[:, TOP_K + k:TOP_K + k + 1]
    h2 = h1_ref[...] + y
    u3 = _rms(h2, g_ple_ref[...]).astype(_BF16)
    gate = jax.nn.sigmoid(_dot(u3, wpg_ref[...]))
    h3 = h2 + gate * _dot(p_ref[...].astype(_BF16), wpp_ref[...])
    out_ref[...] = _rms(h3, g_fin_ref[...]) if last_layer else h3


def _final(dest_k, ys, h1, route, p2, g_ple, wpg, wpp, g_fin, last_layer):
    t = h1.shape[0]
    rows = FINAL_ROWS

    def full(a):
        return pl.BlockSpec(a.shape, lambda i: (0,) * a.ndim)

    def tok(n):
        return pl.BlockSpec((rows, n), lambda i: (i, 0))

    return pl.pallas_call(
        functools.partial(_final_kernel, last_layer=last_layer),
        grid=(t // rows,),
        in_specs=[pl.BlockSpec(memory_space=pl.ANY), pl.BlockSpec(memory_space=pl.ANY), tok(D_MODEL), tok(LANES),
                  tok(PLE_DIM), full(g_ple), full(wpg), full(wpp), full(g_fin)],
        out_specs=tok(D_MODEL),
        out_shape=jax.ShapeDtypeStruct((t, D_MODEL), _F32),
        scratch_shapes=[pltpu.VMEM((2, TOP_K * rows, D_MODEL), _F32), pltpu.SMEM((2, TOP_K * rows), jnp.int32),
                        pltpu.SemaphoreType.DMA((2,)), pltpu.SemaphoreType.DMA((2,))],
        compiler_params=pltpu.CompilerParams(dimension_semantics=("arbitrary",), vmem_limit_bytes=VMEM_LIMIT),
        name="final",
    )(dest_k, ys, h1, route, p2, g_ple, wpg, wpp, g_fin)


def _rope_pad(w, rot):
    half = MLA_ROPE_DIM // 2
    body = jnp.concatenate([-w[:, half:], w[:, :half]], axis=1) if rot else w
    z = jnp.zeros((w.shape[0], MLA_NOPE_DIM), w.dtype)
    return jnp.concatenate([z, body, jnp.zeros((w.shape[0], LANES - MLA_NOPE_DIM - MLA_ROPE_DIM), w.dtype)], axis=1)


def _layer_weights(w_in, w_uq, w_ukv):
    o = Q_LORA_RANK + KV_LORA_RANK
    w_kr = w_in[:, o:o + MLA_ROPE_DIM]
    w1 = jnp.concatenate([w_in[:, :o], _rope_pad(w_kr, False), _rope_pad(w_kr, True), w_in[:, o + MLA_ROPE_DIM:]],
                         axis=1).astype(_BF16)
    uq = w_uq.reshape(Q_LORA_RANK, MLA_HEADS, MLA_NOPE_DIM + MLA_ROPE_DIM)
    zq = jnp.zeros((Q_LORA_RANK, MLA_HEADS, LANES - MLA_NOPE_DIM - MLA_ROPE_DIM), w_uq.dtype)
    wq = jnp.concatenate([uq, zq], axis=2).reshape(Q_LORA_RANK, MLA_HEADS * LANES).astype(_BF16)
    rope = uq[:, :, MLA_NOPE_DIM:]
    half = MLA_ROPE_DIM // 2
    rot = jnp.concatenate([jnp.zeros_like(uq[:, :, :MLA_NOPE_DIM]), -rope[:, :, half:], rope[:, :, :half], zq], axis=2)
    wqr = rot.reshape(Q_LORA_RANK, MLA_HEADS * LANES).astype(_BF16)
    ukv = w_ukv.reshape(KV_LORA_RANK, MLA_HEADS, MLA_NOPE_DIM + MLA_V_DIM)
    zk = jnp.zeros((KV_LORA_RANK, MLA_HEADS, LANES - MLA_NOPE_DIM), w_ukv.dtype)
    wk = jnp.concatenate([ukv[:, :, :MLA_NOPE_DIM], zk], axis=2).reshape(KV_LORA_RANK, MLA_HEADS * LANES).astype(_BF16)
    v = ukv[:, :, MLA_NOPE_DIM:].reshape(KV_LORA_RANK, HEAD_PAIRS, 2, MLA_V_DIM)
    zv = jnp.zeros((KV_LORA_RANK, HEAD_PAIRS, MLA_V_DIM), w_ukv.dtype)
    wv = jnp.stack([jnp.concatenate([v[:, :, 0], zv], axis=2), jnp.concatenate([zv, v[:, :, 1]], axis=2)], axis=2)
    wv = wv.reshape(KV_LORA_RANK, MLA_HEADS * LANES).astype(_BF16)
    return w1, wq, wqr, wk, wv


def kernel(x, p, positions, w_in, g_attn, g_cq, w_uq, g_ckv, w_ukv, g_out_mla, g_out_sb, w_o, g_moe, w_router,
           b_router, w_gu, b_gu, w_dn, b_dn, g_ple, w_ple_gate, w_ple_proj, g_final):
    b, s, d = x.shape
    t = b * s
    depth = w_in.shape[0]
    assert d == D_MODEL and s % MLA_K_TILE == 0 and t % PROJ_ROWS == 0 and (t * TOP_K) % EXPERT_ROWS == 0

    freq = ROPE_THETA ** (-jnp.arange(0, MLA_ROPE_DIM, 2, dtype=_F32) / MLA_ROPE_DIM)
    invf = jnp.concatenate([jnp.zeros((MLA_NOPE_DIM,), _F32), freq, freq,
                            jnp.zeros((LANES - MLA_NOPE_DIM - MLA_ROPE_DIM,), _F32)]).reshape(1, LANES)
    pos = positions.reshape(t, 1)
    idx = jnp.arange(SB_TILE)
    tri = (idx[:, None] >= idx[None, :]).astype(_BF16)
    idx = jnp.arange(PROJ_ROWS)
    ltri = (idx[None, :] < idx[:, None]).astype(_BF16)
    n_pairs = t * TOP_K
    n_blocks = n_pairs // EXPERT_ROWS + N_EXPERTS

    h = x.reshape(t, d)
    for i in range(depth):
        w1, wq, wqr, wk, wv = _layer_weights(w_in[i], w_uq[i], w_ukv[i])
        qm, km, vm, qs, ks, vs = _proj(pos, h, g_attn[i].reshape(1, d), w1, g_cq[i].reshape(1, -1), wq, wqr,
                                       g_ckv[i].reshape(1, -1), wk, wv, invf)

        def seq(a):
            return a.reshape(b, s, a.shape[1])

        om = _mla(seq(qm), seq(km), seq(vm)).reshape(t, -1)
        os_ = _sb(seq(qs), seq(ks), seq(vs), tri).reshape(t, -1)
        h1, u2, route, counts = _post(om, os_, h, g_out_mla[i].reshape(1, -1), g_out_sb[i].reshape(1, -1),
                                      w_o[i].astype(_BF16), g_moe[i].reshape(1, d), w_router[i],
                                      b_router[i].reshape(1, -1), ltri)

        counts = counts.reshape(-1).astype(jnp.int32)
        padded = (counts + EXPERT_ROWS - 1) // EXPERT_ROWS * EXPERT_ROWS
        pad_end = jnp.cumsum(padded)
        pad_start = pad_end - padded
        top_e = route[:, :TOP_K].astype(jnp.int32)
        rank = route[:, 2 * TOP_K:3 * TOP_K].astype(jnp.int32)
        dest = pad_start[top_e] + rank
        row_tok = jnp.zeros((n_blocks * EXPERT_ROWS,), jnp.int32).at[dest.reshape(-1)].set(
            jnp.arange(n_pairs, dtype=jnp.int32) // TOP_K)
        block_start = jnp.arange(n_blocks, dtype=jnp.int32) * EXPERT_ROWS
        block_e = jnp.minimum(jnp.sum((pad_end[None, :] <= block_start[:, None]).astype(jnp.int32), axis=1),
                              N_EXPERTS - 1)
        ys = _experts(block_e, row_tok.reshape(n_blocks, EXPERT_ROWS), u2, w_gu[i], b_gu[i], w_dn[i], b_dn[i])
        dest_k = dest.reshape(t // FINAL_ROWS, FINAL_ROWS, TOP_K).transpose(0, 2, 1).reshape(t // FINAL_ROWS, -1)
        h = _final(dest_k, ys, h1, route, p[i].reshape(t, -1), g_ple[i].reshape(1, d), w_ple_gate[i].astype(_BF16),
                   w_ple_proj[i].astype(_BF16), g_final.reshape(1, d), i == depth - 1)
    return h.reshape(b, s, d)
```

```python
import functools
import math

import jax
import jax.numpy as jnp
from jax import lax
from jax.experimental import pallas as pl
from jax.experimental.pallas import tpu as pltpu

D_MODEL = 1024
PLE_DIM = 256
MLA_HEADS = 8
MLA_NOPE_DIM = 64
MLA_ROPE_DIM = 32
MLA_V_DIM = 64
Q_LORA_RANK = 384
KV_LORA_RANK = 256
SB_HEADS = 8
SB_HEAD_DIM = 64
SB_WIDTH = SB_HEADS * SB_HEAD_DIM
ROPE_THETA = 10000.0
N_EXPERTS = 32
TOP_K = 4
D_FF = 1024
SWIGLU_LIMIT = 7.0
SWIGLU_ALPHA = 1.702
RMS_EPS = 1e-6

LANES = 128
SUBLANES = 8
ROW_TILE = D_MODEL // LANES
HEAD_PAIRS = MLA_HEADS // 2
PROJ_ROWS = 256
GROUP_HEADS = 4
MLA_Q_TILE = 256
MLA_K_TILE = 512
SB_TILE = 256
EXPERT_ROWS = 256
MOE_ROWS = SUBLANES * LANES // TOP_K
DMA_GROUP = 8
VMEM_LIMIT = 56 * 1024 * 1024
SB_LOG_FLOOR = -105.0

_F32 = jnp.float32
_BF16 = jnp.bfloat16


def _rms(x, g):
    return x * lax.rsqrt(jnp.mean(x * x, axis=-1, keepdims=True) + RMS_EPS) * g


def _dot(a, b):
    return jnp.dot(a, b, preferred_element_type=_F32)


def _dot_nt(a, b):
    return lax.dot_general(a, b, (((1,), (1,)), ((), ())), preferred_element_type=_F32)


def _store_row_tiles(ref, value):
    rows = value.shape[0]
    for g in range(ROW_TILE):
        ref[pl.ds(g, rows, stride=ROW_TILE), :] = value[:, g * LANES:(g + 1) * LANES]


def _load_row_tiles(ref, first_row, rows):
    return jnp.concatenate([ref[pl.ds(first_row * ROW_TILE + g, rows, stride=ROW_TILE), :] for g in range(ROW_TILE)],
                           axis=-1)


def _proj_kernel(pos_ref, x_ref, g_attn_ref, w1_ref, g_cq_ref, wq_ref, wqr_ref, g_ckv_ref, wk_ref, wv_ref,
                 invf_ref, qm_ref, km_ref, vm_ref, qs_ref, ks_ref, vs_ref):
    u = _rms(x_ref[...], g_attn_ref[...]).astype(_BF16)
    y = _dot(u, w1_ref[...])
    c_q = y[:, :Q_LORA_RANK]
    c_kv = y[:, Q_LORA_RANK:Q_LORA_RANK + KV_LORA_RANK]
    o = Q_LORA_RANK + KV_LORA_RANK
    k_r = y[:, o:o + LANES]
    k_r_rot = y[:, o + LANES:o + 2 * LANES]
    o += 2 * LANES
    q_s = y[:, o:o + SB_WIDTH]
    k_s = y[:, o + SB_WIDTH:o + 2 * SB_WIDTH]
    v_s = y[:, o + 2 * SB_WIDTH:o + 3 * SB_WIDTH]

    ang = pos_ref[...].astype(_F32) * invf_ref[...]
    cos = jnp.cos(ang)
    sin = jnp.sin(ang)

    cq_n = _rms(c_q, g_cq_ref[...]).astype(_BF16)
    q = _dot(cq_n, wq_ref[...])
    q_rot = _dot(cq_n, wqr_ref[...])
    ckv_n = _rms(c_kv, g_ckv_ref[...]).astype(_BF16)
    k = _dot(ckv_n, wk_ref[...])
    vm_ref[...] = _dot(ckv_n, wv_ref[...]).astype(_BF16)
    k_rope = k_r * cos + k_r_rot * sin

    q_scale = (MLA_NOPE_DIM + MLA_ROPE_DIM) ** -0.5 * math.log2(math.e)
    lane = lax.broadcasted_iota(jnp.int32, (1, LANES), 1)
    low = lane < SB_HEAD_DIM
    for h in range(MLA_HEADS):
        sl = slice(h * LANES, (h + 1) * LANES)
        qm_ref[:, sl] = ((q[:, sl] * cos + q_rot[:, sl] * sin) * q_scale).astype(_BF16)
        km_ref[:, sl] = (k[:, sl] + k_rope).astype(_BF16)
    qs_ref[...] = (q_s * (SB_HEAD_DIM ** -0.5)).astype(_BF16)
    for hp in range(HEAD_PAIRS):
        sl = slice(hp * LANES, (hp + 1) * LANES)
        for half, keep in ((0, low), (1, jnp.logical_not(low))):
            dst = slice((2 * hp + half) * LANES, (2 * hp + half + 1) * LANES)
            ks_ref[:, dst] = jnp.where(keep, k_s[:, sl], 0.0).astype(_BF16)
            vs_ref[:, dst] = jnp.where(keep, v_s[:, sl], 0.0).astype(_BF16)


def _proj(pos, x2, g_attn, w1, g_cq, wq, wqr, g_ckv, wk, wv, invf):
    t = x2.shape[0]
    rows = PROJ_ROWS
    wide = MLA_HEADS * LANES

    def full(a):
        return pl.BlockSpec(a.shape, lambda i: (0,) * a.ndim)

    def tok(n):
        return pl.BlockSpec((rows, n), lambda i: (i, 0))

    outs = [jax.ShapeDtypeStruct((t, n), _BF16) for n in (wide, wide, wide, SB_WIDTH, wide, wide)]
    return pl.pallas_call(
        _proj_kernel,
        grid=(t // rows,),
        in_specs=[tok(1), tok(D_MODEL), full(g_attn), full(w1), full(g_cq), full(wq), full(wqr), full(g_ckv),
                  full(wk), full(wv), full(invf)],
        out_specs=[tok(wide), tok(wide), tok(wide), tok(SB_WIDTH), tok(wide), tok(wide)],
        out_shape=outs,
        compiler_params=pltpu.CompilerParams(dimension_semantics=("parallel",), vmem_limit_bytes=VMEM_LIMIT),
        name="proj",
    )(pos, x2, g_attn, w1, g_cq, wq, wqr, g_ckv, wk, wv, invf)


def _mla_kernel(q_ref, k_ref, v_ref, o_ref):
    tq, tk = MLA_Q_TILE, MLA_K_TILE
    qi = pl.program_id(2)
    row = lax.broadcasted_iota(jnp.int32, (tq, tk), 0)
    col = lax.broadcasted_iota(jnp.int32, (tq, tk), 1)

    def step(kt, carry, masked):
        ks = pl.ds(pl.multiple_of(kt * tk, tk), tk)
        new = []
        for h in range(GROUP_HEADS):
            m, l, acc = carry[h]
            sl = slice(h * LANES, (h + 1) * LANES)
            s = _dot_nt(q_ref[0, :, sl], k_ref[0, ks, sl])
            if masked:
                s = jnp.where(kt * tk + col <= qi * tq + row, s, -jnp.inf)
            m_new = jnp.maximum(m, jnp.max(s, axis=-1, keepdims=True))
            alpha = jnp.exp2(m - m_new)
            p = jnp.exp2(s - m_new)
            l = alpha * l + jnp.sum(p, axis=-1, keepdims=True)
            acc = alpha * acc + _dot(p.astype(_BF16), v_ref[0, ks, sl])
            new.append((m_new, l, acc))
        return tuple(new)

    init = tuple((jnp.full((tq, 1), -jnp.inf, _F32), jnp.zeros((tq, 1), _F32), jnp.zeros((tq, LANES), _F32))
                 for _ in range(GROUP_HEADS))
    last = (qi * tq) // tk
    carry = lax.fori_loop(0, last, functools.partial(step, masked=False), init)
    carry = step(last, carry, True)
    for hp in range(GROUP_HEADS // 2):
        (_, l0, a0), (_, l1, a1) = carry[2 * hp], carry[2 * hp + 1]
        o_ref[0, :, hp * LANES:(hp + 1) * LANES] = a0 / l0 + a1 / l1


def _mla(qm, km, vm):
    b, s, _ = qm.shape
    tq = MLA_Q_TILE
    gw = GROUP_HEADS * LANES
    return pl.pallas_call(
        _mla_kernel,
        grid=(b, MLA_HEADS // GROUP_HEADS, s // tq),
        in_specs=[pl.BlockSpec((1, tq, gw), lambda bi, g, qi: (bi, qi, g)),
                  pl.BlockSpec((1, s, gw), lambda bi, g, qi: (bi, 0, g)),
                  pl.BlockSpec((1, s, gw), lambda bi, g, qi: (bi, 0, g))],
        out_specs=pl.BlockSpec((1, tq, gw // 2), lambda bi, g, qi: (bi, qi, g)),
        out_shape=jax.ShapeDtypeStruct((b, s, HEAD_PAIRS * LANES), _F32),
        compiler_params=pltpu.CompilerParams(dimension_semantics=("parallel", "parallel", "arbitrary"),
                                             vmem_limit_bytes=VMEM_LIMIT),
        name="mla",
    )(qm, km, vm)


def _sb_kernel(q_ref, k_ref, v_ref, tri_ref, o_ref):
    tile = SB_TILE
    qi = pl.program_id(2)
    row = lax.broadcasted_iota(jnp.int32, (tile, tile), 0)
    col = lax.broadcasted_iota(jnp.int32, (tile, tile), 1)
    strict = col < row

    def step(kt, rems, accs, diagonal):
        ks = pl.ds(pl.multiple_of(kt * tile, tile), tile)
        tri = tri_ref[...]
        new_rems, new_accs = [], []
        for h in range(GROUP_HEADS):
            sl = slice(h * LANES, (h + 1) * LANES)
            q = q_ref[0, :, (h // 2) * LANES:(h // 2 + 1) * LANES]
            z = _dot_nt(q, k_ref[0, ks, sl])
            sp = jnp.maximum(z, 0.0) + jnp.log1p(jnp.exp(-jnp.abs(z)))
            log_not = -sp
            if diagonal:
                log_not = jnp.where(strict, log_not, 0.0)
            hi = log_not.astype(_BF16)
            r1 = log_not - hi.astype(_F32)
            mid = r1.astype(_BF16)
            lo = (r1 - mid.astype(_F32)).astype(_BF16)
            incl = _dot(hi, tri) + _dot(mid, tri) + _dot(lo, tri)
            log_a = (z - sp) + (rems[h] + incl - log_not)
            p = jnp.exp(log_a)
            if diagonal:
                p = jnp.where(strict, p, 0.0)
            new_accs.append(accs[h] + _dot(p.astype(_BF16), v_ref[0, ks, sl]))
            new_rems.append(rems[h] + incl[:, :1])
        return tuple(new_rems), tuple(new_accs)

    def rem_max(rems):
        return functools.reduce(jnp.maximum, [jnp.max(r) for r in rems])

    zeros = tuple(jnp.zeros((tile, 1), _F32) for _ in range(GROUP_HEADS))
    rems, accs = step(qi, zeros, tuple(jnp.zeros((tile, LANES), _F32) for _ in range(GROUP_HEADS)), True)

    def cond(state):
        kt, worst, _, _ = state
        return jnp.logical_and(kt >= 0, worst > SB_LOG_FLOOR)

    def body(state):
        kt, _, rems, accs = state
        rems, accs = step(kt, rems, accs, False)
        return kt - 1, rem_max(rems), rems, accs

    _, _, _, accs = lax.while_loop(cond, body, (qi - 1, rem_max(rems), rems, accs))
    for hp in range(GROUP_HEADS // 2):
        o_ref[0, :, hp * LANES:(hp + 1) * LANES] = accs[2 * hp] + accs[2 * hp + 1]


def _sb(qs, ks, vs, tri):
    b, s, _ = qs.shape
    tile = SB_TILE
    gw = GROUP_HEADS * LANES
    return pl.pallas_call(
        _sb_kernel,
        grid=(b, SB_HEADS // GROUP_HEADS, s // tile),
        in_specs=[pl.BlockSpec((1, tile, gw // 2), lambda bi, g, qi: (bi, qi, g)),
                  pl.BlockSpec((1, s, gw), lambda bi, g, qi: (bi, 0, g)),
                  pl.BlockSpec((1, s, gw), lambda bi, g, qi: (bi, 0, g)),
                  pl.BlockSpec((tile, tile), lambda bi, g, qi: (0, 0))],
        out_specs=pl.BlockSpec((1, tile, gw // 2), lambda bi, g, qi: (bi, qi, g)),
        out_shape=jax.ShapeDtypeStruct((b, s, HEAD_PAIRS * LANES), _F32),
        compiler_params=pltpu.CompilerParams(dimension_semantics=("parallel", "parallel", "arbitrary"),
                                             vmem_limit_bytes=VMEM_LIMIT),
        name="sb",
    )(qs, ks, vs, tri)


def _post_kernel(om_ref, os_ref, x_ref, g_om_ref, g_os_ref, wo_ref, g_moe_ref, wr_ref, br_ref, ltri_ref,
                 h1_ref, u2_ref, route_ref, counts_ref, seen_ref):
    rows = PROJ_ROWS

    @pl.when(pl.program_id(0) == 0)
    def _():
        seen_ref[...] = jnp.zeros_like(seen_ref)

    mixed = jnp.concatenate([_rms(om_ref[...], g_om_ref[...]), _rms(os_ref[...], g_os_ref[...])], axis=-1)
    h1 = x_ref[...] + _dot(mixed.astype(_BF16), wo_ref[...])
    h1_ref[...] = h1
    u2 = _rms(h1, g_moe_ref[...])
    _store_row_tiles(u2_ref, u2)
    logits = jnp.dot(u2, wr_ref[...], preferred_element_type=_F32, precision=lax.Precision.HIGHEST) + br_ref[...]

    lane = lax.broadcasted_iota(jnp.int32, (rows, N_EXPERTS), 1).astype(_F32)
    work = logits
    ids, tops = [], []
    onehot = jnp.zeros((rows, N_EXPERTS), _F32)
    for _ in range(TOP_K):
        top = jnp.max(work, axis=-1, keepdims=True)
        idx = jnp.min(jnp.where(work == top, lane, float(N_EXPERTS)), axis=-1, keepdims=True)
        hit = lane == idx
        onehot = jnp.where(hit, 1.0, onehot)
        work = jnp.where(hit, -jnp.inf, work)
        ids.append(idx)
        tops.append(top)
    exps = [jnp.exp(tp - tops[0]) for tp in tops]
    denom = exps[0] + exps[1] + exps[2] + exps[3]
    gates = [e / denom for e in exps]

    before = seen_ref[...] + _dot(ltri_ref[...], onehot.astype(_BF16))
    ranks = [jnp.sum(jnp.where(lane == idx, before, 0.0), axis=-1, keepdims=True) for idx in ids]
    seen_ref[...] = seen_ref[...] + jnp.sum(onehot, axis=0, keepdims=True)
    counts_ref[...] = seen_ref[...]

    out_lane = lax.broadcasted_iota(jnp.int32, (rows, LANES), 1)
    route = jnp.zeros((rows, LANES), _F32)
    for j, val in enumerate(ids + gates + ranks):
        route = jnp.where(out_lane == j, val, route)
    route_ref[...] = route


def _post(om, os_, x2, g_om, g_os, wo, g_moe, wr, br, ltri):
    t = x2.shape[0]
    rows = PROJ_ROWS

    def full(a):
        return pl.BlockSpec(a.shape, lambda i: (0,) * a.ndim)

    def tok(n):
        return pl.BlockSpec((rows, n), lambda i: (i, 0))

    return pl.pallas_call(
        _post_kernel,
        grid=(t // rows,),
        in_specs=[tok(om.shape[1]), tok(os_.shape[1]), tok(D_MODEL), full(g_om), full(g_os), full(wo), full(g_moe),
                  full(wr), full(br), full(ltri)],
        out_specs=[tok(D_MODEL), pl.BlockSpec((rows * ROW_TILE, LANES), lambda i: (i, 0)), tok(LANES),
                   pl.BlockSpec((1, N_EXPERTS), lambda i: (0, 0))],
        out_shape=[jax.ShapeDtypeStruct((t, D_MODEL), _F32), jax.ShapeDtypeStruct((t * ROW_TILE, LANES), _F32),
                   jax.ShapeDtypeStruct((t, LANES), _F32), jax.ShapeDtypeStruct((1, N_EXPERTS), _F32)],
        scratch_shapes=[pltpu.VMEM((1, N_EXPERTS), _F32)],
        compiler_params=pltpu.CompilerParams(dimension_semantics=("arbitrary",), vmem_limit_bytes=VMEM_LIMIT),
        name="post",
    )(om, os_, x2, g_om, g_os, wo, g_moe, wr, br, ltri)


def _padded_count(cnt_ref, e):
    shift = EXPERT_ROWS.bit_length() - 1
    return lax.shift_left(lax.shift_right_logical(cnt_ref[e] + (EXPERT_ROWS - 1), shift), shift)


def _layout_kernel(cnt_ref, ids_ref, rank_ref, dest_ref, start_ref, be_ref):
    shift = EXPERT_ROWS.bit_length() - 1
    nb = be_ref.shape[0]

    def place(e, off):
        padded = _padded_count(cnt_ref, e)
        start_ref[e] = off

        def mark(j, c):
            be_ref[lax.shift_right_logical(off, shift) + j] = e
            return c
        lax.fori_loop(0, lax.shift_right_logical(padded, shift), mark, 0)
        return off + padded
    total = lax.fori_loop(0, N_EXPERTS, place, 0)

    def tail(b, c):
        be_ref[b] = N_EXPERTS - 1
        return c
    lax.fori_loop(lax.shift_right_logical(total, shift), nb, tail, 0)

    ids = ids_ref[...]
    dest = rank_ref[...]
    for e in range(N_EXPERTS):
        dest = dest + jnp.where(ids == e, start_ref[e], 0)
    dest_ref[...] = dest


def _layout(counts, ids, rank, n_blocks):
    grid_spec = pltpu.PrefetchScalarGridSpec(
        num_scalar_prefetch=1,
        grid=(1,),
        in_specs=[pl.BlockSpec(ids.shape, lambda i, c: (0, 0)), pl.BlockSpec(rank.shape, lambda i, c: (0, 0))],
        out_specs=[pl.BlockSpec(ids.shape, lambda i, c: (0, 0)), pl.BlockSpec(memory_space=pltpu.SMEM),
                   pl.BlockSpec(memory_space=pltpu.SMEM)],
    )
    return pl.pallas_call(
        _layout_kernel,
        grid_spec=grid_spec,
        out_shape=[jax.ShapeDtypeStruct(ids.shape, jnp.int32), jax.ShapeDtypeStruct((N_EXPERTS,), jnp.int32),
                   jax.ShapeDtypeStruct((n_blocks,), jnp.int32)],
        compiler_params=pltpu.CompilerParams(dimension_semantics=("arbitrary",)),
        name="layout",
    )(counts, ids, rank)


def _tile_rows(row):
    return pl.ds(pl.multiple_of(row * ROW_TILE, ROW_TILE), ROW_TILE)


def _pair_rows(idx, s, first_token, n_tokens):
    flat = [(first_token + t) * TOP_K + k for t in range(n_tokens) for k in range(TOP_K)]
    return [idx[s, f // LANES, f % LANES] for f in flat]


def _dispatch_kernel(cnt_ref, start_ref, dest_hbm, u_hbm, xs_hbm, idx, zrow, zblk, sem_d, sem_i, sem_z):
    rows = MOE_ROWS
    shift = EXPERT_ROWS.bit_length() - 1
    i = pl.program_id(0)
    nt = pl.num_programs(0)
    nb = xs_hbm.shape[0] // (EXPERT_ROWS * ROW_TILE)
    slot = lax.rem(i, 2)

    def idx_copy(tile, s):
        return pltpu.make_async_copy(dest_hbm.at[pl.ds(pl.multiple_of(tile * SUBLANES, SUBLANES), SUBLANES)],
                                     idx.at[s], sem_i.at[s])

    def zero_row(r):
        return pltpu.make_async_copy(zrow, xs_hbm.at[_tile_rows(r)], sem_z)

    def zero_block(b):
        dst = pl.ds(pl.multiple_of(b * (EXPERT_ROWS * ROW_TILE), EXPERT_ROWS * ROW_TILE), EXPERT_ROWS * ROW_TILE)
        return pltpu.make_async_copy(zblk, xs_hbm.at[dst], sem_z)

    def pad_rows(e, fn):
        def one(r, c):
            fn(zero_row(start_ref[e] + r))
            return c
        lax.fori_loop(cnt_ref[e], _padded_count(cnt_ref, e), one, 0)

    def tail_blocks(fn):
        used = start_ref[N_EXPERTS - 1] + _padded_count(cnt_ref, N_EXPERTS - 1)

        def one(b, c):
            fn(zero_block(b))
            return c
        lax.fori_loop(lax.shift_right_logical(used, shift), nb, one, 0)

    @pl.when(i == 0)
    def _():
        zrow[...] = jnp.zeros_like(zrow)
        zblk[...] = jnp.zeros_like(zblk)
        for fn in (lambda c: c.start(), lambda c: c.wait()):
            def per_expert(e, c, fn=fn):
                pad_rows(e, fn)
                return c
            lax.fori_loop(0, N_EXPERTS, per_expert, 0)
            tail_blocks(fn)
        idx_copy(0, 0).start()

    idx_copy(i, slot).wait()

    @pl.when(i + 1 < nt)
    def _():
        idx_copy(i + 1, 1 - slot).start()

    base = i * rows
    for t0 in range(0, rows, DMA_GROUP):
        dests = _pair_rows(idx, slot, t0, DMA_GROUP)
        for j, dest in enumerate(dests):
            src = u_hbm.at[_tile_rows(base + t0 + j // TOP_K)]
            pltpu.make_async_copy(src, xs_hbm.at[_tile_rows(dest)], sem_d.at[slot]).start()

    def wait_step(s):
        chunk = pl.ds(0, rows * ROW_TILE)
        for _ in range(TOP_K):
            pltpu.make_async_copy(u_hbm.at[chunk], xs_hbm.at[chunk], sem_d.at[s]).wait()

    @pl.when(i > 0)
    def _():
        wait_step(1 - slot)

    @pl.when(i == nt - 1)
    def _():
        wait_step(slot)


def _dispatch(counts, start, dest, u2_tiles, n_blocks):
    nt = dest.shape[0] // SUBLANES
    grid_spec = pltpu.PrefetchScalarGridSpec(
        num_scalar_prefetch=2,
        grid=(nt,),
        in_specs=[pl.BlockSpec(memory_space=pl.ANY), pl.BlockSpec(memory_space=pl.ANY)],
        out_specs=pl.BlockSpec(memory_space=pl.ANY),
        scratch_shapes=[pltpu.SMEM((2, SUBLANES, LANES), jnp.int32), pltpu.VMEM((ROW_TILE, LANES), _F32),
                        pltpu.VMEM((EXPERT_ROWS * ROW_TILE, LANES), _F32), pltpu.SemaphoreType.DMA((2,)),
                        pltpu.SemaphoreType.DMA((2,)), pltpu.SemaphoreType.DMA(())],
    )
    return pl.pallas_call(
        _dispatch_kernel,
        grid_spec=grid_spec,
        out_shape=jax.ShapeDtypeStruct((n_blocks * EXPERT_ROWS * ROW_TILE, LANES), _F32),
        compiler_params=pltpu.CompilerParams(dimension_semantics=("arbitrary",), vmem_limit_bytes=VMEM_LIMIT),
        name="dispatch",
    )(counts, start, dest, u2_tiles)


def _expert_kernel(be_ref, xs_ref, wgu_ref, bgu_ref, wdn_ref, bdn_ref, ys_ref, wgu_bf, wdn_bf):
    rows = EXPERT_ROWS
    i = pl.program_id(0)
    e = be_ref[i]
    prev = be_ref[jnp.maximum(i - 1, 0)]

    @pl.when(jnp.logical_or(i == 0, e != prev))
    def _():
        chunk = 128

        def cast(c, carry):
            rs = pl.ds(pl.multiple_of(c * chunk, chunk), chunk)
            wgu_bf[rs, :] = wgu_ref[0, rs, :].astype(_BF16)
            wdn_bf[rs, :] = wdn_ref[0, rs, :].astype(_BF16)
            return carry
        lax.fori_loop(0, D_MODEL // chunk, cast, 0)

    x = _load_row_tiles(xs_ref, 0, rows).astype(_BF16)
    hh = _dot(x, wgu_bf[...]) + bgu_ref[0]
    glu = jnp.minimum(hh[:, :D_FF], SWIGLU_LIMIT)
    lin = jnp.clip(hh[:, D_FF:], -SWIGLU_LIMIT, SWIGLU_LIMIT)
    act = glu * jax.nn.sigmoid(SWIGLU_ALPHA * glu) * (lin + 1.0)
    _store_row_tiles(ys_ref, _dot(act.astype(_BF16), wdn_bf[...]) + bdn_ref[0])


def _experts(block_e, xs, w_gu, b_gu, w_dn, b_dn):
    nb = block_e.shape[0]
    rows = EXPERT_ROWS
    grid_spec = pltpu.PrefetchScalarGridSpec(
        num_scalar_prefetch=1,
        grid=(nb,),
        in_specs=[pl.BlockSpec((rows * ROW_TILE, LANES), lambda i, be: (i, 0)),
                  pl.BlockSpec((1, D_MODEL, 2 * D_FF), lambda i, be: (be[i], 0, 0)),
                  pl.BlockSpec((1, 1, 2 * D_FF), lambda i, be: (be[i], 0, 0)),
                  pl.BlockSpec((1, D_FF, D_MODEL), lambda i, be: (be[i], 0, 0)),
                  pl.BlockSpec((1, 1, D_MODEL), lambda i, be: (be[i], 0, 0))],
        out_specs=pl.BlockSpec((rows * ROW_TILE, LANES), lambda i, be: (i, 0)),
        scratch_shapes=[pltpu.VMEM((D_MODEL, 2 * D_FF), _BF16), pltpu.VMEM((D_FF, D_MODEL), _BF16)],
    )
    return pl.pallas_call(
        _expert_kernel,
        grid_spec=grid_spec,
        out_shape=jax.ShapeDtypeStruct(xs.shape, _F32),
        compiler_params=pltpu.CompilerParams(dimension_semantics=("arbitrary",), vmem_limit_bytes=VMEM_LIMIT),
        name="experts",
    )(block_e, xs, w_gu, b_gu.reshape(N_EXPERTS, 1, 2 * D_FF), w_dn, b_dn.reshape(N_EXPERTS, 1, D_MODEL))


def _final_kernel(dest_hbm, ys_hbm, h1_ref, route_ref, p_ref, g_ple_ref, wpg_ref, wpp_ref, g_fin_ref,
                  out_ref, ybuf, idx, sem_y, sem_i, *, last_layer):
    rows = MOE_ROWS
    i = pl.program_id(0)
    nt = pl.num_programs(0)
    slot = lax.rem(i, 2)

    def idx_copy(tile, s):
        return pltpu.make_async_copy(dest_hbm.at[pl.ds(pl.multiple_of(tile * SUBLANES, SUBLANES), SUBLANES)],
                                     idx.at[s], sem_i.at[s])

    def gather(s):
        for t0 in range(0, rows, DMA_GROUP):
            srcs = _pair_rows(idx, s, t0, DMA_GROUP)
            for j, src in enumerate(srcs):
                t, k = t0 + j // TOP_K, j % TOP_K
                dst = pl.ds((k * rows + t) * ROW_TILE, ROW_TILE)
                pltpu.make_async_copy(ys_hbm.at[_tile_rows(src)], ybuf.at[s, dst], sem_y.at[s]).start()

    @pl.when(i == 0)
    def _():
        idx_copy(0, 0).start()
        idx_copy(0, 0).wait()
        gather(0)

        @pl.when(nt > 1)
        def _():
            idx_copy(1, 1).start()

    @pl.when(i + 1 < nt)
    def _():
        idx_copy(i + 1, 1 - slot).wait()
        gather(1 - slot)

        @pl.when(i + 2 < nt)
        def _():
            idx_copy(i + 2, slot).start()

    pltpu.make_async_copy(ys_hbm.at[pl.ds(0, TOP_K * rows * ROW_TILE)], ybuf.at[slot], sem_y.at[slot]).wait()
    route = route_ref[...]
    y = jnp.zeros((rows, D_MODEL), _F32)
    for k in range(TOP_K):
        y = y + _load_row_tiles(ybuf.at[slot], k * rows, rows) * route[:, TOP_K + k:TOP_K + k + 1]
    h2 = h1_ref[...] + y
    u3 = _rms(h2, g_ple_ref[...]).astype(_BF16)
    gate = jax.nn.sigmoid(_dot(u3, wpg_ref[...]))
    h3 = h2 + gate * _dot(p_ref[...].astype(_BF16), wpp_ref[...])
    out_ref[...] = _rms(h3, g_fin_ref[...]) if last_layer else h3


def _final(dest, ys, h1, route, p2, g_ple, wpg, wpp, g_fin, last_layer):
    t = h1.shape[0]
    rows = MOE_ROWS

    def full(a):
        return pl.BlockSpec(a.shape, lambda i: (0,) * a.ndim)

    def tok(n):
        return pl.BlockSpec((rows, n), lambda i: (i, 0))

    return pl.pallas_call(
        functools.partial(_final_kernel, last_layer=last_layer),
        grid=(t // rows,),
        in_specs=[pl.BlockSpec(memory_space=pl.ANY), pl.BlockSpec(memory_space=pl.ANY), tok(D_MODEL), tok(LANES),
                  tok(PLE_DIM), full(g_ple), full(wpg), full(wpp), full(g_fin)],
        out_specs=tok(D_MODEL),
        out_shape=jax.ShapeDtypeStruct((t, D_MODEL), _F32),
        scratch_shapes=[pltpu.VMEM((2, TOP_K * rows * ROW_TILE, LANES), _F32),
                        pltpu.SMEM((2, SUBLANES, LANES), jnp.int32),
                        pltpu.SemaphoreType.DMA((2,)), pltpu.SemaphoreType.DMA((2,))],
        compiler_params=pltpu.CompilerParams(dimension_semantics=("arbitrary",), vmem_limit_bytes=VMEM_LIMIT),
        name="final",
    )(dest, ys, h1, route, p2, g_ple, wpg, wpp, g_fin)


def _rope_pad(w, rot):
    half = MLA_ROPE_DIM // 2
    body = jnp.concatenate([-w[:, half:], w[:, :half]], axis=1) if rot else w
    z = jnp.zeros((w.shape[0], MLA_NOPE_DIM), w.dtype)
    return jnp.concatenate([z, body, jnp.zeros((w.shape[0], LANES - MLA_NOPE_DIM - MLA_ROPE_DIM), w.dtype)], axis=1)


def _layer_weights(w_in, w_uq, w_ukv):
    o = Q_LORA_RANK + KV_LORA_RANK
    w_kr = w_in[:, o:o + MLA_ROPE_DIM]
    w1 = jnp.concatenate([w_in[:, :o], _rope_pad(w_kr, False), _rope_pad(w_kr, True), w_in[:, o + MLA_ROPE_DIM:]],
                         axis=1).astype(_BF16)
    uq = w_uq.reshape(Q_LORA_RANK, MLA_HEADS, MLA_NOPE_DIM + MLA_ROPE_DIM)
    zq = jnp.zeros((Q_LORA_RANK, MLA_HEADS, LANES - MLA_NOPE_DIM - MLA_ROPE_DIM), w_uq.dtype)
    wq = jnp.concatenate([uq, zq], axis=2).reshape(Q_LORA_RANK, MLA_HEADS * LANES).astype(_BF16)
    rope = uq[:, :, MLA_NOPE_DIM:]
    half = MLA_ROPE_DIM // 2
    rot = jnp.concatenate([jnp.zeros_like(uq[:, :, :MLA_NOPE_DIM]), -rope[:, :, half:], rope[:, :, :half], zq], axis=2)
    wqr = rot.reshape(Q_LORA_RANK, MLA_HEADS * LANES).astype(_BF16)
    ukv = w_ukv.reshape(KV_LORA_RANK, MLA_HEADS, MLA_NOPE_DIM + MLA_V_DIM)
    zk = jnp.zeros((KV_LORA_RANK, MLA_HEADS, LANES - MLA_NOPE_DIM), w_ukv.dtype)
    wk = jnp.concatenate([ukv[:, :, :MLA_NOPE_DIM], zk], axis=2).reshape(KV_LORA_RANK, MLA_HEADS * LANES).astype(_BF16)
    v = ukv[:, :, MLA_NOPE_DIM:].reshape(KV_LORA_RANK, HEAD_PAIRS, 2, MLA_V_DIM)
    zv = jnp.zeros((KV_LORA_RANK, HEAD_PAIRS, MLA_V_DIM), w_ukv.dtype)
    wv = jnp.stack([jnp.concatenate([v[:, :, 0], zv], axis=2), jnp.concatenate([zv, v[:, :, 1]], axis=2)], axis=2)
    wv = wv.reshape(KV_LORA_RANK, MLA_HEADS * LANES).astype(_BF16)
    return w1, wq, wqr, wk, wv


def kernel(x, p, positions, w_in, g_attn, g_cq, w_uq, g_ckv, w_ukv, g_out_mla, g_out_sb, w_o, g_moe, w_router,
           b_router, w_gu, b_gu, w_dn, b_dn, g_ple, w_ple_gate, w_ple_proj, g_final):
    b, s, d = x.shape
    t = b * s
    depth = w_in.shape[0]
    assert d == D_MODEL and ROW_TILE == SUBLANES and s % MLA_K_TILE == 0 and t % PROJ_ROWS == 0
    assert t % MOE_ROWS == 0 and MOE_ROWS % DMA_GROUP == 0 and (t * TOP_K) % EXPERT_ROWS == 0
    assert EXPERT_ROWS & (EXPERT_ROWS - 1) == 0

    freq = ROPE_THETA ** (-jnp.arange(0, MLA_ROPE_DIM, 2, dtype=_F32) / MLA_ROPE_DIM)
    invf = jnp.concatenate([jnp.zeros((MLA_NOPE_DIM,), _F32), freq, freq,
                            jnp.zeros((LANES - MLA_NOPE_DIM - MLA_ROPE_DIM,), _F32)]).reshape(1, LANES)
    pos = positions.reshape(t, 1)
    idx = jnp.arange(SB_TILE)
    tri = (idx[:, None] >= idx[None, :]).astype(_BF16)
    idx = jnp.arange(PROJ_ROWS)
    ltri = (idx[None, :] < idx[:, None]).astype(_BF16)
    n_blocks = t * TOP_K // EXPERT_ROWS + N_EXPERTS

    h = x.reshape(t, d)
    for i in range(depth):
        w1, wq, wqr, wk, wv = _layer_weights(w_in[i], w_uq[i], w_ukv[i])
        qm, km, vm, qs, ks, vs = _proj(pos, h, g_attn[i].reshape(1, d), w1, g_cq[i].reshape(1, -1), wq, wqr,
                                       g_ckv[i].reshape(1, -1), wk, wv, invf)

        def seq(a):
            return a.reshape(b, s, a.shape[1])

        om = _mla(seq(qm), seq(km), seq(vm)).reshape(t, -1)
        os_ = _sb(seq(qs), seq(ks), seq(vs), tri).reshape(t, -1)
        h1, u2_tiles, route, counts = _post(om, os_, h, g_out_mla[i].reshape(1, -1), g_out_sb[i].reshape(1, -1),
                                            w_o[i].astype(_BF16), g_moe[i].reshape(1, d), w_router[i],
                                            b_router[i].reshape(1, -1), ltri)
        ids = route[:, :TOP_K].astype(jnp.int32).reshape(-1, LANES)
        rank = route[:, 2 * TOP_K:3 * TOP_K].astype(jnp.int32).reshape(-1, LANES)
        counts = counts.reshape(-1).astype(jnp.int32)
        dest, start, block_e = _layout(counts, ids, rank, n_blocks)
        xs = _dispatch(counts, start, dest, u2_tiles, n_blocks)
        ys = _experts(block_e, xs, w_gu[i], b_gu[i], w_dn[i], b_dn[i])
        h = _final(dest, ys, h1, route, p[i].reshape(t, -1), g_ple[i].reshape(1, d), w_ple_gate[i].astype(_BF16),
                   w_ple_proj[i].astype(_BF16), g_final.reshape(1, d), i == depth - 1)
    return h.reshape(b, s, d)
```

```python
import functools
import math

import jax
import jax.numpy as jnp
from jax import lax
from jax.experimental import pallas as pl
from jax.experimental.pallas import tpu as pltpu

D_MODEL = 1024
PLE_DIM = 256
MLA_HEADS = 8
MLA_NOPE_DIM = 64
MLA_ROPE_DIM = 32
MLA_V_DIM = 64
Q_LORA_RANK = 384
KV_LORA_RANK = 256
SB_HEADS = 8
SB_HEAD_DIM = 64
SB_WIDTH = SB_HEADS * SB_HEAD_DIM
ROPE_THETA = 10000.0
N_EXPERTS = 32
TOP_K = 4
D_FF = 1024
SWIGLU_LIMIT = 7.0
SWIGLU_ALPHA = 1.702
RMS_EPS = 1e-6

LANES = 128
SUBLANES = 8
ROW_TILE = D_MODEL // LANES
HEAD_PAIRS = MLA_HEADS // 2
PROJ_ROWS = 256
GROUP_HEADS = 4
MLA_Q_TILE = 256
MLA_K_TILE = 512
SB_TILE = 256
EXPERT_ROWS = 256
MOE_ROWS = SUBLANES * LANES // TOP_K
DMA_GROUP = 8
STAGE_SLOTS = 4
VMEM_LIMIT = 56 * 1024 * 1024
SB_LOG_FLOOR = -105.0

_F32 = jnp.float32
_BF16 = jnp.bfloat16


def _rms(x, g):
    return x * lax.rsqrt(jnp.mean(x * x, axis=-1, keepdims=True) + RMS_EPS) * g


def _dot(a, b):
    return jnp.dot(a, b, preferred_element_type=_F32)


def _dot_nt(a, b):
    return lax.dot_general(a, b, (((1,), (1,)), ((), ())), preferred_element_type=_F32)


def _store_row_tiles(ref, value):
    rows = value.shape[0]
    for g in range(ROW_TILE):
        ref[pl.ds(g, rows, stride=ROW_TILE), :] = value[:, g * LANES:(g + 1) * LANES]


def _load_row_tiles(ref, first_row, rows):
    return jnp.concatenate([ref[pl.ds(first_row * ROW_TILE + g, rows, stride=ROW_TILE), :] for g in range(ROW_TILE)],
                           axis=-1)


def _proj_kernel(pos_ref, x_ref, g_attn_ref, w1_ref, g_cq_ref, wq_ref, wqr_ref, g_ckv_ref, wk_ref, wv_ref,
                 invf_ref, qm_ref, km_ref, vm_ref, qs_ref, ks_ref, vs_ref):
    u = _rms(x_ref[...], g_attn_ref[...]).astype(_BF16)
    y = _dot(u, w1_ref[...])
    c_q = y[:, :Q_LORA_RANK]
    c_kv = y[:, Q_LORA_RANK:Q_LORA_RANK + KV_LORA_RANK]
    o = Q_LORA_RANK + KV_LORA_RANK
    k_r = y[:, o:o + LANES]
    k_r_rot = y[:, o + LANES:o + 2 * LANES]
    o += 2 * LANES
    q_s = y[:, o:o + SB_WIDTH]
    k_s = y[:, o + SB_WIDTH:o + 2 * SB_WIDTH]
    v_s = y[:, o + 2 * SB_WIDTH:o + 3 * SB_WIDTH]

    ang = pos_ref[...].astype(_F32) * invf_ref[...]
    cos = jnp.cos(ang)
    sin = jnp.sin(ang)

    cq_n = _rms(c_q, g_cq_ref[...]).astype(_BF16)
    q = _dot(cq_n, wq_ref[...])
    q_rot = _dot(cq_n, wqr_ref[...])
    ckv_n = _rms(c_kv, g_ckv_ref[...]).astype(_BF16)
    k = _dot(ckv_n, wk_ref[...])
    vm_ref[...] = _dot(ckv_n, wv_ref[...]).astype(_BF16)
    k_rope = k_r * cos + k_r_rot * sin

    q_scale = (MLA_NOPE_DIM + MLA_ROPE_DIM) ** -0.5 * math.log2(math.e)
    lane = lax.broadcasted_iota(jnp.int32, (1, LANES), 1)
    low = lane < SB_HEAD_DIM
    for h in range(MLA_HEADS):
        sl = slice(h * LANES, (h + 1) * LANES)
        qm_ref[:, sl] = ((q[:, sl] * cos + q_rot[:, sl] * sin) * q_scale).astype(_BF16)
        km_ref[:, sl] = (k[:, sl] + k_rope).astype(_BF16)
    qs_ref[...] = (q_s * (SB_HEAD_DIM ** -0.5)).astype(_BF16)
    for hp in range(HEAD_PAIRS):
        sl = slice(hp * LANES, (hp + 1) * LANES)
        for half, keep in ((0, low), (1, jnp.logical_not(low))):
            dst = slice((2 * hp + half) * LANES, (2 * hp + half + 1) * LANES)
            ks_ref[:, dst] = jnp.where(keep, k_s[:, sl], 0.0).astype(_BF16)
            vs_ref[:, dst] = jnp.where(keep, v_s[:, sl], 0.0).astype(_BF16)


def _proj(pos, x2, g_attn, w1, g_cq, wq, wqr, g_ckv, wk, wv, invf):
    t = x2.shape[0]
    rows = PROJ_ROWS
    wide = MLA_HEADS * LANES

    def full(a):
        return pl.BlockSpec(a.shape, lambda i: (0,) * a.ndim)

    def tok(n):
        return pl.BlockSpec((rows, n), lambda i: (i, 0))

    outs = [jax.ShapeDtypeStruct((t, n), _BF16) for n in (wide, wide, wide, SB_WIDTH, wide, wide)]
    return pl.pallas_call(
        _proj_kernel,
        grid=(t // rows,),
        in_specs=[tok(1), tok(D_MODEL), full(g_attn), full(w1), full(g_cq), full(wq), full(wqr), full(g_ckv),
                  full(wk), full(wv), full(invf)],
        out_specs=[tok(wide), tok(wide), tok(wide), tok(SB_WIDTH), tok(wide), tok(wide)],
        out_shape=outs,
        compiler_params=pltpu.CompilerParams(dimension_semantics=("parallel",), vmem_limit_bytes=VMEM_LIMIT),
        name="proj",
    )(pos, x2, g_attn, w1, g_cq, wq, wqr, g_ckv, wk, wv, invf)


def _mla_kernel(q_ref, k_ref, v_ref, o_ref):
    tq, tk = MLA_Q_TILE, MLA_K_TILE
    qi = pl.program_id(2)
    row = lax.broadcasted_iota(jnp.int32, (tq, tk), 0)
    col = lax.broadcasted_iota(jnp.int32, (tq, tk), 1)

    def step(kt, carry, masked):
        ks = pl.ds(pl.multiple_of(kt * tk, tk), tk)
        new = []
        for h in range(GROUP_HEADS):
            m, l, acc = carry[h]
            sl = slice(h * LANES, (h + 1) * LANES)
            s = _dot_nt(q_ref[0, :, sl], k_ref[0, ks, sl])
            if masked:
                s = jnp.where(kt * tk + col <= qi * tq + row, s, -jnp.inf)
            m_new = jnp.maximum(m, jnp.max(s, axis=-1, keepdims=True))
            alpha = jnp.exp2(m - m_new)
            p = jnp.exp2(s - m_new)
            l = alpha * l + jnp.sum(p, axis=-1, keepdims=True)
            acc = alpha * acc + _dot(p.astype(_BF16), v_ref[0, ks, sl])
            new.append((m_new, l, acc))
        return tuple(new)

    init = tuple((jnp.full((tq, 1), -jnp.inf, _F32), jnp.zeros((tq, 1), _F32), jnp.zeros((tq, LANES), _F32))
                 for _ in range(GROUP_HEADS))
    last = (qi * tq) // tk
    carry = lax.fori_loop(0, last, functools.partial(step, masked=False), init)
    carry = step(last, carry, True)
    for hp in range(GROUP_HEADS // 2):
        (_, l0, a0), (_, l1, a1) = carry[2 * hp], carry[2 * hp + 1]
        o_ref[0, :, hp * LANES:(hp + 1) * LANES] = a0 / l0 + a1 / l1


def _mla(qm, km, vm):
    b, s, _ = qm.shape
    tq = MLA_Q_TILE
    gw = GROUP_HEADS * LANES
    return pl.pallas_call(
        _mla_kernel,
        grid=(b, MLA_HEADS // GROUP_HEADS, s // tq),
        in_specs=[pl.BlockSpec((1, tq, gw), lambda bi, g, qi: (bi, qi, g)),
                  pl.BlockSpec((1, s, gw), lambda bi, g, qi: (bi, 0, g)),
                  pl.BlockSpec((1, s, gw), lambda bi, g, qi: (bi, 0, g))],
        out_specs=pl.BlockSpec((1, tq, gw // 2), lambda bi, g, qi: (bi, qi, g)),
        out_shape=jax.ShapeDtypeStruct((b, s, HEAD_PAIRS * LANES), _F32),
        compiler_params=pltpu.CompilerParams(dimension_semantics=("parallel", "parallel", "arbitrary"),
                                             vmem_limit_bytes=VMEM_LIMIT),
        name="mla",
    )(qm, km, vm)


def _sb_kernel(q_ref, k_ref, v_ref, tri_ref, o_ref):
    tile = SB_TILE
    qi = pl.program_id(2)
    row = lax.broadcasted_iota(jnp.int32, (tile, tile), 0)
    col = lax.broadcasted_iota(jnp.int32, (tile, tile), 1)
    strict = col < row

    def step(kt, rems, accs, diagonal):
        ks = pl.ds(pl.multiple_of(kt * tile, tile), tile)
        tri = tri_ref[...]
        new_rems, new_accs = [], []
        for h in range(GROUP_HEADS):
            sl = slice(h * LANES, (h + 1) * LANES)
            q = q_ref[0, :, (h // 2) * LANES:(h // 2 + 1) * LANES]
            z = _dot_nt(q, k_ref[0, ks, sl])
            sp = jnp.maximum(z, 0.0) + jnp.log1p(jnp.exp(-jnp.abs(z)))
            log_not = -sp
            if diagonal:
                log_not = jnp.where(strict, log_not, 0.0)
            hi = log_not.astype(_BF16)
            r1 = log_not - hi.astype(_F32)
            mid = r1.astype(_BF16)
            lo = (r1 - mid.astype(_F32)).astype(_BF16)
            incl = _dot(hi, tri) + _dot(mid, tri) + _dot(lo, tri)
            log_a = (z - sp) + (rems[h] + incl - log_not)
            p = jnp.exp(log_a)
            if diagonal:
                p = jnp.where(strict, p, 0.0)
            new_accs.append(accs[h] + _dot(p.astype(_BF16), v_ref[0, ks, sl]))
            new_rems.append(rems[h] + incl[:, :1])
        return tuple(new_rems), tuple(new_accs)

    def rem_max(rems):
        return functools.reduce(jnp.maximum, [jnp.max(r) for r in rems])

    zeros = tuple(jnp.zeros((tile, 1), _F32) for _ in range(GROUP_HEADS))
    rems, accs = step(qi, zeros, tuple(jnp.zeros((tile, LANES), _F32) for _ in range(GROUP_HEADS)), True)

    def cond(state):
        kt, worst, _, _ = state
        return jnp.logical_and(kt >= 0, worst > SB_LOG_FLOOR)

    def body(state):
        kt, _, rems, accs = state
        rems, accs = step(kt, rems, accs, False)
        return kt - 1, rem_max(rems), rems, accs

    _, _, _, accs = lax.while_loop(cond, body, (qi - 1, rem_max(rems), rems, accs))
    for hp in range(GROUP_HEADS // 2):
        o_ref[0, :, hp * LANES:(hp + 1) * LANES] = accs[2 * hp] + accs[2 * hp + 1]


def _sb(qs, ks, vs, tri):
    b, s, _ = qs.shape
    tile = SB_TILE
    gw = GROUP_HEADS * LANES
    return pl.pallas_call(
        _sb_kernel,
        grid=(b, SB_HEADS // GROUP_HEADS, s // tile),
        in_specs=[pl.BlockSpec((1, tile, gw // 2), lambda bi, g, qi: (bi, qi, g)),
                  pl.BlockSpec((1, s, gw), lambda bi, g, qi: (bi, 0, g)),
                  pl.BlockSpec((1, s, gw), lambda bi, g, qi: (bi, 0, g)),
                  pl.BlockSpec((tile, tile), lambda bi, g, qi: (0, 0))],
        out_specs=pl.BlockSpec((1, tile, gw // 2), lambda bi, g, qi: (bi, qi, g)),
        out_shape=jax.ShapeDtypeStruct((b, s, HEAD_PAIRS * LANES), _F32),
        compiler_params=pltpu.CompilerParams(dimension_semantics=("parallel", "parallel", "arbitrary"),
                                             vmem_limit_bytes=VMEM_LIMIT),
        name="sb",
    )(qs, ks, vs, tri)


def _post_kernel(om_ref, os_ref, x_ref, g_om_ref, g_os_ref, wo_ref, g_moe_ref, wr_ref, br_ref, ltri_ref,
                 h1_ref, u2_ref, route_ref, counts_ref, seen_ref):
    rows = PROJ_ROWS

    @pl.when(pl.program_id(0) == 0)
    def _():
        seen_ref[...] = jnp.zeros_like(seen_ref)

    mixed = jnp.concatenate([_rms(om_ref[...], g_om_ref[...]), _rms(os_ref[...], g_os_ref[...])], axis=-1)
    h1 = x_ref[...] + _dot(mixed.astype(_BF16), wo_ref[...])
    h1_ref[...] = h1
    u2 = _rms(h1, g_moe_ref[...])
    _store_row_tiles(u2_ref, u2)
    logits = jnp.dot(u2, wr_ref[...], preferred_element_type=_F32, precision=lax.Precision.HIGHEST) + br_ref[...]

    lane = lax.broadcasted_iota(jnp.int32, (rows, N_EXPERTS), 1).astype(_F32)
    work = logits
    ids, tops = [], []
    onehot = jnp.zeros((rows, N_EXPERTS), _F32)
    for _ in range(TOP_K):
        top = jnp.max(work, axis=-1, keepdims=True)
        idx = jnp.min(jnp.where(work == top, lane, float(N_EXPERTS)), axis=-1, keepdims=True)
        hit = lane == idx
        onehot = jnp.where(hit, 1.0, onehot)
        work = jnp.where(hit, -jnp.inf, work)
        ids.append(idx)
        tops.append(top)
    exps = [jnp.exp(tp - tops[0]) for tp in tops]
    denom = exps[0] + exps[1] + exps[2] + exps[3]
    gates = [e / denom for e in exps]

    before = seen_ref[...] + _dot(ltri_ref[...], onehot.astype(_BF16))
    ranks = [jnp.sum(jnp.where(lane == idx, before, 0.0), axis=-1, keepdims=True) for idx in ids]
    seen_ref[...] = seen_ref[...] + jnp.sum(onehot, axis=0, keepdims=True)
    counts_ref[...] = seen_ref[...]

    out_lane = lax.broadcasted_iota(jnp.int32, (rows, LANES), 1)
    route = jnp.zeros((rows, LANES), _F32)
    for j, val in enumerate(ids + gates + ranks):
        route = jnp.where(out_lane == j, val, route)
    route_ref[...] = route


def _post(om, os_, x2, g_om, g_os, wo, g_moe, wr, br, ltri):
    t = x2.shape[0]
    rows = PROJ_ROWS

    def full(a):
        return pl.BlockSpec(a.shape, lambda i: (0,) * a.ndim)

    def tok(n):
        return pl.BlockSpec((rows, n), lambda i: (i, 0))

    return pl.pallas_call(
        _post_kernel,
        grid=(t // rows,),
        in_specs=[tok(om.shape[1]), tok(os_.shape[1]), tok(D_MODEL), full(g_om), full(g_os), full(wo), full(g_moe),
                  full(wr), full(br), full(ltri)],
        out_specs=[tok(D_MODEL), pl.BlockSpec((rows * ROW_TILE, LANES), lambda i: (i, 0)), tok(LANES),
                   pl.BlockSpec((1, N_EXPERTS), lambda i: (0, 0))],
        out_shape=[jax.ShapeDtypeStruct((t, D_MODEL), _F32), jax.ShapeDtypeStruct((t * ROW_TILE, LANES), _F32),
                   jax.ShapeDtypeStruct((t, LANES), _F32), jax.ShapeDtypeStruct((1, N_EXPERTS), _F32)],
        scratch_shapes=[pltpu.VMEM((1, N_EXPERTS), _F32)],
        compiler_params=pltpu.CompilerParams(dimension_semantics=("arbitrary",), vmem_limit_bytes=VMEM_LIMIT),
        name="post",
    )(om, os_, x2, g_om, g_os, wo, g_moe, wr, br, ltri)


def _padded_count(cnt_ref, e):
    shift = EXPERT_ROWS.bit_length() - 1
    return lax.shift_left(lax.shift_right_logical(cnt_ref[e] + (EXPERT_ROWS - 1), shift), shift)


def _layout_kernel(cnt_ref, ids_ref, rank_ref, dest_ref, start_ref, be_ref):
    shift = EXPERT_ROWS.bit_length() - 1
    nb = be_ref.shape[0]

    def place(e, off):
        padded = _padded_count(cnt_ref, e)
        start_ref[e] = off

        def mark(j, c):
            be_ref[lax.shift_right_logical(off, shift) + j] = e
            return c
        lax.fori_loop(0, lax.shift_right_logical(padded, shift), mark, 0)
        return off + padded
    total = lax.fori_loop(0, N_EXPERTS, place, 0)

    def tail(b, c):
        be_ref[b] = N_EXPERTS - 1
        return c
    lax.fori_loop(lax.shift_right_logical(total, shift), nb, tail, 0)

    ids = ids_ref[...]
    dest = rank_ref[...]
    for e in range(N_EXPERTS):
        dest = dest + jnp.where(ids == e, start_ref[e], 0)
    dest_ref[...] = dest


def _layout(counts, ids, rank, n_blocks):
    grid_spec = pltpu.PrefetchScalarGridSpec(
        num_scalar_prefetch=1,
        grid=(1,),
        in_specs=[pl.BlockSpec(ids.shape, lambda i, c: (0, 0)), pl.BlockSpec(rank.shape, lambda i, c: (0, 0))],
        out_specs=[pl.BlockSpec(ids.shape, lambda i, c: (0, 0)), pl.BlockSpec(memory_space=pltpu.SMEM),
                   pl.BlockSpec(memory_space=pltpu.SMEM)],
    )
    return pl.pallas_call(
        _layout_kernel,
        grid_spec=grid_spec,
        out_shape=[jax.ShapeDtypeStruct(ids.shape, jnp.int32), jax.ShapeDtypeStruct((N_EXPERTS,), jnp.int32),
                   jax.ShapeDtypeStruct((n_blocks,), jnp.int32)],
        compiler_params=pltpu.CompilerParams(dimension_semantics=("arbitrary",)),
        name="layout",
    )(counts, ids, rank)


def _tile_rows(row):
    return pl.ds(pl.multiple_of(row * ROW_TILE, ROW_TILE), ROW_TILE)


def _for_each_pair(idx, s, body):
    line_tokens = LANES // TOP_K

    def line(g, c):
        for t0 in range(0, line_tokens, DMA_GROUP):
            vals = [idx[s, g, (t0 + t) * TOP_K + k] for t in range(DMA_GROUP) for k in range(TOP_K)]
            for j, row in enumerate(vals):
                body(g * line_tokens + t0 + j // TOP_K, j % TOP_K, row)
        return c
    lax.fori_loop(0, SUBLANES, line, 0)


def _dispatch_kernel(cnt_ref, start_ref, dest_hbm, u_hbm, xs_hbm, idx, stage, zrow, zblk, sem_d, sem_l, sem_i, sem_z):
    rows = MOE_ROWS
    shift = EXPERT_ROWS.bit_length() - 1
    i = pl.program_id(0)
    nt = pl.num_programs(0)
    nb = xs_hbm.shape[0] // (EXPERT_ROWS * ROW_TILE)
    slot = lax.rem(i, 2)
    stage_slot = lax.rem(i, STAGE_SLOTS)

    def idx_copy(tile, s):
        return pltpu.make_async_copy(dest_hbm.at[pl.ds(pl.multiple_of(tile * SUBLANES, SUBLANES), SUBLANES)],
                                     idx.at[s], sem_i.at[s])

    def load(tile):
        s = lax.rem(tile, STAGE_SLOTS)
        src = pl.ds(pl.multiple_of(tile * (rows * ROW_TILE), rows * ROW_TILE), rows * ROW_TILE)
        return pltpu.make_async_copy(u_hbm.at[src], stage.at[s], sem_l.at[s])

    def wait_rows(s):
        chunk = pl.ds(0, rows * ROW_TILE)
        for _ in range(TOP_K):
            pltpu.make_async_copy(stage.at[s], xs_hbm.at[chunk], sem_d.at[s]).wait()

    def zero_row(r):
        return pltpu.make_async_copy(zrow, xs_hbm.at[_tile_rows(r)], sem_z)

    def zero_block(b):
        dst = pl.ds(pl.multiple_of(b * (EXPERT_ROWS * ROW_TILE), EXPERT_ROWS * ROW_TILE), EXPERT_ROWS * ROW_TILE)
        return pltpu.make_async_copy(zblk, xs_hbm.at[dst], sem_z)

    def pad_rows(e, fn):
        def one(r, c):
            fn(zero_row(start_ref[e] + r))
            return c
        lax.fori_loop(cnt_ref[e], _padded_count(cnt_ref, e), one, 0)

    def tail_blocks(fn):
        used = start_ref[N_EXPERTS - 1] + _padded_count(cnt_ref, N_EXPERTS - 1)

        def one(b, c):
            fn(zero_block(b))
            return c
        lax.fori_loop(lax.shift_right_logical(used, shift), nb, one, 0)

    @pl.when(i == 0)
    def _():
        idx_copy(0, 0).start()
        load(0).start()

        @pl.when(nt > 1)
        def _():
            load(1).start()
        zrow[...] = jnp.zeros_like(zrow)
        zblk[...] = jnp.zeros_like(zblk)
        for fn in (lambda c: c.start(), lambda c: c.wait()):
            def per_expert(e, c, fn=fn):
                pad_rows(e, fn)
                return c
            lax.fori_loop(0, N_EXPERTS, per_expert, 0)
            tail_blocks(fn)

    idx_copy(i, slot).wait()

    @pl.when(i + 1 < nt)
    def _():
        idx_copy(i + 1, 1 - slot).start()

    @pl.when(i >= 2)
    def _():
        wait_rows(lax.rem(i + 2, STAGE_SLOTS))

    @pl.when(i + 2 < nt)
    def _():
        load(i + 2).start()

    load(i).wait()

    def copy_row(t, k, row):
        del k
        pltpu.make_async_copy(stage.at[stage_slot, _tile_rows(t)], xs_hbm.at[_tile_rows(row)],
                              sem_d.at[stage_slot]).start()
    _for_each_pair(idx, slot, copy_row)

    @pl.when(i == nt - 1)
    def _():
        @pl.when(nt > 1)
        def _():
            wait_rows(lax.rem(i + STAGE_SLOTS - 1, STAGE_SLOTS))
        wait_rows(stage_slot)


def _dispatch(counts, start, dest, u2_tiles, n_blocks):
    nt = dest.shape[0] // SUBLANES
    grid_spec = pltpu.PrefetchScalarGridSpec(
        num_scalar_prefetch=2,
        grid=(nt,),
        in_specs=[pl.BlockSpec(memory_space=pl.ANY), pl.BlockSpec(memory_space=pl.ANY)],
        out_specs=pl.BlockSpec(memory_space=pl.ANY),
        scratch_shapes=[pltpu.SMEM((2, SUBLANES, LANES), jnp.int32),
                        pltpu.VMEM((STAGE_SLOTS, MOE_ROWS * ROW_TILE, LANES), _F32),
                        pltpu.VMEM((ROW_TILE, LANES), _F32), pltpu.VMEM((EXPERT_ROWS * ROW_TILE, LANES), _F32),
                        pltpu.SemaphoreType.DMA((STAGE_SLOTS,)), pltpu.SemaphoreType.DMA((STAGE_SLOTS,)),
                        pltpu.SemaphoreType.DMA((2,)), pltpu.SemaphoreType.DMA(())],
    )
    return pl.pallas_call(
        _dispatch_kernel,
        grid_spec=grid_spec,
        out_shape=jax.ShapeDtypeStruct((n_blocks * EXPERT_ROWS * ROW_TILE, LANES), _F32),
        compiler_params=pltpu.CompilerParams(dimension_semantics=("arbitrary",), vmem_limit_bytes=VMEM_LIMIT),
        name="dispatch",
    )(counts, start, dest, u2_tiles)


def _expert_kernel(be_ref, xs_ref, wgu_ref, bgu_ref, wdn_ref, bdn_ref, ys_ref, wgu_bf, wdn_bf):
    rows = EXPERT_ROWS
    i = pl.program_id(0)
    e = be_ref[i]
    prev = be_ref[jnp.maximum(i - 1, 0)]

    @pl.when(jnp.logical_or(i == 0, e != prev))
    def _():
        chunk = 128

        def cast(c, carry):
            rs = pl.ds(pl.multiple_of(c * chunk, chunk), chunk)
            wgu_bf[rs, :] = wgu_ref[0, rs, :].astype(_BF16)
            wdn_bf[rs, :] = wdn_ref[0, rs, :].astype(_BF16)
            return carry
        lax.fori_loop(0, D_MODEL // chunk, cast, 0)

    x = _load_row_tiles(xs_ref, 0, rows).astype(_BF16)
    hh = _dot(x, wgu_bf[...]) + bgu_ref[0]
    glu = jnp.minimum(hh[:, :D_FF], SWIGLU_LIMIT)
    lin = jnp.clip(hh[:, D_FF:], -SWIGLU_LIMIT, SWIGLU_LIMIT)
    act = glu * jax.nn.sigmoid(SWIGLU_ALPHA * glu) * (lin + 1.0)
    _store_row_tiles(ys_ref, _dot(act.astype(_BF16), wdn_bf[...]) + bdn_ref[0])


def _experts(block_e, xs, w_gu, b_gu, w_dn, b_dn):
    nb = block_e.shape[0]
    rows = EXPERT_ROWS
    grid_spec = pltpu.PrefetchScalarGridSpec(
        num_scalar_prefetch=1,
        grid=(nb,),
        in_specs=[pl.BlockSpec((rows * ROW_TILE, LANES), lambda i, be: (i, 0)),
                  pl.BlockSpec((1, D_MODEL, 2 * D_FF), lambda i, be: (be[i], 0, 0)),
                  pl.BlockSpec((1, 1, 2 * D_FF), lambda i, be: (be[i], 0, 0)),
                  pl.BlockSpec((1, D_FF, D_MODEL), lambda i, be: (be[i], 0, 0)),
                  pl.BlockSpec((1, 1, D_MODEL), lambda i, be: (be[i], 0, 0))],
        out_specs=pl.BlockSpec((rows * ROW_TILE, LANES), lambda i, be: (i, 0)),
        scratch_shapes=[pltpu.VMEM((D_MODEL, 2 * D_FF), _BF16), pltpu.VMEM((D_FF, D_MODEL), _BF16)],
    )
    return pl.pallas_call(
        _expert_kernel,
        grid_spec=grid_spec,
        out_shape=jax.ShapeDtypeStruct(xs.shape, _F32),
        compiler_params=pltpu.CompilerParams(dimension_semantics=("arbitrary",), vmem_limit_bytes=VMEM_LIMIT),
        name="experts",
    )(block_e, xs, w_gu, b_gu.reshape(N_EXPERTS, 1, 2 * D_FF), w_dn, b_dn.reshape(N_EXPERTS, 1, D_MODEL))


def _final_kernel(dest_hbm, ys_hbm, h1_ref, route_ref, p_ref, g_ple_ref, wpg_ref, wpp_ref, g_fin_ref,
                  out_ref, ybuf, idx, sem_y, sem_i, *, last_layer):
    rows = MOE_ROWS
    i = pl.program_id(0)
    nt = pl.num_programs(0)
    slot = lax.rem(i, 2)

    def idx_copy(tile, s):
        return pltpu.make_async_copy(dest_hbm.at[pl.ds(pl.multiple_of(tile * SUBLANES, SUBLANES), SUBLANES)],
                                     idx.at[s], sem_i.at[s])

    def gather(s):
        def copy_row(t, k, row):
            pltpu.make_async_copy(ys_hbm.at[_tile_rows(row)], ybuf.at[s, _tile_rows(k * rows + t)],
                                  sem_y.at[s]).start()
        _for_each_pair(idx, s, copy_row)

    @pl.when(i == 0)
    def _():
        idx_copy(0, 0).start()
        idx_copy(0, 0).wait()
        gather(0)

        @pl.when(nt > 1)
        def _():
            idx_copy(1, 1).start()

    @pl.when(i + 1 < nt)
    def _():
        idx_copy(i + 1, 1 - slot).wait()
        gather(1 - slot)

        @pl.when(i + 2 < nt)
        def _():
            idx_copy(i + 2, slot).start()

    pltpu.make_async_copy(ys_hbm.at[pl.ds(0, TOP_K * rows * ROW_TILE)], ybuf.at[slot], sem_y.at[slot]).wait()
    route = route_ref[...]
    y = jnp.zeros((rows, D_MODEL), _F32)
    for k in range(TOP_K):
        y = y + _load_row_tiles(ybuf.at[slot], k * rows, rows) * route[:, TOP_K + k:TOP_K + k + 1]
    h2 = h1_ref[...] + y
    u3 = _rms(h2, g_ple_ref[...]).astype(_BF16)
    gate = jax.nn.sigmoid(_dot(u3, wpg_ref[...]))
    h3 = h2 + gate * _dot(p_ref[...].astype(_BF16), wpp_ref[...])
    out_ref[...] = _rms(h3, g_fin_ref[...]) if last_layer else h3


def _final(dest, ys, h1, route, p2, g_ple, wpg, wpp, g_fin, last_layer):
    t = h1.shape[0]
    rows = MOE_ROWS

    def full(a):
        return pl.BlockSpec(a.shape, lambda i: (0,) * a.ndim)

    def tok(n):
        return pl.BlockSpec((rows, n), lambda i: (i, 0))

    return pl.pallas_call(
        functools.partial(_final_kernel, last_layer=last_layer),
        grid=(t // rows,),
        in_specs=[pl.BlockSpec(memory_space=pl.ANY), pl.BlockSpec(memory_space=pl.ANY), tok(D_MODEL), tok(LANES),
                  tok(PLE_DIM), full(g_ple), full(wpg), full(wpp), full(g_fin)],
        out_specs=tok(D_MODEL),
        out_shape=jax.ShapeDtypeStruct((t, D_MODEL), _F32),
        scratch_shapes=[pltpu.VMEM((2, TOP_K * rows * ROW_TILE, LANES), _F32),
                        pltpu.SMEM((2, SUBLANES, LANES), jnp.int32),
                        pltpu.SemaphoreType.DMA((2,)), pltpu.SemaphoreType.DMA((2,))],
        compiler_params=pltpu.CompilerParams(dimension_semantics=("arbitrary",), vmem_limit_bytes=VMEM_LIMIT),
        name="final",
    )(dest, ys, h1, route, p2, g_ple, wpg, wpp, g_fin)


def _rope_pad(w, rot):
    half = MLA_ROPE_DIM // 2
    body = jnp.concatenate([-w[:, half:], w[:, :half]], axis=1) if rot else w
    z = jnp.zeros((w.shape[0], MLA_NOPE_DIM), w.dtype)
    return jnp.concatenate([z, body, jnp.zeros((w.shape[0], LANES - MLA_NOPE_DIM - MLA_ROPE_DIM), w.dtype)], axis=1)


def _layer_weights(w_in, w_uq, w_ukv):
    o = Q_LORA_RANK + KV_LORA_RANK
    w_kr = w_in[:, o:o + MLA_ROPE_DIM]
    w1 = jnp.concatenate([w_in[:, :o], _rope_pad(w_kr, False), _rope_pad(w_kr, True), w_in[:, o + MLA_ROPE_DIM:]],
                         axis=1).astype(_BF16)
    uq = w_uq.reshape(Q_LORA_RANK, MLA_HEADS, MLA_NOPE_DIM + MLA_ROPE_DIM)
    zq = jnp.zeros((Q_LORA_RANK, MLA_HEADS, LANES - MLA_NOPE_DIM - MLA_ROPE_DIM), w_uq.dtype)
    wq = jnp.concatenate([uq, zq], axis=2).reshape(Q_LORA_RANK, MLA_HEADS * LANES).astype(_BF16)
    rope = uq[:, :, MLA_NOPE_DIM:]
    half = MLA_ROPE_DIM // 2
    rot = jnp.concatenate([jnp.zeros_like(uq[:, :, :MLA_NOPE_DIM]), -rope[:, :, half:], rope[:, :, :half], zq], axis=2)
    wqr = rot.reshape(Q_LORA_RANK, MLA_HEADS * LANES).astype(_BF16)
    ukv = w_ukv.reshape(KV_LORA_RANK, MLA_HEADS, MLA_NOPE_DIM + MLA_V_DIM)
    zk = jnp.zeros((KV_LORA_RANK, MLA_HEADS, LANES - MLA_NOPE_DIM), w_ukv.dtype)
    wk = jnp.concatenate([ukv[:, :, :MLA_NOPE_DIM], zk], axis=2).reshape(KV_LORA_RANK, MLA_HEADS * LANES).astype(_BF16)
    v = ukv[:, :, MLA_NOPE_DIM:].reshape(KV_LORA_RANK, HEAD_PAIRS, 2, MLA_V_DIM)
    zv = jnp.zeros((KV_LORA_RANK, HEAD_PAIRS, MLA_V_DIM), w_ukv.dtype)
    wv = jnp.stack([jnp.concatenate([v[:, :, 0], zv], axis=2), jnp.concatenate([zv, v[:, :, 1]], axis=2)], axis=2)
    wv = wv.reshape(KV_LORA_RANK, MLA_HEADS * LANES).astype(_BF16)
    return w1, wq, wqr, wk, wv


def kernel(x, p, positions, w_in, g_attn, g_cq, w_uq, g_ckv, w_ukv, g_out_mla, g_out_sb, w_o, g_moe, w_router,
           b_router, w_gu, b_gu, w_dn, b_dn, g_ple, w_ple_gate, w_ple_proj, g_final):
    b, s, d = x.shape
    t = b * s
    depth = w_in.shape[0]
    assert d == D_MODEL and ROW_TILE == SUBLANES and s % MLA_K_TILE == 0 and t % PROJ_ROWS == 0
    assert t % MOE_ROWS == 0 and (LANES // TOP_K) % DMA_GROUP == 0 and (t * TOP_K) % EXPERT_ROWS == 0
    assert EXPERT_ROWS & (EXPERT_ROWS - 1) == 0

    freq = ROPE_THETA ** (-jnp.arange(0, MLA_ROPE_DIM, 2, dtype=_F32) / MLA_ROPE_DIM)
    invf = jnp.concatenate([jnp.zeros((MLA_NOPE_DIM,), _F32), freq, freq,
                            jnp.zeros((LANES - MLA_NOPE_DIM - MLA_ROPE_DIM,), _F32)]).reshape(1, LANES)
    pos = positions.reshape(t, 1)
    idx = jnp.arange(SB_TILE)
    tri = (idx[:, None] >= idx[None, :]).astype(_BF16)
    idx = jnp.arange(PROJ_ROWS)
    ltri = (idx[None, :] < idx[:, None]).astype(_BF16)
    n_blocks = t * TOP_K // EXPERT_ROWS + N_EXPERTS

    h = x.reshape(t, d)
    for i in range(depth):
        w1, wq, wqr, wk, wv = _layer_weights(w_in[i], w_uq[i], w_ukv[i])
        qm, km, vm, qs, ks, vs = _proj(pos, h, g_attn[i].reshape(1, d), w1, g_cq[i].reshape(1, -1), wq, wqr,
                                       g_ckv[i].reshape(1, -1), wk, wv, invf)

        def seq(a):
            return a.reshape(b, s, a.shape[1])

        om = _mla(seq(qm), seq(km), seq(vm)).reshape(t, -1)
        os_ = _sb(seq(qs), seq(ks), seq(vs), tri).reshape(t, -1)
        h1, u2_tiles, route, counts = _post(om, os_, h, g_out_mla[i].reshape(1, -1), g_out_sb[i].reshape(1, -1),
                                            w_o[i].astype(_BF16), g_moe[i].reshape(1, d), w_router[i],
                                            b_router[i].reshape(1, -1), ltri)
        ids = route[:, :TOP_K].astype(jnp.int32).reshape(-1, LANES)
        rank = route[:, 2 * TOP_K:3 * TOP_K].astype(jnp.int32).reshape(-1, LANES)
        counts = counts.reshape(-1).astype(jnp.int32)
        dest, start, block_e = _layout(counts, ids, rank, n_blocks)
        xs = _dispatch(counts, start, dest, u2_tiles, n_blocks)
        ys = _experts(block_e, xs, w_gu[i], b_gu[i], w_dn[i], b_dn[i])
        h = _final(dest, ys, h1, route, p[i].reshape(t, -1), g_ple[i].reshape(1, d), w_ple_gate[i].astype(_BF16),
                   w_ple_proj[i].astype(_BF16), g_final.reshape(1, d), i == depth - 1)
    return h.reshape(b, s, d)
```

```python
import functools
import math

import jax
import jax.numpy as jnp
from jax import lax
from jax.experimental import pallas as pl
from jax.experimental.pallas import tpu as pltpu

D_MODEL = 1024
PLE_DIM = 256
MLA_HEADS = 8
MLA_NOPE_DIM = 64
MLA_ROPE_DIM = 32
MLA_V_DIM = 64
Q_LORA_RANK = 384
KV_LORA_RANK = 256
SB_HEADS = 8
SB_HEAD_DIM = 64
SB_WIDTH = SB_HEADS * SB_HEAD_DIM
ROPE_THETA = 10000.0
N_EXPERTS = 32
TOP_K = 4
D_FF = 1024
SWIGLU_LIMIT = 7.0
SWIGLU_ALPHA = 1.702
RMS_EPS = 1e-6

LANES = 128
SUBLANES = 8
ROW_TILE = D_MODEL // LANES
HEAD_PAIRS = MLA_HEADS // 2
PROJ_ROWS = 256
GROUP_HEADS = 4
MLA_Q_TILE = 256
MLA_K_TILE = 512
SB_TILE = 256
EXPERT_ROWS = 256
MOE_ROWS = SUBLANES * LANES // TOP_K
DMA_GROUP = 8
STAGE_SLOTS = 4
DMA_THREADS = 2
VMEM_LIMIT = 56 * 1024 * 1024
SB_LOG_FLOOR = -105.0

_F32 = jnp.float32
_BF16 = jnp.bfloat16


def _rms(x, g):
    return x * lax.rsqrt(jnp.mean(x * x, axis=-1, keepdims=True) + RMS_EPS) * g


def _dot(a, b):
    return jnp.dot(a, b, preferred_element_type=_F32)


def _dot_nt(a, b):
    return lax.dot_general(a, b, (((1,), (1,)), ((), ())), preferred_element_type=_F32)


def _store_row_tiles(ref, value):
    rows = value.shape[0]
    for g in range(ROW_TILE):
        ref[pl.ds(g, rows, stride=ROW_TILE), :] = value[:, g * LANES:(g + 1) * LANES]


def _load_row_tiles(ref, first_row, rows):
    return jnp.concatenate([ref[pl.ds(first_row * ROW_TILE + g, rows, stride=ROW_TILE), :] for g in range(ROW_TILE)],
                           axis=-1)


def _proj_kernel(pos_ref, x_ref, g_attn_ref, w1_ref, g_cq_ref, wq_ref, wqr_ref, g_ckv_ref, wk_ref, wv_ref,
                 invf_ref, qm_ref, km_ref, vm_ref, qs_ref, ks_ref, vs_ref):
    u = _rms(x_ref[...], g_attn_ref[...]).astype(_BF16)
    y = _dot(u, w1_ref[...])
    c_q = y[:, :Q_LORA_RANK]
    c_kv = y[:, Q_LORA_RANK:Q_LORA_RANK + KV_LORA_RANK]
    o = Q_LORA_RANK + KV_LORA_RANK
    k_r = y[:, o:o + LANES]
    k_r_rot = y[:, o + LANES:o + 2 * LANES]
    o += 2 * LANES
    q_s = y[:, o:o + SB_WIDTH]
    k_s = y[:, o + SB_WIDTH:o + 2 * SB_WIDTH]
    v_s = y[:, o + 2 * SB_WIDTH:o + 3 * SB_WIDTH]

    ang = pos_ref[...].astype(_F32) * invf_ref[...]
    cos = jnp.cos(ang)
    sin = jnp.sin(ang)

    cq_n = _rms(c_q, g_cq_ref[...]).astype(_BF16)
    q = _dot(cq_n, wq_ref[...])
    q_rot = _dot(cq_n, wqr_ref[...])
    ckv_n = _rms(c_kv, g_ckv_ref[...]).astype(_BF16)
    k = _dot(ckv_n, wk_ref[...])
    vm_ref[...] = _dot(ckv_n, wv_ref[...]).astype(_BF16)
    k_rope = k_r * cos + k_r_rot * sin

    q_scale = (MLA_NOPE_DIM + MLA_ROPE_DIM) ** -0.5 * math.log2(math.e)
    lane = lax.broadcasted_iota(jnp.int32, (1, LANES), 1)
    low = lane < SB_HEAD_DIM
    for h in range(MLA_HEADS):
        sl = slice(h * LANES, (h + 1) * LANES)
        qm_ref[:, sl] = ((q[:, sl] * cos + q_rot[:, sl] * sin) * q_scale).astype(_BF16)
        km_ref[:, sl] = (k[:, sl] + k_rope).astype(_BF16)
    qs_ref[...] = (q_s * (SB_HEAD_DIM ** -0.5)).astype(_BF16)
    for hp in range(HEAD_PAIRS):
        sl = slice(hp * LANES, (hp + 1) * LANES)
        for half, keep in ((0, low), (1, jnp.logical_not(low))):
            dst = slice((2 * hp + half) * LANES, (2 * hp + half + 1) * LANES)
            ks_ref[:, dst] = jnp.where(keep, k_s[:, sl], 0.0).astype(_BF16)
            vs_ref[:, dst] = jnp.where(keep, v_s[:, sl], 0.0).astype(_BF16)


def _proj(pos, x2, g_attn, w1, g_cq, wq, wqr, g_ckv, wk, wv, invf):
    t = x2.shape[0]
    rows = PROJ_ROWS
    wide = MLA_HEADS * LANES

    def full(a):
        return pl.BlockSpec(a.shape, lambda i: (0,) * a.ndim)

    def tok(n):
        return pl.BlockSpec((rows, n), lambda i: (i, 0))

    outs = [jax.ShapeDtypeStruct((t, n), _BF16) for n in (wide, wide, wide, SB_WIDTH, wide, wide)]
    return pl.pallas_call(
        _proj_kernel,
        grid=(t // rows,),
        in_specs=[tok(1), tok(D_MODEL), full(g_attn), full(w1), full(g_cq), full(wq), full(wqr), full(g_ckv),
                  full(wk), full(wv), full(invf)],
        out_specs=[tok(wide), tok(wide), tok(wide), tok(SB_WIDTH), tok(wide), tok(wide)],
        out_shape=outs,
        compiler_params=pltpu.CompilerParams(dimension_semantics=("parallel",), vmem_limit_bytes=VMEM_LIMIT),
        name="proj",
    )(pos, x2, g_attn, w1, g_cq, wq, wqr, g_ckv, wk, wv, invf)


def _mla_kernel(q_ref, k_ref, v_ref, o_ref):
    tq, tk = MLA_Q_TILE, MLA_K_TILE
    qi = pl.program_id(2)
    row = lax.broadcasted_iota(jnp.int32, (tq, tk), 0)
    col = lax.broadcasted_iota(jnp.int32, (tq, tk), 1)
    heads = [slice(h * LANES, (h + 1) * LANES) for h in range(GROUP_HEADS)]

    def key_rows(kt):
        return pl.ds(pl.multiple_of(kt * tk, tk), tk)

    def scores(kt):
        return tuple(_dot_nt(q_ref[0, :, sl], k_ref[0, key_rows(kt), sl]) for sl in heads)

    def absorb(kt, s_all, carry):
        new = []
        for sl, s, (m, l, acc) in zip(heads, s_all, carry):
            m_new = jnp.maximum(m, jnp.max(s, axis=-1, keepdims=True))
            alpha = jnp.exp2(m - m_new)
            p = jnp.exp2(s - m_new)
            l = alpha * l + jnp.sum(p, axis=-1, keepdims=True)
            acc = alpha * acc + _dot(p.astype(_BF16), v_ref[0, key_rows(kt), sl])
            new.append((m_new, l, acc))
        return tuple(new)

    last = (qi * tq) // tk
    causal = last * tk + col <= qi * tq + row
    s_diag = tuple(jnp.where(causal, s, -jnp.inf) for s in scores(last))
    init = tuple((jnp.full((tq, 1), -jnp.inf, _F32), jnp.zeros((tq, 1), _F32), jnp.zeros((tq, LANES), _F32))
                 for _ in range(GROUP_HEADS))

    def body(j, state):
        kt, s_all, carry = state
        s_next = scores(j)
        return j, s_next, absorb(kt, s_all, carry)

    kt, s_all, carry = lax.fori_loop(0, last, body, (last, s_diag, init))
    carry = absorb(kt, s_all, carry)
    for hp in range(GROUP_HEADS // 2):
        (_, l0, a0), (_, l1, a1) = carry[2 * hp], carry[2 * hp + 1]
        o_ref[0, :, hp * LANES:(hp + 1) * LANES] = a0 / l0 + a1 / l1


def _mla(qm, km, vm):
    b, s, _ = qm.shape
    tq = MLA_Q_TILE
    gw = GROUP_HEADS * LANES
    return pl.pallas_call(
        _mla_kernel,
        grid=(b, MLA_HEADS // GROUP_HEADS, s // tq),
        in_specs=[pl.BlockSpec((1, tq, gw), lambda bi, g, qi: (bi, qi, g)),
                  pl.BlockSpec((1, s, gw), lambda bi, g, qi: (bi, 0, g)),
                  pl.BlockSpec((1, s, gw), lambda bi, g, qi: (bi, 0, g))],
        out_specs=pl.BlockSpec((1, tq, gw // 2), lambda bi, g, qi: (bi, qi, g)),
        out_shape=jax.ShapeDtypeStruct((b, s, HEAD_PAIRS * LANES), _F32),
        compiler_params=pltpu.CompilerParams(dimension_semantics=("parallel", "parallel", "arbitrary"),
                                             vmem_limit_bytes=VMEM_LIMIT),
        name="mla",
    )(qm, km, vm)


def _sb_kernel(q_ref, k_ref, v_ref, tri_ref, o_ref):
    tile = SB_TILE
    qi = pl.program_id(2)
    row = lax.broadcasted_iota(jnp.int32, (tile, tile), 0)
    col = lax.broadcasted_iota(jnp.int32, (tile, tile), 1)
    strict = col < row
    heads = [slice(h * LANES, (h + 1) * LANES) for h in range(GROUP_HEADS)]

    def key_rows(kt):
        return pl.ds(pl.multiple_of(kt * tile, tile), tile)

    def tile_terms(kt, diagonal):
        tri = tri_ref[...]
        out = []
        for h, sl in enumerate(heads):
            q = q_ref[0, :, (h // 2) * LANES:(h // 2 + 1) * LANES]
            z = _dot_nt(q, k_ref[0, key_rows(kt), sl])
            sp = jnp.maximum(z, 0.0) + jnp.log(1.0 + jnp.exp(-jnp.abs(z)))
            log_not = -sp
            if diagonal:
                log_not = jnp.where(strict, log_not, 0.0)
            hi = log_not.astype(_BF16)
            lo = (log_not - hi.astype(_F32)).astype(_BF16)
            incl = _dot(hi, tri) + _dot(lo, tri)
            out.append((z - sp, log_not, incl))
        return out

    def absorb(kt, terms, rems, accs, diagonal, live=None):
        new_rems, new_accs = [], []
        for sl, (log_sig, log_not, incl), rem, acc in zip(heads, terms, rems, accs):
            p = jnp.exp(log_sig + (rem + incl - log_not))
            if diagonal:
                p = jnp.where(strict, p, 0.0)
            step = incl[:, :1]
            if live is not None:
                p = jnp.where(live, p, 0.0)
                step = jnp.where(live, step, 0.0)
            new_accs.append(acc + _dot(p.astype(_BF16), v_ref[0, key_rows(kt), sl]))
            new_rems.append(rem + step)
        return tuple(new_rems), tuple(new_accs)

    def rem_max(rems):
        return functools.reduce(jnp.maximum, [jnp.max(r) for r in rems])

    left = jnp.maximum(qi - 1, 0)
    terms_diag = tile_terms(qi, True)
    terms_left = tile_terms(left, False)
    rems = tuple(jnp.zeros((tile, 1), _F32) for _ in range(GROUP_HEADS))
    accs = tuple(jnp.zeros((tile, LANES), _F32) for _ in range(GROUP_HEADS))
    rems, accs = absorb(qi, terms_diag, rems, accs, True)
    rems, accs = absorb(left, terms_left, rems, accs, False, live=qi > 0)

    def cond(state):
        kt, worst, _, _ = state
        return jnp.logical_and(kt >= 0, worst > SB_LOG_FLOOR)

    def body(state):
        kt, _, rems, accs = state
        rems, accs = absorb(kt, tile_terms(kt, False), rems, accs, False)
        return kt - 1, rem_max(rems), rems, accs

    _, _, _, accs = lax.while_loop(cond, body, (qi - 2, rem_max(rems), rems, accs))
    for hp in range(GROUP_HEADS // 2):
        o_ref[0, :, hp * LANES:(hp + 1) * LANES] = accs[2 * hp] + accs[2 * hp + 1]


def _sb(qs, ks, vs, tri):
    b, s, _ = qs.shape
    tile = SB_TILE
    gw = GROUP_HEADS * LANES
    return pl.pallas_call(
        _sb_kernel,
        grid=(b, SB_HEADS // GROUP_HEADS, s // tile),
        in_specs=[pl.BlockSpec((1, tile, gw // 2), lambda bi, g, qi: (bi, qi, g)),
                  pl.BlockSpec((1, s, gw), lambda bi, g, qi: (bi, 0, g)),
                  pl.BlockSpec((1, s, gw), lambda bi, g, qi: (bi, 0, g)),
                  pl.BlockSpec((tile, tile), lambda bi, g, qi: (0, 0))],
        out_specs=pl.BlockSpec((1, tile, gw // 2), lambda bi, g, qi: (bi, qi, g)),
        out_shape=jax.ShapeDtypeStruct((b, s, HEAD_PAIRS * LANES), _F32),
        compiler_params=pltpu.CompilerParams(dimension_semantics=("parallel", "parallel", "arbitrary"),
                                             vmem_limit_bytes=VMEM_LIMIT),
        name="sb",
    )(qs, ks, vs, tri)


def _post_kernel(om_ref, os_ref, x_ref, g_om_ref, g_os_ref, wo_ref, g_moe_ref, wr_ref, br_ref, ltri_ref,
                 h1_ref, u2_ref, route_ref, counts_ref, seen_ref):
    rows = PROJ_ROWS

    @pl.when(pl.program_id(0) == 0)
    def _():
        seen_ref[...] = jnp.zeros_like(seen_ref)

    mixed = jnp.concatenate([_rms(om_ref[...], g_om_ref[...]), _rms(os_ref[...], g_os_ref[...])], axis=-1)
    h1 = x_ref[...] + _dot(mixed.astype(_BF16), wo_ref[...])
    h1_ref[...] = h1
    u2 = _rms(h1, g_moe_ref[...])
    _store_row_tiles(u2_ref, u2)
    logits = jnp.dot(u2, wr_ref[...], preferred_element_type=_F32, precision=lax.Precision.HIGHEST) + br_ref[...]

    lane = lax.broadcasted_iota(jnp.int32, (rows, N_EXPERTS), 1).astype(_F32)
    work = logits
    ids, tops = [], []
    onehot = jnp.zeros((rows, N_EXPERTS), _F32)
    for _ in range(TOP_K):
        top = jnp.max(work, axis=-1, keepdims=True)
        idx = jnp.min(jnp.where(work == top, lane, float(N_EXPERTS)), axis=-1, keepdims=True)
        hit = lane == idx
        onehot = jnp.where(hit, 1.0, onehot)
        work = jnp.where(hit, -jnp.inf, work)
        ids.append(idx)
        tops.append(top)
    exps = [jnp.exp(tp - tops[0]) for tp in tops]
    denom = exps[0] + exps[1] + exps[2] + exps[3]
    gates = [e / denom for e in exps]

    before = seen_ref[...] + _dot(ltri_ref[...], onehot.astype(_BF16))
    ranks = [jnp.sum(jnp.where(lane == idx, before, 0.0), axis=-1, keepdims=True) for idx in ids]
    seen_ref[...] = seen_ref[...] + jnp.sum(onehot, axis=0, keepdims=True)
    counts_ref[...] = seen_ref[...]

    out_lane = lax.broadcasted_iota(jnp.int32, (rows, LANES), 1)
    route = jnp.zeros((rows, LANES), _F32)
    for j, val in enumerate(ids + gates + ranks):
        route = jnp.where(out_lane == j, val, route)
    route_ref[...] = route


def _post(om, os_, x2, g_om, g_os, wo, g_moe, wr, br, ltri):
    t = x2.shape[0]
    rows = PROJ_ROWS

    def full(a):
        return pl.BlockSpec(a.shape, lambda i: (0,) * a.ndim)

    def tok(n):
        return pl.BlockSpec((rows, n), lambda i: (i, 0))

    return pl.pallas_call(
        _post_kernel,
        grid=(t // rows,),
        in_specs=[tok(om.shape[1]), tok(os_.shape[1]), tok(D_MODEL), full(g_om), full(g_os), full(wo), full(g_moe),
                  full(wr), full(br), full(ltri)],
        out_specs=[tok(D_MODEL), pl.BlockSpec((rows * ROW_TILE, LANES), lambda i: (i, 0)), tok(LANES),
                   pl.BlockSpec((1, N_EXPERTS), lambda i: (0, 0))],
        out_shape=[jax.ShapeDtypeStruct((t, D_MODEL), _F32), jax.ShapeDtypeStruct((t * ROW_TILE, LANES), _F32),
                   jax.ShapeDtypeStruct((t, LANES), _F32), jax.ShapeDtypeStruct((1, N_EXPERTS), _F32)],
        scratch_shapes=[pltpu.VMEM((1, N_EXPERTS), _F32)],
        compiler_params=pltpu.CompilerParams(dimension_semantics=("arbitrary",), vmem_limit_bytes=VMEM_LIMIT),
        name="post",
    )(om, os_, x2, g_om, g_os, wo, g_moe, wr, br, ltri)


def _padded_count(cnt_ref, e):
    shift = EXPERT_ROWS.bit_length() - 1
    return lax.shift_left(lax.shift_right_logical(cnt_ref[e] + (EXPERT_ROWS - 1), shift), shift)


def _layout_kernel(cnt_ref, ids_ref, rank_ref, dest_ref, start_ref, be_ref):
    shift = EXPERT_ROWS.bit_length() - 1
    nb = be_ref.shape[0]

    def place(e, off):
        padded = _padded_count(cnt_ref, e)
        start_ref[e] = off

        def mark(j, c):
            be_ref[lax.shift_right_logical(off, shift) + j] = e
            return c
        lax.fori_loop(0, lax.shift_right_logical(padded, shift), mark, 0)
        return off + padded
    total = lax.fori_loop(0, N_EXPERTS, place, 0)

    def tail(b, c):
        be_ref[b] = N_EXPERTS - 1
        return c
    lax.fori_loop(lax.shift_right_logical(total, shift), nb, tail, 0)

    ids = ids_ref[...]
    dest = rank_ref[...]
    for e in range(N_EXPERTS):
        dest = dest + jnp.where(ids == e, start_ref[e], 0)
    dest_ref[...] = dest


def _layout(counts, ids, rank, n_blocks):
    grid_spec = pltpu.PrefetchScalarGridSpec(
        num_scalar_prefetch=1,
        grid=(1,),
        in_specs=[pl.BlockSpec(ids.shape, lambda i, c: (0, 0)), pl.BlockSpec(rank.shape, lambda i, c: (0, 0))],
        out_specs=[pl.BlockSpec(ids.shape, lambda i, c: (0, 0)), pl.BlockSpec(memory_space=pltpu.SMEM),
                   pl.BlockSpec(memory_space=pltpu.SMEM)],
    )
    return pl.pallas_call(
        _layout_kernel,
        grid_spec=grid_spec,
        out_shape=[jax.ShapeDtypeStruct(ids.shape, jnp.int32), jax.ShapeDtypeStruct((N_EXPERTS,), jnp.int32),
                   jax.ShapeDtypeStruct((n_blocks,), jnp.int32)],
        compiler_params=pltpu.CompilerParams(dimension_semantics=("arbitrary",)),
        name="layout",
    )(counts, ids, rank)


def _tile_rows(row):
    return pl.ds(pl.multiple_of(row * ROW_TILE, ROW_TILE), ROW_TILE)


def _for_each_pair(idx, s, body):
    line_tokens = LANES // TOP_K

    def line(g, c):
        for t0 in range(0, line_tokens, DMA_GROUP):
            vals = [idx[s, g, (t0 + t) * TOP_K + k] for t in range(DMA_GROUP) for k in range(TOP_K)]
            for j, row in enumerate(vals):
                body(g * line_tokens + t0 + j // TOP_K, j % TOP_K, row, j % DMA_THREADS)
        return c
    lax.fori_loop(0, SUBLANES, line, 0)


def _dispatch_kernel(cnt_ref, start_ref, dest_hbm, u_hbm, xs_hbm, idx, stage, zrow, zblk, sem_d, sem_l, sem_i, sem_z):
    rows = MOE_ROWS
    shift = EXPERT_ROWS.bit_length() - 1
    i = pl.program_id(0)
    nt = pl.num_programs(0)
    nb = xs_hbm.shape[0] // (EXPERT_ROWS * ROW_TILE)
    slot = lax.rem(i, 2)
    stage_slot = lax.rem(i, STAGE_SLOTS)

    def idx_copy(tile, s):
        return pltpu.make_async_copy(dest_hbm.at[pl.ds(pl.multiple_of(tile * SUBLANES, SUBLANES), SUBLANES)],
                                     idx.at[s], sem_i.at[s])

    def load(tile):
        s = lax.rem(tile, STAGE_SLOTS)
        src = pl.ds(pl.multiple_of(tile * (rows * ROW_TILE), rows * ROW_TILE), rows * ROW_TILE)
        return pltpu.make_async_copy(u_hbm.at[src], stage.at[s], sem_l.at[s])

    def wait_rows(s):
        chunk = pl.ds(0, rows * ROW_TILE)
        for _ in range(TOP_K):
            pltpu.make_async_copy(stage.at[s], xs_hbm.at[chunk], sem_d.at[s]).wait()

    def zero_row(r):
        return pltpu.make_async_copy(zrow, xs_hbm.at[_tile_rows(r)], sem_z)

    def zero_block(b):
        dst = pl.ds(pl.multiple_of(b * (EXPERT_ROWS * ROW_TILE), EXPERT_ROWS * ROW_TILE), EXPERT_ROWS * ROW_TILE)
        return pltpu.make_async_copy(zblk, xs_hbm.at[dst], sem_z)

    def pad_rows(e, fn):
        def one(r, c):
            fn(zero_row(start_ref[e] + r))
            return c
        lax.fori_loop(cnt_ref[e], _padded_count(cnt_ref, e), one, 0)

    def tail_blocks(fn):
        used = start_ref[N_EXPERTS - 1] + _padded_count(cnt_ref, N_EXPERTS - 1)

        def one(b, c):
            fn(zero_block(b))
            return c
        lax.fori_loop(lax.shift_right_logical(used, shift), nb, one, 0)

    @pl.when(i == 0)
    def _():
        idx_copy(0, 0).start()
        load(0).start()

        @pl.when(nt > 1)
        def _():
            load(1).start()
        zrow[...] = jnp.zeros_like(zrow)
        zblk[...] = jnp.zeros_like(zblk)
        for fn in (lambda c: c.start(), lambda c: c.wait()):
            def per_expert(e, c, fn=fn):
                pad_rows(e, fn)
                return c
            lax.fori_loop(0, N_EXPERTS, per_expert, 0)
            tail_blocks(fn)

    idx_copy(i, slot).wait()

    @pl.when(i + 1 < nt)
    def _():
        idx_copy(i + 1, 1 - slot).start()

    @pl.when(i >= 2)
    def _():
        wait_rows(lax.rem(i + 2, STAGE_SLOTS))

    @pl.when(i + 2 < nt)
    def _():
        load(i + 2).start()

    load(i).wait()

    def copy_row(t, k, row, thread):
        del k
        pltpu.make_async_copy(stage.at[stage_slot, _tile_rows(t)], xs_hbm.at[_tile_rows(row)],
                              sem_d.at[stage_slot]).start(priority=thread)
    _for_each_pair(idx, slot, copy_row)

    @pl.when(i == nt - 1)
    def _():
        @pl.when(nt > 1)
        def _():
            wait_rows(lax.rem(i + STAGE_SLOTS - 1, STAGE_SLOTS))
        wait_rows(stage_slot)


def _dispatch(counts, start, dest, u2_tiles, n_blocks):
    nt = dest.shape[0] // SUBLANES
    grid_spec = pltpu.PrefetchScalarGridSpec(
        num_scalar_prefetch=2,
        grid=(nt,),
        in_specs=[pl.BlockSpec(memory_space=pl.ANY), pl.BlockSpec(memory_space=pl.ANY)],
        out_specs=pl.BlockSpec(memory_space=pl.ANY),
        scratch_shapes=[pltpu.SMEM((2, SUBLANES, LANES), jnp.int32),
                        pltpu.VMEM((STAGE_SLOTS, MOE_ROWS * ROW_TILE, LANES), _F32),
                        pltpu.VMEM((ROW_TILE, LANES), _F32), pltpu.VMEM((EXPERT_ROWS * ROW_TILE, LANES), _F32),
                        pltpu.SemaphoreType.DMA((STAGE_SLOTS,)), pltpu.SemaphoreType.DMA((STAGE_SLOTS,)),
                        pltpu.SemaphoreType.DMA((2,)), pltpu.SemaphoreType.DMA(())],
    )
    return pl.pallas_call(
        _dispatch_kernel,
        grid_spec=grid_spec,
        out_shape=jax.ShapeDtypeStruct((n_blocks * EXPERT_ROWS * ROW_TILE, LANES), _F32),
        compiler_params=pltpu.CompilerParams(dimension_semantics=("arbitrary",), vmem_limit_bytes=VMEM_LIMIT),
        name="dispatch",
    )(counts, start, dest, u2_tiles)


def _expert_kernel(be_ref, xs_ref, wgu_ref, bgu_ref, wdn_ref, bdn_ref, ys_ref, wgu_bf, wdn_bf):
    rows = EXPERT_ROWS
    i = pl.program_id(0)
    e = be_ref[i]
    prev = be_ref[jnp.maximum(i - 1, 0)]

    @pl.when(jnp.logical_or(i == 0, e != prev))
    def _():
        chunk = 128

        def cast(c, carry):
            rs = pl.ds(pl.multiple_of(c * chunk, chunk), chunk)
            wgu_bf[rs, :] = wgu_ref[0, rs, :].astype(_BF16)
            wdn_bf[rs, :] = wdn_ref[0, rs, :].astype(_BF16)
            return carry
        lax.fori_loop(0, D_MODEL // chunk, cast, 0)

    x = _load_row_tiles(xs_ref, 0, rows).astype(_BF16)
    hh = _dot(x, wgu_bf[...]) + bgu_ref[0]
    glu = jnp.minimum(hh[:, :D_FF], SWIGLU_LIMIT)
    lin = jnp.clip(hh[:, D_FF:], -SWIGLU_LIMIT, SWIGLU_LIMIT)
    act = glu * jax.nn.sigmoid(SWIGLU_ALPHA * glu) * (lin + 1.0)
    _store_row_tiles(ys_ref, _dot(act.astype(_BF16), wdn_bf[...]) + bdn_ref[0])


def _experts(block_e, xs, w_gu, b_gu, w_dn, b_dn):
    nb = block_e.shape[0]
    rows = EXPERT_ROWS
    grid_spec = pltpu.PrefetchScalarGridSpec(
        num_scalar_prefetch=1,
        grid=(nb,),
        in_specs=[pl.BlockSpec((rows * ROW_TILE, LANES), lambda i, be: (i, 0)),
                  pl.BlockSpec((1, D_MODEL, 2 * D_FF), lambda i, be: (be[i], 0, 0)),
                  pl.BlockSpec((1, 1, 2 * D_FF), lambda i, be: (be[i], 0, 0)),
                  pl.BlockSpec((1, D_FF, D_MODEL), lambda i, be: (be[i], 0, 0)),
                  pl.BlockSpec((1, 1, D_MODEL), lambda i, be: (be[i], 0, 0))],
        out_specs=pl.BlockSpec((rows * ROW_TILE, LANES), lambda i, be: (i, 0)),
        scratch_shapes=[pltpu.VMEM((D_MODEL, 2 * D_FF), _BF16), pltpu.VMEM((D_FF, D_MODEL), _BF16)],
    )
    return pl.pallas_call(
        _expert_kernel,
        grid_spec=grid_spec,
        out_shape=jax.ShapeDtypeStruct(xs.shape, _F32),
        compiler_params=pltpu.CompilerParams(dimension_semantics=("arbitrary",), vmem_limit_bytes=VMEM_LIMIT),
        name="experts",
    )(block_e, xs, w_gu, b_gu.reshape(N_EXPERTS, 1, 2 * D_FF), w_dn, b_dn.reshape(N_EXPERTS, 1, D_MODEL))


def _final_kernel(dest_hbm, ys_hbm, h1_ref, route_ref, p_ref, g_ple_ref, wpg_ref, wpp_ref, g_fin_ref,
                  out_ref, ybuf, idx, sem_y, sem_i, *, last_layer):
    rows = MOE_ROWS
    i = pl.program_id(0)
    nt = pl.num_programs(0)
    slot = lax.rem(i, 2)

    def idx_copy(tile, s):
        return pltpu.make_async_copy(dest_hbm.at[pl.ds(pl.multiple_of(tile * SUBLANES, SUBLANES), SUBLANES)],
                                     idx.at[s], sem_i.at[s])

    def gather(s):
        def copy_row(t, k, row, thread):
            pltpu.make_async_copy(ys_hbm.at[_tile_rows(row)], ybuf.at[s, _tile_rows(k * rows + t)],
                                  sem_y.at[s]).start(priority=thread)
        _for_each_pair(idx, s, copy_row)

    @pl.when(i == 0)
    def _():
        idx_copy(0, 0).start()
        idx_copy(0, 0).wait()
        gather(0)

        @pl.when(nt > 1)
        def _():
            idx_copy(1, 1).start()

    @pl.when(i + 1 < nt)
    def _():
        idx_copy(i + 1, 1 - slot).wait()
        gather(1 - slot)

        @pl.when(i + 2 < nt)
        def _():
            idx_copy(i + 2, slot).start()

    pltpu.make_async_copy(ys_hbm.at[pl.ds(0, TOP_K * rows * ROW_TILE)], ybuf.at[slot], sem_y.at[slot]).wait()
    route = route_ref[...]
    y = jnp.zeros((rows, D_MODEL), _F32)
    for k in range(TOP_K):
        y = y + _load_row_tiles(ybuf.at[slot], k * rows, rows) * route[:, TOP_K + k:TOP_K + k + 1]
    h2 = h1_ref[...] + y
    u3 = _rms(h2, g_ple_ref[...]).astype(_BF16)
    gate = jax.nn.sigmoid(_dot(u3, wpg_ref[...]))
    h3 = h2 + gate * _dot(p_ref[...].astype(_BF16), wpp_ref[...])
    out_ref[...] = _rms(h3, g_fin_ref[...]) if last_layer else h3


def _final(dest, ys, h1, route, p2, g_ple, wpg, wpp, g_fin, last_layer):
    t = h1.shape[0]
    rows = MOE_ROWS

    def full(a):
        return pl.BlockSpec(a.shape, lambda i: (0,) * a.ndim)

    def tok(n):
        return pl.BlockSpec((rows, n), lambda i: (i, 0))

    return pl.pallas_call(
        functools.partial(_final_kernel, last_layer=last_layer),
        grid=(t // rows,),
        in_specs=[pl.BlockSpec(memory_space=pl.ANY), pl.BlockSpec(memory_space=pl.ANY), tok(D_MODEL), tok(LANES),
                  tok(PLE_DIM), full(g_ple), full(wpg), full(wpp), full(g_fin)],
        out_specs=tok(D_MODEL),
        out_shape=jax.ShapeDtypeStruct((t, D_MODEL), _F32),
        scratch_shapes=[pltpu.VMEM((2, TOP_K * rows * ROW_TILE, LANES), _F32),
                        pltpu.SMEM((2, SUBLANES, LANES), jnp.int32),
                        pltpu.SemaphoreType.DMA((2,)), pltpu.SemaphoreType.DMA((2,))],
        compiler_params=pltpu.CompilerParams(dimension_semantics=("arbitrary",), vmem_limit_bytes=VMEM_LIMIT),
        name="final",
    )(dest, ys, h1, route, p2, g_ple, wpg, wpp, g_fin)


def _rope_pad(w, rot):
    half = MLA_ROPE_DIM // 2
    body = jnp.concatenate([-w[:, half:], w[:, :half]], axis=1) if rot else w
    z = jnp.zeros((w.shape[0], MLA_NOPE_DIM), w.dtype)
    return jnp.concatenate([z, body, jnp.zeros((w.shape[0], LANES - MLA_NOPE_DIM - MLA_ROPE_DIM), w.dtype)], axis=1)


def _layer_weights(w_in, w_uq, w_ukv):
    o = Q_LORA_RANK + KV_LORA_RANK
    w_kr = w_in[:, o:o + MLA_ROPE_DIM]
    w1 = jnp.concatenate([w_in[:, :o], _rope_pad(w_kr, False), _rope_pad(w_kr, True), w_in[:, o + MLA_ROPE_DIM:]],
                         axis=1).astype(_BF16)
    uq = w_uq.reshape(Q_LORA_RANK, MLA_HEADS, MLA_NOPE_DIM + MLA_ROPE_DIM)
    zq = jnp.zeros((Q_LORA_RANK, MLA_HEADS, LANES - MLA_NOPE_DIM - MLA_ROPE_DIM), w_uq.dtype)
    wq = jnp.concatenate([uq, zq], axis=2).reshape(Q_LORA_RANK, MLA_HEADS * LANES).astype(_BF16)
    rope = uq[:, :, MLA_NOPE_DIM:]
    half = MLA_ROPE_DIM // 2
    rot = jnp.concatenate([jnp.zeros_like(uq[:, :, :MLA_NOPE_DIM]), -rope[:, :, half:], rope[:, :, :half], zq], axis=2)
    wqr = rot.reshape(Q_LORA_RANK, MLA_HEADS * LANES).astype(_BF16)
    ukv = w_ukv.reshape(KV_LORA_RANK, MLA_HEADS, MLA_NOPE_DIM + MLA_V_DIM)
    zk = jnp.zeros((KV_LORA_RANK, MLA_HEADS, LANES - MLA_NOPE_DIM), w_ukv.dtype)
    wk = jnp.concatenate([ukv[:, :, :MLA_NOPE_DIM], zk], axis=2).reshape(KV_LORA_RANK, MLA_HEADS * LANES).astype(_BF16)
    v = ukv[:, :, MLA_NOPE_DIM:].reshape(KV_LORA_RANK, HEAD_PAIRS, 2, MLA_V_DIM)
    zv = jnp.zeros((KV_LORA_RANK, HEAD_PAIRS, MLA_V_DIM), w_ukv.dtype)
    wv = jnp.stack([jnp.concatenate([v[:, :, 0], zv], axis=2), jnp.concatenate([zv, v[:, :, 1]], axis=2)], axis=2)
    wv = wv.reshape(KV_LORA_RANK, MLA_HEADS * LANES).astype(_BF16)
    return w1, wq, wqr, wk, wv


def kernel(x, p, positions, w_in, g_attn, g_cq, w_uq, g_ckv, w_ukv, g_out_mla, g_out_sb, w_o, g_moe, w_router,
           b_router, w_gu, b_gu, w_dn, b_dn, g_ple, w_ple_gate, w_ple_proj, g_final):
    b, s, d = x.shape
    t = b * s
    depth = w_in.shape[0]
    assert d == D_MODEL and ROW_TILE == SUBLANES and s % MLA_K_TILE == 0 and t % PROJ_ROWS == 0
    assert t % MOE_ROWS == 0 and (LANES // TOP_K) % DMA_GROUP == 0 and (t * TOP_K) % EXPERT_ROWS == 0
    assert EXPERT_ROWS & (EXPERT_ROWS - 1) == 0

    freq = ROPE_THETA ** (-jnp.arange(0, MLA_ROPE_DIM, 2, dtype=_F32) / MLA_ROPE_DIM)
    invf = jnp.concatenate([jnp.zeros((MLA_NOPE_DIM,), _F32), freq, freq,
                            jnp.zeros((LANES - MLA_NOPE_DIM - MLA_ROPE_DIM,), _F32)]).reshape(1, LANES)
    pos = positions.reshape(t, 1)
    idx = jnp.arange(SB_TILE)
    tri = (idx[:, None] >= idx[None, :]).astype(_BF16)
    idx = jnp.arange(PROJ_ROWS)
    ltri = (idx[None, :] < idx[:, None]).astype(_BF16)
    n_blocks = t * TOP_K // EXPERT_ROWS + N_EXPERTS

    h = x.reshape(t, d)
    for i in range(depth):
        w1, wq, wqr, wk, wv = _layer_weights(w_in[i], w_uq[i], w_ukv[i])
        qm, km, vm, qs, ks, vs = _proj(pos, h, g_attn[i].reshape(1, d), w1, g_cq[i].reshape(1, -1), wq, wqr,
                                       g_ckv[i].reshape(1, -1), wk, wv, invf)

        def seq(a):
            return a.reshape(b, s, a.shape[1])

        om = _mla(seq(qm), seq(km), seq(vm)).reshape(t, -1)
        os_ = _sb(seq(qs), seq(ks), seq(vs), tri).reshape(t, -1)
        h1, u2_tiles, route, counts = _post(om, os_, h, g_out_mla[i].reshape(1, -1), g_out_sb[i].reshape(1, -1),
                                            w_o[i].astype(_BF16), g_moe[i].reshape(1, d), w_router[i],
                                            b_router[i].reshape(1, -1), ltri)
        ids = route[:, :TOP_K].astype(jnp.int32).reshape(-1, LANES)
        rank = route[:, 2 * TOP_K:3 * TOP_K].astype(jnp.int32).reshape(-1, LANES)
        counts = counts.reshape(-1).astype(jnp.int32)
        dest, start, block_e = _layout(counts, ids, rank, n_blocks)
        xs = _dispatch(counts, start, dest, u2_tiles, n_blocks)
        ys = _experts(block_e, xs, w_gu[i], b_gu[i], w_dn[i], b_dn[i])
        h = _final(dest, ys, h1, route, p[i].reshape(t, -1), g_ple[i].reshape(1, d), w_ple_gate[i].astype(_BF16),
                   w_ple_proj[i].astype(_BF16), g_final.reshape(1, d), i == depth - 1)
    return h.reshape(b, s, d)
```

```python
import functools
import math

import jax
import jax.numpy as jnp
from jax import lax
from jax.experimental import pallas as pl
from jax.experimental.pallas import tpu as pltpu

D_MODEL = 1024
PLE_DIM = 256
MLA_HEADS = 8
MLA_NOPE_DIM = 64
MLA_ROPE_DIM = 32
MLA_V_DIM = 64
Q_LORA_RANK = 384
KV_LORA_RANK = 256
SB_HEADS = 8
SB_HEAD_DIM = 64
SB_WIDTH = SB_HEADS * SB_HEAD_DIM
ROPE_THETA = 10000.0
N_EXPERTS = 32
TOP_K = 4
D_FF = 1024
SWIGLU_LIMIT = 7.0
SWIGLU_ALPHA = 1.702
RMS_EPS = 1e-6

LANES = 128
SUBLANES = 8
ROW_TILE = D_MODEL // LANES
HEAD_PAIRS = MLA_HEADS // 2
PROJ_ROWS = 256
GROUP_HEADS = 4
MLA_Q_TILE = 256
MLA_K_TILE = 512
SB_TILE = 256
EXPERT_ROWS = 256
ROW_PARTS = 2
MOE_ROWS = SUBLANES * LANES // TOP_K
DMA_GROUP = 8
STAGE_SLOTS = 4
DMA_THREADS = 2
VMEM_LIMIT = 56 * 1024 * 1024
SB_LOG_FLOOR = -105.0

_F32 = jnp.float32
_BF16 = jnp.bfloat16


def _rms(x, g):
    return x * lax.rsqrt(jnp.mean(x * x, axis=-1, keepdims=True) + RMS_EPS) * g


def _dot(a, b):
    return jnp.dot(a, b, preferred_element_type=_F32)


def _dot_nt(a, b):
    return lax.dot_general(a, b, (((1,), (1,)), ((), ())), preferred_element_type=_F32)


def _store_row_tiles(ref, value, first_row=0):
    rows = value.shape[0]
    for g in range(ROW_TILE):
        ref[pl.ds(first_row * ROW_TILE + g, rows, stride=ROW_TILE), :] = value[:, g * LANES:(g + 1) * LANES]


def _load_row_tiles(ref, first_row, rows):
    return jnp.concatenate([ref[pl.ds(first_row * ROW_TILE + g, rows, stride=ROW_TILE), :] for g in range(ROW_TILE)],
                           axis=-1)


def _proj_kernel(pos_ref, x_ref, g_attn_ref, w1_ref, g_cq_ref, wq_ref, wqr_ref, g_ckv_ref, wk_ref, wv_ref,
                 invf_ref, qm_ref, km_ref, vm_ref, qs_ref, ks_ref, vs_ref):
    u = _rms(x_ref[...], g_attn_ref[...]).astype(_BF16)
    y = _dot(u, w1_ref[...])
    c_q = y[:, :Q_LORA_RANK]
    c_kv = y[:, Q_LORA_RANK:Q_LORA_RANK + KV_LORA_RANK]
    o = Q_LORA_RANK + KV_LORA_RANK
    k_r = y[:, o:o + LANES]
    k_r_rot = y[:, o + LANES:o + 2 * LANES]
    o += 2 * LANES
    q_s = y[:, o:o + SB_WIDTH]
    k_s = y[:, o + SB_WIDTH:o + 2 * SB_WIDTH]
    v_s = y[:, o + 2 * SB_WIDTH:o + 3 * SB_WIDTH]

    ang = pos_ref[...].astype(_F32) * invf_ref[...]
    cos = jnp.cos(ang)
    sin = jnp.sin(ang)

    cq_n = _rms(c_q, g_cq_ref[...]).astype(_BF16)
    q = _dot(cq_n, wq_ref[...])
    q_rot = _dot(cq_n, wqr_ref[...])
    ckv_n = _rms(c_kv, g_ckv_ref[...]).astype(_BF16)
    k = _dot(ckv_n, wk_ref[...])
    vm_ref[...] = _dot(ckv_n, wv_ref[...]).astype(_BF16)
    k_rope = k_r * cos + k_r_rot * sin

    q_scale = (MLA_NOPE_DIM + MLA_ROPE_DIM) ** -0.5 * math.log2(math.e)
    lane = lax.broadcasted_iota(jnp.int32, (1, LANES), 1)
    low = lane < SB_HEAD_DIM
    for h in range(MLA_HEADS):
        sl = slice(h * LANES, (h + 1) * LANES)
        qm_ref[:, sl] = ((q[:, sl] * cos + q_rot[:, sl] * sin) * q_scale).astype(_BF16)
        km_ref[:, sl] = (k[:, sl] + k_rope).astype(_BF16)
    qs_ref[...] = (q_s * (SB_HEAD_DIM ** -0.5)).astype(_BF16)
    for hp in range(HEAD_PAIRS):
        sl = slice(hp * LANES, (hp + 1) * LANES)
        for half, keep in ((0, low), (1, jnp.logical_not(low))):
            dst = slice((2 * hp + half) * LANES, (2 * hp + half + 1) * LANES)
            ks_ref[:, dst] = jnp.where(keep, k_s[:, sl], 0.0).astype(_BF16)
            vs_ref[:, dst] = jnp.where(keep, v_s[:, sl], 0.0).astype(_BF16)


def _proj(pos, x2, g_attn, w1, g_cq, wq, wqr, g_ckv, wk, wv, invf):
    t = x2.shape[0]
    rows = PROJ_ROWS
    wide = MLA_HEADS * LANES

    def full(a):
        return pl.BlockSpec(a.shape, lambda i: (0,) * a.ndim)

    def tok(n):
        return pl.BlockSpec((rows, n), lambda i: (i, 0))

    outs = [jax.ShapeDtypeStruct((t, n), _BF16) for n in (wide, wide, wide, SB_WIDTH, wide, wide)]
    return pl.pallas_call(
        _proj_kernel,
        grid=(t // rows,),
        in_specs=[tok(1), tok(D_MODEL), full(g_attn), full(w1), full(g_cq), full(wq), full(wqr), full(g_ckv),
                  full(wk), full(wv), full(invf)],
        out_specs=[tok(wide), tok(wide), tok(wide), tok(SB_WIDTH), tok(wide), tok(wide)],
        out_shape=outs,
        compiler_params=pltpu.CompilerParams(dimension_semantics=("parallel",), vmem_limit_bytes=VMEM_LIMIT),
        name="proj",
    )(pos, x2, g_attn, w1, g_cq, wq, wqr, g_ckv, wk, wv, invf)


def _mla_kernel(q_ref, k_ref, v_ref, o_ref):
    tq, tk = MLA_Q_TILE, MLA_K_TILE
    qi = pl.program_id(2)
    row = lax.broadcasted_iota(jnp.int32, (tq, tk), 0)
    col = lax.broadcasted_iota(jnp.int32, (tq, tk), 1)
    heads = [slice(h * LANES, (h + 1) * LANES) for h in range(GROUP_HEADS)]

    def key_rows(kt):
        return pl.ds(pl.multiple_of(kt * tk, tk), tk)

    def scores(kt):
        return tuple(_dot_nt(q_ref[0, :, sl], k_ref[0, key_rows(kt), sl]) for sl in heads)

    def absorb(kt, s_all, carry):
        new = []
        for sl, s, (m, l, acc) in zip(heads, s_all, carry):
            m_new = jnp.maximum(m, jnp.max(s, axis=-1, keepdims=True))
            alpha = jnp.exp2(m - m_new)
            p = jnp.exp2(s - m_new)
            l = alpha * l + jnp.sum(p, axis=-1, keepdims=True)
            acc = alpha * acc + _dot(p.astype(_BF16), v_ref[0, key_rows(kt), sl])
            new.append((m_new, l, acc))
        return tuple(new)

    last = (qi * tq) // tk
    causal = last * tk + col <= qi * tq + row
    s_diag = tuple(jnp.where(causal, s, -jnp.inf) for s in scores(last))
    init = tuple((jnp.full((tq, 1), -jnp.inf, _F32), jnp.zeros((tq, 1), _F32), jnp.zeros((tq, LANES), _F32))
                 for _ in range(GROUP_HEADS))

    def body(j, state):
        kt, s_all, carry = state
        s_next = scores(j)
        return j, s_next, absorb(kt, s_all, carry)

    kt, s_all, carry = lax.fori_loop(0, last, body, (last, s_diag, init))
    carry = absorb(kt, s_all, carry)
    for hp in range(GROUP_HEADS // 2):
        (_, l0, a0), (_, l1, a1) = carry[2 * hp], carry[2 * hp + 1]
        o_ref[0, :, hp * LANES:(hp + 1) * LANES] = a0 / l0 + a1 / l1


def _mla(qm, km, vm):
    b, s, _ = qm.shape
    tq = MLA_Q_TILE
    gw = GROUP_HEADS * LANES
    return pl.pallas_call(
        _mla_kernel,
        grid=(b, MLA_HEADS // GROUP_HEADS, s // tq),
        in_specs=[pl.BlockSpec((1, tq, gw), lambda bi, g, qi: (bi, qi, g)),
                  pl.BlockSpec((1, s, gw), lambda bi, g, qi: (bi, 0, g)),
                  pl.BlockSpec((1, s, gw), lambda bi, g, qi: (bi, 0, g))],
        out_specs=pl.BlockSpec((1, tq, gw // 2), lambda bi, g, qi: (bi, qi, g)),
        out_shape=jax.ShapeDtypeStruct((b, s, HEAD_PAIRS * LANES), _F32),
        compiler_params=pltpu.CompilerParams(dimension_semantics=("parallel", "parallel", "arbitrary"),
                                             vmem_limit_bytes=VMEM_LIMIT),
        name="mla",
    )(qm, km, vm)


def _sb_kernel(q_ref, k_ref, v_ref, tri_ref, o_ref):
    tile = SB_TILE
    qi = pl.program_id(2)
    row = lax.broadcasted_iota(jnp.int32, (tile, tile), 0)
    col = lax.broadcasted_iota(jnp.int32, (tile, tile), 1)
    strict = col < row
    heads = [slice(h * LANES, (h + 1) * LANES) for h in range(GROUP_HEADS)]

    def key_rows(kt):
        return pl.ds(pl.multiple_of(kt * tile, tile), tile)

    def tile_terms(kt, diagonal):
        tri = tri_ref[...]
        out = []
        for h, sl in enumerate(heads):
            q = q_ref[0, :, (h // 2) * LANES:(h // 2 + 1) * LANES]
            z = _dot_nt(q, k_ref[0, key_rows(kt), sl])
            sp = jnp.maximum(z, 0.0) + jnp.log(1.0 + jnp.exp(-jnp.abs(z)))
            log_not = -sp
            if diagonal:
                log_not = jnp.where(strict, log_not, 0.0)
            hi = log_not.astype(_BF16)
            lo = (log_not - hi.astype(_F32)).astype(_BF16)
            incl = _dot(hi, tri) + _dot(lo, tri)
            out.append((z - sp, log_not, incl))
        return out

    def absorb(kt, terms, rems, accs, diagonal, live=None):
        new_rems, new_accs = [], []
        for sl, (log_sig, log_not, incl), rem, acc in zip(heads, terms, rems, accs):
            p = jnp.exp(log_sig + (rem + incl - log_not))
            if diagonal:
                p = jnp.where(strict, p, 0.0)
            step = incl[:, :1]
            if live is not None:
                p = jnp.where(live, p, 0.0)
                step = jnp.where(live, step, 0.0)
            new_accs.append(acc + _dot(p.astype(_BF16), v_ref[0, key_rows(kt), sl]))
            new_rems.append(rem + step)
        return tuple(new_rems), tuple(new_accs)

    def rem_max(rems):
        return functools.reduce(jnp.maximum, [jnp.max(r) for r in rems])

    left = jnp.maximum(qi - 1, 0)
    terms_diag = tile_terms(qi, True)
    terms_left = tile_terms(left, False)
    rems = tuple(jnp.zeros((tile, 1), _F32) for _ in range(GROUP_HEADS))
    accs = tuple(jnp.zeros((tile, LANES), _F32) for _ in range(GROUP_HEADS))
    rems, accs = absorb(qi, terms_diag, rems, accs, True)
    rems, accs = absorb(left, terms_left, rems, accs, False, live=qi > 0)

    def cond(state):
        kt, worst, _, _ = state
        return jnp.logical_and(kt >= 0, worst > SB_LOG_FLOOR)

    def body(state):
        kt, _, rems, accs = state
        rems, accs = absorb(kt, tile_terms(kt, False), rems, accs, False)
        return kt - 1, rem_max(rems), rems, accs

    _, _, _, accs = lax.while_loop(cond, body, (qi - 2, rem_max(rems), rems, accs))
    for hp in range(GROUP_HEADS // 2):
        o_ref[0, :, hp * LANES:(hp + 1) * LANES] = accs[2 * hp] + accs[2 * hp + 1]


def _sb(qs, ks, vs, tri):
    b, s, _ = qs.shape
    tile = SB_TILE
    gw = GROUP_HEADS * LANES
    return pl.pallas_call(
        _sb_kernel,
        grid=(b, SB_HEADS // GROUP_HEADS, s // tile),
        in_specs=[pl.BlockSpec((1, tile, gw // 2), lambda bi, g, qi: (bi, qi, g)),
                  pl.BlockSpec((1, s, gw), lambda bi, g, qi: (bi, 0, g)),
                  pl.BlockSpec((1, s, gw), lambda bi, g, qi: (bi, 0, g)),
                  pl.BlockSpec((tile, tile), lambda bi, g, qi: (0, 0))],
        out_specs=pl.BlockSpec((1, tile, gw // 2), lambda bi, g, qi: (bi, qi, g)),
        out_shape=jax.ShapeDtypeStruct((b, s, HEAD_PAIRS * LANES), _F32),
        compiler_params=pltpu.CompilerParams(dimension_semantics=("parallel", "parallel", "arbitrary"),
                                             vmem_limit_bytes=VMEM_LIMIT),
        name="sb",
    )(qs, ks, vs, tri)


def _post_kernel(om_ref, os_ref, x_ref, g_om_ref, g_os_ref, wo_ref, g_moe_ref, wr_ref, br_ref, ltri_ref,
                 h1_ref, u2_ref, route_ref, counts_ref, seen_ref):
    rows = PROJ_ROWS

    @pl.when(pl.program_id(0) == 0)
    def _():
        seen_ref[...] = jnp.zeros_like(seen_ref)

    part = rows // ROW_PARTS
    lane = lax.broadcasted_iota(jnp.int32, (part, N_EXPERTS), 1).astype(_F32)
    out_lane = lax.broadcasted_iota(jnp.int32, (part, LANES), 1)
    seen = seen_ref[...]
    w_hi = wr_ref[...].astype(_BF16)
    w_lo = (wr_ref[...] - w_hi.astype(_F32)).astype(_BF16)
    for r0 in range(0, rows, part):
        rs = slice(r0, r0 + part)
        mixed = jnp.concatenate([_rms(om_ref[rs, :], g_om_ref[...]), _rms(os_ref[rs, :], g_os_ref[...])], axis=-1)
        h1 = x_ref[rs, :] + _dot(mixed.astype(_BF16), wo_ref[...])
        h1_ref[rs, :] = h1
        u2 = _rms(h1, g_moe_ref[...])
        _store_row_tiles(u2_ref, u2, r0)
        u_hi = u2.astype(_BF16)
        u_lo = (u2 - u_hi.astype(_F32)).astype(_BF16)
        logits = _dot(u_hi, w_hi) + _dot(u_hi, w_lo) + _dot(u_lo, w_hi) + br_ref[...]

        work = logits
        ids, tops = [], []
        onehot = jnp.zeros((part, N_EXPERTS), _F32)
        for _ in range(TOP_K):
            top = jnp.max(work, axis=-1, keepdims=True)
            idx = jnp.min(jnp.where(work == top, lane, float(N_EXPERTS)), axis=-1, keepdims=True)
            hit = lane == idx
            onehot = jnp.where(hit, 1.0, onehot)
            work = jnp.where(hit, -jnp.inf, work)
            ids.append(idx)
            tops.append(top)
        exps = [jnp.exp(tp - tops[0]) for tp in tops]
        denom = exps[0] + exps[1] + exps[2] + exps[3]
        gates = [e / denom for e in exps]

        before = seen + _dot(ltri_ref[...], onehot.astype(_BF16))
        ranks = [jnp.sum(jnp.where(lane == idx, before, 0.0), axis=-1, keepdims=True) for idx in ids]
        seen = seen + jnp.sum(onehot, axis=0, keepdims=True)

        route = jnp.zeros((part, LANES), _F32)
        for j, val in enumerate(ids + gates + ranks):
            route = jnp.where(out_lane == j, val, route)
        route_ref[rs, :] = route
    seen_ref[...] = seen
    counts_ref[...] = seen


def _post(om, os_, x2, g_om, g_os, wo, g_moe, wr, br, ltri):
    t = x2.shape[0]
    rows = PROJ_ROWS

    def full(a):
        return pl.BlockSpec(a.shape, lambda i: (0,) * a.ndim)

    def tok(n):
        return pl.BlockSpec((rows, n), lambda i: (i, 0))

    return pl.pallas_call(
        _post_kernel,
        grid=(t // rows,),
        in_specs=[tok(om.shape[1]), tok(os_.shape[1]), tok(D_MODEL), full(g_om), full(g_os), full(wo), full(g_moe),
                  full(wr), full(br), full(ltri)],
        out_specs=[tok(D_MODEL), pl.BlockSpec((rows * ROW_TILE, LANES), lambda i: (i, 0)), tok(LANES),
                   pl.BlockSpec((1, N_EXPERTS), lambda i: (0, 0))],
        out_shape=[jax.ShapeDtypeStruct((t, D_MODEL), _F32), jax.ShapeDtypeStruct((t * ROW_TILE, LANES), _F32),
                   jax.ShapeDtypeStruct((t, LANES), _F32), jax.ShapeDtypeStruct((1, N_EXPERTS), _F32)],
        scratch_shapes=[pltpu.VMEM((1, N_EXPERTS), _F32)],
        compiler_params=pltpu.CompilerParams(dimension_semantics=("arbitrary",), vmem_limit_bytes=VMEM_LIMIT),
        name="post",
    )(om, os_, x2, g_om, g_os, wo, g_moe, wr, br, ltri)


def _padded_count(cnt_ref, e):
    shift = EXPERT_ROWS.bit_length() - 1
    return lax.shift_left(lax.shift_right_logical(cnt_ref[e] + (EXPERT_ROWS - 1), shift), shift)


def _layout_kernel(cnt_ref, ids_ref, rank_ref, dest_ref, start_ref, be_ref, used_ref):
    shift = EXPERT_ROWS.bit_length() - 1
    nb = be_ref.shape[0]

    def place(e, off):
        padded = _padded_count(cnt_ref, e)
        start_ref[e] = off

        def mark(j, c):
            be_ref[lax.shift_right_logical(off, shift) + j] = e
            return c
        lax.fori_loop(0, lax.shift_right_logical(padded, shift), mark, 0)
        return off + padded
    total = lax.fori_loop(0, N_EXPERTS, place, 0)
    used_ref[0] = lax.shift_right_logical(total, shift)

    def tail(b, c):
        be_ref[b] = N_EXPERTS - 1
        return c
    lax.fori_loop(lax.shift_right_logical(total, shift), nb, tail, 0)

    ids = ids_ref[...]
    dest = rank_ref[...]
    for e in range(N_EXPERTS):
        dest = dest + jnp.where(ids == e, start_ref[e], 0)
    dest_ref[...] = dest


def _layout(counts, ids, rank, n_blocks):
    grid_spec = pltpu.PrefetchScalarGridSpec(
        num_scalar_prefetch=1,
        grid=(1,),
        in_specs=[pl.BlockSpec(ids.shape, lambda i, c: (0, 0)), pl.BlockSpec(rank.shape, lambda i, c: (0, 0))],
        out_specs=[pl.BlockSpec(ids.shape, lambda i, c: (0, 0)), pl.BlockSpec(memory_space=pltpu.SMEM),
                   pl.BlockSpec(memory_space=pltpu.SMEM), pl.BlockSpec(memory_space=pltpu.SMEM)],
    )
    return pl.pallas_call(
        _layout_kernel,
        grid_spec=grid_spec,
        out_shape=[jax.ShapeDtypeStruct(ids.shape, jnp.int32), jax.ShapeDtypeStruct((N_EXPERTS,), jnp.int32),
                   jax.ShapeDtypeStruct((n_blocks,), jnp.int32), jax.ShapeDtypeStruct((1,), jnp.int32)],
        compiler_params=pltpu.CompilerParams(dimension_semantics=("arbitrary",)),
        name="layout",
    )(counts, ids, rank)


def _tile_rows(row):
    return pl.ds(pl.multiple_of(row * ROW_TILE, ROW_TILE), ROW_TILE)


def _for_each_pair(idx, s, body):
    line_tokens = LANES // TOP_K

    def line(g, c):
        for t0 in range(0, line_tokens, DMA_GROUP):
            vals = [idx[s, g, (t0 + t) * TOP_K + k] for t in range(DMA_GROUP) for k in range(TOP_K)]
            for j, row in enumerate(vals):
                body(g * line_tokens + t0 + j // TOP_K, j % TOP_K, row, j % DMA_THREADS)
        return c
    lax.fori_loop(0, SUBLANES, line, 0)


def _dispatch_kernel(cnt_ref, start_ref, dest_hbm, u_hbm, xs_hbm, idx, stage, zrow, zblk, sem_d, sem_l, sem_i, sem_z):
    rows = MOE_ROWS
    shift = EXPERT_ROWS.bit_length() - 1
    i = pl.program_id(0)
    nt = pl.num_programs(0)
    nb = xs_hbm.shape[0] // (EXPERT_ROWS * ROW_TILE)
    slot = lax.rem(i, 2)
    stage_slot = lax.rem(i, STAGE_SLOTS)

    def idx_copy(tile, s):
        return pltpu.make_async_copy(dest_hbm.at[pl.ds(pl.multiple_of(tile * SUBLANES, SUBLANES), SUBLANES)],
                                     idx.at[s], sem_i.at[s])

    def load(tile):
        s = lax.rem(tile, STAGE_SLOTS)
        src = pl.ds(pl.multiple_of(tile * (rows * ROW_TILE), rows * ROW_TILE), rows * ROW_TILE)
        return pltpu.make_async_copy(u_hbm.at[src], stage.at[s], sem_l.at[s])

    def wait_rows(s):
        chunk = pl.ds(0, rows * ROW_TILE)
        for _ in range(TOP_K):
            pltpu.make_async_copy(stage.at[s], xs_hbm.at[chunk], sem_d.at[s]).wait()

    def zero_row(r):
        return pltpu.make_async_copy(zrow, xs_hbm.at[_tile_rows(r)], sem_z)

    def zero_block(b):
        dst = pl.ds(pl.multiple_of(b * (EXPERT_ROWS * ROW_TILE), EXPERT_ROWS * ROW_TILE), EXPERT_ROWS * ROW_TILE)
        return pltpu.make_async_copy(zblk, xs_hbm.at[dst], sem_z)

    def pad_rows(e, fn):
        def one(r, c):
            fn(zero_row(start_ref[e] + r))
            return c
        lax.fori_loop(cnt_ref[e], _padded_count(cnt_ref, e), one, 0)

    def tail_blocks(fn):
        used = start_ref[N_EXPERTS - 1] + _padded_count(cnt_ref, N_EXPERTS - 1)

        def one(b, c):
            fn(zero_block(b))
            return c
        lax.fori_loop(lax.shift_right_logical(used, shift), nb, one, 0)

    @pl.when(i == 0)
    def _():
        idx_copy(0, 0).start()
        load(0).start()

        @pl.when(nt > 1)
        def _():
            load(1).start()
        zrow[...] = jnp.zeros_like(zrow)
        zblk[...] = jnp.zeros_like(zblk)
        for fn in (lambda c: c.start(), lambda c: c.wait()):
            def per_expert(e, c, fn=fn):
                pad_rows(e, fn)
                return c
            lax.fori_loop(0, N_EXPERTS, per_expert, 0)
            tail_blocks(fn)

    idx_copy(i, slot).wait()

    @pl.when(i + 1 < nt)
    def _():
        idx_copy(i + 1, 1 - slot).start()

    @pl.when(i >= 2)
    def _():
        wait_rows(lax.rem(i + 2, STAGE_SLOTS))

    @pl.when(i + 2 < nt)
    def _():
        load(i + 2).start()

    load(i).wait()

    def copy_row(t, k, row, thread):
        del k
        pltpu.make_async_copy(stage.at[stage_slot, _tile_rows(t)], xs_hbm.at[_tile_rows(row)],
                              sem_d.at[stage_slot]).start(priority=thread)
    _for_each_pair(idx, slot, copy_row)

    @pl.when(i == nt - 1)
    def _():
        @pl.when(nt > 1)
        def _():
            wait_rows(lax.rem(i + STAGE_SLOTS - 1, STAGE_SLOTS))
        wait_rows(stage_slot)


def _dispatch(counts, start, dest, u2_tiles, n_blocks):
    nt = dest.shape[0] // SUBLANES
    grid_spec = pltpu.PrefetchScalarGridSpec(
        num_scalar_prefetch=2,
        grid=(nt,),
        in_specs=[pl.BlockSpec(memory_space=pl.ANY), pl.BlockSpec(memory_space=pl.ANY)],
        out_specs=pl.BlockSpec(memory_space=pl.ANY),
        scratch_shapes=[pltpu.SMEM((2, SUBLANES, LANES), jnp.int32),
                        pltpu.VMEM((STAGE_SLOTS, MOE_ROWS * ROW_TILE, LANES), _F32),
                        pltpu.VMEM((ROW_TILE, LANES), _F32), pltpu.VMEM((EXPERT_ROWS * ROW_TILE, LANES), _F32),
                        pltpu.SemaphoreType.DMA((STAGE_SLOTS,)), pltpu.SemaphoreType.DMA((STAGE_SLOTS,)),
                        pltpu.SemaphoreType.DMA((2,)), pltpu.SemaphoreType.DMA(())],
    )
    return pl.pallas_call(
        _dispatch_kernel,
        grid_spec=grid_spec,
        out_shape=jax.ShapeDtypeStruct((n_blocks * EXPERT_ROWS * ROW_TILE, LANES), _F32),
        compiler_params=pltpu.CompilerParams(dimension_semantics=("arbitrary",), vmem_limit_bytes=VMEM_LIMIT),
        name="dispatch",
    )(counts, start, dest, u2_tiles)


def _expert_kernel(be_ref, used_ref, xs_ref, wgu_ref, bgu_ref, wdn_ref, bdn_ref, ys_ref, wgu_bf, wdn_bf):
    rows = EXPERT_ROWS
    i = pl.program_id(0)
    e = be_ref[i]
    prev = be_ref[jnp.maximum(i - 1, 0)]
    in_use = i < used_ref[0]

    @pl.when(jnp.logical_and(in_use, jnp.logical_or(i == 0, e != prev)))
    def _():
        chunk = 128

        def cast(c, carry):
            rs = pl.ds(pl.multiple_of(c * chunk, chunk), chunk)
            wgu_bf[rs, :] = wgu_ref[0, rs, :].astype(_BF16)
            wdn_bf[rs, :] = wdn_ref[0, rs, :].astype(_BF16)
            return carry
        lax.fori_loop(0, D_MODEL // chunk, cast, 0)

    @pl.when(in_use)
    def _():
        x = _load_row_tiles(xs_ref, 0, rows).astype(_BF16)
        hh = _dot(x, wgu_bf[...]) + bgu_ref[0]
        glu = jnp.minimum(hh[:, :D_FF], SWIGLU_LIMIT)
        lin = jnp.clip(hh[:, D_FF:], -SWIGLU_LIMIT, SWIGLU_LIMIT)
        act = glu * jax.nn.sigmoid(SWIGLU_ALPHA * glu) * (lin + 1.0)
        _store_row_tiles(ys_ref, _dot(act.astype(_BF16), wdn_bf[...]) + bdn_ref[0])

    @pl.when(jnp.logical_not(in_use))
    def _():
        ys_ref[...] = jnp.zeros_like(ys_ref)


def _experts(block_e, used, xs, w_gu, b_gu, w_dn, b_dn):
    nb = block_e.shape[0]
    rows = EXPERT_ROWS
    grid_spec = pltpu.PrefetchScalarGridSpec(
        num_scalar_prefetch=2,
        grid=(nb,),
        in_specs=[pl.BlockSpec((rows * ROW_TILE, LANES), lambda i, be, u: (i, 0)),
                  pl.BlockSpec((1, D_MODEL, 2 * D_FF), lambda i, be, u: (be[i], 0, 0)),
                  pl.BlockSpec((1, 1, 2 * D_FF), lambda i, be, u: (be[i], 0, 0)),
                  pl.BlockSpec((1, D_FF, D_MODEL), lambda i, be, u: (be[i], 0, 0)),
                  pl.BlockSpec((1, 1, D_MODEL), lambda i, be, u: (be[i], 0, 0))],
        out_specs=pl.BlockSpec((rows * ROW_TILE, LANES), lambda i, be, u: (i, 0)),
        scratch_shapes=[pltpu.VMEM((D_MODEL, 2 * D_FF), _BF16), pltpu.VMEM((D_FF, D_MODEL), _BF16)],
    )
    return pl.pallas_call(
        _expert_kernel,
        grid_spec=grid_spec,
        out_shape=jax.ShapeDtypeStruct(xs.shape, _F32),
        compiler_params=pltpu.CompilerParams(dimension_semantics=("arbitrary",), vmem_limit_bytes=VMEM_LIMIT),
        name="experts",
    )(block_e, used, xs, w_gu, b_gu.reshape(N_EXPERTS, 1, 2 * D_FF), w_dn, b_dn.reshape(N_EXPERTS, 1, D_MODEL))


def _final_kernel(dest_hbm, ys_hbm, h1_ref, route_ref, p_ref, g_ple_ref, wpg_ref, wpp_ref, g_fin_ref,
                  out_ref, ybuf, idx, sem_y, sem_i, *, last_layer):
    rows = MOE_ROWS
    i = pl.program_id(0)
    nt = pl.num_programs(0)
    slot = lax.rem(i, 2)

    def idx_copy(tile, s):
        return pltpu.make_async_copy(dest_hbm.at[pl.ds(pl.multiple_of(tile * SUBLANES, SUBLANES), SUBLANES)],
                                     idx.at[s], sem_i.at[s])

    def gather(s):
        def copy_row(t, k, row, thread):
            pltpu.make_async_copy(ys_hbm.at[_tile_rows(row)], ybuf.at[s, _tile_rows(k * rows + t)],
                                  sem_y.at[s]).start(priority=thread)
        _for_each_pair(idx, s, copy_row)

    @pl.when(i == 0)
    def _():
        idx_copy(0, 0).start()
        idx_copy(0, 0).wait()
        gather(0)

        @pl.when(nt > 1)
        def _():
            idx_copy(1, 1).start()

    @pl.when(i + 1 < nt)
    def _():
        idx_copy(i + 1, 1 - slot).wait()
        gather(1 - slot)

        @pl.when(i + 2 < nt)
        def _():
            idx_copy(i + 2, slot).start()

    pltpu.make_async_copy(ys_hbm.at[pl.ds(0, TOP_K * rows * ROW_TILE)], ybuf.at[slot], sem_y.at[slot]).wait()
    part = rows // ROW_PARTS
    for r0 in range(0, rows, part):
        rs = slice(r0, r0 + part)
        route = route_ref[rs, :]
        y = jnp.zeros((part, D_MODEL), _F32)
        for k in range(TOP_K):
            y = y + _load_row_tiles(ybuf.at[slot], k * rows + r0, part) * route[:, TOP_K + k:TOP_K + k + 1]
        h2 = h1_ref[rs, :] + y
        u3 = _rms(h2, g_ple_ref[...]).astype(_BF16)
        gate = jax.nn.sigmoid(_dot(u3, wpg_ref[...]))
        h3 = h2 + gate * _dot(p_ref[rs, :].astype(_BF16), wpp_ref[...])
        out_ref[rs, :] = _rms(h3, g_fin_ref[...]) if last_layer else h3


def _final(dest, ys, h1, route, p2, g_ple, wpg, wpp, g_fin, last_layer):
    t = h1.shape[0]
    rows = MOE_ROWS

    def full(a):
        return pl.BlockSpec(a.shape, lambda i: (0,) * a.ndim)

    def tok(n):
        return pl.BlockSpec((rows, n), lambda i: (i, 0))

    return pl.pallas_call(
        functools.partial(_final_kernel, last_layer=last_layer),
        grid=(t // rows,),
        in_specs=[pl.BlockSpec(memory_space=pl.ANY), pl.BlockSpec(memory_space=pl.ANY), tok(D_MODEL), tok(LANES),
                  tok(PLE_DIM), full(g_ple), full(wpg), full(wpp), full(g_fin)],
        out_specs=tok(D_MODEL),
        out_shape=jax.ShapeDtypeStruct((t, D_MODEL), _F32),
        scratch_shapes=[pltpu.VMEM((2, TOP_K * rows * ROW_TILE, LANES), _F32),
                        pltpu.SMEM((2, SUBLANES, LANES), jnp.int32),
                        pltpu.SemaphoreType.DMA((2,)), pltpu.SemaphoreType.DMA((2,))],
        compiler_params=pltpu.CompilerParams(dimension_semantics=("arbitrary",), vmem_limit_bytes=VMEM_LIMIT),
        name="final",
    )(dest, ys, h1, route, p2, g_ple, wpg, wpp, g_fin)


def _rope_pad(w, rot):
    half = MLA_ROPE_DIM // 2
    body = jnp.concatenate([-w[:, half:], w[:, :half]], axis=1) if rot else w
    z = jnp.zeros((w.shape[0], MLA_NOPE_DIM), w.dtype)
    return jnp.concatenate([z, body, jnp.zeros((w.shape[0], LANES - MLA_NOPE_DIM - MLA_ROPE_DIM), w.dtype)], axis=1)


def _layer_weights(w_in, w_uq, w_ukv):
    o = Q_LORA_RANK + KV_LORA_RANK
    w_kr = w_in[:, o:o + MLA_ROPE_DIM]
    w1 = jnp.concatenate([w_in[:, :o], _rope_pad(w_kr, False), _rope_pad(w_kr, True), w_in[:, o + MLA_ROPE_DIM:]],
                         axis=1).astype(_BF16)
    uq = w_uq.reshape(Q_LORA_RANK, MLA_HEADS, MLA_NOPE_DIM + MLA_ROPE_DIM)
    zq = jnp.zeros((Q_LORA_RANK, MLA_HEADS, LANES - MLA_NOPE_DIM - MLA_ROPE_DIM), w_uq.dtype)
    wq = jnp.concatenate([uq, zq], axis=2).reshape(Q_LORA_RANK, MLA_HEADS * LANES).astype(_BF16)
    rope = uq[:, :, MLA_NOPE_DIM:]
    half = MLA_ROPE_DIM // 2
    rot = jnp.concatenate([jnp.zeros_like(uq[:, :, :MLA_NOPE_DIM]), -rope[:, :, half:], rope[:, :, :half], zq], axis=2)
    wqr = rot.reshape(Q_LORA_RANK, MLA_HEADS * LANES).astype(_BF16)
    ukv = w_ukv.reshape(KV_LORA_RANK, MLA_HEADS, MLA_NOPE_DIM + MLA_V_DIM)
    zk = jnp.zeros((KV_LORA_RANK, MLA_HEADS, LANES - MLA_NOPE_DIM), w_ukv.dtype)
    wk = jnp.concatenate([ukv[:, :, :MLA_NOPE_DIM], zk], axis=2).reshape(KV_LORA_RANK, MLA_HEADS * LANES).astype(_BF16)
    v = ukv[:, :, MLA_NOPE_DIM:].reshape(KV_LORA_RANK, HEAD_PAIRS, 2, MLA_V_DIM)
    zv = jnp.zeros((KV_LORA_RANK, HEAD_PAIRS, MLA_V_DIM), w_ukv.dtype)
    wv = jnp.stack([jnp.concatenate([v[:, :, 0], zv], axis=2), jnp.concatenate([zv, v[:, :, 1]], axis=2)], axis=2)
    wv = wv.reshape(KV_LORA_RANK, MLA_HEADS * LANES).astype(_BF16)
    return w1, wq, wqr, wk, wv


def kernel(x, p, positions, w_in, g_attn, g_cq, w_uq, g_ckv, w_ukv, g_out_mla, g_out_sb, w_o, g_moe, w_router,
           b_router, w_gu, b_gu, w_dn, b_dn, g_ple, w_ple_gate, w_ple_proj, g_final):
    b, s, d = x.shape
    t = b * s
    depth = w_in.shape[0]
    assert d == D_MODEL and ROW_TILE == SUBLANES and s % MLA_K_TILE == 0 and t % PROJ_ROWS == 0
    assert t % MOE_ROWS == 0 and (LANES // TOP_K) % DMA_GROUP == 0 and (t * TOP_K) % EXPERT_ROWS == 0
    assert EXPERT_ROWS & (EXPERT_ROWS - 1) == 0

    freq = ROPE_THETA ** (-jnp.arange(0, MLA_ROPE_DIM, 2, dtype=_F32) / MLA_ROPE_DIM)
    invf = jnp.concatenate([jnp.zeros((MLA_NOPE_DIM,), _F32), freq, freq,
                            jnp.zeros((LANES - MLA_NOPE_DIM - MLA_ROPE_DIM,), _F32)]).reshape(1, LANES)
    pos = positions.reshape(t, 1)
    idx = jnp.arange(SB_TILE)
    tri = (idx[:, None] >= idx[None, :]).astype(_BF16)
    idx = jnp.arange(PROJ_ROWS // ROW_PARTS)
    ltri = (idx[None, :] < idx[:, None]).astype(_BF16)
    n_blocks = t * TOP_K // EXPERT_ROWS + N_EXPERTS

    h = x.reshape(t, d)
    for i in range(depth):
        w1, wq, wqr, wk, wv = _layer_weights(w_in[i], w_uq[i], w_ukv[i])
        qm, km, vm, qs, ks, vs = _proj(pos, h, g_attn[i].reshape(1, d), w1, g_cq[i].reshape(1, -1), wq, wqr,
                                       g_ckv[i].reshape(1, -1), wk, wv, invf)

        def seq(a):
            return a.reshape(b, s, a.shape[1])

        om = _mla(seq(qm), seq(km), seq(vm)).reshape(t, -1)
        os_ = _sb(seq(qs), seq(ks), seq(vs), tri).reshape(t, -1)
        h1, u2_tiles, route, counts = _post(om, os_, h, g_out_mla[i].reshape(1, -1), g_out_sb[i].reshape(1, -1),
                                            w_o[i].astype(_BF16), g_moe[i].reshape(1, d), w_router[i],
                                            b_router[i].reshape(1, -1), ltri)
        ids = route[:, :TOP_K].astype(jnp.int32).reshape(-1, LANES)
        rank = route[:, 2 * TOP_K:3 * TOP_K].astype(jnp.int32).reshape(-1, LANES)
        counts = counts.reshape(-1).astype(jnp.int32)
        dest, start, block_e, used = _layout(counts, ids, rank, n_blocks)
        xs = _dispatch(counts, start, dest, u2_tiles, n_blocks)
        ys = _experts(block_e, used, xs, w_gu[i], b_gu[i], w_dn[i], b_dn[i])
        h = _final(dest, ys, h1, route, p[i].reshape(t, -1), g_ple[i].reshape(1, d), w_ple_gate[i].astype(_BF16),
                   w_ple_proj[i].astype(_BF16), g_final.reshape(1, d), i == depth - 1)
    return h.reshape(b, s, d)
```

```python
import functools
import math

import jax
import jax.numpy as jnp
from jax import lax
from jax.experimental import pallas as pl
from jax.experimental.pallas import tpu as pltpu

D_MODEL = 1024
PLE_DIM = 256
MLA_HEADS = 8
MLA_NOPE_DIM = 64
MLA_ROPE_DIM = 32
MLA_V_DIM = 64
Q_LORA_RANK = 384
KV_LORA_RANK = 256
SB_HEADS = 8
SB_HEAD_DIM = 64
SB_WIDTH = SB_HEADS * SB_HEAD_DIM
ROPE_THETA = 10000.0
N_EXPERTS = 32
TOP_K = 4
D_FF = 1024
SWIGLU_LIMIT = 7.0
SWIGLU_ALPHA = 1.702
RMS_EPS = 1e-6

LANES = 128
SUBLANES = 8
ROW_TILE = D_MODEL // LANES
HEAD_PAIRS = MLA_HEADS // 2
PROJ_ROWS = 256
GROUP_HEADS = 4
MLA_Q_TILE = 256
MLA_K_TILE = 512
SB_TILE = 256
EXPERT_ROWS = 256
ROW_PARTS = 2
MOE_ROWS = SUBLANES * LANES // TOP_K
DMA_GROUP = 8
STAGE_SLOTS = 4
DMA_THREADS = 2
VMEM_LIMIT = 56 * 1024 * 1024
SB_LOG_FLOOR = -105.0

_F32 = jnp.float32
_BF16 = jnp.bfloat16


def _rms(x, g):
    return x * lax.rsqrt(jnp.mean(x * x, axis=-1, keepdims=True) + RMS_EPS) * g


def _dot(a, b):
    return jnp.dot(a, b, preferred_element_type=_F32)


def _dot_nt(a, b):
    return lax.dot_general(a, b, (((1,), (1,)), ((), ())), preferred_element_type=_F32)


def _store_row_tiles(ref, value, first_row=0):
    rows = value.shape[0]
    for g in range(ROW_TILE):
        ref[pl.ds(first_row * ROW_TILE + g, rows, stride=ROW_TILE), :] = value[:, g * LANES:(g + 1) * LANES]


def _load_row_tiles(ref, first_row, rows):
    return jnp.concatenate([ref[pl.ds(first_row * ROW_TILE + g, rows, stride=ROW_TILE), :] for g in range(ROW_TILE)],
                           axis=-1)


def _proj_kernel(pos_ref, x_ref, g_attn_ref, w1_ref, g_cq_ref, wq_ref, wqr_ref, g_ckv_ref, wk_ref, wv_ref,
                 invf_ref, qm_ref, km_ref, vm_ref, qs_ref, ks_ref, vs_ref):
    u = _rms(x_ref[...], g_attn_ref[...]).astype(_BF16)
    y = _dot(u, w1_ref[...])
    c_q = y[:, :Q_LORA_RANK]
    c_kv = y[:, Q_LORA_RANK:Q_LORA_RANK + KV_LORA_RANK]
    o = Q_LORA_RANK + KV_LORA_RANK
    k_r = y[:, o:o + LANES]
    k_r_rot = y[:, o + LANES:o + 2 * LANES]
    o += 2 * LANES
    q_s = y[:, o:o + SB_WIDTH]
    k_s = y[:, o + SB_WIDTH:o + 2 * SB_WIDTH]
    v_s = y[:, o + 2 * SB_WIDTH:o + 3 * SB_WIDTH]

    ang = pos_ref[...].astype(_F32) * invf_ref[...]
    cos = jnp.cos(ang)
    sin = jnp.sin(ang)

    cq_n = _rms(c_q, g_cq_ref[...]).astype(_BF16)
    q = _dot(cq_n, wq_ref[...])
    q_rot = _dot(cq_n, wqr_ref[...])
    ckv_n = _rms(c_kv, g_ckv_ref[...]).astype(_BF16)
    k = _dot(ckv_n, wk_ref[...])
    vm_ref[...] = _dot(ckv_n, wv_ref[...]).astype(_BF16)
    k_rope = k_r * cos + k_r_rot * sin

    q_scale = (MLA_NOPE_DIM + MLA_ROPE_DIM) ** -0.5 * math.log2(math.e)
    lane = lax.broadcasted_iota(jnp.int32, (1, LANES), 1)
    low = lane < SB_HEAD_DIM
    for h in range(MLA_HEADS):
        sl = slice(h * LANES, (h + 1) * LANES)
        qm_ref[:, sl] = ((q[:, sl] * cos + q_rot[:, sl] * sin) * q_scale).astype(_BF16)
        km_ref[:, sl] = (k[:, sl] + k_rope).astype(_BF16)
    qs_ref[...] = (q_s * (SB_HEAD_DIM ** -0.5)).astype(_BF16)
    for hp in range(HEAD_PAIRS):
        sl = slice(hp * LANES, (hp + 1) * LANES)
        for half, keep in ((0, low), (1, jnp.logical_not(low))):
            dst = slice((2 * hp + half) * LANES, (2 * hp + half + 1) * LANES)
            ks_ref[:, dst] = jnp.where(keep, k_s[:, sl], 0.0).astype(_BF16)
            vs_ref[:, dst] = jnp.where(keep, v_s[:, sl], 0.0).astype(_BF16)


def _proj(pos, x2, g_attn, w1, g_cq, wq, wqr, g_ckv, wk, wv, invf):
    t = x2.shape[0]
    rows = PROJ_ROWS
    wide = MLA_HEADS * LANES

    def full(a):
        return pl.BlockSpec(a.shape, lambda i: (0,) * a.ndim)

    def tok(n):
        return pl.BlockSpec((rows, n), lambda i: (i, 0))

    outs = [jax.ShapeDtypeStruct((t, n), _BF16) for n in (wide, wide, wide, SB_WIDTH, wide, wide)]
    return pl.pallas_call(
        _proj_kernel,
        grid=(t // rows,),
        in_specs=[tok(1), tok(D_MODEL), full(g_attn), full(w1), full(g_cq), full(wq), full(wqr), full(g_ckv),
                  full(wk), full(wv), full(invf)],
        out_specs=[tok(wide), tok(wide), tok(wide), tok(SB_WIDTH), tok(wide), tok(wide)],
        out_shape=outs,
        compiler_params=pltpu.CompilerParams(dimension_semantics=("parallel",), vmem_limit_bytes=VMEM_LIMIT),
        name="proj",
    )(pos, x2, g_attn, w1, g_cq, wq, wqr, g_ckv, wk, wv, invf)


def _mla_kernel(q_ref, k_ref, v_ref, o_ref, s_a, s_b, m_ref, l_ref, acc_ref):
    tq, tk = MLA_Q_TILE, MLA_K_TILE
    qi = pl.program_id(2)
    heads = [slice(h * LANES, (h + 1) * LANES) for h in range(GROUP_HEADS)]

    def key_rows(kt):
        return pl.ds(pl.multiple_of(kt * tk, tk), tk)

    def score(dst, kt, mask=None):
        for h, sl in enumerate(heads):
            s = _dot_nt(q_ref[0, :, sl], k_ref[0, key_rows(kt), sl])
            dst[h] = s if mask is None else jnp.where(mask, s, -jnp.inf)

    def absorb(src, kt):
        for h, sl in enumerate(heads):
            s = src[h]
            m = m_ref[h]
            m_new = jnp.maximum(m, jnp.max(s, axis=-1, keepdims=True))
            alpha = jnp.exp2(m - m_new)
            p = jnp.exp2(s - m_new)
            m_ref[h] = m_new
            l_ref[h] = alpha * l_ref[h] + jnp.sum(p, axis=-1, keepdims=True)
            acc_ref[h] = alpha * acc_ref[h] + _dot(p.astype(_BF16), v_ref[0, key_rows(kt), sl])

    m_ref[...] = jnp.full(m_ref.shape, -jnp.inf, _F32)
    l_ref[...] = jnp.zeros(l_ref.shape, _F32)
    acc_ref[...] = jnp.zeros(acc_ref.shape, _F32)
    last = (qi * tq) // tk
    row = lax.broadcasted_iota(jnp.int32, (tq, tk), 0)
    col = lax.broadcasted_iota(jnp.int32, (tq, tk), 1)
    score(s_a, last, mask=last * tk + col <= qi * tq + row)

    def tile_in_a(i):
        return jnp.where(i == 0, last, 2 * i - 1)

    def pair(i, c):
        score(s_b, 2 * i)
        absorb(s_a, tile_in_a(i))
        score(s_a, 2 * i + 1)
        absorb(s_b, 2 * i)
        return c

    pairs = last // 2
    lax.fori_loop(0, pairs, pair, 0)

    @pl.when(last % 2 == 1)
    def _():
        score(s_b, last - 1)
        absorb(s_a, tile_in_a(pairs))
        absorb(s_b, last - 1)

    @pl.when(last % 2 == 0)
    def _():
        absorb(s_a, tile_in_a(pairs))

    for hp in range(GROUP_HEADS // 2):
        o_ref[0, :, hp * LANES:(hp + 1) * LANES] = (acc_ref[2 * hp] / l_ref[2 * hp]
                                                    + acc_ref[2 * hp + 1] / l_ref[2 * hp + 1])


def _mla(qm, km, vm):
    b, s, _ = qm.shape
    tq = MLA_Q_TILE
    gw = GROUP_HEADS * LANES
    return pl.pallas_call(
        _mla_kernel,
        grid=(b, MLA_HEADS // GROUP_HEADS, s // tq),
        in_specs=[pl.BlockSpec((1, tq, gw), lambda bi, g, qi: (bi, qi, g)),
                  pl.BlockSpec((1, s, gw), lambda bi, g, qi: (bi, 0, g)),
                  pl.BlockSpec((1, s, gw), lambda bi, g, qi: (bi, 0, g))],
        out_specs=pl.BlockSpec((1, tq, gw // 2), lambda bi, g, qi: (bi, qi, g)),
        out_shape=jax.ShapeDtypeStruct((b, s, HEAD_PAIRS * LANES), _F32),
        scratch_shapes=[pltpu.VMEM((GROUP_HEADS, tq, MLA_K_TILE), _F32), pltpu.VMEM((GROUP_HEADS, tq, MLA_K_TILE), _F32),
                        pltpu.VMEM((GROUP_HEADS, tq, 1), _F32), pltpu.VMEM((GROUP_HEADS, tq, 1), _F32),
                        pltpu.VMEM((GROUP_HEADS, tq, LANES), _F32)],
        compiler_params=pltpu.CompilerParams(dimension_semantics=("parallel", "parallel", "arbitrary"),
                                             vmem_limit_bytes=VMEM_LIMIT),
        name="mla",
    )(qm, km, vm)


def _sb_kernel(q_ref, k_ref, v_ref, tri_ref, o_ref):
    tile = SB_TILE
    qi = pl.program_id(2)
    row = lax.broadcasted_iota(jnp.int32, (tile, tile), 0)
    col = lax.broadcasted_iota(jnp.int32, (tile, tile), 1)
    strict = col < row
    heads = [slice(h * LANES, (h + 1) * LANES) for h in range(GROUP_HEADS)]

    def key_rows(kt):
        return pl.ds(pl.multiple_of(kt * tile, tile), tile)

    def tile_terms(kt, diagonal):
        tri = tri_ref[...]
        out = []
        for h, sl in enumerate(heads):
            q = q_ref[0, :, (h // 2) * LANES:(h // 2 + 1) * LANES]
            z = _dot_nt(q, k_ref[0, key_rows(kt), sl])
            sp = jnp.maximum(z, 0.0) + jnp.log(1.0 + jnp.exp(-jnp.abs(z)))
            log_not = -sp
            if diagonal:
                log_not = jnp.where(strict, log_not, 0.0)
            hi = log_not.astype(_BF16)
            lo = (log_not - hi.astype(_F32)).astype(_BF16)
            incl = _dot(hi, tri) + _dot(lo, tri)
            out.append((z - sp, log_not, incl))
        return out

    def absorb(kt, terms, rems, accs, diagonal, live=None):
        new_rems, new_accs = [], []
        for sl, (log_sig, log_not, incl), rem, acc in zip(heads, terms, rems, accs):
            p = jnp.exp(log_sig + (rem + incl - log_not))
            if diagonal:
                p = jnp.where(strict, p, 0.0)
            step = incl[:, :1]
            if live is not None:
                p = jnp.where(live, p, 0.0)
                step = jnp.where(live, step, 0.0)
            new_accs.append(acc + _dot(p.astype(_BF16), v_ref[0, key_rows(kt), sl]))
            new_rems.append(rem + step)
        return tuple(new_rems), tuple(new_accs)

    def rem_max(rems):
        return functools.reduce(jnp.maximum, [jnp.max(r) for r in rems])

    left = jnp.maximum(qi - 1, 0)
    terms_diag = tile_terms(qi, True)
    terms_left = tile_terms(left, False)
    rems = tuple(jnp.zeros((tile, 1), _F32) for _ in range(GROUP_HEADS))
    accs = tuple(jnp.zeros((tile, LANES), _F32) for _ in range(GROUP_HEADS))
    rems, accs = absorb(qi, terms_diag, rems, accs, True)
    rems, accs = absorb(left, terms_left, rems, accs, False, live=qi > 0)

    def cond(state):
        kt, worst, _, _ = state
        return jnp.logical_and(kt >= 0, worst > SB_LOG_FLOOR)

    def body(state):
        kt, _, rems, accs = state
        rems, accs = absorb(kt, tile_terms(kt, False), rems, accs, False)
        return kt - 1, rem_max(rems), rems, accs

    _, _, _, accs = lax.while_loop(cond, body, (qi - 2, rem_max(rems), rems, accs))
    for hp in range(GROUP_HEADS // 2):
        o_ref[0, :, hp * LANES:(hp + 1) * LANES] = accs[2 * hp] + accs[2 * hp + 1]


def _sb(qs, ks, vs, tri):
    b, s, _ = qs.shape
    tile = SB_TILE
    gw = GROUP_HEADS * LANES
    return pl.pallas_call(
        _sb_kernel,
        grid=(b, SB_HEADS // GROUP_HEADS, s // tile),
        in_specs=[pl.BlockSpec((1, tile, gw // 2), lambda bi, g, qi: (bi, qi, g)),
                  pl.BlockSpec((1, s, gw), lambda bi, g, qi: (bi, 0, g)),
                  pl.BlockSpec((1, s, gw), lambda bi, g, qi: (bi, 0, g)),
                  pl.BlockSpec((tile, tile), lambda bi, g, qi: (0, 0))],
        out_specs=pl.BlockSpec((1, tile, gw // 2), lambda bi, g, qi: (bi, qi, g)),
        out_shape=jax.ShapeDtypeStruct((b, s, HEAD_PAIRS * LANES), _F32),
        compiler_params=pltpu.CompilerParams(dimension_semantics=("parallel", "parallel", "arbitrary"),
                                             vmem_limit_bytes=VMEM_LIMIT),
        name="sb",
    )(qs, ks, vs, tri)


def _post_kernel(om_ref, os_ref, x_ref, g_om_ref, g_os_ref, wo_ref, g_moe_ref, wr_ref, br_ref, ltri_ref,
                 h1_ref, u2_ref, route_ref, counts_ref, seen_ref):
    rows = PROJ_ROWS

    @pl.when(pl.program_id(0) == 0)
    def _():
        seen_ref[...] = jnp.zeros_like(seen_ref)

    part = rows // ROW_PARTS
    lane = lax.broadcasted_iota(jnp.int32, (part, N_EXPERTS), 1).astype(_F32)
    out_lane = lax.broadcasted_iota(jnp.int32, (part, LANES), 1)
    seen = seen_ref[...]
    w_hi = wr_ref[...].astype(_BF16)
    w_lo = (wr_ref[...] - w_hi.astype(_F32)).astype(_BF16)
    for r0 in range(0, rows, part):
        rs = slice(r0, r0 + part)
        mixed = jnp.concatenate([_rms(om_ref[rs, :], g_om_ref[...]), _rms(os_ref[rs, :], g_os_ref[...])], axis=-1)
        h1 = x_ref[rs, :] + _dot(mixed.astype(_BF16), wo_ref[...])
        h1_ref[rs, :] = h1
        u2 = _rms(h1, g_moe_ref[...])
        _store_row_tiles(u2_ref, u2, r0)
        u_hi = u2.astype(_BF16)
        u_lo = (u2 - u_hi.astype(_F32)).astype(_BF16)
        logits = _dot(u_hi, w_hi) + _dot(u_hi, w_lo) + _dot(u_lo, w_hi) + br_ref[...]

        work = logits
        ids, tops = [], []
        onehot = jnp.zeros((part, N_EXPERTS), _F32)
        for _ in range(TOP_K):
            top = jnp.max(work, axis=-1, keepdims=True)
            idx = jnp.min(jnp.where(work == top, lane, float(N_EXPERTS)), axis=-1, keepdims=True)
            hit = lane == idx
            onehot = jnp.where(hit, 1.0, onehot)
            work = jnp.where(hit, -jnp.inf, work)
            ids.append(idx)
            tops.append(top)
        exps = [jnp.exp(tp - tops[0]) for tp in tops]
        denom = exps[0] + exps[1] + exps[2] + exps[3]
        gates = [e / denom for e in exps]

        before = seen + _dot(ltri_ref[...], onehot.astype(_BF16))
        ranks = [jnp.sum(jnp.where(lane == idx, before, 0.0), axis=-1, keepdims=True) for idx in ids]
        seen = seen + jnp.sum(onehot, axis=0, keepdims=True)

        route = jnp.zeros((part, LANES), _F32)
        for j, val in enumerate(ids + gates + ranks):
            route = jnp.where(out_lane == j, val, route)
        route_ref[rs, :] = route
    seen_ref[...] = seen
    counts_ref[...] = seen


def _post(om, os_, x2, g_om, g_os, wo, g_moe, wr, br, ltri):
    t = x2.shape[0]
    rows = PROJ_ROWS

    def full(a):
        return pl.BlockSpec(a.shape, lambda i: (0,) * a.ndim)

    def tok(n):
        return pl.BlockSpec((rows, n), lambda i: (i, 0))

    return pl.pallas_call(
        _post_kernel,
        grid=(t // rows,),
        in_specs=[tok(om.shape[1]), tok(os_.shape[1]), tok(D_MODEL), full(g_om), full(g_os), full(wo), full(g_moe),
                  full(wr), full(br), full(ltri)],
        out_specs=[tok(D_MODEL), pl.BlockSpec((rows * ROW_TILE, LANES), lambda i: (i, 0)), tok(LANES),
                   pl.BlockSpec((1, N_EXPERTS), lambda i: (0, 0))],
        out_shape=[jax.ShapeDtypeStruct((t, D_MODEL), _F32), jax.ShapeDtypeStruct((t * ROW_TILE, LANES), _F32),
                   jax.ShapeDtypeStruct((t, LANES), _F32), jax.ShapeDtypeStruct((1, N_EXPERTS), _F32)],
        scratch_shapes=[pltpu.VMEM((1, N_EXPERTS), _F32)],
        compiler_params=pltpu.CompilerParams(dimension_semantics=("arbitrary",), vmem_limit_bytes=VMEM_LIMIT),
        name="post",
    )(om, os_, x2, g_om, g_os, wo, g_moe, wr, br, ltri)


def _padded_count(cnt_ref, e):
    shift = EXPERT_ROWS.bit_length() - 1
    return lax.shift_left(lax.shift_right_logical(cnt_ref[e] + (EXPERT_ROWS - 1), shift), shift)


def _layout_kernel(cnt_ref, ids_ref, rank_ref, dest_ref, start_ref, be_ref, used_ref):
    shift = EXPERT_ROWS.bit_length() - 1
    nb = be_ref.shape[0]

    def place(e, off):
        padded = _padded_count(cnt_ref, e)
        start_ref[e] = off

        def mark(j, c):
            be_ref[lax.shift_right_logical(off, shift) + j] = e
            return c
        lax.fori_loop(0, lax.shift_right_logical(padded, shift), mark, 0)
        return off + padded
    total = lax.fori_loop(0, N_EXPERTS, place, 0)
    used_ref[0] = lax.shift_right_logical(total, shift)

    def tail(b, c):
        be_ref[b] = N_EXPERTS - 1
        return c
    lax.fori_loop(lax.shift_right_logical(total, shift), nb, tail, 0)

    ids = ids_ref[...]
    dest = rank_ref[...]
    for e in range(N_EXPERTS):
        dest = dest + jnp.where(ids == e, start_ref[e], 0)
    dest_ref[...] = dest


def _layout(counts, ids, rank, n_blocks):
    grid_spec = pltpu.PrefetchScalarGridSpec(
        num_scalar_prefetch=1,
        grid=(1,),
        in_specs=[pl.BlockSpec(ids.shape, lambda i, c: (0, 0)), pl.BlockSpec(rank.shape, lambda i, c: (0, 0))],
        out_specs=[pl.BlockSpec(ids.shape, lambda i, c: (0, 0)), pl.BlockSpec(memory_space=pltpu.SMEM),
                   pl.BlockSpec(memory_space=pltpu.SMEM), pl.BlockSpec(memory_space=pltpu.SMEM)],
    )
    return pl.pallas_call(
        _layout_kernel,
        grid_spec=grid_spec,
        out_shape=[jax.ShapeDtypeStruct(ids.shape, jnp.int32), jax.ShapeDtypeStruct((N_EXPERTS,), jnp.int32),
                   jax.ShapeDtypeStruct((n_blocks,), jnp.int32), jax.ShapeDtypeStruct((1,), jnp.int32)],
        compiler_params=pltpu.CompilerParams(dimension_semantics=("arbitrary",)),
        name="layout",
    )(counts, ids, rank)


def _tile_rows(row):
    return pl.ds(pl.multiple_of(row * ROW_TILE, ROW_TILE), ROW_TILE)


def _for_each_pair(idx, s, body):
    line_tokens = LANES // TOP_K

    def line(g, c):
        for t0 in range(0, line_tokens, DMA_GROUP):
            vals = [idx[s, g, (t0 + t) * TOP_K + k] for t in range(DMA_GROUP) for k in range(TOP_K)]
            for j, row in enumerate(vals):
                body(g * line_tokens + t0 + j // TOP_K, j % TOP_K, row, j % DMA_THREADS)
        return c
    lax.fori_loop(0, SUBLANES, line, 0)


def _dispatch_kernel(cnt_ref, start_ref, dest_hbm, u_hbm, xs_hbm, idx, stage, zrow, zblk, sem_d, sem_l, sem_i, sem_z):
    rows = MOE_ROWS
    shift = EXPERT_ROWS.bit_length() - 1
    i = pl.program_id(0)
    nt = pl.num_programs(0)
    nb = xs_hbm.shape[0] // (EXPERT_ROWS * ROW_TILE)
    slot = lax.rem(i, 2)
    stage_slot = lax.rem(i, STAGE_SLOTS)

    def idx_copy(tile, s):
        return pltpu.make_async_copy(dest_hbm.at[pl.ds(pl.multiple_of(tile * SUBLANES, SUBLANES), SUBLANES)],
                                     idx.at[s], sem_i.at[s])

    def load(tile):
        s = lax.rem(tile, STAGE_SLOTS)
        src = pl.ds(pl.multiple_of(tile * (rows * ROW_TILE), rows * ROW_TILE), rows * ROW_TILE)
        return pltpu.make_async_copy(u_hbm.at[src], stage.at[s], sem_l.at[s])

    def wait_rows(s):
        chunk = pl.ds(0, rows * ROW_TILE)
        for _ in range(TOP_K):
            pltpu.make_async_copy(stage.at[s], xs_hbm.at[chunk], sem_d.at[s]).wait()

    def zero_row(r):
        return pltpu.make_async_copy(zrow, xs_hbm.at[_tile_rows(r)], sem_z)

    def zero_block(b):
        dst = pl.ds(pl.multiple_of(b * (EXPERT_ROWS * ROW_TILE), EXPERT_ROWS * ROW_TILE), EXPERT_ROWS * ROW_TILE)
        return pltpu.make_async_copy(zblk, xs_hbm.at[dst], sem_z)

    def pad_rows(e, fn):
        def one(r, c):
            fn(zero_row(start_ref[e] + r))
            return c
        lax.fori_loop(cnt_ref[e], _padded_count(cnt_ref, e), one, 0)

    def tail_blocks(fn):
        used = start_ref[N_EXPERTS - 1] + _padded_count(cnt_ref, N_EXPERTS - 1)

        def one(b, c):
            fn(zero_block(b))
            return c
        lax.fori_loop(lax.shift_right_logical(used, shift), nb, one, 0)

    @pl.when(i == 0)
    def _():
        idx_copy(0, 0).start()
        load(0).start()

        @pl.when(nt > 1)
        def _():
            load(1).start()
        zrow[...] = jnp.zeros_like(zrow)
        zblk[...] = jnp.zeros_like(zblk)
        for fn in (lambda c: c.start(), lambda c: c.wait()):
            def per_expert(e, c, fn=fn):
                pad_rows(e, fn)
                return c
            lax.fori_loop(0, N_EXPERTS, per_expert, 0)
            tail_blocks(fn)

    idx_copy(i, slot).wait()

    @pl.when(i + 1 < nt)
    def _():
        idx_copy(i + 1, 1 - slot).start()

    @pl.when(i >= 2)
    def _():
        wait_rows(lax.rem(i + 2, STAGE_SLOTS))

    @pl.when(i + 2 < nt)
    def _():
        load(i + 2).start()

    load(i).wait()

    def copy_row(t, k, row, thread):
        del k
        pltpu.make_async_copy(stage.at[stage_slot, _tile_rows(t)], xs_hbm.at[_tile_rows(row)],
                              sem_d.at[stage_slot]).start(priority=thread)
    _for_each_pair(idx, slot, copy_row)

    @pl.when(i == nt - 1)
    def _():
        @pl.when(nt > 1)
        def _():
            wait_rows(lax.rem(i + STAGE_SLOTS - 1, STAGE_SLOTS))
        wait_rows(stage_slot)


def _dispatch(counts, start, dest, u2_tiles, n_blocks):
    nt = dest.shape[0] // SUBLANES
    grid_spec = pltpu.PrefetchScalarGridSpec(
        num_scalar_prefetch=2,
        grid=(nt,),
        in_specs=[pl.BlockSpec(memory_space=pl.ANY), pl.BlockSpec(memory_space=pl.ANY)],
        out_specs=pl.BlockSpec(memory_space=pl.ANY),
        scratch_shapes=[pltpu.SMEM((2, SUBLANES, LANES), jnp.int32),
                        pltpu.VMEM((STAGE_SLOTS, MOE_ROWS * ROW_TILE, LANES), _F32),
                        pltpu.VMEM((ROW_TILE, LANES), _F32), pltpu.VMEM((EXPERT_ROWS * ROW_TILE, LANES), _F32),
                        pltpu.SemaphoreType.DMA((STAGE_SLOTS,)), pltpu.SemaphoreType.DMA((STAGE_SLOTS,)),
                        pltpu.SemaphoreType.DMA((2,)), pltpu.SemaphoreType.DMA(())],
    )
    return pl.pallas_call(
        _dispatch_kernel,
        grid_spec=grid_spec,
        out_shape=jax.ShapeDtypeStruct((n_blocks * EXPERT_ROWS * ROW_TILE, LANES), _F32),
        compiler_params=pltpu.CompilerParams(dimension_semantics=("arbitrary",), vmem_limit_bytes=VMEM_LIMIT),
        name="dispatch",
    )(counts, start, dest, u2_tiles)


def _expert_kernel(be_ref, used_ref, xs_ref, wgu_ref, bgu_ref, wdn_ref, bdn_ref, ys_ref, wgu_bf, wdn_bf):
    rows = EXPERT_ROWS
    i = pl.program_id(0)
    e = be_ref[i]
    prev = be_ref[jnp.maximum(i - 1, 0)]
    in_use = i < used_ref[0]

    @pl.when(jnp.logical_and(in_use, jnp.logical_or(i == 0, e != prev)))
    def _():
        chunk = 128

        def cast(c, carry):
            rs = pl.ds(pl.multiple_of(c * chunk, chunk), chunk)
            wgu_bf[rs, :] = wgu_ref[0, rs, :].astype(_BF16)
            wdn_bf[rs, :] = wdn_ref[0, rs, :].astype(_BF16)
            return carry
        lax.fori_loop(0, D_MODEL // chunk, cast, 0)

    @pl.when(in_use)
    def _():
        x = _load_row_tiles(xs_ref, 0, rows).astype(_BF16)
        hh = _dot(x, wgu_bf[...]) + bgu_ref[0]
        glu = jnp.minimum(hh[:, :D_FF], SWIGLU_LIMIT)
        lin = jnp.clip(hh[:, D_FF:], -SWIGLU_LIMIT, SWIGLU_LIMIT)
        act = glu * jax.nn.sigmoid(SWIGLU_ALPHA * glu) * (lin + 1.0)
        _store_row_tiles(ys_ref, _dot(act.astype(_BF16), wdn_bf[...]) + bdn_ref[0])

    @pl.when(jnp.logical_not(in_use))
    def _():
        ys_ref[...] = jnp.zeros_like(ys_ref)


def _experts(block_e, used, xs, w_gu, b_gu, w_dn, b_dn):
    nb = block_e.shape[0]
    rows = EXPERT_ROWS
    grid_spec = pltpu.PrefetchScalarGridSpec(
        num_scalar_prefetch=2,
        grid=(nb,),
        in_specs=[pl.BlockSpec((rows * ROW_TILE, LANES), lambda i, be, u: (i, 0)),
                  pl.BlockSpec((1, D_MODEL, 2 * D_FF), lambda i, be, u: (be[i], 0, 0)),
                  pl.BlockSpec((1, 1, 2 * D_FF), lambda i, be, u: (be[i], 0, 0)),
                  pl.BlockSpec((1, D_FF, D_MODEL), lambda i, be, u: (be[i], 0, 0)),
                  pl.BlockSpec((1, 1, D_MODEL), lambda i, be, u: (be[i], 0, 0))],
        out_specs=pl.BlockSpec((rows * ROW_TILE, LANES), lambda i, be, u: (i, 0)),
        scratch_shapes=[pltpu.VMEM((D_MODEL, 2 * D_FF), _BF16), pltpu.VMEM((D_FF, D_MODEL), _BF16)],
    )
    return pl.pallas_call(
        _expert_kernel,
        grid_spec=grid_spec,
        out_shape=jax.ShapeDtypeStruct(xs.shape, _F32),
        compiler_params=pltpu.CompilerParams(dimension_semantics=("arbitrary",), vmem_limit_bytes=VMEM_LIMIT),
        name="experts",
    )(block_e, used, xs, w_gu, b_gu.reshape(N_EXPERTS, 1, 2 * D_FF), w_dn, b_dn.reshape(N_EXPERTS, 1, D_MODEL))


def _final_kernel(dest_hbm, ys_hbm, h1_ref, route_ref, p_ref, g_ple_ref, wpg_ref, wpp_ref, g_fin_ref,
                  out_ref, ybuf, idx, sem_y, sem_i, *, last_layer):
    rows = MOE_ROWS
    i = pl.program_id(0)
    nt = pl.num_programs(0)
    slot = lax.rem(i, 2)

    def idx_copy(tile, s):
        return pltpu.make_async_copy(dest_hbm.at[pl.ds(pl.multiple_of(tile * SUBLANES, SUBLANES), SUBLANES)],
                                     idx.at[s], sem_i.at[s])

    def gather(s):
        def copy_row(t, k, row, thread):
            pltpu.make_async_copy(ys_hbm.at[_tile_rows(row)], ybuf.at[s, _tile_rows(k * rows + t)],
                                  sem_y.at[s]).start(priority=thread)
        _for_each_pair(idx, s, copy_row)

    @pl.when(i == 0)
    def _():
        idx_copy(0, 0).start()
        idx_copy(0, 0).wait()
        gather(0)

        @pl.when(nt > 1)
        def _():
            idx_copy(1, 1).start()

    @pl.when(i + 1 < nt)
    def _():
        idx_copy(i + 1, 1 - slot).wait()
        gather(1 - slot)

        @pl.when(i + 2 < nt)
        def _():
            idx_copy(i + 2, slot).start()

    pltpu.make_async_copy(ys_hbm.at[pl.ds(0, TOP_K * rows * ROW_TILE)], ybuf.at[slot], sem_y.at[slot]).wait()
    part = rows // ROW_PARTS
    for r0 in range(0, rows, part):
        rs = slice(r0, r0 + part)
        route = route_ref[rs, :]
        y = jnp.zeros((part, D_MODEL), _F32)
        for k in range(TOP_K):
            y = y + _load_row_tiles(ybuf.at[slot], k * rows + r0, part) * route[:, TOP_K + k:TOP_K + k + 1]
        h2 = h1_ref[rs, :] + y
        u3 = _rms(h2, g_ple_ref[...]).astype(_BF16)
        gate = jax.nn.sigmoid(_dot(u3, wpg_ref[...]))
        h3 = h2 + gate * _dot(p_ref[rs, :].astype(_BF16), wpp_ref[...])
        out_ref[rs, :] = _rms(h3, g_fin_ref[...]) if last_layer else h3


def _final(dest, ys, h1, route, p2, g_ple, wpg, wpp, g_fin, last_layer):
    t = h1.shape[0]
    rows = MOE_ROWS

    def full(a):
        return pl.BlockSpec(a.shape, lambda i: (0,) * a.ndim)

    def tok(n):
        return pl.BlockSpec((rows, n), lambda i: (i, 0))

    return pl.pallas_call(
        functools.partial(_final_kernel, last_layer=last_layer),
        grid=(t // rows,),
        in_specs=[pl.BlockSpec(memory_space=pl.ANY), pl.BlockSpec(memory_space=pl.ANY), tok(D_MODEL), tok(LANES),
                  tok(PLE_DIM), full(g_ple), full(wpg), full(wpp), full(g_fin)],
        out_specs=tok(D_MODEL),
        out_shape=jax.ShapeDtypeStruct((t, D_MODEL), _F32),
        scratch_shapes=[pltpu.VMEM((2, TOP_K * rows * ROW_TILE, LANES), _F32),
                        pltpu.SMEM((2, SUBLANES, LANES), jnp.int32),
                        pltpu.SemaphoreType.DMA((2,)), pltpu.SemaphoreType.DMA((2,))],
        compiler_params=pltpu.CompilerParams(dimension_semantics=("arbitrary",), vmem_limit_bytes=VMEM_LIMIT),
        name="final",
    )(dest, ys, h1, route, p2, g_ple, wpg, wpp, g_fin)


def _rope_pad(w, rot):
    half = MLA_ROPE_DIM // 2
    body = jnp.concatenate([-w[:, half:], w[:, :half]], axis=1) if rot else w
    z = jnp.zeros((w.shape[0], MLA_NOPE_DIM), w.dtype)
    return jnp.concatenate([z, body, jnp.zeros((w.shape[0], LANES - MLA_NOPE_DIM - MLA_ROPE_DIM), w.dtype)], axis=1)


def _layer_weights(w_in, w_uq, w_ukv):
    o = Q_LORA_RANK + KV_LORA_RANK
    w_kr = w_in[:, o:o + MLA_ROPE_DIM]
    w1 = jnp.concatenate([w_in[:, :o], _rope_pad(w_kr, False), _rope_pad(w_kr, True), w_in[:, o + MLA_ROPE_DIM:]],
                         axis=1).astype(_BF16)
    uq = w_uq.reshape(Q_LORA_RANK, MLA_HEADS, MLA_NOPE_DIM + MLA_ROPE_DIM)
    zq = jnp.zeros((Q_LORA_RANK, MLA_HEADS, LANES - MLA_NOPE_DIM - MLA_ROPE_DIM), w_uq.dtype)
    wq = jnp.concatenate([uq, zq], axis=2).reshape(Q_LORA_RANK, MLA_HEADS * LANES).astype(_BF16)
    rope = uq[:, :, MLA_NOPE_DIM:]
    half = MLA_ROPE_DIM // 2
    rot = jnp.concatenate([jnp.zeros_like(uq[:, :, :MLA_NOPE_DIM]), -rope[:, :, half:], rope[:, :, :half], zq], axis=2)
    wqr = rot.reshape(Q_LORA_RANK, MLA_HEADS * LANES).astype(_BF16)
    ukv = w_ukv.reshape(KV_LORA_RANK, MLA_HEADS, MLA_NOPE_DIM + MLA_V_DIM)
    zk = jnp.zeros((KV_LORA_RANK, MLA_HEADS, LANES - MLA_NOPE_DIM), w_ukv.dtype)
    wk = jnp.concatenate([ukv[:, :, :MLA_NOPE_DIM], zk], axis=2).reshape(KV_LORA_RANK, MLA_HEADS * LANES).astype(_BF16)
    v = ukv[:, :, MLA_NOPE_DIM:].reshape(KV_LORA_RANK, HEAD_PAIRS, 2, MLA_V_DIM)
    zv = jnp.zeros((KV_LORA_RANK, HEAD_PAIRS, MLA_V_DIM), w_ukv.dtype)
    wv = jnp.stack([jnp.concatenate([v[:, :, 0], zv], axis=2), jnp.concatenate([zv, v[:, :, 1]], axis=2)], axis=2)
    wv = wv.reshape(KV_LORA_RANK, MLA_HEADS * LANES).astype(_BF16)
    return w1, wq, wqr, wk, wv


def kernel(x, p, positions, w_in, g_attn, g_cq, w_uq, g_ckv, w_ukv, g_out_mla, g_out_sb, w_o, g_moe, w_router,
           b_router, w_gu, b_gu, w_dn, b_dn, g_ple, w_ple_gate, w_ple_proj, g_final):
    b, s, d = x.shape
    t = b * s
    depth = w_in.shape[0]
    assert d == D_MODEL and ROW_TILE == SUBLANES and s % MLA_K_TILE == 0 and t % PROJ_ROWS == 0
    assert t % MOE_ROWS == 0 and (LANES // TOP_K) % DMA_GROUP == 0 and (t * TOP_K) % EXPERT_ROWS == 0
    assert EXPERT_ROWS & (EXPERT_ROWS - 1) == 0

    freq = ROPE_THETA ** (-jnp.arange(0, MLA_ROPE_DIM, 2, dtype=_F32) / MLA_ROPE_DIM)
    invf = jnp.concatenate([jnp.zeros((MLA_NOPE_DIM,), _F32), freq, freq,
                            jnp.zeros((LANES - MLA_NOPE_DIM - MLA_ROPE_DIM,), _F32)]).reshape(1, LANES)
    pos = positions.reshape(t, 1)
    idx = jnp.arange(SB_TILE)
    tri = (idx[:, None] >= idx[None, :]).astype(_BF16)
    idx = jnp.arange(PROJ_ROWS // ROW_PARTS)
    ltri = (idx[None, :] < idx[:, None]).astype(_BF16)
    n_blocks = t * TOP_K // EXPERT_ROWS + N_EXPERTS

    h = x.reshape(t, d)
    for i in range(depth):
        w1, wq, wqr, wk, wv = _layer_weights(w_in[i], w_uq[i], w_ukv[i])
        qm, km, vm, qs, ks, vs = _proj(pos, h, g_attn[i].reshape(1, d), w1, g_cq[i].reshape(1, -1), wq, wqr,
                                       g_ckv[i].reshape(1, -1), wk, wv, invf)

        def seq(a):
            return a.reshape(b, s, a.shape[1])

        om = _mla(seq(qm), seq(km), seq(vm)).reshape(t, -1)
        os_ = _sb(seq(qs), seq(ks), seq(vs), tri).reshape(t, -1)
        h1, u2_tiles, route, counts = _post(om, os_, h, g_out_mla[i].reshape(1, -1), g_out_sb[i].reshape(1, -1),
                                            w_o[i].astype(_BF16), g_moe[i].reshape(1, d), w_router[i],
                                            b_router[i].reshape(1, -1), ltri)
        ids = route[:, :TOP_K].astype(jnp.int32).reshape(-1, LANES)
        rank = route[:, 2 * TOP_K:3 * TOP_K].astype(jnp.int32).reshape(-1, LANES)
        counts = counts.reshape(-1).astype(jnp.int32)
        dest, start, block_e, used = _layout(counts, ids, rank, n_blocks)
        xs = _dispatch(counts, start, dest, u2_tiles, n_blocks)
        ys = _experts(block_e, used, xs, w_gu[i], b_gu[i], w_dn[i], b_dn[i])
        h = _final(dest, ys, h1, route, p[i].reshape(t, -1), g_ple[i].reshape(1, d), w_ple_gate[i].astype(_BF16),
                   w_ple_proj[i].astype(_BF16), g_final.reshape(1, d), i == depth - 1)
    return h.reshape(b, s, d)
```

```python
import functools
import math

import jax
import jax.numpy as jnp
from jax import lax
from jax.experimental import pallas as pl
from jax.experimental.pallas import tpu as pltpu

D_MODEL = 1024
PLE_DIM = 256
MLA_HEADS = 8
MLA_NOPE_DIM = 64
MLA_ROPE_DIM = 32
MLA_V_DIM = 64
Q_LORA_RANK = 384
KV_LORA_RANK = 256
SB_HEADS = 8
SB_HEAD_DIM = 64
SB_WIDTH = SB_HEADS * SB_HEAD_DIM
ROPE_THETA = 10000.0
N_EXPERTS = 32
TOP_K = 4
D_FF = 1024
SWIGLU_LIMIT = 7.0
SWIGLU_ALPHA = 1.702
RMS_EPS = 1e-6

LANES = 128
SUBLANES = 8
ROW_TILE = D_MODEL // LANES
HEAD_PAIRS = MLA_HEADS // 2
PROJ_ROWS = 256
GROUP_HEADS = 4
MLA_Q_TILE = 256
MLA_K_TILE = 512
SB_TILE = 256
EXPERT_ROWS = 256
ROW_PARTS = 2
MOE_ROWS = SUBLANES * LANES // TOP_K
DMA_GROUP = 8
STAGE_SLOTS = 4
DMA_THREADS = 2
VMEM_LIMIT = 56 * 1024 * 1024
SB_LOG_FLOOR = -105.0

_F32 = jnp.float32
_BF16 = jnp.bfloat16


def _rms(x, g):
    return x * lax.rsqrt(jnp.mean(x * x, axis=-1, keepdims=True) + RMS_EPS) * g


def _dot(a, b):
    return jnp.dot(a, b, preferred_element_type=_F32)


def _dot_nt(a, b):
    return lax.dot_general(a, b, (((1,), (1,)), ((), ())), preferred_element_type=_F32)


def _store_row_tiles(ref, value, first_row=0):
    rows = value.shape[0]
    for g in range(ROW_TILE):
        ref[pl.ds(first_row * ROW_TILE + g, rows, stride=ROW_TILE), :] = value[:, g * LANES:(g + 1) * LANES]


def _load_row_tiles(ref, first_row, rows):
    return jnp.concatenate([ref[pl.ds(first_row * ROW_TILE + g, rows, stride=ROW_TILE), :] for g in range(ROW_TILE)],
                           axis=-1)


def _proj_kernel(pos_ref, x_ref, g_attn_ref, w1_ref, g_cq_ref, wq_ref, wqr_ref, g_ckv_ref, wk_ref, wv_ref,
                 invf_ref, qm_ref, km_ref, vm_ref, qs_ref, ks_ref, vs_ref):
    u = _rms(x_ref[...], g_attn_ref[...]).astype(_BF16)
    y = _dot(u, w1_ref[...])
    c_q = y[:, :Q_LORA_RANK]
    c_kv = y[:, Q_LORA_RANK:Q_LORA_RANK + KV_LORA_RANK]
    o = Q_LORA_RANK + KV_LORA_RANK
    k_r = y[:, o:o + LANES]
    k_r_rot = y[:, o + LANES:o + 2 * LANES]
    o += 2 * LANES
    q_s = y[:, o:o + SB_WIDTH]
    k_s = y[:, o + SB_WIDTH:o + 2 * SB_WIDTH]
    v_s = y[:, o + 2 * SB_WIDTH:o + 3 * SB_WIDTH]

    ang = pos_ref[...].astype(_F32) * invf_ref[...]
    cos = jnp.cos(ang)
    sin = jnp.sin(ang)

    cq_n = _rms(c_q, g_cq_ref[...]).astype(_BF16)
    q = _dot(cq_n, wq_ref[...])
    q_rot = _dot(cq_n, wqr_ref[...])
    ckv_n = _rms(c_kv, g_ckv_ref[...]).astype(_BF16)
    k = _dot(ckv_n, wk_ref[...])
    wide_lane = lax.broadcasted_iota(jnp.int32, (1, MLA_HEADS * LANES), 1)
    ones_lane = (wide_lane % (2 * LANES) == MLA_V_DIM) | (wide_lane % (2 * LANES) == LANES)
    vm_ref[...] = jnp.where(ones_lane, 1.0, _dot(ckv_n, wv_ref[...])).astype(_BF16)
    k_rope = k_r * cos + k_r_rot * sin

    q_scale = (MLA_NOPE_DIM + MLA_ROPE_DIM) ** -0.5 * math.log2(math.e)
    lane = lax.broadcasted_iota(jnp.int32, (1, LANES), 1)
    low = lane < SB_HEAD_DIM
    for h in range(MLA_HEADS):
        sl = slice(h * LANES, (h + 1) * LANES)
        qm_ref[:, sl] = ((q[:, sl] * cos + q_rot[:, sl] * sin) * q_scale).astype(_BF16)
        km_ref[:, sl] = (k[:, sl] + k_rope).astype(_BF16)
    qs_ref[...] = (q_s * (SB_HEAD_DIM ** -0.5)).astype(_BF16)
    for hp in range(HEAD_PAIRS):
        sl = slice(hp * LANES, (hp + 1) * LANES)
        for half, keep in ((0, low), (1, jnp.logical_not(low))):
            dst = slice((2 * hp + half) * LANES, (2 * hp + half + 1) * LANES)
            ks_ref[:, dst] = jnp.where(keep, k_s[:, sl], 0.0).astype(_BF16)
            vs_ref[:, dst] = jnp.where(keep, v_s[:, sl], 0.0).astype(_BF16)


def _proj(pos, x2, g_attn, w1, g_cq, wq, wqr, g_ckv, wk, wv, invf):
    t = x2.shape[0]
    rows = PROJ_ROWS
    wide = MLA_HEADS * LANES

    def full(a):
        return pl.BlockSpec(a.shape, lambda i: (0,) * a.ndim)

    def tok(n):
        return pl.BlockSpec((rows, n), lambda i: (i, 0))

    outs = [jax.ShapeDtypeStruct((t, n), _BF16) for n in (wide, wide, wide, SB_WIDTH, wide, wide)]
    return pl.pallas_call(
        _proj_kernel,
        grid=(t // rows,),
        in_specs=[tok(1), tok(D_MODEL), full(g_attn), full(w1), full(g_cq), full(wq), full(wqr), full(g_ckv),
                  full(wk), full(wv), full(invf)],
        out_specs=[tok(wide), tok(wide), tok(wide), tok(SB_WIDTH), tok(wide), tok(wide)],
        out_shape=outs,
        compiler_params=pltpu.CompilerParams(dimension_semantics=("parallel",), vmem_limit_bytes=VMEM_LIMIT),
        name="proj",
    )(pos, x2, g_attn, w1, g_cq, wq, wqr, g_ckv, wk, wv, invf)


def _mla_kernel(q_ref, k_ref, v_ref, o_ref, s_a, s_b, m_ref, acc_ref):
    tq, tk = MLA_Q_TILE, MLA_K_TILE
    qi = pl.program_id(2)
    heads = [slice(h * LANES, (h + 1) * LANES) for h in range(GROUP_HEADS)]

    def key_rows(kt):
        return pl.ds(pl.multiple_of(kt * tk, tk), tk)

    def score(dst, kt, mask=None):
        for h, sl in enumerate(heads):
            s = _dot_nt(q_ref[0, :, sl], k_ref[0, key_rows(kt), sl])
            dst[h] = s if mask is None else jnp.where(mask, s, -jnp.inf)

    def absorb(src, kt):
        for h, sl in enumerate(heads):
            s = src[h]
            m = m_ref[h]
            m_new = jnp.maximum(m, jnp.max(s, axis=-1, keepdims=True))
            alpha = jnp.exp2(m - m_new)
            p = jnp.exp2(s - m_new)
            m_ref[h] = m_new
            acc_ref[h] = alpha * acc_ref[h] + _dot(p.astype(_BF16), v_ref[0, key_rows(kt), sl])

    m_ref[...] = jnp.full(m_ref.shape, -jnp.inf, _F32)
    acc_ref[...] = jnp.zeros(acc_ref.shape, _F32)
    last = (qi * tq) // tk
    row = lax.broadcasted_iota(jnp.int32, (tq, tk), 0)
    col = lax.broadcasted_iota(jnp.int32, (tq, tk), 1)
    score(s_a, last, mask=last * tk + col <= qi * tq + row)

    def tile_in_a(i):
        return jnp.where(i == 0, last, 2 * i - 1)

    def pair(i, c):
        score(s_b, 2 * i)
        absorb(s_a, tile_in_a(i))
        score(s_a, 2 * i + 1)
        absorb(s_b, 2 * i)
        return c

    pairs = last // 2
    lax.fori_loop(0, pairs, pair, 0)

    @pl.when(last % 2 == 1)
    def _():
        score(s_b, last - 1)
        absorb(s_a, tile_in_a(pairs))
        absorb(s_b, last - 1)

    @pl.when(last % 2 == 0)
    def _():
        absorb(s_a, tile_in_a(pairs))

    low = lax.broadcasted_iota(jnp.int32, (tq, LANES), 1) < MLA_V_DIM
    for hp in range(GROUP_HEADS // 2):
        even, odd = acc_ref[2 * hp], acc_ref[2 * hp + 1]
        o_ref[0, :, hp * LANES:(hp + 1) * LANES] = jnp.where(low, even / even[:, MLA_V_DIM:MLA_V_DIM + 1],
                                                             odd / odd[:, :1])


def _mla(qm, km, vm):
    b, s, _ = qm.shape
    tq = MLA_Q_TILE
    gw = GROUP_HEADS * LANES
    return pl.pallas_call(
        _mla_kernel,
        grid=(b, MLA_HEADS // GROUP_HEADS, s // tq),
        in_specs=[pl.BlockSpec((1, tq, gw), lambda bi, g, qi: (bi, qi, g)),
                  pl.BlockSpec((1, s, gw), lambda bi, g, qi: (bi, 0, g)),
                  pl.BlockSpec((1, s, gw), lambda bi, g, qi: (bi, 0, g))],
        out_specs=pl.BlockSpec((1, tq, gw // 2), lambda bi, g, qi: (bi, qi, g)),
        out_shape=jax.ShapeDtypeStruct((b, s, HEAD_PAIRS * LANES), _F32),
        scratch_shapes=[pltpu.VMEM((GROUP_HEADS, tq, MLA_K_TILE), _F32), pltpu.VMEM((GROUP_HEADS, tq, MLA_K_TILE), _F32),
                        pltpu.VMEM((GROUP_HEADS, tq, 1), _F32), pltpu.VMEM((GROUP_HEADS, tq, LANES), _F32)],
        compiler_params=pltpu.CompilerParams(dimension_semantics=("parallel", "parallel", "arbitrary"),
                                             vmem_limit_bytes=VMEM_LIMIT),
        name="mla",
    )(qm, km, vm)


def _sb_kernel(q_ref, k_ref, v_ref, tri_ref, o_ref):
    tile = SB_TILE
    qi = pl.program_id(2)
    row = lax.broadcasted_iota(jnp.int32, (tile, tile), 0)
    col = lax.broadcasted_iota(jnp.int32, (tile, tile), 1)
    strict = col < row
    heads = [slice(h * LANES, (h + 1) * LANES) for h in range(GROUP_HEADS)]

    def key_rows(kt):
        return pl.ds(pl.multiple_of(kt * tile, tile), tile)

    def tile_terms(kt, diagonal):
        tri = tri_ref[...]
        out = []
        for h, sl in enumerate(heads):
            q = q_ref[0, :, (h // 2) * LANES:(h // 2 + 1) * LANES]
            z = _dot_nt(q, k_ref[0, key_rows(kt), sl])
            sp = jnp.maximum(z, 0.0) + jnp.log(1.0 + jnp.exp(-jnp.abs(z)))
            if diagonal:
                sp = jnp.where(strict, sp, 0.0)
            hi = sp.astype(_BF16)
            lo = (sp - hi.astype(_F32)).astype(_BF16)
            out.append((z, _dot(jnp.concatenate([hi, lo], axis=-1), tri)))
        return out

    def absorb(kt, terms, rems, accs, diagonal, live=None):
        new_rems, new_accs = [], []
        for sl, (z, incl), rem, acc in zip(heads, terms, rems, accs):
            p = jnp.exp(z + (rem + incl))
            if diagonal:
                p = jnp.where(strict, p, 0.0)
            step = incl[:, :1]
            if live is not None:
                p = jnp.where(live, p, 0.0)
                step = jnp.where(live, step, 0.0)
            new_accs.append(acc + _dot(p.astype(_BF16), v_ref[0, key_rows(kt), sl]))
            new_rems.append(rem + step)
        return tuple(new_rems), tuple(new_accs)

    def rem_max(rems):
        return functools.reduce(jnp.maximum, [jnp.max(r) for r in rems])

    left = jnp.maximum(qi - 1, 0)
    terms_diag = tile_terms(qi, True)
    terms_left = tile_terms(left, False)
    rems = tuple(jnp.zeros((tile, 1), _F32) for _ in range(GROUP_HEADS))
    accs = tuple(jnp.zeros((tile, LANES), _F32) for _ in range(GROUP_HEADS))
    rems, accs = absorb(qi, terms_diag, rems, accs, True)
    rems, accs = absorb(left, terms_left, rems, accs, False, live=qi > 0)

    def cond(state):
        kt, worst, _, _ = state
        return jnp.logical_and(kt >= 0, worst > SB_LOG_FLOOR)

    def body(state):
        kt, _, rems, accs = state
        rems, accs = absorb(kt, tile_terms(kt, False), rems, accs, False)
        return kt - 1, rem_max(rems), rems, accs

    _, _, _, accs = lax.while_loop(cond, body, (qi - 2, rem_max(rems), rems, accs))
    for hp in range(GROUP_HEADS // 2):
        o_ref[0, :, hp * LANES:(hp + 1) * LANES] = accs[2 * hp] + accs[2 * hp + 1]


def _sb(qs, ks, vs, tri):
    b, s, _ = qs.shape
    tile = SB_TILE
    gw = GROUP_HEADS * LANES
    return pl.pallas_call(
        _sb_kernel,
        grid=(b, SB_HEADS // GROUP_HEADS, s // tile),
        in_specs=[pl.BlockSpec((1, tile, gw // 2), lambda bi, g, qi: (bi, qi, g)),
                  pl.BlockSpec((1, s, gw), lambda bi, g, qi: (bi, 0, g)),
                  pl.BlockSpec((1, s, gw), lambda bi, g, qi: (bi, 0, g)),
                  pl.BlockSpec((2 * tile, tile), lambda bi, g, qi: (0, 0))],
        out_specs=pl.BlockSpec((1, tile, gw // 2), lambda bi, g, qi: (bi, qi, g)),
        out_shape=jax.ShapeDtypeStruct((b, s, HEAD_PAIRS * LANES), _F32),
        compiler_params=pltpu.CompilerParams(dimension_semantics=("parallel", "parallel", "arbitrary"),
                                             vmem_limit_bytes=VMEM_LIMIT),
        name="sb",
    )(qs, ks, vs, tri)


def _post_kernel(om_ref, os_ref, x_ref, g_om_ref, g_os_ref, wo_ref, g_moe_ref, wr_ref, br_ref, ltri_ref,
                 h1_ref, u2_ref, route_ref, counts_ref, seen_ref):
    rows = PROJ_ROWS

    @pl.when(pl.program_id(0) == 0)
    def _():
        seen_ref[...] = jnp.zeros_like(seen_ref)

    part = rows // ROW_PARTS
    lane = lax.broadcasted_iota(jnp.int32, (part, N_EXPERTS), 1).astype(_F32)
    out_lane = lax.broadcasted_iota(jnp.int32, (part, LANES), 1)
    seen = seen_ref[...]
    w_hi = wr_ref[...].astype(_BF16)
    w_lo = (wr_ref[...] - w_hi.astype(_F32)).astype(_BF16)
    for r0 in range(0, rows, part):
        rs = slice(r0, r0 + part)
        mixed = jnp.concatenate([_rms(om_ref[rs, :], g_om_ref[...]), _rms(os_ref[rs, :], g_os_ref[...])], axis=-1)
        h1 = x_ref[rs, :] + _dot(mixed.astype(_BF16), wo_ref[...])
        h1_ref[rs, :] = h1
        u2 = _rms(h1, g_moe_ref[...])
        _store_row_tiles(u2_ref, u2, r0)
        u_hi = u2.astype(_BF16)
        u_lo = (u2 - u_hi.astype(_F32)).astype(_BF16)
        logits = _dot(u_hi, w_hi) + _dot(u_hi, w_lo) + _dot(u_lo, w_hi) + br_ref[...]

        work = logits
        ids, tops = [], []
        onehot = jnp.zeros((part, N_EXPERTS), _F32)
        for _ in range(TOP_K):
            top = jnp.max(work, axis=-1, keepdims=True)
            idx = jnp.min(jnp.where(work == top, lane, float(N_EXPERTS)), axis=-1, keepdims=True)
            hit = lane == idx
            onehot = jnp.where(hit, 1.0, onehot)
            work = jnp.where(hit, -jnp.inf, work)
            ids.append(idx)
            tops.append(top)
        exps = [jnp.exp(tp - tops[0]) for tp in tops]
        denom = exps[0] + exps[1] + exps[2] + exps[3]
        gates = [e / denom for e in exps]

        before = seen + _dot(ltri_ref[...], onehot.astype(_BF16))
        ranks = [jnp.sum(jnp.where(lane == idx, before, 0.0), axis=-1, keepdims=True) for idx in ids]
        seen = seen + jnp.sum(onehot, axis=0, keepdims=True)

        route = jnp.zeros((part, LANES), _F32)
        for j, val in enumerate(ids + gates + ranks):
            route = jnp.where(out_lane == j, val, route)
        route_ref[rs, :] = route
    seen_ref[...] = seen
    counts_ref[...] = seen


def _post(om, os_, x2, g_om, g_os, wo, g_moe, wr, br, ltri):
    t = x2.shape[0]
    rows = PROJ_ROWS

    def full(a):
        return pl.BlockSpec(a.shape, lambda i: (0,) * a.ndim)

    def tok(n):
        return pl.BlockSpec((rows, n), lambda i: (i, 0))

    return pl.pallas_call(
        _post_kernel,
        grid=(t // rows,),
        in_specs=[tok(om.shape[1]), tok(os_.shape[1]), tok(D_MODEL), full(g_om), full(g_os), full(wo), full(g_moe),
                  full(wr), full(br), full(ltri)],
        out_specs=[tok(D_MODEL), pl.BlockSpec((rows * ROW_TILE, LANES), lambda i: (i, 0)), tok(LANES),
                   pl.BlockSpec((1, N_EXPERTS), lambda i: (0, 0))],
        out_shape=[jax.ShapeDtypeStruct((t, D_MODEL), _F32), jax.ShapeDtypeStruct((t * ROW_TILE, LANES), _F32),
                   jax.ShapeDtypeStruct((t, LANES), _F32), jax.ShapeDtypeStruct((1, N_EXPERTS), _F32)],
        scratch_shapes=[pltpu.VMEM((1, N_EXPERTS), _F32)],
        compiler_params=pltpu.CompilerParams(dimension_semantics=("arbitrary",), vmem_limit_bytes=VMEM_LIMIT),
        name="post",
    )(om, os_, x2, g_om, g_os, wo, g_moe, wr, br, ltri)


def _padded_count(cnt_ref, e):
    shift = EXPERT_ROWS.bit_length() - 1
    return lax.shift_left(lax.shift_right_logical(cnt_ref[e] + (EXPERT_ROWS - 1), shift), shift)


def _layout_kernel(cnt_ref, ids_ref, rank_ref, dest_ref, start_ref, be_ref, used_ref):
    shift = EXPERT_ROWS.bit_length() - 1
    nb = be_ref.shape[0]

    def place(e, off):
        padded = _padded_count(cnt_ref, e)
        start_ref[e] = off

        def mark(j, c):
            be_ref[lax.shift_right_logical(off, shift) + j] = e
            return c
        lax.fori_loop(0, lax.shift_right_logical(padded, shift), mark, 0)
        return off + padded
    total = lax.fori_loop(0, N_EXPERTS, place, 0)
    used_ref[0] = lax.shift_right_logical(total, shift)

    def tail(b, c):
        be_ref[b] = N_EXPERTS - 1
        return c
    lax.fori_loop(lax.shift_right_logical(total, shift), nb, tail, 0)

    ids = ids_ref[...]
    dest = rank_ref[...]
    for e in range(N_EXPERTS):
        dest = dest + jnp.where(ids == e, start_ref[e], 0)
    dest_ref[...] = dest


def _layout(counts, ids, rank, n_blocks):
    grid_spec = pltpu.PrefetchScalarGridSpec(
        num_scalar_prefetch=1,
        grid=(1,),
        in_specs=[pl.BlockSpec(ids.shape, lambda i, c: (0, 0)), pl.BlockSpec(rank.shape, lambda i, c: (0, 0))],
        out_specs=[pl.BlockSpec(ids.shape, lambda i, c: (0, 0)), pl.BlockSpec(memory_space=pltpu.SMEM),
                   pl.BlockSpec(memory_space=pltpu.SMEM), pl.BlockSpec(memory_space=pltpu.SMEM)],
    )
    return pl.pallas_call(
        _layout_kernel,
        grid_spec=grid_spec,
        out_shape=[jax.ShapeDtypeStruct(ids.shape, jnp.int32), jax.ShapeDtypeStruct((N_EXPERTS,), jnp.int32),
                   jax.ShapeDtypeStruct((n_blocks,), jnp.int32), jax.ShapeDtypeStruct((1,), jnp.int32)],
        compiler_params=pltpu.CompilerParams(dimension_semantics=("arbitrary",)),
        name="layout",
    )(counts, ids, rank)


def _tile_rows(row):
    return pl.ds(pl.multiple_of(row * ROW_TILE, ROW_TILE), ROW_TILE)


def _for_each_pair(idx, s, body):
    line_tokens = LANES // TOP_K

    def line(g, c):
        for t0 in range(0, line_tokens, DMA_GROUP):
            vals = [idx[s, g, (t0 + t) * TOP_K + k] for t in range(DMA_GROUP) for k in range(TOP_K)]
            for j, row in enumerate(vals):
                body(g * line_tokens + t0 + j // TOP_K, j % TOP_K, row, j % DMA_THREADS)
        return c
    lax.fori_loop(0, SUBLANES, line, 0)


def _dispatch_kernel(cnt_ref, start_ref, dest_hbm, u_hbm, xs_hbm, idx, stage, zrow, zblk, sem_d, sem_l, sem_i, sem_z):
    rows = MOE_ROWS
    shift = EXPERT_ROWS.bit_length() - 1
    i = pl.program_id(0)
    nt = pl.num_programs(0)
    nb = xs_hbm.shape[0] // (EXPERT_ROWS * ROW_TILE)
    slot = lax.rem(i, 2)
    stage_slot = lax.rem(i, STAGE_SLOTS)

    def idx_copy(tile, s):
        return pltpu.make_async_copy(dest_hbm.at[pl.ds(pl.multiple_of(tile * SUBLANES, SUBLANES), SUBLANES)],
                                     idx.at[s], sem_i.at[s])

    def load(tile):
        s = lax.rem(tile, STAGE_SLOTS)
        src = pl.ds(pl.multiple_of(tile * (rows * ROW_TILE), rows * ROW_TILE), rows * ROW_TILE)
        return pltpu.make_async_copy(u_hbm.at[src], stage.at[s], sem_l.at[s])

    def wait_rows(s):
        chunk = pl.ds(0, rows * ROW_TILE)
        for _ in range(TOP_K):
            pltpu.make_async_copy(stage.at[s], xs_hbm.at[chunk], sem_d.at[s]).wait()

    def zero_row(r):
        return pltpu.make_async_copy(zrow, xs_hbm.at[_tile_rows(r)], sem_z)

    def zero_block(b):
        dst = pl.ds(pl.multiple_of(b * (EXPERT_ROWS * ROW_TILE), EXPERT_ROWS * ROW_TILE), EXPERT_ROWS * ROW_TILE)
        return pltpu.make_async_copy(zblk, xs_hbm.at[dst], sem_z)

    def pad_rows(e, fn):
        def one(r, c):
            fn(zero_row(start_ref[e] + r))
            return c
        lax.fori_loop(cnt_ref[e], _padded_count(cnt_ref, e), one, 0)

    def tail_blocks(fn):
        used = start_ref[N_EXPERTS - 1] + _padded_count(cnt_ref, N_EXPERTS - 1)

        def one(b, c):
            fn(zero_block(b))
            return c
        lax.fori_loop(lax.shift_right_logical(used, shift), nb, one, 0)

    @pl.when(i == 0)
    def _():
        idx_copy(0, 0).start()
        load(0).start()

        @pl.when(nt > 1)
        def _():
            load(1).start()
        zrow[...] = jnp.zeros_like(zrow)
        zblk[...] = jnp.zeros_like(zblk)
        for fn in (lambda c: c.start(), lambda c: c.wait()):
            def per_expert(e, c, fn=fn):
                pad_rows(e, fn)
                return c
            lax.fori_loop(0, N_EXPERTS, per_expert, 0)
            tail_blocks(fn)

    idx_copy(i, slot).wait()

    @pl.when(i + 1 < nt)
    def _():
        idx_copy(i + 1, 1 - slot).start()

    @pl.when(i >= 2)
    def _():
        wait_rows(lax.rem(i + 2, STAGE_SLOTS))

    @pl.when(i + 2 < nt)
    def _():
        load(i + 2).start()

    load(i).wait()

    def copy_row(t, k, row, thread):
        del k
        pltpu.make_async_copy(stage.at[stage_slot, _tile_rows(t)], xs_hbm.at[_tile_rows(row)],
                              sem_d.at[stage_slot]).start(priority=thread)
    _for_each_pair(idx, slot, copy_row)

    @pl.when(i == nt - 1)
    def _():
        @pl.when(nt > 1)
        def _():
            wait_rows(lax.rem(i + STAGE_SLOTS - 1, STAGE_SLOTS))
        wait_rows(stage_slot)


def _dispatch(counts, start, dest, u2_tiles, n_blocks):
    nt = dest.shape[0] // SUBLANES
    grid_spec = pltpu.PrefetchScalarGridSpec(
        num_scalar_prefetch=2,
        grid=(nt,),
        in_specs=[pl.BlockSpec(memory_space=pl.ANY), pl.BlockSpec(memory_space=pl.ANY)],
        out_specs=pl.BlockSpec(memory_space=pl.ANY),
        scratch_shapes=[pltpu.SMEM((2, SUBLANES, LANES), jnp.int32),
                        pltpu.VMEM((STAGE_SLOTS, MOE_ROWS * ROW_TILE, LANES), _F32),
                        pltpu.VMEM((ROW_TILE, LANES), _F32), pltpu.VMEM((EXPERT_ROWS * ROW_TILE, LANES), _F32),
                        pltpu.SemaphoreType.DMA((STAGE_SLOTS,)), pltpu.SemaphoreType.DMA((STAGE_SLOTS,)),
                        pltpu.SemaphoreType.DMA((2,)), pltpu.SemaphoreType.DMA(())],
    )
    return pl.pallas_call(
        _dispatch_kernel,
        grid_spec=grid_spec,
        out_shape=jax.ShapeDtypeStruct((n_blocks * EXPERT_ROWS * ROW_TILE, LANES), _F32),
        compiler_params=pltpu.CompilerParams(dimension_semantics=("arbitrary",), vmem_limit_bytes=VMEM_LIMIT),
        name="dispatch",
    )(counts, start, dest, u2_tiles)


def _expert_kernel(be_ref, used_ref, xs_ref, wgu_ref, bgu_ref, wdn_ref, bdn_ref, ys_ref, wgu_bf, wdn_bf):
    rows = EXPERT_ROWS
    i = pl.program_id(0)
    e = be_ref[i]
    prev = be_ref[jnp.maximum(i - 1, 0)]
    in_use = i < used_ref[0]

    @pl.when(jnp.logical_and(in_use, jnp.logical_or(i == 0, e != prev)))
    def _():
        chunk = 128

        def cast(c, carry):
            rs = pl.ds(pl.multiple_of(c * chunk, chunk), chunk)
            wgu_bf[rs, :] = wgu_ref[0, rs, :].astype(_BF16)
            wdn_bf[rs, :] = wdn_ref[0, rs, :].astype(_BF16)
            return carry
        lax.fori_loop(0, D_MODEL // chunk, cast, 0)

    @pl.when(in_use)
    def _():
        x = _load_row_tiles(xs_ref, 0, rows).astype(_BF16)
        hh = _dot(x, wgu_bf[...]) + bgu_ref[0]
        glu = jnp.minimum(hh[:, :D_FF], SWIGLU_LIMIT)
        lin = jnp.clip(hh[:, D_FF:], -SWIGLU_LIMIT, SWIGLU_LIMIT)
        act = glu * jax.nn.sigmoid(SWIGLU_ALPHA * glu) * (lin + 1.0)
        _store_row_tiles(ys_ref, _dot(act.astype(_BF16), wdn_bf[...]) + bdn_ref[0])

    @pl.when(jnp.logical_not(in_use))
    def _():
        ys_ref[...] = jnp.zeros_like(ys_ref)


def _experts(block_e, used, xs, w_gu, b_gu, w_dn, b_dn):
    nb = block_e.shape[0]
    rows = EXPERT_ROWS
    grid_spec = pltpu.PrefetchScalarGridSpec(
        num_scalar_prefetch=2,
        grid=(nb,),
        in_specs=[pl.BlockSpec((rows * ROW_TILE, LANES), lambda i, be, u: (i, 0)),
                  pl.BlockSpec((1, D_MODEL, 2 * D_FF), lambda i, be, u: (be[i], 0, 0)),
                  pl.BlockSpec((1, 1, 2 * D_FF), lambda i, be, u: (be[i], 0, 0)),
                  pl.BlockSpec((1, D_FF, D_MODEL), lambda i, be, u: (be[i], 0, 0)),
                  pl.BlockSpec((1, 1, D_MODEL), lambda i, be, u: (be[i], 0, 0))],
        out_specs=pl.BlockSpec((rows * ROW_TILE, LANES), lambda i, be, u: (i, 0)),
        scratch_shapes=[pltpu.VMEM((D_MODEL, 2 * D_FF), _BF16), pltpu.VMEM((D_FF, D_MODEL), _BF16)],
    )
    return pl.pallas_call(
        _expert_kernel,
        grid_spec=grid_spec,
        out_shape=jax.ShapeDtypeStruct(xs.shape, _F32),
        compiler_params=pltpu.CompilerParams(dimension_semantics=("arbitrary",), vmem_limit_bytes=VMEM_LIMIT),
        name="experts",
    )(block_e, used, xs, w_gu, b_gu.reshape(N_EXPERTS, 1, 2 * D_FF), w_dn, b_dn.reshape(N_EXPERTS, 1, D_MODEL))


def _final_kernel(dest_hbm, ys_hbm, h1_ref, route_ref, p_ref, g_ple_ref, wpg_ref, wpp_ref, g_fin_ref,
                  out_ref, ybuf, idx, sem_y, sem_i, *, last_layer):
    rows = MOE_ROWS
    i = pl.program_id(0)
    nt = pl.num_programs(0)
    slot = lax.rem(i, 2)

    def idx_copy(tile, s):
        return pltpu.make_async_copy(dest_hbm.at[pl.ds(pl.multiple_of(tile * SUBLANES, SUBLANES), SUBLANES)],
                                     idx.at[s], sem_i.at[s])

    def gather(s):
        def copy_row(t, k, row, thread):
            pltpu.make_async_copy(ys_hbm.at[_tile_rows(row)], ybuf.at[s, _tile_rows(k * rows + t)],
                                  sem_y.at[s]).start(priority=thread)
        _for_each_pair(idx, s, copy_row)

    @pl.when(i == 0)
    def _():
        idx_copy(0, 0).start()
        idx_copy(0, 0).wait()
        gather(0)

        @pl.when(nt > 1)
        def _():
            idx_copy(1, 1).start()

    @pl.when(i + 1 < nt)
    def _():
        idx_copy(i + 1, 1 - slot).wait()
        gather(1 - slot)

        @pl.when(i + 2 < nt)
        def _():
            idx_copy(i + 2, slot).start()

    pltpu.make_async_copy(ys_hbm.at[pl.ds(0, TOP_K * rows * ROW_TILE)], ybuf.at[slot], sem_y.at[slot]).wait()
    part = rows // ROW_PARTS
    for r0 in range(0, rows, part):
        rs = slice(r0, r0 + part)
        route = route_ref[rs, :]
        y = jnp.zeros((part, D_MODEL), _F32)
        for k in range(TOP_K):
            y = y + _load_row_tiles(ybuf.at[slot], k * rows + r0, part) * route[:, TOP_K + k:TOP_K + k + 1]
        h2 = h1_ref[rs, :] + y
        u3 = _rms(h2, g_ple_ref[...]).astype(_BF16)
        gate = jax.nn.sigmoid(_dot(u3, wpg_ref[...]))
        h3 = h2 + gate * _dot(p_ref[rs, :].astype(_BF16), wpp_ref[...])
        out_ref[rs, :] = _rms(h3, g_fin_ref[...]) if last_layer else h3


def _final(dest, ys, h1, route, p2, g_ple, wpg, wpp, g_fin, last_layer):
    t = h1.shape[0]
    rows = MOE_ROWS

    def full(a):
        return pl.BlockSpec(a.shape, lambda i: (0,) * a.ndim)

    def tok(n):
        return pl.BlockSpec((rows, n), lambda i: (i, 0))

    return pl.pallas_call(
        functools.partial(_final_kernel, last_layer=last_layer),
        grid=(t // rows,),
        in_specs=[pl.BlockSpec(memory_space=pl.ANY), pl.BlockSpec(memory_space=pl.ANY), tok(D_MODEL), tok(LANES),
                  tok(PLE_DIM), full(g_ple), full(wpg), full(wpp), full(g_fin)],
        out_specs=tok(D_MODEL),
        out_shape=jax.ShapeDtypeStruct((t, D_MODEL), _F32),
        scratch_shapes=[pltpu.VMEM((2, TOP_K * rows * ROW_TILE, LANES), _F32),
                        pltpu.SMEM((2, SUBLANES, LANES), jnp.int32),
                        pltpu.SemaphoreType.DMA((2,)), pltpu.SemaphoreType.DMA((2,))],
        compiler_params=pltpu.CompilerParams(dimension_semantics=("arbitrary",), vmem_limit_bytes=VMEM_LIMIT),
        name="final",
    )(dest, ys, h1, route, p2, g_ple, wpg, wpp, g_fin)


def _rope_pad(w, rot):
    half = MLA_ROPE_DIM // 2
    body = jnp.concatenate([-w[:, half:], w[:, :half]], axis=1) if rot else w
    z = jnp.zeros((w.shape[0], MLA_NOPE_DIM), w.dtype)
    return jnp.concatenate([z, body, jnp.zeros((w.shape[0], LANES - MLA_NOPE_DIM - MLA_ROPE_DIM), w.dtype)], axis=1)


def _layer_weights(w_in, w_uq, w_ukv):
    o = Q_LORA_RANK + KV_LORA_RANK
    w_kr = w_in[:, o:o + MLA_ROPE_DIM]
    w1 = jnp.concatenate([w_in[:, :o], _rope_pad(w_kr, False), _rope_pad(w_kr, True), w_in[:, o + MLA_ROPE_DIM:]],
                         axis=1).astype(_BF16)
    uq = w_uq.reshape(Q_LORA_RANK, MLA_HEADS, MLA_NOPE_DIM + MLA_ROPE_DIM)
    zq = jnp.zeros((Q_LORA_RANK, MLA_HEADS, LANES - MLA_NOPE_DIM - MLA_ROPE_DIM), w_uq.dtype)
    wq = jnp.concatenate([uq, zq], axis=2).reshape(Q_LORA_RANK, MLA_HEADS * LANES).astype(_BF16)
    rope = uq[:, :, MLA_NOPE_DIM:]
    half = MLA_ROPE_DIM // 2
    rot = jnp.concatenate([jnp.zeros_like(uq[:, :, :MLA_NOPE_DIM]), -rope[:, :, half:], rope[:, :, :half], zq], axis=2)
    wqr = rot.reshape(Q_LORA_RANK, MLA_HEADS * LANES).astype(_BF16)
    ukv = w_ukv.reshape(KV_LORA_RANK, MLA_HEADS, MLA_NOPE_DIM + MLA_V_DIM)
    zk = jnp.zeros((KV_LORA_RANK, MLA_HEADS, LANES - MLA_NOPE_DIM), w_ukv.dtype)
    wk = jnp.concatenate([ukv[:, :, :MLA_NOPE_DIM], zk], axis=2).reshape(KV_LORA_RANK, MLA_HEADS * LANES).astype(_BF16)
    v = ukv[:, :, MLA_NOPE_DIM:].reshape(KV_LORA_RANK, HEAD_PAIRS, 2, MLA_V_DIM)
    zv = jnp.zeros((KV_LORA_RANK, HEAD_PAIRS, MLA_V_DIM), w_ukv.dtype)
    wv = jnp.stack([jnp.concatenate([v[:, :, 0], zv], axis=2), jnp.concatenate([zv, v[:, :, 1]], axis=2)], axis=2)
    wv = wv.reshape(KV_LORA_RANK, MLA_HEADS * LANES).astype(_BF16)
    return w1, wq, wqr, wk, wv


def kernel(x, p, positions, w_in, g_attn, g_cq, w_uq, g_ckv, w_ukv, g_out_mla, g_out_sb, w_o, g_moe, w_router,
           b_router, w_gu, b_gu, w_dn, b_dn, g_ple, w_ple_gate, w_ple_proj, g_final):
    b, s, d = x.shape
    t = b * s
    depth = w_in.shape[0]
    assert d == D_MODEL and ROW_TILE == SUBLANES and s % MLA_K_TILE == 0 and t % PROJ_ROWS == 0
    assert t % MOE_ROWS == 0 and (LANES // TOP_K) % DMA_GROUP == 0 and (t * TOP_K) % EXPERT_ROWS == 0
    assert EXPERT_ROWS & (EXPERT_ROWS - 1) == 0

    freq = ROPE_THETA ** (-jnp.arange(0, MLA_ROPE_DIM, 2, dtype=_F32) / MLA_ROPE_DIM)
    invf = jnp.concatenate([jnp.zeros((MLA_NOPE_DIM,), _F32), freq, freq,
                            jnp.zeros((LANES - MLA_NOPE_DIM - MLA_ROPE_DIM,), _F32)]).reshape(1, LANES)
    pos = positions.reshape(t, 1)
    idx = jnp.arange(SB_TILE)
    tri = -(idx[:, None] >= idx[None, :]).astype(_BF16)
    tri = jnp.concatenate([tri, tri], axis=0)
    idx = jnp.arange(PROJ_ROWS // ROW_PARTS)
    ltri = (idx[None, :] < idx[:, None]).astype(_BF16)
    n_blocks = t * TOP_K // EXPERT_ROWS + N_EXPERTS

    h = x.reshape(t, d)
    for i in range(depth):
        w1, wq, wqr, wk, wv = _layer_weights(w_in[i], w_uq[i], w_ukv[i])
        qm, km, vm, qs, ks, vs = _proj(pos, h, g_attn[i].reshape(1, d), w1, g_cq[i].reshape(1, -1), wq, wqr,
                                       g_ckv[i].reshape(1, -1), wk, wv, invf)

        def seq(a):
            return a.reshape(b, s, a.shape[1])

        om = _mla(seq(qm), seq(km), seq(vm)).reshape(t, -1)
        os_ = _sb(seq(qs), seq(ks), seq(vs), tri).reshape(t, -1)
        h1, u2_tiles, route, counts = _post(om, os_, h, g_out_mla[i].reshape(1, -1), g_out_sb[i].reshape(1, -1),
                                            w_o[i].astype(_BF16), g_moe[i].reshape(1, d), w_router[i],
                                            b_router[i].reshape(1, -1), ltri)
        ids = route[:, :TOP_K].astype(jnp.int32).reshape(-1, LANES)
        rank = route[:, 2 * TOP_K:3 * TOP_K].astype(jnp.int32).reshape(-1, LANES)
        counts = counts.reshape(-1).astype(jnp.int32)
        dest, start, block_e, used = _layout(counts, ids, rank, n_blocks)
        xs = _dispatch(counts, start, dest, u2_tiles, n_blocks)
        ys = _experts(block_e, used, xs, w_gu[i], b_gu[i], w_dn[i], b_dn[i])
        h = _final(dest, ys, h1, route, p[i].reshape(t, -1), g_ple[i].reshape(1, d), w_ple_gate[i].astype(_BF16),
                   w_ple_proj[i].astype(_BF16), g_final.reshape(1, d), i == depth - 1)
    return h.reshape(b, s, d)
```

```python
import functools
import math

import jax
import jax.numpy as jnp
from jax import lax
from jax.experimental import pallas as pl
from jax.experimental.pallas import tpu as pltpu

D_MODEL = 1024
PLE_DIM = 256
MLA_HEADS = 8
MLA_NOPE_DIM = 64
MLA_ROPE_DIM = 32
MLA_V_DIM = 64
Q_LORA_RANK = 384
KV_LORA_RANK = 256
SB_HEADS = 8
SB_HEAD_DIM = 64
SB_WIDTH = SB_HEADS * SB_HEAD_DIM
ROPE_THETA = 10000.0
N_EXPERTS = 32
TOP_K = 4
D_FF = 1024
SWIGLU_LIMIT = 7.0
SWIGLU_ALPHA = 1.702
RMS_EPS = 1e-6

LANES = 128
SUBLANES = 8
ROW_TILE = D_MODEL // LANES
HEAD_PAIRS = MLA_HEADS // 2
PROJ_ROWS = 256
GROUP_HEADS = 4
MLA_Q_TILE = 256
MLA_K_TILE = 512
SB_TILE = 256
EXPERT_ROWS = 256
ROW_PARTS = 2
MOE_ROWS = SUBLANES * LANES // TOP_K
DMA_GROUP = 8
STAGE_SLOTS = 4
DMA_THREADS = 2
VMEM_LIMIT = 56 * 1024 * 1024
SB_LOG_FLOOR = -105.0

_F32 = jnp.float32
_BF16 = jnp.bfloat16


def _rms(x, g):
    return x * lax.rsqrt(jnp.mean(x * x, axis=-1, keepdims=True) + RMS_EPS) * g


def _dot(a, b):
    return jnp.dot(a, b, preferred_element_type=_F32)


def _dot_nt(a, b):
    return lax.dot_general(a, b, (((1,), (1,)), ((), ())), preferred_element_type=_F32)


def _store_row_tiles(ref, value, first_row=0):
    rows = value.shape[0]
    for g in range(ROW_TILE):
        ref[pl.ds(first_row * ROW_TILE + g, rows, stride=ROW_TILE), :] = value[:, g * LANES:(g + 1) * LANES]


def _load_row_tiles(ref, first_row, rows):
    return jnp.concatenate([ref[pl.ds(first_row * ROW_TILE + g, rows, stride=ROW_TILE), :] for g in range(ROW_TILE)],
                           axis=-1)


def _proj_kernel(pos_ref, x_ref, g_attn_ref, w1_ref, g_cq_ref, wq_ref, wqr_ref, g_ckv_ref, wk_ref, wv_ref,
                 invf_ref, qm_ref, km_ref, vm_ref, qs_ref, ks_ref, vs_ref):
    u = _rms(x_ref[...], g_attn_ref[...]).astype(_BF16)
    y = _dot(u, w1_ref[...])
    c_q = y[:, :Q_LORA_RANK]
    c_kv = y[:, Q_LORA_RANK:Q_LORA_RANK + KV_LORA_RANK]
    o = Q_LORA_RANK + KV_LORA_RANK
    k_r = y[:, o:o + LANES]
    k_r_rot = y[:, o + LANES:o + 2 * LANES]
    o += 2 * LANES
    q_s = y[:, o:o + SB_WIDTH]
    k_s = y[:, o + SB_WIDTH:o + 2 * SB_WIDTH]
    v_s = y[:, o + 2 * SB_WIDTH:o + 3 * SB_WIDTH]

    ang = pos_ref[...].astype(_F32) * invf_ref[...]
    cos = jnp.cos(ang)
    sin = jnp.sin(ang)

    cq_n = _rms(c_q, g_cq_ref[...]).astype(_BF16)
    q = _dot(cq_n, wq_ref[...])
    q_rot = _dot(cq_n, wqr_ref[...])
    ckv_n = _rms(c_kv, g_ckv_ref[...]).astype(_BF16)
    k = _dot(ckv_n, wk_ref[...])
    wide_lane = lax.broadcasted_iota(jnp.int32, (1, MLA_HEADS * LANES), 1)
    ones_lane = (wide_lane % (2 * LANES) == MLA_V_DIM) | (wide_lane % (2 * LANES) == LANES)
    vm_ref[...] = jnp.where(ones_lane, 1.0, _dot(ckv_n, wv_ref[...])).astype(_BF16)
    k_rope = k_r * cos + k_r_rot * sin

    q_scale = (MLA_NOPE_DIM + MLA_ROPE_DIM) ** -0.5 * math.log2(math.e)
    lane = lax.broadcasted_iota(jnp.int32, (1, LANES), 1)
    low = lane < SB_HEAD_DIM
    for h in range(MLA_HEADS):
        sl = slice(h * LANES, (h + 1) * LANES)
        qm_ref[:, sl] = ((q[:, sl] * cos + q_rot[:, sl] * sin) * q_scale).astype(_BF16)
        km_ref[:, sl] = (k[:, sl] + k_rope).astype(_BF16)
    qs_ref[...] = (q_s * (SB_HEAD_DIM ** -0.5)).astype(_BF16)
    for hp in range(HEAD_PAIRS):
        sl = slice(hp * LANES, (hp + 1) * LANES)
        for half, keep in ((0, low), (1, jnp.logical_not(low))):
            dst = slice((2 * hp + half) * LANES, (2 * hp + half + 1) * LANES)
            ks_ref[:, dst] = jnp.where(keep, k_s[:, sl], 0.0).astype(_BF16)
            vs_ref[:, dst] = jnp.where(keep, v_s[:, sl], 0.0).astype(_BF16)


def _proj(pos, x2, g_attn, w1, g_cq, wq, wqr, g_ckv, wk, wv, invf):
    t = x2.shape[0]
    rows = PROJ_ROWS
    wide = MLA_HEADS * LANES

    def full(a):
        return pl.BlockSpec(a.shape, lambda i: (0,) * a.ndim)

    def tok(n):
        return pl.BlockSpec((rows, n), lambda i: (i, 0))

    outs = [jax.ShapeDtypeStruct((t, n), _BF16) for n in (wide, wide, wide, SB_WIDTH, wide, wide)]
    return pl.pallas_call(
        _proj_kernel,
        grid=(t // rows,),
        in_specs=[tok(1), tok(D_MODEL), full(g_attn), full(w1), full(g_cq), full(wq), full(wqr), full(g_ckv),
                  full(wk), full(wv), full(invf)],
        out_specs=[tok(wide), tok(wide), tok(wide), tok(SB_WIDTH), tok(wide), tok(wide)],
        out_shape=outs,
        compiler_params=pltpu.CompilerParams(dimension_semantics=("parallel",), vmem_limit_bytes=VMEM_LIMIT),
        name="proj",
    )(pos, x2, g_attn, w1, g_cq, wq, wqr, g_ckv, wk, wv, invf)


def _mla_kernel(q_ref, k_ref, v_ref, o_ref, s_a, s_b, m_ref, acc_ref):
    tq, tk = MLA_Q_TILE, MLA_K_TILE
    qi = pl.program_id(2)
    heads = [slice(h * LANES, (h + 1) * LANES) for h in range(GROUP_HEADS)]

    def key_rows(kt):
        return pl.ds(pl.multiple_of(kt * tk, tk), tk)

    def score(dst, kt, mask=None):
        for h, sl in enumerate(heads):
            s = _dot_nt(q_ref[0, :, sl], k_ref[0, key_rows(kt), sl])
            dst[h] = s if mask is None else jnp.where(mask, s, -jnp.inf)

    def absorb(src, kt):
        for h, sl in enumerate(heads):
            s = src[h]
            m = m_ref[h]
            m_new = jnp.maximum(m, jnp.max(s, axis=-1, keepdims=True))
            alpha = jnp.exp2(m - m_new)
            p = jnp.exp2(s - m_new)
            m_ref[h] = m_new
            acc_ref[h] = alpha * acc_ref[h] + _dot(p.astype(_BF16), v_ref[0, key_rows(kt), sl])

    m_ref[...] = jnp.full(m_ref.shape, -jnp.inf, _F32)
    acc_ref[...] = jnp.zeros(acc_ref.shape, _F32)
    last = (qi * tq) // tk
    row = lax.broadcasted_iota(jnp.int32, (tq, tk), 0)
    col = lax.broadcasted_iota(jnp.int32, (tq, tk), 1)
    score(s_a, last, mask=last * tk + col <= qi * tq + row)

    def tile_in_a(i):
        return jnp.where(i == 0, last, 2 * i - 1)

    def pair(i, c):
        score(s_b, 2 * i)
        absorb(s_a, tile_in_a(i))
        score(s_a, 2 * i + 1)
        absorb(s_b, 2 * i)
        return c

    pairs = last // 2
    lax.fori_loop(0, pairs, pair, 0)

    @pl.when(last % 2 == 1)
    def _():
        score(s_b, last - 1)
        absorb(s_a, tile_in_a(pairs))
        absorb(s_b, last - 1)

    @pl.when(last % 2 == 0)
    def _():
        absorb(s_a, tile_in_a(pairs))

    low = lax.broadcasted_iota(jnp.int32, (tq, LANES), 1) < MLA_V_DIM
    for hp in range(GROUP_HEADS // 2):
        even, odd = acc_ref[2 * hp], acc_ref[2 * hp + 1]
        o_ref[0, :, hp * LANES:(hp + 1) * LANES] = jnp.where(low, even / even[:, MLA_V_DIM:MLA_V_DIM + 1],
                                                             odd / odd[:, :1])


def _mla(qm, km, vm):
    b, s, _ = qm.shape
    tq = MLA_Q_TILE
    gw = GROUP_HEADS * LANES
    return pl.pallas_call(
        _mla_kernel,
        grid=(b, MLA_HEADS // GROUP_HEADS, s // tq),
        in_specs=[pl.BlockSpec((1, tq, gw), lambda bi, g, qi: (bi, qi, g)),
                  pl.BlockSpec((1, s, gw), lambda bi, g, qi: (bi, 0, g)),
                  pl.BlockSpec((1, s, gw), lambda bi, g, qi: (bi, 0, g))],
        out_specs=pl.BlockSpec((1, tq, gw // 2), lambda bi, g, qi: (bi, qi, g)),
        out_shape=jax.ShapeDtypeStruct((b, s, HEAD_PAIRS * LANES), _F32),
        scratch_shapes=[pltpu.VMEM((GROUP_HEADS, tq, MLA_K_TILE), _F32), pltpu.VMEM((GROUP_HEADS, tq, MLA_K_TILE), _F32),
                        pltpu.VMEM((GROUP_HEADS, tq, 1), _F32), pltpu.VMEM((GROUP_HEADS, tq, LANES), _F32)],
        compiler_params=pltpu.CompilerParams(dimension_semantics=("parallel", "parallel", "arbitrary"),
                                             vmem_limit_bytes=VMEM_LIMIT),
        name="mla",
    )(qm, km, vm)


def _sb_kernel(q_ref, k_ref, v_ref, tri_ref, o_ref):
    tile = SB_TILE
    qi = pl.program_id(2)
    row = lax.broadcasted_iota(jnp.int32, (tile, tile), 0)
    col = lax.broadcasted_iota(jnp.int32, (tile, tile), 1)
    strict = col < row
    heads = [slice(h * LANES, (h + 1) * LANES) for h in range(GROUP_HEADS)]

    def key_rows(kt):
        return pl.ds(pl.multiple_of(kt * tile, tile), tile)

    def tile_terms(kt, diagonal):
        tri = tri_ref[...]
        out = []
        for h, sl in enumerate(heads):
            q = q_ref[0, :, (h // 2) * LANES:(h // 2 + 1) * LANES]
            z = _dot_nt(q, k_ref[0, key_rows(kt), sl])
            sp = jnp.maximum(z, 0.0) + jnp.log(1.0 + jnp.exp(-jnp.abs(z)))
            if diagonal:
                sp = jnp.where(strict, sp, 0.0)
            hi = sp.astype(_BF16)
            lo = (sp - hi.astype(_F32)).astype(_BF16)
            out.append((z, _dot(jnp.concatenate([hi, lo], axis=-1), tri)))
        return out

    def absorb(kt, terms, rems, accs, diagonal, live=None):
        new_rems, new_accs = [], []
        for sl, (z, incl), rem, acc in zip(heads, terms, rems, accs):
            p = jnp.exp(z + (rem + incl))
            if diagonal:
                p = jnp.where(strict, p, 0.0)
            step = incl[:, :1]
            if live is not None:
                p = jnp.where(live, p, 0.0)
                step = jnp.where(live, step, 0.0)
            new_accs.append(acc + _dot(p.astype(_BF16), v_ref[0, key_rows(kt), sl]))
            new_rems.append(rem + step)
        return tuple(new_rems), tuple(new_accs)

    def rem_max(rems):
        return functools.reduce(jnp.maximum, [jnp.max(r) for r in rems])

    left = jnp.maximum(qi - 1, 0)
    terms_diag = tile_terms(qi, True)
    terms_left = tile_terms(left, False)
    rems = tuple(jnp.zeros((tile, 1), _F32) for _ in range(GROUP_HEADS))
    accs = tuple(jnp.zeros((tile, LANES), _F32) for _ in range(GROUP_HEADS))
    rems, accs = absorb(qi, terms_diag, rems, accs, True)
    rems, accs = absorb(left, terms_left, rems, accs, False, live=qi > 0)

    def cond(state):
        kt, worst, _, _ = state
        return jnp.logical_and(kt >= 0, worst > SB_LOG_FLOOR)

    def body(state):
        kt, _, rems, accs = state
        rems, accs = absorb(kt, tile_terms(kt, False), rems, accs, False)
        return kt - 1, rem_max(rems), rems, accs

    _, _, _, accs = lax.while_loop(cond, body, (qi - 2, rem_max(rems), rems, accs))
    for hp in range(GROUP_HEADS // 2):
        o_ref[0, :, hp * LANES:(hp + 1) * LANES] = accs[2 * hp] + accs[2 * hp + 1]


def _sb(qs, ks, vs, tri):
    b, s, _ = qs.shape
    tile = SB_TILE
    gw = GROUP_HEADS * LANES
    return pl.pallas_call(
        _sb_kernel,
        grid=(b, SB_HEADS // GROUP_HEADS, s // tile),
        in_specs=[pl.BlockSpec((1, tile, gw // 2), lambda bi, g, qi: (bi, qi, g)),
                  pl.BlockSpec((1, s, gw), lambda bi, g, qi: (bi, 0, g)),
                  pl.BlockSpec((1, s, gw), lambda bi, g, qi: (bi, 0, g)),
                  pl.BlockSpec((2 * tile, tile), lambda bi, g, qi: (0, 0))],
        out_specs=pl.BlockSpec((1, tile, gw // 2), lambda bi, g, qi: (bi, qi, g)),
        out_shape=jax.ShapeDtypeStruct((b, s, HEAD_PAIRS * LANES), _F32),
        compiler_params=pltpu.CompilerParams(dimension_semantics=("parallel", "parallel", "arbitrary"),
                                             vmem_limit_bytes=VMEM_LIMIT),
        name="sb",
    )(qs, ks, vs, tri)


def _post_kernel(om_ref, os_ref, x_ref, g_om_ref, g_os_ref, wo_ref, g_moe_ref, wr_ref, br_ref, ltri_ref,
                 h1_ref, u2_ref, route_ref, counts_ref, seen_ref):
    rows = PROJ_ROWS

    @pl.when(pl.program_id(0) == 0)
    def _():
        seen_ref[...] = jnp.zeros_like(seen_ref)

    part = rows // ROW_PARTS
    lane = lax.broadcasted_iota(jnp.int32, (part, N_EXPERTS), 1).astype(_F32)
    out_lane = lax.broadcasted_iota(jnp.int32, (part, LANES), 1)
    seen = seen_ref[...]
    w_hi = wr_ref[...].astype(_BF16)
    w_lo = (wr_ref[...] - w_hi.astype(_F32)).astype(_BF16)
    for r0 in range(0, rows, part):
        rs = slice(r0, r0 + part)
        mixed = jnp.concatenate([_rms(om_ref[rs, :], g_om_ref[...]), _rms(os_ref[rs, :], g_os_ref[...])], axis=-1)
        h1 = x_ref[rs, :] + _dot(mixed.astype(_BF16), wo_ref[...])
        h1_ref[rs, :] = h1
        u2 = _rms(h1, g_moe_ref[...])
        _store_row_tiles(u2_ref, u2, r0)
        u_hi = u2.astype(_BF16)
        u_lo = (u2 - u_hi.astype(_F32)).astype(_BF16)
        logits = _dot(u_hi, w_hi) + _dot(u_hi, w_lo) + _dot(u_lo, w_hi) + br_ref[...]

        work = logits
        ids, tops = [], []
        onehot = jnp.zeros((part, N_EXPERTS), _F32)
        for _ in range(TOP_K):
            top = jnp.max(work, axis=-1, keepdims=True)
            idx = jnp.min(jnp.where(work == top, lane, float(N_EXPERTS)), axis=-1, keepdims=True)
            hit = lane == idx
            onehot = jnp.where(hit, 1.0, onehot)
            work = jnp.where(hit, -jnp.inf, work)
            ids.append(idx)
            tops.append(top)
        exps = [jnp.exp(tp - tops[0]) for tp in tops]
        denom = exps[0] + exps[1] + exps[2] + exps[3]
        gates = [e / denom for e in exps]

        before = seen + _dot(ltri_ref[...], onehot.astype(_BF16))
        ranks = [jnp.sum(jnp.where(lane == idx, before, 0.0), axis=-1, keepdims=True) for idx in ids]
        seen = seen + jnp.sum(onehot, axis=0, keepdims=True)

        route = jnp.zeros((part, LANES), _F32)
        for j, val in enumerate(ids + gates + ranks):
            route = jnp.where(out_lane == j, val, route)
        route_ref[rs, :] = route
    seen_ref[...] = seen
    counts_ref[...] = seen


def _post(om, os_, x2, g_om, g_os, wo, g_moe, wr, br, ltri):
    t = x2.shape[0]
    rows = PROJ_ROWS

    def full(a):
        return pl.BlockSpec(a.shape, lambda i: (0,) * a.ndim)

    def tok(n):
        return pl.BlockSpec((rows, n), lambda i: (i, 0))

    return pl.pallas_call(
        _post_kernel,
        grid=(t // rows,),
        in_specs=[tok(om.shape[1]), tok(os_.shape[1]), tok(D_MODEL), full(g_om), full(g_os), full(wo), full(g_moe),
                  full(wr), full(br), full(ltri)],
        out_specs=[tok(D_MODEL), pl.BlockSpec((rows * ROW_TILE, LANES), lambda i: (i, 0)), tok(LANES),
                   pl.BlockSpec((1, N_EXPERTS), lambda i: (0, 0))],
        out_shape=[jax.ShapeDtypeStruct((t, D_MODEL), _F32), jax.ShapeDtypeStruct((t * ROW_TILE, LANES), _F32),
                   jax.ShapeDtypeStruct((t, LANES), _F32), jax.ShapeDtypeStruct((1, N_EXPERTS), _F32)],
        scratch_shapes=[pltpu.VMEM((1, N_EXPERTS), _F32)],
        compiler_params=pltpu.CompilerParams(dimension_semantics=("arbitrary",), vmem_limit_bytes=VMEM_LIMIT),
        name="post",
    )(om, os_, x2, g_om, g_os, wo, g_moe, wr, br, ltri)


def _padded_count(cnt_ref, e):
    shift = EXPERT_ROWS.bit_length() - 1
    return lax.shift_left(lax.shift_right_logical(cnt_ref[e] + (EXPERT_ROWS - 1), shift), shift)


def _layout_kernel(cnt_ref, ids_ref, rank_ref, dest_ref, start_ref, be_ref, used_ref):
    shift = EXPERT_ROWS.bit_length() - 1
    nb = be_ref.shape[0]

    def place(e, off):
        padded = _padded_count(cnt_ref, e)
        start_ref[e] = off

        def mark(j, c):
            be_ref[lax.shift_right_logical(off, shift) + j] = e
            return c
        lax.fori_loop(0, lax.shift_right_logical(padded, shift), mark, 0)
        return off + padded
    total = lax.fori_loop(0, N_EXPERTS, place, 0)
    used_ref[0] = lax.shift_right_logical(total, shift)

    def tail(b, c):
        be_ref[b] = N_EXPERTS - 1
        return c
    lax.fori_loop(lax.shift_right_logical(total, shift), nb, tail, 0)

    ids = ids_ref[...]
    dest = rank_ref[...]
    for e in range(N_EXPERTS):
        dest = dest + jnp.where(ids == e, start_ref[e], 0)
    dest_ref[...] = dest


def _layout(counts, ids, rank, n_blocks):
    grid_spec = pltpu.PrefetchScalarGridSpec(
        num_scalar_prefetch=1,
        grid=(1,),
        in_specs=[pl.BlockSpec(ids.shape, lambda i, c: (0, 0)), pl.BlockSpec(rank.shape, lambda i, c: (0, 0))],
        out_specs=[pl.BlockSpec(ids.shape, lambda i, c: (0, 0)), pl.BlockSpec(memory_space=pltpu.SMEM),
                   pl.BlockSpec(memory_space=pltpu.SMEM), pl.BlockSpec(memory_space=pltpu.SMEM)],
    )
    return pl.pallas_call(
        _layout_kernel,
        grid_spec=grid_spec,
        out_shape=[jax.ShapeDtypeStruct(ids.shape, jnp.int32), jax.ShapeDtypeStruct((N_EXPERTS,), jnp.int32),
                   jax.ShapeDtypeStruct((n_blocks,), jnp.int32), jax.ShapeDtypeStruct((1,), jnp.int32)],
        compiler_params=pltpu.CompilerParams(dimension_semantics=("arbitrary",)),
        name="layout",
    )(counts, ids, rank)


def _tile_rows(row):
    return pl.ds(pl.multiple_of(row * ROW_TILE, ROW_TILE), ROW_TILE)


def _for_each_pair(idx, s, body):
    line_tokens = LANES // TOP_K

    def line(g, c):
        for t0 in range(0, line_tokens, DMA_GROUP):
            vals = [idx[s, g, (t0 + t) * TOP_K + k] for t in range(DMA_GROUP) for k in range(TOP_K)]
            for j, row in enumerate(vals):
                body(g * line_tokens + t0 + j // TOP_K, j % TOP_K, row, j % DMA_THREADS)
        return c
    lax.fori_loop(0, SUBLANES, line, 0)


def _dispatch_kernel(cnt_ref, start_ref, dest_hbm, u_hbm, xs_hbm, idx, stage, zrow, zblk, sem_d, sem_l, sem_i, sem_z):
    rows = MOE_ROWS
    shift = EXPERT_ROWS.bit_length() - 1
    i = pl.program_id(0)
    nt = pl.num_programs(0)
    nb = xs_hbm.shape[0] // (EXPERT_ROWS * ROW_TILE)
    slot = lax.rem(i, 2)
    stage_slot = lax.rem(i, STAGE_SLOTS)

    def idx_copy(tile, s):
        return pltpu.make_async_copy(dest_hbm.at[pl.ds(pl.multiple_of(tile * SUBLANES, SUBLANES), SUBLANES)],
                                     idx.at[s], sem_i.at[s])

    def load(tile):
        s = lax.rem(tile, STAGE_SLOTS)
        src = pl.ds(pl.multiple_of(tile * (rows * ROW_TILE), rows * ROW_TILE), rows * ROW_TILE)
        return pltpu.make_async_copy(u_hbm.at[src], stage.at[s], sem_l.at[s])

    def wait_rows(s):
        chunk = pl.ds(0, rows * ROW_TILE)
        for _ in range(TOP_K):
            pltpu.make_async_copy(stage.at[s], xs_hbm.at[chunk], sem_d.at[s]).wait()

    def zero_row(r):
        return pltpu.make_async_copy(zrow, xs_hbm.at[_tile_rows(r)], sem_z)

    def zero_block(b):
        dst = pl.ds(pl.multiple_of(b * (EXPERT_ROWS * ROW_TILE), EXPERT_ROWS * ROW_TILE), EXPERT_ROWS * ROW_TILE)
        return pltpu.make_async_copy(zblk, xs_hbm.at[dst], sem_z)

    def pad_rows(e, fn):
        def one(r, c):
            fn(zero_row(start_ref[e] + r))
            return c
        lax.fori_loop(cnt_ref[e], _padded_count(cnt_ref, e), one, 0)

    def tail_blocks(fn):
        used = start_ref[N_EXPERTS - 1] + _padded_count(cnt_ref, N_EXPERTS - 1)

        def one(b, c):
            fn(zero_block(b))
            return c
        lax.fori_loop(lax.shift_right_logical(used, shift), nb, one, 0)

    @pl.when(i == 0)
    def _():
        idx_copy(0, 0).start()
        load(0).start()

        @pl.when(nt > 1)
        def _():
            load(1).start()
        zrow[...] = jnp.zeros_like(zrow)
        zblk[...] = jnp.zeros_like(zblk)
        for fn in (lambda c: c.start(), lambda c: c.wait()):
            def per_expert(e, c, fn=fn):
                pad_rows(e, fn)
                return c
            lax.fori_loop(0, N_EXPERTS, per_expert, 0)
            tail_blocks(fn)

    idx_copy(i, slot).wait()

    @pl.when(i + 1 < nt)
    def _():
        idx_copy(i + 1, 1 - slot).start()

    @pl.when(i >= 2)
    def _():
        wait_rows(lax.rem(i + 2, STAGE_SLOTS))

    @pl.when(i + 2 < nt)
    def _():
        load(i + 2).start()

    load(i).wait()

    def copy_row(t, k, row, thread):
        del k
        pltpu.make_async_copy(stage.at[stage_slot, _tile_rows(t)], xs_hbm.at[_tile_rows(row)],
                              sem_d.at[stage_slot]).start(priority=thread)
    _for_each_pair(idx, slot, copy_row)

    @pl.when(i == nt - 1)
    def _():
        @pl.when(nt > 1)
        def _():
            wait_rows(lax.rem(i + STAGE_SLOTS - 1, STAGE_SLOTS))
        wait_rows(stage_slot)


def _dispatch(counts, start, dest, u2_tiles, n_blocks):
    nt = dest.shape[0] // SUBLANES
    grid_spec = pltpu.PrefetchScalarGridSpec(
        num_scalar_prefetch=2,
        grid=(nt,),
        in_specs=[pl.BlockSpec(memory_space=pl.ANY), pl.BlockSpec(memory_space=pl.ANY)],
        out_specs=pl.BlockSpec(memory_space=pl.ANY),
        scratch_shapes=[pltpu.SMEM((2, SUBLANES, LANES), jnp.int32),
                        pltpu.VMEM((STAGE_SLOTS, MOE_ROWS * ROW_TILE, LANES), _F32),
                        pltpu.VMEM((ROW_TILE, LANES), _F32), pltpu.VMEM((EXPERT_ROWS * ROW_TILE, LANES), _F32),
                        pltpu.SemaphoreType.DMA((STAGE_SLOTS,)), pltpu.SemaphoreType.DMA((STAGE_SLOTS,)),
                        pltpu.SemaphoreType.DMA((2,)), pltpu.SemaphoreType.DMA(())],
    )
    return pl.pallas_call(
        _dispatch_kernel,
        grid_spec=grid_spec,
        out_shape=jax.ShapeDtypeStruct((n_blocks * EXPERT_ROWS * ROW_TILE, LANES), _F32),
        compiler_params=pltpu.CompilerParams(dimension_semantics=("arbitrary",), vmem_limit_bytes=VMEM_LIMIT),
        name="dispatch",
    )(counts, start, dest, u2_tiles)


def _expert_kernel(be_ref, used_ref, xs_ref, wgu_hbm, bgu_ref, wdn_hbm, bdn_ref, ys_ref,
                   wgu_f32, wdn_f32, wgu_bf, wdn_bf, sem_w):
    rows = EXPERT_ROWS
    i = pl.program_id(0)
    nb = pl.num_programs(0)
    used = used_ref[0]
    e = be_ref[i]
    prev = be_ref[jnp.maximum(i - 1, 0)]
    in_use = i < used

    def fetch(expert):
        return (pltpu.make_async_copy(wgu_hbm.at[expert], wgu_f32, sem_w.at[0]),
                pltpu.make_async_copy(wdn_hbm.at[expert], wdn_f32, sem_w.at[1]))

    @pl.when(i == 0)
    def _():
        for c in fetch(e):
            c.start()

    @pl.when(jnp.logical_and(in_use, jnp.logical_or(i == 0, e != prev)))
    def _():
        for c in fetch(e):
            c.wait()
        chunk = 128

        def cast(c, carry):
            rs = pl.ds(pl.multiple_of(c * chunk, chunk), chunk)
            wgu_bf[rs, :] = wgu_f32[rs, :].astype(_BF16)
            wdn_bf[rs, :] = wdn_f32[rs, :].astype(_BF16)
            return carry
        lax.fori_loop(0, D_MODEL // chunk, cast, 0)

        def same_expert(j):
            return jnp.logical_and(j < used, be_ref[jnp.minimum(j, nb - 1)] == e)
        nxt = lax.while_loop(same_expert, lambda j: j + 1, i)

        @pl.when(nxt < used)
        def _():
            for c in fetch(be_ref[jnp.minimum(nxt, nb - 1)]):
                c.start()

    @pl.when(in_use)
    def _():
        x = _load_row_tiles(xs_ref, 0, rows).astype(_BF16)
        hh = _dot(x, wgu_bf[...]) + bgu_ref[0]
        glu = jnp.minimum(hh[:, :D_FF], SWIGLU_LIMIT)
        lin = jnp.clip(hh[:, D_FF:], -SWIGLU_LIMIT, SWIGLU_LIMIT)
        act = glu * jax.nn.sigmoid(SWIGLU_ALPHA * glu) * (lin + 1.0)
        _store_row_tiles(ys_ref, _dot(act.astype(_BF16), wdn_bf[...]) + bdn_ref[0])

    @pl.when(jnp.logical_not(in_use))
    def _():
        ys_ref[...] = jnp.zeros_like(ys_ref)


def _experts(block_e, used, xs, w_gu, b_gu, w_dn, b_dn):
    nb = block_e.shape[0]
    rows = EXPERT_ROWS
    grid_spec = pltpu.PrefetchScalarGridSpec(
        num_scalar_prefetch=2,
        grid=(nb,),
        in_specs=[pl.BlockSpec((rows * ROW_TILE, LANES), lambda i, be, u: (i, 0)),
                  pl.BlockSpec(memory_space=pl.ANY),
                  pl.BlockSpec((1, 1, 2 * D_FF), lambda i, be, u: (be[i], 0, 0)),
                  pl.BlockSpec(memory_space=pl.ANY),
                  pl.BlockSpec((1, 1, D_MODEL), lambda i, be, u: (be[i], 0, 0))],
        out_specs=pl.BlockSpec((rows * ROW_TILE, LANES), lambda i, be, u: (i, 0)),
        scratch_shapes=[pltpu.VMEM((D_MODEL, 2 * D_FF), _F32), pltpu.VMEM((D_FF, D_MODEL), _F32),
                        pltpu.VMEM((D_MODEL, 2 * D_FF), _BF16), pltpu.VMEM((D_FF, D_MODEL), _BF16),
                        pltpu.SemaphoreType.DMA((2,))],
    )
    return pl.pallas_call(
        _expert_kernel,
        grid_spec=grid_spec,
        out_shape=jax.ShapeDtypeStruct(xs.shape, _F32),
        compiler_params=pltpu.CompilerParams(dimension_semantics=("arbitrary",), vmem_limit_bytes=VMEM_LIMIT),
        name="experts",
    )(block_e, used, xs, w_gu, b_gu.reshape(N_EXPERTS, 1, 2 * D_FF), w_dn, b_dn.reshape(N_EXPERTS, 1, D_MODEL))


def _final_kernel(dest_hbm, ys_hbm, h1_ref, route_ref, p_ref, g_ple_ref, wpg_ref, wpp_ref, g_fin_ref,
                  out_ref, ybuf, idx, sem_y, sem_i, *, last_layer):
    rows = MOE_ROWS
    i = pl.program_id(0)
    nt = pl.num_programs(0)
    slot = lax.rem(i, 2)

    def idx_copy(tile, s):
        return pltpu.make_async_copy(dest_hbm.at[pl.ds(pl.multiple_of(tile * SUBLANES, SUBLANES), SUBLANES)],
                                     idx.at[s], sem_i.at[s])

    def gather(s):
        def copy_row(t, k, row, thread):
            pltpu.make_async_copy(ys_hbm.at[_tile_rows(row)], ybuf.at[s, _tile_rows(k * rows + t)],
                                  sem_y.at[s]).start(priority=thread)
        _for_each_pair(idx, s, copy_row)

    @pl.when(i == 0)
    def _():
        idx_copy(0, 0).start()
        idx_copy(0, 0).wait()
        gather(0)

        @pl.when(nt > 1)
        def _():
            idx_copy(1, 1).start()

    @pl.when(i + 1 < nt)
    def _():
        idx_copy(i + 1, 1 - slot).wait()
        gather(1 - slot)

        @pl.when(i + 2 < nt)
        def _():
            idx_copy(i + 2, slot).start()

    pltpu.make_async_copy(ys_hbm.at[pl.ds(0, TOP_K * rows * ROW_TILE)], ybuf.at[slot], sem_y.at[slot]).wait()
    part = rows // ROW_PARTS
    for r0 in range(0, rows, part):
        rs = slice(r0, r0 + part)
        route = route_ref[rs, :]
        y = jnp.zeros((part, D_MODEL), _F32)
        for k in range(TOP_K):
            y = y + _load_row_tiles(ybuf.at[slot], k * rows + r0, part) * route[:, TOP_K + k:TOP_K + k + 1]
        h2 = h1_ref[rs, :] + y
        u3 = _rms(h2, g_ple_ref[...]).astype(_BF16)
        gate = jax.nn.sigmoid(_dot(u3, wpg_ref[...]))
        h3 = h2 + gate * _dot(p_ref[rs, :].astype(_BF16), wpp_ref[...])
        out_ref[rs, :] = _rms(h3, g_fin_ref[...]) if last_layer else h3


def _final(dest, ys, h1, route, p2, g_ple, wpg, wpp, g_fin, last_layer):
    t = h1.shape[0]
    rows = MOE_ROWS

    def full(a):
        return pl.BlockSpec(a.shape, lambda i: (0,) * a.ndim)

    def tok(n):
        return pl.BlockSpec((rows, n), lambda i: (i, 0))

    return pl.pallas_call(
        functools.partial(_final_kernel, last_layer=last_layer),
        grid=(t // rows,),
        in_specs=[pl.BlockSpec(memory_space=pl.ANY), pl.BlockSpec(memory_space=pl.ANY), tok(D_MODEL), tok(LANES),
                  tok(PLE_DIM), full(g_ple), full(wpg), full(wpp), full(g_fin)],
        out_specs=tok(D_MODEL),
        out_shape=jax.ShapeDtypeStruct((t, D_MODEL), _F32),
        scratch_shapes=[pltpu.VMEM((2, TOP_K * rows * ROW_TILE, LANES), _F32),
                        pltpu.SMEM((2, SUBLANES, LANES), jnp.int32),
                        pltpu.SemaphoreType.DMA((2,)), pltpu.SemaphoreType.DMA((2,))],
        compiler_params=pltpu.CompilerParams(dimension_semantics=("arbitrary",), vmem_limit_bytes=VMEM_LIMIT),
        name="final",
    )(dest, ys, h1, route, p2, g_ple, wpg, wpp, g_fin)


def _rope_pad(w, rot):
    half = MLA_ROPE_DIM // 2
    body = jnp.concatenate([-w[:, half:], w[:, :half]], axis=1) if rot else w
    z = jnp.zeros((w.shape[0], MLA_NOPE_DIM), w.dtype)
    return jnp.concatenate([z, body, jnp.zeros((w.shape[0], LANES - MLA_NOPE_DIM - MLA_ROPE_DIM), w.dtype)], axis=1)


def _layer_weights(w_in, w_uq, w_ukv):
    o = Q_LORA_RANK + KV_LORA_RANK
    w_kr = w_in[:, o:o + MLA_ROPE_DIM]
    w1 = jnp.concatenate([w_in[:, :o], _rope_pad(w_kr, False), _rope_pad(w_kr, True), w_in[:, o + MLA_ROPE_DIM:]],
                         axis=1).astype(_BF16)
    uq = w_uq.reshape(Q_LORA_RANK, MLA_HEADS, MLA_NOPE_DIM + MLA_ROPE_DIM)
    zq = jnp.zeros((Q_LORA_RANK, MLA_HEADS, LANES - MLA_NOPE_DIM - MLA_ROPE_DIM), w_uq.dtype)
    wq = jnp.concatenate([uq, zq], axis=2).reshape(Q_LORA_RANK, MLA_HEADS * LANES).astype(_BF16)
    rope = uq[:, :, MLA_NOPE_DIM:]
    half = MLA_ROPE_DIM // 2
    rot = jnp.concatenate([jnp.zeros_like(uq[:, :, :MLA_NOPE_DIM]), -rope[:, :, half:], rope[:, :, :half], zq], axis=2)
    wqr = rot.reshape(Q_LORA_RANK, MLA_HEADS * LANES).astype(_BF16)
    ukv = w_ukv.reshape(KV_LORA_RANK, MLA_HEADS, MLA_NOPE_DIM + MLA_V_DIM)
    zk = jnp.zeros((KV_LORA_RANK, MLA_HEADS, LANES - MLA_NOPE_DIM), w_ukv.dtype)
    wk = jnp.concatenate([ukv[:, :, :MLA_NOPE_DIM], zk], axis=2).reshape(KV_LORA_RANK, MLA_HEADS * LANES).astype(_BF16)
    v = ukv[:, :, MLA_NOPE_DIM:].reshape(KV_LORA_RANK, HEAD_PAIRS, 2, MLA_V_DIM)
    zv = jnp.zeros((KV_LORA_RANK, HEAD_PAIRS, MLA_V_DIM), w_ukv.dtype)
    wv = jnp.stack([jnp.concatenate([v[:, :, 0], zv], axis=2), jnp.concatenate([zv, v[:, :, 1]], axis=2)], axis=2)
    wv = wv.reshape(KV_LORA_RANK, MLA_HEADS * LANES).astype(_BF16)
    return w1, wq, wqr, wk, wv


def kernel(x, p, positions, w_in, g_attn, g_cq, w_uq, g_ckv, w_ukv, g_out_mla, g_out_sb, w_o, g_moe, w_router,
           b_router, w_gu, b_gu, w_dn, b_dn, g_ple, w_ple_gate, w_ple_proj, g_final):
    b, s, d = x.shape
    t = b * s
    depth = w_in.shape[0]
    assert d == D_MODEL and ROW_TILE == SUBLANES and s % MLA_K_TILE == 0 and t % PROJ_ROWS == 0
    assert t % MOE_ROWS == 0 and (LANES // TOP_K) % DMA_GROUP == 0 and (t * TOP_K) % EXPERT_ROWS == 0
    assert EXPERT_ROWS & (EXPERT_ROWS - 1) == 0

    freq = ROPE_THETA ** (-jnp.arange(0, MLA_ROPE_DIM, 2, dtype=_F32) / MLA_ROPE_DIM)
    invf = jnp.concatenate([jnp.zeros((MLA_NOPE_DIM,), _F32), freq, freq,
                            jnp.zeros((LANES - MLA_NOPE_DIM - MLA_ROPE_DIM,), _F32)]).reshape(1, LANES)
    pos = positions.reshape(t, 1)
    idx = jnp.arange(SB_TILE)
    tri = -(idx[:, None] >= idx[None, :]).astype(_BF16)
    tri = jnp.concatenate([tri, tri], axis=0)
    idx = jnp.arange(PROJ_ROWS // ROW_PARTS)
    ltri = (idx[None, :] < idx[:, None]).astype(_BF16)
    n_blocks = t * TOP_K // EXPERT_ROWS + N_EXPERTS

    h = x.reshape(t, d)
    for i in range(depth):
        w1, wq, wqr, wk, wv = _layer_weights(w_in[i], w_uq[i], w_ukv[i])
        qm, km, vm, qs, ks, vs = _proj(pos, h, g_attn[i].reshape(1, d), w1, g_cq[i].reshape(1, -1), wq, wqr,
                                       g_ckv[i].reshape(1, -1), wk, wv, invf)

        def seq(a):
            return a.reshape(b, s, a.shape[1])

        om = _mla(seq(qm), seq(km), seq(vm)).reshape(t, -1)
        os_ = _sb(seq(qs), seq(ks), seq(vs), tri).reshape(t, -1)
        h1, u2_tiles, route, counts = _post(om, os_, h, g_out_mla[i].reshape(1, -1), g_out_sb[i].reshape(1, -1),
                                            w_o[i].astype(_BF16), g_moe[i].reshape(1, d), w_router[i],
                                            b_router[i].reshape(1, -1), ltri)
        ids = route[:, :TOP_K].astype(jnp.int32).reshape(-1, LANES)
        rank = route[:, 2 * TOP_K:3 * TOP_K].astype(jnp.int32).reshape(-1, LANES)
        counts = counts.reshape(-1).astype(jnp.int32)
        dest, start, block_e, used = _layout(counts, ids, rank, n_blocks)
        xs = _dispatch(counts, start, dest, u2_tiles, n_blocks)
        ys = _experts(block_e, used, xs, w_gu[i], b_gu[i], w_dn[i], b_dn[i])
        h = _final(dest, ys, h1, route, p[i].reshape(t, -1), g_ple[i].reshape(1, d), w_ple_gate[i].astype(_BF16),
                   w_ple_proj[i].astype(_BF16), g_final.reshape(1, d), i == depth - 1)
    return h.reshape(b, s, d)
```

```python
import functools
import math

import jax
import jax.numpy as jnp
from jax import lax
from jax.experimental import pallas as pl
from jax.experimental.pallas import tpu as pltpu

D_MODEL = 1024
PLE_DIM = 256
MLA_HEADS = 8
MLA_NOPE_DIM = 64
MLA_ROPE_DIM = 32
MLA_V_DIM = 64
Q_LORA_RANK = 384
KV_LORA_RANK = 256
SB_HEADS = 8
SB_HEAD_DIM = 64
SB_WIDTH = SB_HEADS * SB_HEAD_DIM
ROPE_THETA = 10000.0
N_EXPERTS = 32
TOP_K = 4
D_FF = 1024
SWIGLU_LIMIT = 7.0
SWIGLU_ALPHA = 1.702
RMS_EPS = 1e-6

LANES = 128
SUBLANES = 8
ROW_TILE = D_MODEL // LANES
HEAD_PAIRS = MLA_HEADS // 2
PROJ_ROWS = 256
GROUP_HEADS = 4
SB_GROUP_HEADS = 4
MLA_Q_TILE = 256
MLA_K_TILE = 512
SB_TILE = 256
EXPERT_ROWS = 256
ROW_PARTS = 2
POST_PARTS = 4
MOE_ROWS = SUBLANES * LANES // TOP_K
DMA_GROUP = 8
STAGE_SLOTS = 4
DMA_THREADS = 2
VMEM_LIMIT = 56 * 1024 * 1024
SB_LOG_FLOOR = -105.0

_F32 = jnp.float32
_BF16 = jnp.bfloat16


def _rms(x, g):
    return x * lax.rsqrt(jnp.mean(x * x, axis=-1, keepdims=True) + RMS_EPS) * g


def _dot(a, b):
    return jnp.dot(a, b, preferred_element_type=_F32)


def _dot_nt(a, b):
    return lax.dot_general(a, b, (((1,), (1,)), ((), ())), preferred_element_type=_F32)


def _store_row_tiles(ref, value, first_row=0):
    rows = value.shape[0]
    for g in range(ROW_TILE):
        ref[pl.ds(first_row * ROW_TILE + g, rows, stride=ROW_TILE), :] = value[:, g * LANES:(g + 1) * LANES]


def _load_row_tiles(ref, first_row, rows):
    return jnp.concatenate([ref[pl.ds(first_row * ROW_TILE + g, rows, stride=ROW_TILE), :] for g in range(ROW_TILE)],
                           axis=-1)


def _proj_kernel(pos_ref, x_ref, g_attn_ref, w1_ref, g_cq_ref, wq_ref, wqr_ref, g_ckv_ref, wk_ref, wv_ref,
                 invf_ref, qm_ref, km_ref, vm_ref, qs_ref, ks_ref, vs_ref):
    u = _rms(x_ref[...], g_attn_ref[...]).astype(_BF16)
    y = _dot(u, w1_ref[...])
    c_q = y[:, :Q_LORA_RANK]
    c_kv = y[:, Q_LORA_RANK:Q_LORA_RANK + KV_LORA_RANK]
    o = Q_LORA_RANK + KV_LORA_RANK
    k_r = y[:, o:o + LANES]
    k_r_rot = y[:, o + LANES:o + 2 * LANES]
    o += 2 * LANES
    q_s = y[:, o:o + SB_WIDTH]
    k_s = y[:, o + SB_WIDTH:o + 2 * SB_WIDTH]
    v_s = y[:, o + 2 * SB_WIDTH:o + 3 * SB_WIDTH]

    ang = pos_ref[...].astype(_F32) * invf_ref[...]
    cos = jnp.cos(ang)
    sin = jnp.sin(ang)

    cq_n = _rms(c_q, g_cq_ref[...]).astype(_BF16)
    q = _dot(cq_n, wq_ref[...])
    q_rot = _dot(cq_n, wqr_ref[...])
    ckv_n = _rms(c_kv, g_ckv_ref[...]).astype(_BF16)
    k = _dot(ckv_n, wk_ref[...])
    wide_lane = lax.broadcasted_iota(jnp.int32, (1, MLA_HEADS * LANES), 1)
    ones_lane = (wide_lane % (2 * LANES) == MLA_V_DIM) | (wide_lane % (2 * LANES) == LANES)
    vm_ref[...] = jnp.where(ones_lane, 1.0, _dot(ckv_n, wv_ref[...])).astype(_BF16)
    k_rope = k_r * cos + k_r_rot * sin

    q_scale = (MLA_NOPE_DIM + MLA_ROPE_DIM) ** -0.5 * math.log2(math.e)
    lane = lax.broadcasted_iota(jnp.int32, (1, LANES), 1)
    low = lane < SB_HEAD_DIM
    for h in range(MLA_HEADS):
        sl = slice(h * LANES, (h + 1) * LANES)
        qm_ref[:, sl] = ((q[:, sl] * cos + q_rot[:, sl] * sin) * q_scale).astype(_BF16)
        km_ref[:, sl] = (k[:, sl] + k_rope).astype(_BF16)
    qs_ref[...] = (q_s * (SB_HEAD_DIM ** -0.5)).astype(_BF16)
    for hp in range(HEAD_PAIRS):
        sl = slice(hp * LANES, (hp + 1) * LANES)
        for half, keep in ((0, low), (1, jnp.logical_not(low))):
            dst = slice((2 * hp + half) * LANES, (2 * hp + half + 1) * LANES)
            ks_ref[:, dst] = jnp.where(keep, k_s[:, sl], 0.0).astype(_BF16)
            vs_ref[:, dst] = jnp.where(keep, v_s[:, sl], 0.0).astype(_BF16)


def _proj(pos, x2, g_attn, w1, g_cq, wq, wqr, g_ckv, wk, wv, invf):
    t = x2.shape[0]
    rows = PROJ_ROWS
    wide = MLA_HEADS * LANES

    def full(a):
        return pl.BlockSpec(a.shape, lambda i: (0,) * a.ndim)

    def tok(n):
        return pl.BlockSpec((rows, n), lambda i: (i, 0))

    outs = [jax.ShapeDtypeStruct((t, n), _BF16) for n in (wide, wide, wide, SB_WIDTH, wide, wide)]
    return pl.pallas_call(
        _proj_kernel,
        grid=(t // rows,),
        in_specs=[tok(1), tok(D_MODEL), full(g_attn), full(w1), full(g_cq), full(wq), full(wqr), full(g_ckv),
                  full(wk), full(wv), full(invf)],
        out_specs=[tok(wide), tok(wide), tok(wide), tok(SB_WIDTH), tok(wide), tok(wide)],
        out_shape=outs,
        compiler_params=pltpu.CompilerParams(dimension_semantics=("parallel",), vmem_limit_bytes=VMEM_LIMIT),
        name="proj",
    )(pos, x2, g_attn, w1, g_cq, wq, wqr, g_ckv, wk, wv, invf)


def _mla_kernel(q_ref, k_ref, v_ref, o_ref, s_a, s_b, m_ref, acc_ref):
    tq, tk = MLA_Q_TILE, MLA_K_TILE
    qi = pl.program_id(2)
    heads = [slice(h * LANES, (h + 1) * LANES) for h in range(GROUP_HEADS)]

    def key_rows(kt):
        return pl.ds(pl.multiple_of(kt * tk, tk), tk)

    def score(dst, kt, mask=None):
        for h, sl in enumerate(heads):
            s = _dot_nt(q_ref[0, :, sl], k_ref[0, key_rows(kt), sl])
            dst[h] = s if mask is None else jnp.where(mask, s, -jnp.inf)

    def absorb(src, kt):
        for h, sl in enumerate(heads):
            s = src[h]
            m = m_ref[h]
            m_new = jnp.maximum(m, jnp.max(s, axis=-1, keepdims=True))
            alpha = jnp.exp2(m - m_new)
            p = jnp.exp2(s - m_new)
            m_ref[h] = m_new
            acc_ref[h] = alpha * acc_ref[h] + _dot(p.astype(_BF16), v_ref[0, key_rows(kt), sl])

    m_ref[...] = jnp.full(m_ref.shape, -jnp.inf, _F32)
    acc_ref[...] = jnp.zeros(acc_ref.shape, _F32)
    last = (qi * tq) // tk
    row = lax.broadcasted_iota(jnp.int32, (tq, tk), 0)
    col = lax.broadcasted_iota(jnp.int32, (tq, tk), 1)
    score(s_a, last, mask=last * tk + col <= qi * tq + row)

    def tile_in_a(i):
        return jnp.where(i == 0, last, 2 * i - 1)

    def pair(i, c):
        score(s_b, 2 * i)
        absorb(s_a, tile_in_a(i))
        score(s_a, 2 * i + 1)
        absorb(s_b, 2 * i)
        return c

    pairs = last // 2
    lax.fori_loop(0, pairs, pair, 0)

    @pl.when(last % 2 == 1)
    def _():
        score(s_b, last - 1)
        absorb(s_a, tile_in_a(pairs))
        absorb(s_b, last - 1)

    @pl.when(last % 2 == 0)
    def _():
        absorb(s_a, tile_in_a(pairs))

    low = lax.broadcasted_iota(jnp.int32, (tq, LANES), 1) < MLA_V_DIM
    for hp in range(GROUP_HEADS // 2):
        even, odd = acc_ref[2 * hp], acc_ref[2 * hp + 1]
        o_ref[0, :, hp * LANES:(hp + 1) * LANES] = jnp.where(low, even / even[:, MLA_V_DIM:MLA_V_DIM + 1],
                                                             odd / odd[:, :1])


def _mla(qm, km, vm):
    b, s, _ = qm.shape
    tq = MLA_Q_TILE
    gw = GROUP_HEADS * LANES
    return pl.pallas_call(
        _mla_kernel,
        grid=(b, MLA_HEADS // GROUP_HEADS, s // tq),
        in_specs=[pl.BlockSpec((1, tq, gw), lambda bi, g, qi: (bi, qi, g)),
                  pl.BlockSpec((1, s, gw), lambda bi, g, qi: (bi, 0, g)),
                  pl.BlockSpec((1, s, gw), lambda bi, g, qi: (bi, 0, g))],
        out_specs=pl.BlockSpec((1, tq, gw // 2), lambda bi, g, qi: (bi, qi, g)),
        out_shape=jax.ShapeDtypeStruct((b, s, HEAD_PAIRS * LANES), _F32),
        scratch_shapes=[pltpu.VMEM((GROUP_HEADS, tq, MLA_K_TILE), _F32), pltpu.VMEM((GROUP_HEADS, tq, MLA_K_TILE), _F32),
                        pltpu.VMEM((GROUP_HEADS, tq, 1), _F32), pltpu.VMEM((GROUP_HEADS, tq, LANES), _F32)],
        compiler_params=pltpu.CompilerParams(dimension_semantics=("parallel", "parallel", "arbitrary"),
                                             vmem_limit_bytes=VMEM_LIMIT),
        name="mla",
    )(qm, km, vm)


def _sb_kernel(q_ref, k_ref, v_ref, tri_ref, o_ref):
    tile = SB_TILE
    qi = pl.program_id(2)
    row = lax.broadcasted_iota(jnp.int32, (tile, tile), 0)
    col = lax.broadcasted_iota(jnp.int32, (tile, tile), 1)
    strict = col < row
    heads = [slice(h * LANES, (h + 1) * LANES) for h in range(SB_GROUP_HEADS)]

    def key_rows(kt):
        return pl.ds(pl.multiple_of(kt * tile, tile), tile)

    def tile_terms(kt, diagonal):
        tri = tri_ref[...]
        out = []
        for h, sl in enumerate(heads):
            q = q_ref[0, :, (h // 2) * LANES:(h // 2 + 1) * LANES]
            z = _dot_nt(q, k_ref[0, key_rows(kt), sl])
            sp = jnp.maximum(z, 0.0) + jnp.log(1.0 + jnp.exp(-jnp.abs(z)))
            if diagonal:
                sp = jnp.where(strict, sp, 0.0)
            hi = sp.astype(_BF16)
            lo = (sp - hi.astype(_F32)).astype(_BF16)
            out.append((z, _dot(jnp.concatenate([hi, lo], axis=-1), tri)))
        return out

    def absorb(kt, terms, rems, accs, diagonal, live=None):
        new_rems, new_accs = [], []
        for sl, (z, incl), rem, acc in zip(heads, terms, rems, accs):
            p = jnp.exp(z + (rem + incl))
            if diagonal:
                p = jnp.where(strict, p, 0.0)
            step = incl[:, :1]
            if live is not None:
                p = jnp.where(live, p, 0.0)
                step = jnp.where(live, step, 0.0)
            new_accs.append(acc + _dot(p.astype(_BF16), v_ref[0, key_rows(kt), sl]))
            new_rems.append(rem + step)
        return tuple(new_rems), tuple(new_accs)

    def rem_max(rems):
        return functools.reduce(jnp.maximum, [jnp.max(r) for r in rems])

    left = jnp.maximum(qi - 1, 0)
    terms_diag = tile_terms(qi, True)
    terms_left = tile_terms(left, False)
    rems = tuple(jnp.zeros((tile, 1), _F32) for _ in range(SB_GROUP_HEADS))
    accs = tuple(jnp.zeros((tile, LANES), _F32) for _ in range(SB_GROUP_HEADS))
    rems, accs = absorb(qi, terms_diag, rems, accs, True)
    rems, accs = absorb(left, terms_left, rems, accs, False, live=qi > 0)

    def cond(state):
        kt, worst, _, _ = state
        return jnp.logical_and(kt >= 0, worst > SB_LOG_FLOOR)

    def body(state):
        kt, _, rems, accs = state
        rems, accs = absorb(kt, tile_terms(kt, False), rems, accs, False)
        return kt - 1, rem_max(rems), rems, accs

    _, _, _, accs = lax.while_loop(cond, body, (qi - 2, rem_max(rems), rems, accs))
    for hp in range(SB_GROUP_HEADS // 2):
        o_ref[0, :, hp * LANES:(hp + 1) * LANES] = accs[2 * hp] + accs[2 * hp + 1]


def _sb(qs, ks, vs, tri):
    b, s, _ = qs.shape
    tile = SB_TILE
    gw = SB_GROUP_HEADS * LANES
    return pl.pallas_call(
        _sb_kernel,
        grid=(b, SB_HEADS // SB_GROUP_HEADS, s // tile),
        in_specs=[pl.BlockSpec((1, tile, gw // 2), lambda bi, g, qi: (bi, qi, g)),
                  pl.BlockSpec((1, s, gw), lambda bi, g, qi: (bi, 0, g)),
                  pl.BlockSpec((1, s, gw), lambda bi, g, qi: (bi, 0, g)),
                  pl.BlockSpec((2 * tile, tile), lambda bi, g, qi: (0, 0))],
        out_specs=pl.BlockSpec((1, tile, gw // 2), lambda bi, g, qi: (bi, qi, g)),
        out_shape=jax.ShapeDtypeStruct((b, s, HEAD_PAIRS * LANES), _F32),
        compiler_params=pltpu.CompilerParams(dimension_semantics=("parallel", "parallel", "arbitrary"),
                                             vmem_limit_bytes=VMEM_LIMIT),
        name="sb",
    )(qs, ks, vs, tri)


def _post_kernel(om_ref, os_ref, x_ref, g_om_ref, g_os_ref, wo_ref, g_moe_ref, wr_ref, br_ref, ltri_ref,
                 h1_ref, u2_ref, route_ref, counts_ref, seen_ref):
    rows = PROJ_ROWS

    @pl.when(pl.program_id(0) == 0)
    def _():
        seen_ref[...] = jnp.zeros_like(seen_ref)

    mixed = jnp.concatenate([_rms(om_ref[...], g_om_ref[...]), _rms(os_ref[...], g_os_ref[...])], axis=-1)
    h1 = x_ref[...] + _dot(mixed.astype(_BF16), wo_ref[...])
    h1_ref[...] = h1
    u2 = _rms(h1, g_moe_ref[...])
    _store_row_tiles(u2_ref, u2)
    w_hi = wr_ref[...].astype(_BF16)
    w_lo = (wr_ref[...] - w_hi.astype(_F32)).astype(_BF16)
    u_hi = u2.astype(_BF16)
    u_lo = (u2 - u_hi.astype(_F32)).astype(_BF16)
    logits = _dot(u_hi, w_hi) + _dot(u_hi, w_lo) + _dot(u_lo, w_hi) + br_ref[...]

    part = rows // POST_PARTS
    lane = lax.broadcasted_iota(jnp.int32, (part, N_EXPERTS), 1).astype(_F32)
    out_lane = lax.broadcasted_iota(jnp.int32, (part, LANES), 1)
    work = [logits[r0:r0 + part, :] for r0 in range(0, rows, part)]
    onehot = [jnp.zeros((part, N_EXPERTS), _F32) for _ in work]
    ids = [[] for _ in work]
    tops = [[] for _ in work]
    for _ in range(TOP_K):
        for n in range(POST_PARTS):
            top = jnp.max(work[n], axis=-1, keepdims=True)
            idx = jnp.min(jnp.where(work[n] == top, lane, float(N_EXPERTS)), axis=-1, keepdims=True)
            hit = lane == idx
            onehot[n] = jnp.where(hit, 1.0, onehot[n])
            work[n] = jnp.where(hit, -jnp.inf, work[n])
            ids[n].append(idx)
            tops[n].append(top)

    seen = seen_ref[...]
    for n in range(POST_PARTS):
        exps = [jnp.exp(tp - tops[n][0]) for tp in tops[n]]
        denom = exps[0] + exps[1] + exps[2] + exps[3]
        gates = [e / denom for e in exps]
        before = seen + _dot(ltri_ref[...], onehot[n].astype(_BF16))
        ranks = [jnp.sum(jnp.where(lane == idx, before, 0.0), axis=-1, keepdims=True) for idx in ids[n]]
        seen = seen + jnp.sum(onehot[n], axis=0, keepdims=True)
        route = jnp.zeros((part, LANES), _F32)
        for j, val in enumerate(ids[n] + gates + ranks):
            route = jnp.where(out_lane == j, val, route)
        route_ref[n * part:(n + 1) * part, :] = route
    seen_ref[...] = seen
    counts_ref[...] = seen


def _post(om, os_, x2, g_om, g_os, wo, g_moe, wr, br, ltri):
    t = x2.shape[0]
    rows = PROJ_ROWS

    def full(a):
        return pl.BlockSpec(a.shape, lambda i: (0,) * a.ndim)

    def tok(n):
        return pl.BlockSpec((rows, n), lambda i: (i, 0))

    return pl.pallas_call(
        _post_kernel,
        grid=(t // rows,),
        in_specs=[tok(om.shape[1]), tok(os_.shape[1]), tok(D_MODEL), full(g_om), full(g_os), full(wo), full(g_moe),
                  full(wr), full(br), full(ltri)],
        out_specs=[tok(D_MODEL), pl.BlockSpec((rows * ROW_TILE, LANES), lambda i: (i, 0)), tok(LANES),
                   pl.BlockSpec((1, N_EXPERTS), lambda i: (0, 0))],
        out_shape=[jax.ShapeDtypeStruct((t, D_MODEL), _F32), jax.ShapeDtypeStruct((t * ROW_TILE, LANES), _F32),
                   jax.ShapeDtypeStruct((t, LANES), _F32), jax.ShapeDtypeStruct((1, N_EXPERTS), _F32)],
        scratch_shapes=[pltpu.VMEM((1, N_EXPERTS), _F32)],
        compiler_params=pltpu.CompilerParams(dimension_semantics=("arbitrary",), vmem_limit_bytes=VMEM_LIMIT),
        name="post",
    )(om, os_, x2, g_om, g_os, wo, g_moe, wr, br, ltri)


def _padded_count(cnt_ref, e):
    shift = EXPERT_ROWS.bit_length() - 1
    return lax.shift_left(lax.shift_right_logical(cnt_ref[e] + (EXPERT_ROWS - 1), shift), shift)


def _layout_kernel(cnt_ref, ids_ref, rank_ref, dest_ref, start_ref, be_ref, used_ref):
    shift = EXPERT_ROWS.bit_length() - 1
    nb = be_ref.shape[0]

    def place(e, off):
        padded = _padded_count(cnt_ref, e)
        start_ref[e] = off

        def mark(j, c):
            be_ref[lax.shift_right_logical(off, shift) + j] = e
            return c
        lax.fori_loop(0, lax.shift_right_logical(padded, shift), mark, 0)
        return off + padded
    total = lax.fori_loop(0, N_EXPERTS, place, 0)
    used_ref[0] = lax.shift_right_logical(total, shift)

    def tail(b, c):
        be_ref[b] = N_EXPERTS - 1
        return c
    lax.fori_loop(lax.shift_right_logical(total, shift), nb, tail, 0)

    ids = ids_ref[...]
    dest = rank_ref[...]
    for e in range(N_EXPERTS):
        dest = dest + jnp.where(ids == e, start_ref[e], 0)
    dest_ref[...] = dest


def _layout(counts, ids, rank, n_blocks):
    grid_spec = pltpu.PrefetchScalarGridSpec(
        num_scalar_prefetch=1,
        grid=(1,),
        in_specs=[pl.BlockSpec(ids.shape, lambda i, c: (0, 0)), pl.BlockSpec(rank.shape, lambda i, c: (0, 0))],
        out_specs=[pl.BlockSpec(ids.shape, lambda i, c: (0, 0)), pl.BlockSpec(memory_space=pltpu.SMEM),
                   pl.BlockSpec(memory_space=pltpu.SMEM), pl.BlockSpec(memory_space=pltpu.SMEM)],
    )
    return pl.pallas_call(
        _layout_kernel,
        grid_spec=grid_spec,
        out_shape=[jax.ShapeDtypeStruct(ids.shape, jnp.int32), jax.ShapeDtypeStruct((N_EXPERTS,), jnp.int32),
                   jax.ShapeDtypeStruct((n_blocks,), jnp.int32), jax.ShapeDtypeStruct((1,), jnp.int32)],
        compiler_params=pltpu.CompilerParams(dimension_semantics=("arbitrary",)),
        name="layout",
    )(counts, ids, rank)


def _tile_rows(row):
    return pl.ds(pl.multiple_of(row * ROW_TILE, ROW_TILE), ROW_TILE)


def _for_each_pair(idx, s, body):
    line_tokens = LANES // TOP_K

    def line(g, c):
        for t0 in range(0, line_tokens, DMA_GROUP):
            vals = [idx[s, g, (t0 + t) * TOP_K + k] for t in range(DMA_GROUP) for k in range(TOP_K)]
            for j, row in enumerate(vals):
                body(g * line_tokens + t0 + j // TOP_K, j % TOP_K, row, j % DMA_THREADS)
        return c
    lax.fori_loop(0, SUBLANES, line, 0)


def _dispatch_kernel(cnt_ref, start_ref, dest_hbm, u_hbm, xs_hbm, idx, stage, zrow, zblk, sem_d, sem_l, sem_i, sem_z):
    rows = MOE_ROWS
    shift = EXPERT_ROWS.bit_length() - 1
    i = pl.program_id(0)
    nt = pl.num_programs(0)
    nb = xs_hbm.shape[0] // (EXPERT_ROWS * ROW_TILE)
    slot = lax.rem(i, 2)
    stage_slot = lax.rem(i, STAGE_SLOTS)

    def idx_copy(tile, s):
        return pltpu.make_async_copy(dest_hbm.at[pl.ds(pl.multiple_of(tile * SUBLANES, SUBLANES), SUBLANES)],
                                     idx.at[s], sem_i.at[s])

    def load(tile):
        s = lax.rem(tile, STAGE_SLOTS)
        src = pl.ds(pl.multiple_of(tile * (rows * ROW_TILE), rows * ROW_TILE), rows * ROW_TILE)
        return pltpu.make_async_copy(u_hbm.at[src], stage.at[s], sem_l.at[s])

    def wait_rows(s):
        chunk = pl.ds(0, rows * ROW_TILE)
        for _ in range(TOP_K):
            pltpu.make_async_copy(stage.at[s], xs_hbm.at[chunk], sem_d.at[s]).wait()

    def zero_row(r):
        return pltpu.make_async_copy(zrow, xs_hbm.at[_tile_rows(r)], sem_z)

    def zero_block(b):
        dst = pl.ds(pl.multiple_of(b * (EXPERT_ROWS * ROW_TILE), EXPERT_ROWS * ROW_TILE), EXPERT_ROWS * ROW_TILE)
        return pltpu.make_async_copy(zblk, xs_hbm.at[dst], sem_z)

    def pad_rows(e, fn):
        def one(r, c):
            fn(zero_row(start_ref[e] + r))
            return c
        lax.fori_loop(cnt_ref[e], _padded_count(cnt_ref, e), one, 0)

    def tail_blocks(fn):
        used = start_ref[N_EXPERTS - 1] + _padded_count(cnt_ref, N_EXPERTS - 1)

        def one(b, c):
            fn(zero_block(b))
            return c
        lax.fori_loop(lax.shift_right_logical(used, shift), nb, one, 0)

    @pl.when(i == 0)
    def _():
        idx_copy(0, 0).start()
        load(0).start()

        @pl.when(nt > 1)
        def _():
            load(1).start()
        zrow[...] = jnp.zeros_like(zrow)
        zblk[...] = jnp.zeros_like(zblk)
        for fn in (lambda c: c.start(), lambda c: c.wait()):
            def per_expert(e, c, fn=fn):
                pad_rows(e, fn)
                return c
            lax.fori_loop(0, N_EXPERTS, per_expert, 0)
            tail_blocks(fn)

    idx_copy(i, slot).wait()

    @pl.when(i + 1 < nt)
    def _():
        idx_copy(i + 1, 1 - slot).start()

    @pl.when(i >= 2)
    def _():
        wait_rows(lax.rem(i + 2, STAGE_SLOTS))

    @pl.when(i + 2 < nt)
    def _():
        load(i + 2).start()

    load(i).wait()

    def copy_row(t, k, row, thread):
        del k
        pltpu.make_async_copy(stage.at[stage_slot, _tile_rows(t)], xs_hbm.at[_tile_rows(row)],
                              sem_d.at[stage_slot]).start(priority=thread)
    _for_each_pair(idx, slot, copy_row)

    @pl.when(i == nt - 1)
    def _():
        @pl.when(nt > 1)
        def _():
            wait_rows(lax.rem(i + STAGE_SLOTS - 1, STAGE_SLOTS))
        wait_rows(stage_slot)


def _dispatch(counts, start, dest, u2_tiles, n_blocks):
    nt = dest.shape[0] // SUBLANES
    grid_spec = pltpu.PrefetchScalarGridSpec(
        num_scalar_prefetch=2,
        grid=(nt,),
        in_specs=[pl.BlockSpec(memory_space=pl.ANY), pl.BlockSpec(memory_space=pl.ANY)],
        out_specs=pl.BlockSpec(memory_space=pl.ANY),
        scratch_shapes=[pltpu.SMEM((2, SUBLANES, LANES), jnp.int32),
                        pltpu.VMEM((STAGE_SLOTS, MOE_ROWS * ROW_TILE, LANES), _F32),
                        pltpu.VMEM((ROW_TILE, LANES), _F32), pltpu.VMEM((EXPERT_ROWS * ROW_TILE, LANES), _F32),
                        pltpu.SemaphoreType.DMA((STAGE_SLOTS,)), pltpu.SemaphoreType.DMA((STAGE_SLOTS,)),
                        pltpu.SemaphoreType.DMA((2,)), pltpu.SemaphoreType.DMA(())],
    )
    return pl.pallas_call(
        _dispatch_kernel,
        grid_spec=grid_spec,
        out_shape=jax.ShapeDtypeStruct((n_blocks * EXPERT_ROWS * ROW_TILE, LANES), _F32),
        compiler_params=pltpu.CompilerParams(dimension_semantics=("arbitrary",), vmem_limit_bytes=VMEM_LIMIT),
        name="dispatch",
    )(counts, start, dest, u2_tiles)


def _expert_kernel(be_ref, used_ref, xs_ref, wgu_hbm, bgu_ref, wdn_hbm, bdn_ref, ys_ref,
                   wgu_f32, wdn_f32, wgu_bf, wdn_bf, sem_w):
    rows = EXPERT_ROWS
    i = pl.program_id(0)
    nb = pl.num_programs(0)
    used = used_ref[0]
    e = be_ref[i]
    prev = be_ref[jnp.maximum(i - 1, 0)]
    in_use = i < used

    def fetch(expert):
        return (pltpu.make_async_copy(wgu_hbm.at[expert], wgu_f32, sem_w.at[0]),
                pltpu.make_async_copy(wdn_hbm.at[expert], wdn_f32, sem_w.at[1]))

    @pl.when(i == 0)
    def _():
        for c in fetch(e):
            c.start()

    @pl.when(jnp.logical_and(in_use, jnp.logical_or(i == 0, e != prev)))
    def _():
        for c in fetch(e):
            c.wait()
        chunk = 128

        def cast(c, carry):
            rs = pl.ds(pl.multiple_of(c * chunk, chunk), chunk)
            wgu_bf[rs, :] = wgu_f32[rs, :].astype(_BF16)
            wdn_bf[rs, :] = wdn_f32[rs, :].astype(_BF16)
            return carry
        lax.fori_loop(0, D_MODEL // chunk, cast, 0)

        def same_expert(j):
            return jnp.logical_and(j < used, be_ref[jnp.minimum(j, nb - 1)] == e)
        nxt = lax.while_loop(same_expert, lambda j: j + 1, i)

        @pl.when(nxt < used)
        def _():
            for c in fetch(be_ref[jnp.minimum(nxt, nb - 1)]):
                c.start()

    @pl.when(in_use)
    def _():
        x = _load_row_tiles(xs_ref, 0, rows).astype(_BF16)
        hh = _dot(x, wgu_bf[...]) + bgu_ref[0]
        glu = jnp.minimum(hh[:, :D_FF], SWIGLU_LIMIT)
        lin = jnp.clip(hh[:, D_FF:], -SWIGLU_LIMIT, SWIGLU_LIMIT)
        act = glu * jax.nn.sigmoid(SWIGLU_ALPHA * glu) * (lin + 1.0)
        _store_row_tiles(ys_ref, _dot(act.astype(_BF16), wdn_bf[...]) + bdn_ref[0])

    @pl.when(jnp.logical_not(in_use))
    def _():
        ys_ref[...] = jnp.zeros_like(ys_ref)


def _experts(block_e, used, xs, w_gu, b_gu, w_dn, b_dn):
    nb = block_e.shape[0]
    rows = EXPERT_ROWS
    grid_spec = pltpu.PrefetchScalarGridSpec(
        num_scalar_prefetch=2,
        grid=(nb,),
        in_specs=[pl.BlockSpec((rows * ROW_TILE, LANES), lambda i, be, u: (i, 0)),
                  pl.BlockSpec(memory_space=pl.ANY),
                  pl.BlockSpec((1, 1, 2 * D_FF), lambda i, be, u: (be[i], 0, 0)),
                  pl.BlockSpec(memory_space=pl.ANY),
                  pl.BlockSpec((1, 1, D_MODEL), lambda i, be, u: (be[i], 0, 0))],
        out_specs=pl.BlockSpec((rows * ROW_TILE, LANES), lambda i, be, u: (i, 0)),
        scratch_shapes=[pltpu.VMEM((D_MODEL, 2 * D_FF), _F32), pltpu.VMEM((D_FF, D_MODEL), _F32),
                        pltpu.VMEM((D_MODEL, 2 * D_FF), _BF16), pltpu.VMEM((D_FF, D_MODEL), _BF16),
                        pltpu.SemaphoreType.DMA((2,))],
    )
    return pl.pallas_call(
        _expert_kernel,
        grid_spec=grid_spec,
        out_shape=jax.ShapeDtypeStruct(xs.shape, _F32),
        compiler_params=pltpu.CompilerParams(dimension_semantics=("arbitrary",), vmem_limit_bytes=VMEM_LIMIT),
        name="experts",
    )(block_e, used, xs, w_gu, b_gu.reshape(N_EXPERTS, 1, 2 * D_FF), w_dn, b_dn.reshape(N_EXPERTS, 1, D_MODEL))


def _final_kernel(dest_hbm, ys_hbm, h1_ref, route_ref, p_ref, g_ple_ref, wpg_ref, wpp_ref, g_fin_ref,
                  out_ref, ybuf, idx, sem_y, sem_i, *, last_layer):
    rows = MOE_ROWS
    i = pl.program_id(0)
    nt = pl.num_programs(0)
    slot = lax.rem(i, 2)

    def idx_copy(tile, s):
        return pltpu.make_async_copy(dest_hbm.at[pl.ds(pl.multiple_of(tile * SUBLANES, SUBLANES), SUBLANES)],
                                     idx.at[s], sem_i.at[s])

    def gather(s):
        def copy_row(t, k, row, thread):
            pltpu.make_async_copy(ys_hbm.at[_tile_rows(row)], ybuf.at[s, _tile_rows(k * rows + t)],
                                  sem_y.at[s]).start(priority=thread)
        _for_each_pair(idx, s, copy_row)

    @pl.when(i == 0)
    def _():
        idx_copy(0, 0).start()
        idx_copy(0, 0).wait()
        gather(0)

        @pl.when(nt > 1)
        def _():
            idx_copy(1, 1).start()

    @pl.when(i + 1 < nt)
    def _():
        idx_copy(i + 1, 1 - slot).wait()
        gather(1 - slot)

        @pl.when(i + 2 < nt)
        def _():
            idx_copy(i + 2, slot).start()

    pltpu.make_async_copy(ys_hbm.at[pl.ds(0, TOP_K * rows * ROW_TILE)], ybuf.at[slot], sem_y.at[slot]).wait()
    part = rows // ROW_PARTS
    for r0 in range(0, rows, part):
        rs = slice(r0, r0 + part)
        route = route_ref[rs, :]
        y = jnp.zeros((part, D_MODEL), _F32)
        for k in range(TOP_K):
            y = y + _load_row_tiles(ybuf.at[slot], k * rows + r0, part) * route[:, TOP_K + k:TOP_K + k + 1]
        h2 = h1_ref[rs, :] + y
        u3 = _rms(h2, g_ple_ref[...]).astype(_BF16)
        gate = jax.nn.sigmoid(_dot(u3, wpg_ref[...]))
        h3 = h2 + gate * _dot(p_ref[rs, :].astype(_BF16), wpp_ref[...])
        out_ref[rs, :] = _rms(h3, g_fin_ref[...]) if last_layer else h3


def _final(dest, ys, h1, route, p2, g_ple, wpg, wpp, g_fin, last_layer):
    t = h1.shape[0]
    rows = MOE_ROWS

    def full(a):
        return pl.BlockSpec(a.shape, lambda i: (0,) * a.ndim)

    def tok(n):
        return pl.BlockSpec((rows, n), lambda i: (i, 0))

    return pl.pallas_call(
        functools.partial(_final_kernel, last_layer=last_layer),
        grid=(t // rows,),
        in_specs=[pl.BlockSpec(memory_space=pl.ANY), pl.BlockSpec(memory_space=pl.ANY), tok(D_MODEL), tok(LANES),
                  tok(PLE_DIM), full(g_ple), full(wpg), full(wpp), full(g_fin)],
        out_specs=tok(D_MODEL),
        out_shape=jax.ShapeDtypeStruct((t, D_MODEL), _F32),
        scratch_shapes=[pltpu.VMEM((2, TOP_K * rows * ROW_TILE, LANES), _F32),
                        pltpu.SMEM((2, SUBLANES, LANES), jnp.int32),
                        pltpu.SemaphoreType.DMA((2,)), pltpu.SemaphoreType.DMA((2,))],
        compiler_params=pltpu.CompilerParams(dimension_semantics=("arbitrary",), vmem_limit_bytes=VMEM_LIMIT),
        name="final",
    )(dest, ys, h1, route, p2, g_ple, wpg, wpp, g_fin)


def _rope_pad(w, rot):
    half = MLA_ROPE_DIM // 2
    body = jnp.concatenate([-w[:, half:], w[:, :half]], axis=1) if rot else w
    z = jnp.zeros((w.shape[0], MLA_NOPE_DIM), w.dtype)
    return jnp.concatenate([z, body, jnp.zeros((w.shape[0], LANES - MLA_NOPE_DIM - MLA_ROPE_DIM), w.dtype)], axis=1)


def _layer_weights(w_in, w_uq, w_ukv):
    o = Q_LORA_RANK + KV_LORA_RANK
    w_kr = w_in[:, o:o + MLA_ROPE_DIM]
    w1 = jnp.concatenate([w_in[:, :o], _rope_pad(w_kr, False), _rope_pad(w_kr, True), w_in[:, o + MLA_ROPE_DIM:]],
                         axis=1).astype(_BF16)
    uq = w_uq.reshape(Q_LORA_RANK, MLA_HEADS, MLA_NOPE_DIM + MLA_ROPE_DIM)
    zq = jnp.zeros((Q_LORA_RANK, MLA_HEADS, LANES - MLA_NOPE_DIM - MLA_ROPE_DIM), w_uq.dtype)
    wq = jnp.concatenate([uq, zq], axis=2).reshape(Q_LORA_RANK, MLA_HEADS * LANES).astype(_BF16)
    rope = uq[:, :, MLA_NOPE_DIM:]
    half = MLA_ROPE_DIM // 2
    rot = jnp.concatenate([jnp.zeros_like(uq[:, :, :MLA_NOPE_DIM]), -rope[:, :, half:], rope[:, :, :half], zq], axis=2)
    wqr = rot.reshape(Q_LORA_RANK, MLA_HEADS * LANES).astype(_BF16)
    ukv = w_ukv.reshape(KV_LORA_RANK, MLA_HEADS, MLA_NOPE_DIM + MLA_V_DIM)
    zk = jnp.zeros((KV_LORA_RANK, MLA_HEADS, LANES - MLA_NOPE_DIM), w_ukv.dtype)
    wk = jnp.concatenate([ukv[:, :, :MLA_NOPE_DIM], zk], axis=2).reshape(KV_LORA_RANK, MLA_HEADS * LANES).astype(_BF16)
    v = ukv[:, :, MLA_NOPE_DIM:].reshape(KV_LORA_RANK, HEAD_PAIRS, 2, MLA_V_DIM)
    zv = jnp.zeros((KV_LORA_RANK, HEAD_PAIRS, MLA_V_DIM), w_ukv.dtype)
    wv = jnp.stack([jnp.concatenate([v[:, :, 0], zv], axis=2), jnp.concatenate([zv, v[:, :, 1]], axis=2)], axis=2)
    wv = wv.reshape(KV_LORA_RANK, MLA_HEADS * LANES).astype(_BF16)
    return w1, wq, wqr, wk, wv


def kernel(x, p, positions, w_in, g_attn, g_cq, w_uq, g_ckv, w_ukv, g_out_mla, g_out_sb, w_o, g_moe, w_router,
           b_router, w_gu, b_gu, w_dn, b_dn, g_ple, w_ple_gate, w_ple_proj, g_final):
    b, s, d = x.shape
    t = b * s
    depth = w_in.shape[0]
    assert d == D_MODEL and ROW_TILE == SUBLANES and s % MLA_K_TILE == 0 and t % PROJ_ROWS == 0
    assert t % MOE_ROWS == 0 and (LANES // TOP_K) % DMA_GROUP == 0 and (t * TOP_K) % EXPERT_ROWS == 0
    assert EXPERT_ROWS & (EXPERT_ROWS - 1) == 0

    freq = ROPE_THETA ** (-jnp.arange(0, MLA_ROPE_DIM, 2, dtype=_F32) / MLA_ROPE_DIM)
    invf = jnp.concatenate([jnp.zeros((MLA_NOPE_DIM,), _F32), freq, freq,
                            jnp.zeros((LANES - MLA_NOPE_DIM - MLA_ROPE_DIM,), _F32)]).reshape(1, LANES)
    pos = positions.reshape(t, 1)
    idx = jnp.arange(SB_TILE)
    tri = -(idx[:, None] >= idx[None, :]).astype(_BF16)
    tri = jnp.concatenate([tri, tri], axis=0)
    idx = jnp.arange(PROJ_ROWS // POST_PARTS)
    ltri = (idx[None, :] < idx[:, None]).astype(_BF16)
    n_blocks = t * TOP_K // EXPERT_ROWS + N_EXPERTS

    h = x.reshape(t, d)
    for i in range(depth):
        w1, wq, wqr, wk, wv = _layer_weights(w_in[i], w_uq[i], w_ukv[i])
        qm, km, vm, qs, ks, vs = _proj(pos, h, g_attn[i].reshape(1, d), w1, g_cq[i].reshape(1, -1), wq, wqr,
                                       g_ckv[i].reshape(1, -1), wk, wv, invf)

        def seq(a):
            return a.reshape(b, s, a.shape[1])

        om = _mla(seq(qm), seq(km), seq(vm)).reshape(t, -1)
        os_ = _sb(seq(qs), seq(ks), seq(vs), tri).reshape(t, -1)
        h1, u2_tiles, route, counts = _post(om, os_, h, g_out_mla[i].reshape(1, -1), g_out_sb[i].reshape(1, -1),
                                            w_o[i].astype(_BF16), g_moe[i].reshape(1, d), w_router[i],
                                            b_router[i].reshape(1, -1), ltri)
        ids = route[:, :TOP_K].astype(jnp.int32).reshape(-1, LANES)
        rank = route[:, 2 * TOP_K:3 * TOP_K].astype(jnp.int32).reshape(-1, LANES)
        counts = counts.reshape(-1).astype(jnp.int32)
        dest, start, block_e, used = _layout(counts, ids, rank, n_blocks)
        xs = _dispatch(counts, start, dest, u2_tiles, n_blocks)
        ys = _experts(block_e, used, xs, w_gu[i], b_gu[i], w_dn[i], b_dn[i])
        h = _final(dest, ys, h1, route, p[i].reshape(t, -1), g_ple[i].reshape(1, d), w_ple_gate[i].astype(_BF16),
                   w_ple_proj[i].astype(_BF16), g_final.reshape(1, d), i == depth - 1)
    return h.reshape(b, s, d)
```

```python
import functools
import math

import jax
import jax.numpy as jnp
from jax import lax
from jax.experimental import pallas as pl
from jax.experimental.pallas import tpu as pltpu

D_MODEL = 1024
PLE_DIM = 256
MLA_HEADS = 8
MLA_NOPE_DIM = 64
MLA_ROPE_DIM = 32
MLA_V_DIM = 64
Q_LORA_RANK = 384
KV_LORA_RANK = 256
SB_HEADS = 8
SB_HEAD_DIM = 64
SB_WIDTH = SB_HEADS * SB_HEAD_DIM
ROPE_THETA = 10000.0
N_EXPERTS = 32
TOP_K = 4
D_FF = 1024
SWIGLU_LIMIT = 7.0
SWIGLU_ALPHA = 1.702
RMS_EPS = 1e-6

LANES = 128
SUBLANES = 8
ROW_TILE = D_MODEL // LANES
HEAD_PAIRS = MLA_HEADS // 2
PROJ_ROWS = 256
GROUP_HEADS = 4
SB_GROUP_HEADS = 4
MLA_Q_TILE = 512
MLA_K_TILE = 512
SB_TILE = 256
EXPERT_ROWS = 256
ROW_PARTS = 2
POST_PARTS = 4
MOE_ROWS = SUBLANES * LANES // TOP_K
DMA_GROUP = 8
STAGE_SLOTS = 4
DMA_THREADS = 2
VMEM_LIMIT = 56 * 1024 * 1024
SB_LOG_FLOOR = -105.0

_F32 = jnp.float32
_BF16 = jnp.bfloat16


def _rms(x, g):
    return x * lax.rsqrt(jnp.mean(x * x, axis=-1, keepdims=True) + RMS_EPS) * g


def _dot(a, b):
    return jnp.dot(a, b, preferred_element_type=_F32)


def _dot_nt(a, b):
    return lax.dot_general(a, b, (((1,), (1,)), ((), ())), preferred_element_type=_F32)


def _store_row_tiles(ref, value, first_row=0):
    rows = value.shape[0]
    for g in range(ROW_TILE):
        ref[pl.ds(first_row * ROW_TILE + g, rows, stride=ROW_TILE), :] = value[:, g * LANES:(g + 1) * LANES]


def _load_row_tiles(ref, first_row, rows):
    return jnp.concatenate([ref[pl.ds(first_row * ROW_TILE + g, rows, stride=ROW_TILE), :] for g in range(ROW_TILE)],
                           axis=-1)


def _proj_kernel(pos_ref, x_ref, g_attn_ref, w1_ref, g_cq_ref, wq_ref, wqr_ref, g_ckv_ref, wk_ref, wv_ref,
                 invf_ref, qm_ref, km_ref, vm_ref, qs_ref, ks_ref, vs_ref):
    u = _rms(x_ref[...], g_attn_ref[...]).astype(_BF16)
    y = _dot(u, w1_ref[...])
    c_q = y[:, :Q_LORA_RANK]
    c_kv = y[:, Q_LORA_RANK:Q_LORA_RANK + KV_LORA_RANK]
    o = Q_LORA_RANK + KV_LORA_RANK
    k_r = y[:, o:o + LANES]
    k_r_rot = y[:, o + LANES:o + 2 * LANES]
    o += 2 * LANES
    q_s = y[:, o:o + SB_WIDTH]
    k_s = y[:, o + SB_WIDTH:o + 2 * SB_WIDTH]
    v_s = y[:, o + 2 * SB_WIDTH:o + 3 * SB_WIDTH]

    ang = pos_ref[...].astype(_F32) * invf_ref[...]
    cos = jnp.cos(ang)
    sin = jnp.sin(ang)

    cq_n = _rms(c_q, g_cq_ref[...]).astype(_BF16)
    q = _dot(cq_n, wq_ref[...])
    q_rot = _dot(cq_n, wqr_ref[...])
    ckv_n = _rms(c_kv, g_ckv_ref[...]).astype(_BF16)
    k = _dot(ckv_n, wk_ref[...])
    wide_lane = lax.broadcasted_iota(jnp.int32, (1, MLA_HEADS * LANES), 1)
    ones_lane = (wide_lane % (2 * LANES) == MLA_V_DIM) | (wide_lane % (2 * LANES) == LANES)
    vm_ref[...] = jnp.where(ones_lane, 1.0, _dot(ckv_n, wv_ref[...])).astype(_BF16)
    k_rope = k_r * cos + k_r_rot * sin

    q_scale = (MLA_NOPE_DIM + MLA_ROPE_DIM) ** -0.5 * math.log2(math.e)
    lane = lax.broadcasted_iota(jnp.int32, (1, LANES), 1)
    low = lane < SB_HEAD_DIM
    for h in range(MLA_HEADS):
        sl = slice(h * LANES, (h + 1) * LANES)
        qm_ref[:, sl] = ((q[:, sl] * cos + q_rot[:, sl] * sin) * q_scale).astype(_BF16)
        km_ref[:, sl] = (k[:, sl] + k_rope).astype(_BF16)
    qs_ref[...] = (q_s * (SB_HEAD_DIM ** -0.5)).astype(_BF16)
    for hp in range(HEAD_PAIRS):
        sl = slice(hp * LANES, (hp + 1) * LANES)
        for half, keep in ((0, low), (1, jnp.logical_not(low))):
            dst = slice((2 * hp + half) * LANES, (2 * hp + half + 1) * LANES)
            ks_ref[:, dst] = jnp.where(keep, k_s[:, sl], 0.0).astype(_BF16)
            vs_ref[:, dst] = jnp.where(keep, v_s[:, sl], 0.0).astype(_BF16)


def _proj(pos, x2, g_attn, w1, g_cq, wq, wqr, g_ckv, wk, wv, invf):
    t = x2.shape[0]
    rows = PROJ_ROWS
    wide = MLA_HEADS * LANES

    def full(a):
        return pl.BlockSpec(a.shape, lambda i: (0,) * a.ndim)

    def tok(n):
        return pl.BlockSpec((rows, n), lambda i: (i, 0))

    outs = [jax.ShapeDtypeStruct((t, n), _BF16) for n in (wide, wide, wide, SB_WIDTH, wide, wide)]
    return pl.pallas_call(
        _proj_kernel,
        grid=(t // rows,),
        in_specs=[tok(1), tok(D_MODEL), full(g_attn), full(w1), full(g_cq), full(wq), full(wqr), full(g_ckv),
                  full(wk), full(wv), full(invf)],
        out_specs=[tok(wide), tok(wide), tok(wide), tok(SB_WIDTH), tok(wide), tok(wide)],
        out_shape=outs,
        compiler_params=pltpu.CompilerParams(dimension_semantics=("parallel",), vmem_limit_bytes=VMEM_LIMIT),
        name="proj",
    )(pos, x2, g_attn, w1, g_cq, wq, wqr, g_ckv, wk, wv, invf)


def _mla_kernel(q_ref, k_ref, v_ref, o_ref, s_a, s_b, m_ref, acc_ref):
    tq, tk = MLA_Q_TILE, MLA_K_TILE
    qi = pl.program_id(2)
    heads = [slice(h * LANES, (h + 1) * LANES) for h in range(GROUP_HEADS)]

    def key_rows(kt):
        return pl.ds(pl.multiple_of(kt * tk, tk), tk)

    def score(dst, kt, mask=None):
        for h, sl in enumerate(heads):
            s = _dot_nt(q_ref[0, :, sl], k_ref[0, key_rows(kt), sl])
            dst[h] = s if mask is None else jnp.where(mask, s, -jnp.inf)

    def absorb(src, kt):
        for h, sl in enumerate(heads):
            s = src[h]
            m = m_ref[h]
            m_new = jnp.maximum(m, jnp.max(s, axis=-1, keepdims=True))
            alpha = jnp.exp2(m - m_new)
            p = jnp.exp2(s - m_new)
            m_ref[h] = m_new
            acc_ref[h] = alpha * acc_ref[h] + _dot(p.astype(_BF16), v_ref[0, key_rows(kt), sl])

    m_ref[...] = jnp.full(m_ref.shape, -jnp.inf, _F32)
    acc_ref[...] = jnp.zeros(acc_ref.shape, _F32)
    last = (qi * tq) // tk
    row = lax.broadcasted_iota(jnp.int32, (tq, tk), 0)
    col = lax.broadcasted_iota(jnp.int32, (tq, tk), 1)
    score(s_a, last, mask=last * tk + col <= qi * tq + row)

    def tile_in_a(i):
        return jnp.where(i == 0, last, 2 * i - 1)

    def pair(i, c):
        score(s_b, 2 * i)
        absorb(s_a, tile_in_a(i))
        score(s_a, 2 * i + 1)
        absorb(s_b, 2 * i)
        return c

    pairs = last // 2
    lax.fori_loop(0, pairs, pair, 0)

    @pl.when(last % 2 == 1)
    def _():
        score(s_b, last - 1)
        absorb(s_a, tile_in_a(pairs))
        absorb(s_b, last - 1)

    @pl.when(last % 2 == 0)
    def _():
        absorb(s_a, tile_in_a(pairs))

    low = lax.broadcasted_iota(jnp.int32, (tq, LANES), 1) < MLA_V_DIM
    for hp in range(GROUP_HEADS // 2):
        even, odd = acc_ref[2 * hp], acc_ref[2 * hp + 1]
        o_ref[0, :, hp * LANES:(hp + 1) * LANES] = jnp.where(low, even / even[:, MLA_V_DIM:MLA_V_DIM + 1],
                                                             odd / odd[:, :1])


def _mla(qm, km, vm):
    b, s, _ = qm.shape
    tq = MLA_Q_TILE
    gw = GROUP_HEADS * LANES
    return pl.pallas_call(
        _mla_kernel,
        grid=(b, MLA_HEADS // GROUP_HEADS, s // tq),
        in_specs=[pl.BlockSpec((1, tq, gw), lambda bi, g, qi: (bi, qi, g)),
                  pl.BlockSpec((1, s, gw), lambda bi, g, qi: (bi, 0, g)),
                  pl.BlockSpec((1, s, gw), lambda bi, g, qi: (bi, 0, g))],
        out_specs=pl.BlockSpec((1, tq, gw // 2), lambda bi, g, qi: (bi, qi, g)),
        out_shape=jax.ShapeDtypeStruct((b, s, HEAD_PAIRS * LANES), _F32),
        scratch_shapes=[pltpu.VMEM((GROUP_HEADS, tq, MLA_K_TILE), _F32), pltpu.VMEM((GROUP_HEADS, tq, MLA_K_TILE), _F32),
                        pltpu.VMEM((GROUP_HEADS, tq, 1), _F32), pltpu.VMEM((GROUP_HEADS, tq, LANES), _F32)],
        compiler_params=pltpu.CompilerParams(dimension_semantics=("parallel", "parallel", "arbitrary"),
                                             vmem_limit_bytes=VMEM_LIMIT),
        name="mla",
    )(qm, km, vm)


def _sb_kernel(q_ref, k_ref, v_ref, tri_ref, o_ref, rem_ref, acc_ref):
    tile = SB_TILE
    qi = pl.program_id(2)
    row = lax.broadcasted_iota(jnp.int32, (tile, tile), 0)
    col = lax.broadcasted_iota(jnp.int32, (tile, tile), 1)
    strict = col < row
    heads = [slice(h * LANES, (h + 1) * LANES) for h in range(SB_GROUP_HEADS)]

    def key_rows(kt):
        return pl.ds(pl.multiple_of(kt * tile, tile), tile)

    def tile_terms(kt, diagonal):
        tri = tri_ref[...]
        out = []
        for h, sl in enumerate(heads):
            q = q_ref[0, :, (h // 2) * LANES:(h // 2 + 1) * LANES]
            z = _dot_nt(q, k_ref[0, key_rows(kt), sl])
            sp = jnp.maximum(z, 0.0) + jnp.log(1.0 + jnp.exp(-jnp.abs(z)))
            if diagonal:
                sp = jnp.where(strict, sp, 0.0)
            hi = sp.astype(_BF16)
            lo = (sp - hi.astype(_F32)).astype(_BF16)
            out.append((z, _dot(jnp.concatenate([hi, lo], axis=-1), tri)))
        return out

    def absorb(kt, terms, rems, accs, diagonal, live=None):
        new_rems, new_accs = [], []
        for sl, (z, incl), rem, acc in zip(heads, terms, rems, accs):
            p = jnp.exp(z + (rem + incl))
            if diagonal:
                p = jnp.where(strict, p, 0.0)
            step = incl[:, :1]
            if live is not None:
                p = jnp.where(live, p, 0.0)
                step = jnp.where(live, step, 0.0)
            new_accs.append(acc + _dot(p.astype(_BF16), v_ref[0, key_rows(kt), sl]))
            new_rems.append(rem + step)
        return tuple(new_rems), tuple(new_accs)

    def rem_max(rems):
        return functools.reduce(jnp.maximum, [jnp.max(r) for r in rems])

    left = jnp.maximum(qi - 1, 0)
    terms_diag = tile_terms(qi, True)
    terms_left = tile_terms(left, False)
    rems = tuple(jnp.zeros((tile, 1), _F32) for _ in range(SB_GROUP_HEADS))
    accs = tuple(jnp.zeros((tile, LANES), _F32) for _ in range(SB_GROUP_HEADS))
    rems, accs = absorb(qi, terms_diag, rems, accs, True)
    rems, accs = absorb(left, terms_left, rems, accs, False, live=qi > 0)

    def save(rems, accs):
        for h in range(SB_GROUP_HEADS):
            rem_ref[h] = rems[h]
            acc_ref[h] = accs[h]
    save(rems, accs)

    def cond(state):
        kt, worst = state
        return jnp.logical_and(kt >= 0, worst > SB_LOG_FLOOR)

    def body(state):
        kt, _ = state
        rems = tuple(rem_ref[h] for h in range(SB_GROUP_HEADS))
        accs = tuple(acc_ref[h] for h in range(SB_GROUP_HEADS))
        rems, accs = absorb(kt, tile_terms(kt, False), rems, accs, False)
        save(rems, accs)
        return kt - 1, rem_max(rems)

    lax.while_loop(cond, body, (qi - 2, rem_max(rems)))
    for hp in range(SB_GROUP_HEADS // 2):
        o_ref[0, :, hp * LANES:(hp + 1) * LANES] = acc_ref[2 * hp] + acc_ref[2 * hp + 1]


def _sb(qs, ks, vs, tri):
    b, s, _ = qs.shape
    tile = SB_TILE
    gw = SB_GROUP_HEADS * LANES
    return pl.pallas_call(
        _sb_kernel,
        grid=(b, SB_HEADS // SB_GROUP_HEADS, s // tile),
        in_specs=[pl.BlockSpec((1, tile, gw // 2), lambda bi, g, qi: (bi, qi, g)),
                  pl.BlockSpec((1, s, gw), lambda bi, g, qi: (bi, 0, g)),
                  pl.BlockSpec((1, s, gw), lambda bi, g, qi: (bi, 0, g)),
                  pl.BlockSpec((2 * tile, tile), lambda bi, g, qi: (0, 0))],
        out_specs=pl.BlockSpec((1, tile, gw // 2), lambda bi, g, qi: (bi, qi, g)),
        out_shape=jax.ShapeDtypeStruct((b, s, HEAD_PAIRS * LANES), _F32),
        scratch_shapes=[pltpu.VMEM((SB_GROUP_HEADS, tile, 1), _F32), pltpu.VMEM((SB_GROUP_HEADS, tile, LANES), _F32)],
        compiler_params=pltpu.CompilerParams(dimension_semantics=("parallel", "parallel", "arbitrary"),
                                             vmem_limit_bytes=VMEM_LIMIT),
        name="sb",
    )(qs, ks, vs, tri)


def _post_kernel(om_ref, os_ref, x_ref, g_om_ref, g_os_ref, wo_ref, g_moe_ref, wr_ref, br_ref, ltri_ref,
                 h1_ref, u2_ref, route_ref, counts_ref, seen_ref):
    rows = PROJ_ROWS

    @pl.when(pl.program_id(0) == 0)
    def _():
        seen_ref[...] = jnp.zeros_like(seen_ref)

    mixed = jnp.concatenate([_rms(om_ref[...], g_om_ref[...]), _rms(os_ref[...], g_os_ref[...])], axis=-1)
    h1 = x_ref[...] + _dot(mixed.astype(_BF16), wo_ref[...])
    h1_ref[...] = h1
    u2 = _rms(h1, g_moe_ref[...])
    _store_row_tiles(u2_ref, u2)
    w_hi = wr_ref[...].astype(_BF16)
    w_lo = (wr_ref[...] - w_hi.astype(_F32)).astype(_BF16)
    u_hi = u2.astype(_BF16)
    u_lo = (u2 - u_hi.astype(_F32)).astype(_BF16)
    logits = _dot(u_hi, w_hi) + _dot(u_hi, w_lo) + _dot(u_lo, w_hi) + br_ref[...]

    part = rows // POST_PARTS
    lane = lax.broadcasted_iota(jnp.int32, (part, N_EXPERTS), 1).astype(_F32)
    out_lane = lax.broadcasted_iota(jnp.int32, (part, LANES), 1)
    work = [logits[r0:r0 + part, :] for r0 in range(0, rows, part)]
    onehot = [jnp.zeros((part, N_EXPERTS), _F32) for _ in work]
    ids = [[] for _ in work]
    tops = [[] for _ in work]
    for _ in range(TOP_K):
        for n in range(POST_PARTS):
            top = jnp.max(work[n], axis=-1, keepdims=True)
            idx = jnp.min(jnp.where(work[n] == top, lane, float(N_EXPERTS)), axis=-1, keepdims=True)
            hit = lane == idx
            onehot[n] = jnp.where(hit, 1.0, onehot[n])
            work[n] = jnp.where(hit, -jnp.inf, work[n])
            ids[n].append(idx)
            tops[n].append(top)

    seen = seen_ref[...]
    for n in range(POST_PARTS):
        exps = [jnp.exp(tp - tops[n][0]) for tp in tops[n]]
        denom = exps[0] + exps[1] + exps[2] + exps[3]
        gates = [e / denom for e in exps]
        before = seen + _dot(ltri_ref[...], onehot[n].astype(_BF16))
        ranks = [jnp.sum(jnp.where(lane == idx, before, 0.0), axis=-1, keepdims=True) for idx in ids[n]]
        seen = seen + jnp.sum(onehot[n], axis=0, keepdims=True)
        route = jnp.zeros((part, LANES), _F32)
        for j, val in enumerate(ids[n] + gates + ranks):
            route = jnp.where(out_lane == j, val, route)
        route_ref[n * part:(n + 1) * part, :] = route
    seen_ref[...] = seen
    counts_ref[...] = seen


def _post(om, os_, x2, g_om, g_os, wo, g_moe, wr, br, ltri):
    t = x2.shape[0]
    rows = PROJ_ROWS

    def full(a):
        return pl.BlockSpec(a.shape, lambda i: (0,) * a.ndim)

    def tok(n):
        return pl.BlockSpec((rows, n), lambda i: (i, 0))

    return pl.pallas_call(
        _post_kernel,
        grid=(t // rows,),
        in_specs=[tok(om.shape[1]), tok(os_.shape[1]), tok(D_MODEL), full(g_om), full(g_os), full(wo), full(g_moe),
                  full(wr), full(br), full(ltri)],
        out_specs=[tok(D_MODEL), pl.BlockSpec((rows * ROW_TILE, LANES), lambda i: (i, 0)), tok(LANES),
                   pl.BlockSpec((1, N_EXPERTS), lambda i: (0, 0))],
        out_shape=[jax.ShapeDtypeStruct((t, D_MODEL), _F32), jax.ShapeDtypeStruct((t * ROW_TILE, LANES), _F32),
                   jax.ShapeDtypeStruct((t, LANES), _F32), jax.ShapeDtypeStruct((1, N_EXPERTS), _F32)],
        scratch_shapes=[pltpu.VMEM((1, N_EXPERTS), _F32)],
        compiler_params=pltpu.CompilerParams(dimension_semantics=("arbitrary",), vmem_limit_bytes=VMEM_LIMIT),
        name="post",
    )(om, os_, x2, g_om, g_os, wo, g_moe, wr, br, ltri)


def _padded_count(cnt_ref, e):
    shift = EXPERT_ROWS.bit_length() - 1
    return lax.shift_left(lax.shift_right_logical(cnt_ref[e] + (EXPERT_ROWS - 1), shift), shift)


def _layout_kernel(cnt_ref, ids_ref, rank_ref, dest_ref, start_ref, be_ref, used_ref):
    shift = EXPERT_ROWS.bit_length() - 1
    nb = be_ref.shape[0]

    def place(e, off):
        padded = _padded_count(cnt_ref, e)
        start_ref[e] = off

        def mark(j, c):
            be_ref[lax.shift_right_logical(off, shift) + j] = e
            return c
        lax.fori_loop(0, lax.shift_right_logical(padded, shift), mark, 0)
        return off + padded
    total = lax.fori_loop(0, N_EXPERTS, place, 0)
    used_ref[0] = lax.shift_right_logical(total, shift)

    def tail(b, c):
        be_ref[b] = N_EXPERTS - 1
        return c
    lax.fori_loop(lax.shift_right_logical(total, shift), nb, tail, 0)

    ids = ids_ref[...]
    dest = rank_ref[...]
    for e in range(N_EXPERTS):
        dest = dest + jnp.where(ids == e, start_ref[e], 0)
    dest_ref[...] = dest


def _layout(counts, ids, rank, n_blocks):
    grid_spec = pltpu.PrefetchScalarGridSpec(
        num_scalar_prefetch=1,
        grid=(1,),
        in_specs=[pl.BlockSpec(ids.shape, lambda i, c: (0, 0)), pl.BlockSpec(rank.shape, lambda i, c: (0, 0))],
        out_specs=[pl.BlockSpec(ids.shape, lambda i, c: (0, 0)), pl.BlockSpec(memory_space=pltpu.SMEM),
                   pl.BlockSpec(memory_space=pltpu.SMEM), pl.BlockSpec(memory_space=pltpu.SMEM)],
    )
    return pl.pallas_call(
        _layout_kernel,
        grid_spec=grid_spec,
        out_shape=[jax.ShapeDtypeStruct(ids.shape, jnp.int32), jax.ShapeDtypeStruct((N_EXPERTS,), jnp.int32),
                   jax.ShapeDtypeStruct((n_blocks,), jnp.int32), jax.ShapeDtypeStruct((1,), jnp.int32)],
        compiler_params=pltpu.CompilerParams(dimension_semantics=("arbitrary",)),
        name="layout",
    )(counts, ids, rank)


def _tile_rows(row):
    return pl.ds(pl.multiple_of(row * ROW_TILE, ROW_TILE), ROW_TILE)


def _for_each_pair(idx, s, body):
    line_tokens = LANES // TOP_K

    def line(g, c):
        for t0 in range(0, line_tokens, DMA_GROUP):
            vals = [idx[s, g, (t0 + t) * TOP_K + k] for t in range(DMA_GROUP) for k in range(TOP_K)]
            for j, row in enumerate(vals):
                body(g * line_tokens + t0 + j // TOP_K, j % TOP_K, row, j % DMA_THREADS)
        return c
    lax.fori_loop(0, SUBLANES, line, 0)


def _dispatch_kernel(cnt_ref, start_ref, dest_hbm, u_hbm, xs_hbm, idx, stage, zrow, zblk, sem_d, sem_l, sem_i, sem_z):
    rows = MOE_ROWS
    shift = EXPERT_ROWS.bit_length() - 1
    i = pl.program_id(0)
    nt = pl.num_programs(0)
    nb = xs_hbm.shape[0] // (EXPERT_ROWS * ROW_TILE)
    slot = lax.rem(i, 2)
    stage_slot = lax.rem(i, STAGE_SLOTS)

    def idx_copy(tile, s):
        return pltpu.make_async_copy(dest_hbm.at[pl.ds(pl.multiple_of(tile * SUBLANES, SUBLANES), SUBLANES)],
                                     idx.at[s], sem_i.at[s])

    def load(tile):
        s = lax.rem(tile, STAGE_SLOTS)
        src = pl.ds(pl.multiple_of(tile * (rows * ROW_TILE), rows * ROW_TILE), rows * ROW_TILE)
        return pltpu.make_async_copy(u_hbm.at[src], stage.at[s], sem_l.at[s])

    def wait_rows(s):
        chunk = pl.ds(0, rows * ROW_TILE)
        for _ in range(TOP_K):
            pltpu.make_async_copy(stage.at[s], xs_hbm.at[chunk], sem_d.at[s]).wait()

    def zero_row(r):
        return pltpu.make_async_copy(zrow, xs_hbm.at[_tile_rows(r)], sem_z)

    def zero_block(b):
        dst = pl.ds(pl.multiple_of(b * (EXPERT_ROWS * ROW_TILE), EXPERT_ROWS * ROW_TILE), EXPERT_ROWS * ROW_TILE)
        return pltpu.make_async_copy(zblk, xs_hbm.at[dst], sem_z)

    def pad_rows(e, fn):
        def one(r, c):
            fn(zero_row(start_ref[e] + r))
            return c
        lax.fori_loop(cnt_ref[e], _padded_count(cnt_ref, e), one, 0)

    def tail_blocks(fn):
        used = start_ref[N_EXPERTS - 1] + _padded_count(cnt_ref, N_EXPERTS - 1)

        def one(b, c):
            fn(zero_block(b))
            return c
        lax.fori_loop(lax.shift_right_logical(used, shift), nb, one, 0)

    @pl.when(i == 0)
    def _():
        idx_copy(0, 0).start()
        load(0).start()

        @pl.when(nt > 1)
        def _():
            load(1).start()
        zrow[...] = jnp.zeros_like(zrow)
        zblk[...] = jnp.zeros_like(zblk)
        for fn in (lambda c: c.start(), lambda c: c.wait()):
            def per_expert(e, c, fn=fn):
                pad_rows(e, fn)
                return c
            lax.fori_loop(0, N_EXPERTS, per_expert, 0)
            tail_blocks(fn)

    idx_copy(i, slot).wait()

    @pl.when(i + 1 < nt)
    def _():
        idx_copy(i + 1, 1 - slot).start()

    @pl.when(i >= 2)
    def _():
        wait_rows(lax.rem(i + 2, STAGE_SLOTS))

    @pl.when(i + 2 < nt)
    def _():
        load(i + 2).start()

    load(i).wait()

    def copy_row(t, k, row, thread):
        del k
        pltpu.make_async_copy(stage.at[stage_slot, _tile_rows(t)], xs_hbm.at[_tile_rows(row)],
                              sem_d.at[stage_slot]).start(priority=thread)
    _for_each_pair(idx, slot, copy_row)

    @pl.when(i == nt - 1)
    def _():
        @pl.when(nt > 1)
        def _():
            wait_rows(lax.rem(i + STAGE_SLOTS - 1, STAGE_SLOTS))
        wait_rows(stage_slot)


def _dispatch(counts, start, dest, u2_tiles, n_blocks):
    nt = dest.shape[0] // SUBLANES
    grid_spec = pltpu.PrefetchScalarGridSpec(
        num_scalar_prefetch=2,
        grid=(nt,),
        in_specs=[pl.BlockSpec(memory_space=pl.ANY), pl.BlockSpec(memory_space=pl.ANY)],
        out_specs=pl.BlockSpec(memory_space=pl.ANY),
        scratch_shapes=[pltpu.SMEM((2, SUBLANES, LANES), jnp.int32),
                        pltpu.VMEM((STAGE_SLOTS, MOE_ROWS * ROW_TILE, LANES), _F32),
                        pltpu.VMEM((ROW_TILE, LANES), _F32), pltpu.VMEM((EXPERT_ROWS * ROW_TILE, LANES), _F32),
                        pltpu.SemaphoreType.DMA((STAGE_SLOTS,)), pltpu.SemaphoreType.DMA((STAGE_SLOTS,)),
                        pltpu.SemaphoreType.DMA((2,)), pltpu.SemaphoreType.DMA(())],
    )
    return pl.pallas_call(
        _dispatch_kernel,
        grid_spec=grid_spec,
        out_shape=jax.ShapeDtypeStruct((n_blocks * EXPERT_ROWS * ROW_TILE, LANES), _F32),
        compiler_params=pltpu.CompilerParams(dimension_semantics=("arbitrary",), vmem_limit_bytes=VMEM_LIMIT),
        name="dispatch",
    )(counts, start, dest, u2_tiles)


def _expert_kernel(be_ref, used_ref, xs_ref, wgu_hbm, bgu_ref, wdn_hbm, bdn_ref, ys_ref,
                   wgu_f32, wdn_f32, wgu_bf, wdn_bf, sem_w):
    rows = EXPERT_ROWS
    i = pl.program_id(0)
    nb = pl.num_programs(0)
    used = used_ref[0]
    e = be_ref[i]
    prev = be_ref[jnp.maximum(i - 1, 0)]
    in_use = i < used

    def fetch(expert):
        return (pltpu.make_async_copy(wgu_hbm.at[expert], wgu_f32, sem_w.at[0]),
                pltpu.make_async_copy(wdn_hbm.at[expert], wdn_f32, sem_w.at[1]))

    @pl.when(i == 0)
    def _():
        for c in fetch(e):
            c.start()

    @pl.when(jnp.logical_and(in_use, jnp.logical_or(i == 0, e != prev)))
    def _():
        for c in fetch(e):
            c.wait()
        chunk = 128

        def cast(c, carry):
            rs = pl.ds(pl.multiple_of(c * chunk, chunk), chunk)
            wgu_bf[rs, :] = wgu_f32[rs, :].astype(_BF16)
            wdn_bf[rs, :] = wdn_f32[rs, :].astype(_BF16)
            return carry
        lax.fori_loop(0, D_MODEL // chunk, cast, 0)

        def same_expert(j):
            return jnp.logical_and(j < used, be_ref[jnp.minimum(j, nb - 1)] == e)
        nxt = lax.while_loop(same_expert, lambda j: j + 1, i)

        @pl.when(nxt < used)
        def _():
            for c in fetch(be_ref[jnp.minimum(nxt, nb - 1)]):
                c.start()

    @pl.when(in_use)
    def _():
        x = _load_row_tiles(xs_ref, 0, rows).astype(_BF16)
        hh = _dot(x, wgu_bf[...]) + bgu_ref[0]
        glu = jnp.minimum(hh[:, :D_FF], SWIGLU_LIMIT)
        lin = jnp.clip(hh[:, D_FF:], -SWIGLU_LIMIT, SWIGLU_LIMIT)
        act = glu * jax.nn.sigmoid(SWIGLU_ALPHA * glu) * (lin + 1.0)
        _store_row_tiles(ys_ref, _dot(act.astype(_BF16), wdn_bf[...]) + bdn_ref[0])

    @pl.when(jnp.logical_not(in_use))
    def _():
        ys_ref[...] = jnp.zeros_like(ys_ref)


def _experts(block_e, used, xs, w_gu, b_gu, w_dn, b_dn):
    nb = block_e.shape[0]
    rows = EXPERT_ROWS
    grid_spec = pltpu.PrefetchScalarGridSpec(
        num_scalar_prefetch=2,
        grid=(nb,),
        in_specs=[pl.BlockSpec((rows * ROW_TILE, LANES), lambda i, be, u: (i, 0)),
                  pl.BlockSpec(memory_space=pl.ANY),
                  pl.BlockSpec((1, 1, 2 * D_FF), lambda i, be, u: (be[i], 0, 0)),
                  pl.BlockSpec(memory_space=pl.ANY),
                  pl.BlockSpec((1, 1, D_MODEL), lambda i, be, u: (be[i], 0, 0))],
        out_specs=pl.BlockSpec((rows * ROW_TILE, LANES), lambda i, be, u: (i, 0)),
        scratch_shapes=[pltpu.VMEM((D_MODEL, 2 * D_FF), _F32), pltpu.VMEM((D_FF, D_MODEL), _F32),
                        pltpu.VMEM((D_MODEL, 2 * D_FF), _BF16), pltpu.VMEM((D_FF, D_MODEL), _BF16),
                        pltpu.SemaphoreType.DMA((2,))],
    )
    return pl.pallas_call(
        _expert_kernel,
        grid_spec=grid_spec,
        out_shape=jax.ShapeDtypeStruct(xs.shape, _F32),
        compiler_params=pltpu.CompilerParams(dimension_semantics=("arbitrary",), vmem_limit_bytes=VMEM_LIMIT),
        name="experts",
    )(block_e, used, xs, w_gu, b_gu.reshape(N_EXPERTS, 1, 2 * D_FF), w_dn, b_dn.reshape(N_EXPERTS, 1, D_MODEL))


def _final_kernel(dest_hbm, ys_hbm, h1_ref, route_ref, p_ref, g_ple_ref, wpg_ref, wpp_ref, g_fin_ref,
                  out_ref, ybuf, idx, sem_y, sem_i, *, last_layer):
    rows = MOE_ROWS
    i = pl.program_id(0)
    nt = pl.num_programs(0)
    slot = lax.rem(i, 2)

    def idx_copy(tile, s):
        return pltpu.make_async_copy(dest_hbm.at[pl.ds(pl.multiple_of(tile * SUBLANES, SUBLANES), SUBLANES)],
                                     idx.at[s], sem_i.at[s])

    def gather(s):
        def copy_row(t, k, row, thread):
            pltpu.make_async_copy(ys_hbm.at[_tile_rows(row)], ybuf.at[s, _tile_rows(k * rows + t)],
                                  sem_y.at[s]).start(priority=thread)
        _for_each_pair(idx, s, copy_row)

    @pl.when(i == 0)
    def _():
        idx_copy(0, 0).start()
        idx_copy(0, 0).wait()
        gather(0)

        @pl.when(nt > 1)
        def _():
            idx_copy(1, 1).start()

    @pl.when(i + 1 < nt)
    def _():
        idx_copy(i + 1, 1 - slot).wait()
        gather(1 - slot)

        @pl.when(i + 2 < nt)
        def _():
            idx_copy(i + 2, slot).start()

    pltpu.make_async_copy(ys_hbm.at[pl.ds(0, TOP_K * rows * ROW_TILE)], ybuf.at[slot], sem_y.at[slot]).wait()
    part = rows // ROW_PARTS
    for r0 in range(0, rows, part):
        rs = slice(r0, r0 + part)
        route = route_ref[rs, :]
        y = jnp.zeros((part, D_MODEL), _F32)
        for k in range(TOP_K):
            y = y + _load_row_tiles(ybuf.at[slot], k * rows + r0, part) * route[:, TOP_K + k:TOP_K + k + 1]
        h2 = h1_ref[rs, :] + y
        u3 = _rms(h2, g_ple_ref[...]).astype(_BF16)
        gate = jax.nn.sigmoid(_dot(u3, wpg_ref[...]))
        h3 = h2 + gate * _dot(p_ref[rs, :].astype(_BF16), wpp_ref[...])
        out_ref[rs, :] = _rms(h3, g_fin_ref[...]) if last_layer else h3


def _final(dest, ys, h1, route, p2, g_ple, wpg, wpp, g_fin, last_layer):
    t = h1.shape[0]
    rows = MOE_ROWS

    def full(a):
        return pl.BlockSpec(a.shape, lambda i: (0,) * a.ndim)

    def tok(n):
        return pl.BlockSpec((rows, n), lambda i: (i, 0))

    return pl.pallas_call(
        functools.partial(_final_kernel, last_layer=last_layer),
        grid=(t // rows,),
        in_specs=[pl.BlockSpec(memory_space=pl.ANY), pl.BlockSpec(memory_space=pl.ANY), tok(D_MODEL), tok(LANES),
                  tok(PLE_DIM), full(g_ple), full(wpg), full(wpp), full(g_fin)],
        out_specs=tok(D_MODEL),
        out_shape=jax.ShapeDtypeStruct((t, D_MODEL), _F32),
        scratch_shapes=[pltpu.VMEM((2, TOP_K * rows * ROW_TILE, LANES), _F32),
                        pltpu.SMEM((2, SUBLANES, LANES), jnp.int32),
                        pltpu.SemaphoreType.DMA((2,)), pltpu.SemaphoreType.DMA((2,))],
        compiler_params=pltpu.CompilerParams(dimension_semantics=("arbitrary",), vmem_limit_bytes=VMEM_LIMIT),
        name="final",
    )(dest, ys, h1, route, p2, g_ple, wpg, wpp, g_fin)


def _rope_pad(w, rot):
    half = MLA_ROPE_DIM // 2
    body = jnp.concatenate([-w[:, half:], w[:, :half]], axis=1) if rot else w
    z = jnp.zeros((w.shape[0], MLA_NOPE_DIM), w.dtype)
    return jnp.concatenate([z, body, jnp.zeros((w.shape[0], LANES - MLA_NOPE_DIM - MLA_ROPE_DIM), w.dtype)], axis=1)


def _layer_weights(w_in, w_uq, w_ukv):
    o = Q_LORA_RANK + KV_LORA_RANK
    w_kr = w_in[:, o:o + MLA_ROPE_DIM]
    w1 = jnp.concatenate([w_in[:, :o], _rope_pad(w_kr, False), _rope_pad(w_kr, True), w_in[:, o + MLA_ROPE_DIM:]],
                         axis=1).astype(_BF16)
    uq = w_uq.reshape(Q_LORA_RANK, MLA_HEADS, MLA_NOPE_DIM + MLA_ROPE_DIM)
    zq = jnp.zeros((Q_LORA_RANK, MLA_HEADS, LANES - MLA_NOPE_DIM - MLA_ROPE_DIM), w_uq.dtype)
    wq = jnp.concatenate([uq, zq], axis=2).reshape(Q_LORA_RANK, MLA_HEADS * LANES).astype(_BF16)
    rope = uq[:, :, MLA_NOPE_DIM:]
    half = MLA_ROPE_DIM // 2
    rot = jnp.concatenate([jnp.zeros_like(uq[:, :, :MLA_NOPE_DIM]), -rope[:, :, half:], rope[:, :, :half], zq], axis=2)
    wqr = rot.reshape(Q_LORA_RANK, MLA_HEADS * LANES).astype(_BF16)
    ukv = w_ukv.reshape(KV_LORA_RANK, MLA_HEADS, MLA_NOPE_DIM + MLA_V_DIM)
    zk = jnp.zeros((KV_LORA_RANK, MLA_HEADS, LANES - MLA_NOPE_DIM), w_ukv.dtype)
    wk = jnp.concatenate([ukv[:, :, :MLA_NOPE_DIM], zk], axis=2).reshape(KV_LORA_RANK, MLA_HEADS * LANES).astype(_BF16)
    v = ukv[:, :, MLA_NOPE_DIM:].reshape(KV_LORA_RANK, HEAD_PAIRS, 2, MLA_V_DIM)
    zv = jnp.zeros((KV_LORA_RANK, HEAD_PAIRS, MLA_V_DIM), w_ukv.dtype)
    wv = jnp.stack([jnp.concatenate([v[:, :, 0], zv], axis=2), jnp.concatenate([zv, v[:, :, 1]], axis=2)], axis=2)
    wv = wv.reshape(KV_LORA_RANK, MLA_HEADS * LANES).astype(_BF16)
    return w1, wq, wqr, wk, wv


def kernel(x, p, positions, w_in, g_attn, g_cq, w_uq, g_ckv, w_ukv, g_out_mla, g_out_sb, w_o, g_moe, w_router,
           b_router, w_gu, b_gu, w_dn, b_dn, g_ple, w_ple_gate, w_ple_proj, g_final):
    b, s, d = x.shape
    t = b * s
    depth = w_in.shape[0]
    assert d == D_MODEL and ROW_TILE == SUBLANES and s % MLA_K_TILE == 0 and t % PROJ_ROWS == 0
    assert t % MOE_ROWS == 0 and (LANES // TOP_K) % DMA_GROUP == 0 and (t * TOP_K) % EXPERT_ROWS == 0
    assert EXPERT_ROWS & (EXPERT_ROWS - 1) == 0

    freq = ROPE_THETA ** (-jnp.arange(0, MLA_ROPE_DIM, 2, dtype=_F32) / MLA_ROPE_DIM)
    invf = jnp.concatenate([jnp.zeros((MLA_NOPE_DIM,), _F32), freq, freq,
                            jnp.zeros((LANES - MLA_NOPE_DIM - MLA_ROPE_DIM,), _F32)]).reshape(1, LANES)
    pos = positions.reshape(t, 1)
    idx = jnp.arange(SB_TILE)
    tri = -(idx[:, None] >= idx[None, :]).astype(_BF16)
    tri = jnp.concatenate([tri, tri], axis=0)
    idx = jnp.arange(PROJ_ROWS // POST_PARTS)
    ltri = (idx[None, :] < idx[:, None]).astype(_BF16)
    n_blocks = t * TOP_K // EXPERT_ROWS + N_EXPERTS

    h = x.reshape(t, d)
    for i in range(depth):
        w1, wq, wqr, wk, wv = _layer_weights(w_in[i], w_uq[i], w_ukv[i])
        qm, km, vm, qs, ks, vs = _proj(pos, h, g_attn[i].reshape(1, d), w1, g_cq[i].reshape(1, -1), wq, wqr,
                                       g_ckv[i].reshape(1, -1), wk, wv, invf)

        def seq(a):
            return a.reshape(b, s, a.shape[1])

        om = _mla(seq(qm), seq(km), seq(vm)).reshape(t, -1)
        os_ = _sb(seq(qs), seq(ks), seq(vs), tri).reshape(t, -1)
        h1, u2_tiles, route, counts = _post(om, os_, h, g_out_mla[i].reshape(1, -1), g_out_sb[i].reshape(1, -1),
                                            w_o[i].astype(_BF16), g_moe[i].reshape(1, d), w_router[i],
                                            b_router[i].reshape(1, -1), ltri)
        ids = route[:, :TOP_K].astype(jnp.int32).reshape(-1, LANES)
        rank = route[:, 2 * TOP_K:3 * TOP_K].astype(jnp.int32).reshape(-1, LANES)
        counts = counts.reshape(-1).astype(jnp.int32)
        dest, start, block_e, used = _layout(counts, ids, rank, n_blocks)
        xs = _dispatch(counts, start, dest, u2_tiles, n_blocks)
        ys = _experts(block_e, used, xs, w_gu[i], b_gu[i], w_dn[i], b_dn[i])
        h = _final(dest, ys, h1, route, p[i].reshape(t, -1), g_ple[i].reshape(1, d), w_ple_gate[i].astype(_BF16),
                   w_ple_proj[i].astype(_BF16), g_final.reshape(1, d), i == depth - 1)
    return h.reshape(b, s, d)
```

```python
import functools
import math

import jax
import jax.numpy as jnp
from jax import lax
from jax.experimental import pallas as pl
from jax.experimental.pallas import tpu as pltpu

D_MODEL = 1024
PLE_DIM = 256
MLA_HEADS = 8
MLA_NOPE_DIM = 64
MLA_ROPE_DIM = 32
MLA_V_DIM = 64
Q_LORA_RANK = 384
KV_LORA_RANK = 256
SB_HEADS = 8
SB_HEAD_DIM = 64
SB_WIDTH = SB_HEADS * SB_HEAD_DIM
ROPE_THETA = 10000.0
N_EXPERTS = 32
TOP_K = 4
D_FF = 1024
SWIGLU_LIMIT = 7.0
SWIGLU_ALPHA = 1.702
RMS_EPS = 1e-6

LANES = 128
SUBLANES = 8
ROW_TILE = D_MODEL // LANES
HEAD_PAIRS = MLA_HEADS // 2
PROJ_ROWS = 256
GROUP_HEADS = 4
SB_GROUP_HEADS = 4
MLA_Q_TILE = 512
MLA_K_TILE = 512
SB_TILE = 256
SB_FIRST_TILES = 2
EXPERT_ROWS = 256
ROW_PARTS = 2
POST_PARTS = 4
MOE_ROWS = SUBLANES * LANES // TOP_K
DMA_GROUP = 32
STAGE_SLOTS = 4
DMA_THREADS = 2
VMEM_LIMIT = 56 * 1024 * 1024
SB_LOG_FLOOR = -105.0

_F32 = jnp.float32
_BF16 = jnp.bfloat16


def _rms(x, g):
    return x * lax.rsqrt(jnp.mean(x * x, axis=-1, keepdims=True) + RMS_EPS) * g


def _dot(a, b):
    return jnp.dot(a, b, preferred_element_type=_F32)


def _dot_nt(a, b):
    return lax.dot_general(a, b, (((1,), (1,)), ((), ())), preferred_element_type=_F32)


def _store_row_tiles(ref, value, first_row=0):
    rows = value.shape[0]
    for g in range(ROW_TILE):
        ref[pl.ds(first_row * ROW_TILE + g, rows, stride=ROW_TILE), :] = value[:, g * LANES:(g + 1) * LANES]


def _load_row_tiles(ref, first_row, rows):
    return jnp.concatenate([ref[pl.ds(first_row * ROW_TILE + g, rows, stride=ROW_TILE), :] for g in range(ROW_TILE)],
                           axis=-1)


def _proj_kernel(pos_ref, x_ref, g_attn_ref, w1_ref, g_cq_ref, wq_ref, wqr_ref, g_ckv_ref, wk_ref, wv_ref,
                 invf_ref, qm_ref, km_ref, vm_ref, qs_ref, ks_ref, vs_ref):
    u = _rms(x_ref[...], g_attn_ref[...]).astype(_BF16)
    y = _dot(u, w1_ref[...])
    c_q = y[:, :Q_LORA_RANK]
    c_kv = y[:, Q_LORA_RANK:Q_LORA_RANK + KV_LORA_RANK]
    o = Q_LORA_RANK + KV_LORA_RANK
    k_r = y[:, o:o + LANES]
    k_r_rot = y[:, o + LANES:o + 2 * LANES]
    o += 2 * LANES
    q_s = y[:, o:o + SB_WIDTH]
    k_s = y[:, o + SB_WIDTH:o + 2 * SB_WIDTH]
    v_s = y[:, o + 2 * SB_WIDTH:o + 3 * SB_WIDTH]

    ang = pos_ref[...].astype(_F32) * invf_ref[...]
    cos = jnp.cos(ang)
    sin = jnp.sin(ang)

    cq_n = _rms(c_q, g_cq_ref[...]).astype(_BF16)
    q = _dot(cq_n, wq_ref[...])
    q_rot = _dot(cq_n, wqr_ref[...])
    ckv_n = _rms(c_kv, g_ckv_ref[...]).astype(_BF16)
    k = _dot(ckv_n, wk_ref[...])
    wide_lane = lax.broadcasted_iota(jnp.int32, (1, MLA_HEADS * LANES), 1)
    ones_lane = (wide_lane % (2 * LANES) == MLA_V_DIM) | (wide_lane % (2 * LANES) == LANES)
    vm_ref[...] = jnp.where(ones_lane, 1.0, _dot(ckv_n, wv_ref[...])).astype(_BF16)
    k_rope = k_r * cos + k_r_rot * sin

    q_scale = (MLA_NOPE_DIM + MLA_ROPE_DIM) ** -0.5 * math.log2(math.e)
    lane = lax.broadcasted_iota(jnp.int32, (1, LANES), 1)
    low = lane < SB_HEAD_DIM
    for h in range(MLA_HEADS):
        sl = slice(h * LANES, (h + 1) * LANES)
        qm_ref[:, sl] = ((q[:, sl] * cos + q_rot[:, sl] * sin) * q_scale).astype(_BF16)
        km_ref[:, sl] = (k[:, sl] + k_rope).astype(_BF16)
    qs_ref[...] = (q_s * (SB_HEAD_DIM ** -0.5)).astype(_BF16)
    for hp in range(HEAD_PAIRS):
        sl = slice(hp * LANES, (hp + 1) * LANES)
        for half, keep in ((0, low), (1, jnp.logical_not(low))):
            dst = slice((2 * hp + half) * LANES, (2 * hp + half + 1) * LANES)
            ks_ref[:, dst] = jnp.where(keep, k_s[:, sl], 0.0).astype(_BF16)
            vs_ref[:, dst] = jnp.where(keep, v_s[:, sl], 0.0).astype(_BF16)


def _proj(pos, x2, g_attn, w1, g_cq, wq, wqr, g_ckv, wk, wv, invf):
    t = x2.shape[0]
    rows = PROJ_ROWS
    wide = MLA_HEADS * LANES

    def full(a):
        return pl.BlockSpec(a.shape, lambda i: (0,) * a.ndim)

    def tok(n):
        return pl.BlockSpec((rows, n), lambda i: (i, 0))

    outs = [jax.ShapeDtypeStruct((t, n), _BF16) for n in (wide, wide, wide, SB_WIDTH, wide, wide)]
    return pl.pallas_call(
        _proj_kernel,
        grid=(t // rows,),
        in_specs=[tok(1), tok(D_MODEL), full(g_attn), full(w1), full(g_cq), full(wq), full(wqr), full(g_ckv),
                  full(wk), full(wv), full(invf)],
        out_specs=[tok(wide), tok(wide), tok(wide), tok(SB_WIDTH), tok(wide), tok(wide)],
        out_shape=outs,
        compiler_params=pltpu.CompilerParams(dimension_semantics=("parallel",), vmem_limit_bytes=VMEM_LIMIT),
        name="proj",
    )(pos, x2, g_attn, w1, g_cq, wq, wqr, g_ckv, wk, wv, invf)


def _mla_kernel(q_ref, k_ref, v_ref, o_ref, s_a, s_b, m_ref, acc_ref):
    tq, tk = MLA_Q_TILE, MLA_K_TILE
    qi = pl.program_id(2)
    heads = [slice(h * LANES, (h + 1) * LANES) for h in range(GROUP_HEADS)]

    def key_rows(kt):
        return pl.ds(pl.multiple_of(kt * tk, tk), tk)

    def score(dst, kt, mask=None):
        for h, sl in enumerate(heads):
            s = _dot_nt(q_ref[0, :, sl], k_ref[0, key_rows(kt), sl])
            dst[h] = s if mask is None else jnp.where(mask, s, -jnp.inf)

    def absorb(src, kt):
        for h, sl in enumerate(heads):
            s = src[h]
            m = m_ref[h]
            m_new = jnp.maximum(m, jnp.max(s, axis=-1, keepdims=True))
            alpha = jnp.exp2(m - m_new)
            p = jnp.exp2(s - m_new)
            m_ref[h] = m_new
            acc_ref[h] = alpha * acc_ref[h] + _dot(p.astype(_BF16), v_ref[0, key_rows(kt), sl])

    m_ref[...] = jnp.full(m_ref.shape, -jnp.inf, _F32)
    acc_ref[...] = jnp.zeros(acc_ref.shape, _F32)
    last = (qi * tq) // tk
    row = lax.broadcasted_iota(jnp.int32, (tq, tk), 0)
    col = lax.broadcasted_iota(jnp.int32, (tq, tk), 1)
    score(s_a, last, mask=last * tk + col <= qi * tq + row)

    def tile_in_a(i):
        return jnp.where(i == 0, last, 2 * i - 1)

    def pair(i, c):
        score(s_b, 2 * i)
        absorb(s_a, tile_in_a(i))
        score(s_a, 2 * i + 1)
        absorb(s_b, 2 * i)
        return c

    pairs = last // 2
    lax.fori_loop(0, pairs, pair, 0)

    @pl.when(last % 2 == 1)
    def _():
        score(s_b, last - 1)
        absorb(s_a, tile_in_a(pairs))
        absorb(s_b, last - 1)

    @pl.when(last % 2 == 0)
    def _():
        absorb(s_a, tile_in_a(pairs))

    low = lax.broadcasted_iota(jnp.int32, (tq, LANES), 1) < MLA_V_DIM
    for hp in range(GROUP_HEADS // 2):
        even, odd = acc_ref[2 * hp], acc_ref[2 * hp + 1]
        o_ref[0, :, hp * LANES:(hp + 1) * LANES] = jnp.where(low, even / even[:, MLA_V_DIM:MLA_V_DIM + 1],
                                                             odd / odd[:, :1])


def _mla(qm, km, vm):
    b, s, _ = qm.shape
    tq = MLA_Q_TILE
    gw = GROUP_HEADS * LANES
    return pl.pallas_call(
        _mla_kernel,
        grid=(b, MLA_HEADS // GROUP_HEADS, s // tq),
        in_specs=[pl.BlockSpec((1, tq, gw), lambda bi, g, qi: (bi, qi, g)),
                  pl.BlockSpec((1, s, gw), lambda bi, g, qi: (bi, 0, g)),
                  pl.BlockSpec((1, s, gw), lambda bi, g, qi: (bi, 0, g))],
        out_specs=pl.BlockSpec((1, tq, gw // 2), lambda bi, g, qi: (bi, qi, g)),
        out_shape=jax.ShapeDtypeStruct((b, s, HEAD_PAIRS * LANES), _F32),
        scratch_shapes=[pltpu.VMEM((GROUP_HEADS, tq, MLA_K_TILE), _F32), pltpu.VMEM((GROUP_HEADS, tq, MLA_K_TILE), _F32),
                        pltpu.VMEM((GROUP_HEADS, tq, 1), _F32), pltpu.VMEM((GROUP_HEADS, tq, LANES), _F32)],
        compiler_params=pltpu.CompilerParams(dimension_semantics=("parallel", "parallel", "arbitrary"),
                                             vmem_limit_bytes=VMEM_LIMIT),
        name="mla",
    )(qm, km, vm)


def _sb_kernel(q_ref, k_ref, v_ref, tri_ref, o_ref, rem_ref, acc_ref):
    tile = SB_TILE
    qi = pl.program_id(2)
    row = lax.broadcasted_iota(jnp.int32, (tile, tile), 0)
    col = lax.broadcasted_iota(jnp.int32, (tile, tile), 1)
    strict = col < row
    heads = [slice(h * LANES, (h + 1) * LANES) for h in range(SB_GROUP_HEADS)]

    def key_rows(kt):
        return pl.ds(pl.multiple_of(kt * tile, tile), tile)

    def tile_terms(kt, diagonal):
        tri = tri_ref[...]
        out = []
        for h, sl in enumerate(heads):
            q = q_ref[0, :, (h // 2) * LANES:(h // 2 + 1) * LANES]
            z = _dot_nt(q, k_ref[0, key_rows(kt), sl])
            sp = jnp.maximum(z, 0.0) + jnp.log(1.0 + jnp.exp(-jnp.abs(z)))
            if diagonal:
                sp = jnp.where(strict, sp, 0.0)
            hi = sp.astype(_BF16)
            lo = (sp - hi.astype(_F32)).astype(_BF16)
            out.append((z, _dot(jnp.concatenate([hi, lo], axis=-1), tri)))
        return out

    def absorb(kt, terms, rems, accs, diagonal, live=None):
        new_rems, new_accs = [], []
        for sl, (z, incl), rem, acc in zip(heads, terms, rems, accs):
            p = jnp.exp(z + (rem + incl))
            if diagonal:
                p = jnp.where(strict, p, 0.0)
            step = incl[:, :1]
            if live is not None:
                p = jnp.where(live, p, 0.0)
                step = jnp.where(live, step, 0.0)
            new_accs.append(acc + _dot(p.astype(_BF16), v_ref[0, key_rows(kt), sl]))
            new_rems.append(rem + step)
        return tuple(new_rems), tuple(new_accs)

    def rem_max(rems):
        return functools.reduce(jnp.maximum, [jnp.max(r) for r in rems])

    terms = [tile_terms(qi, True)] + [tile_terms(jnp.maximum(qi - d, 0), False) for d in range(1, SB_FIRST_TILES)]
    rems = tuple(jnp.zeros((tile, 1), _F32) for _ in range(SB_GROUP_HEADS))
    accs = tuple(jnp.zeros((tile, LANES), _F32) for _ in range(SB_GROUP_HEADS))
    rems, accs = absorb(qi, terms[0], rems, accs, True)
    for d in range(1, SB_FIRST_TILES):
        rems, accs = absorb(jnp.maximum(qi - d, 0), terms[d], rems, accs, False, live=qi >= d)

    def save(rems, accs):
        for h in range(SB_GROUP_HEADS):
            rem_ref[h] = rems[h]
            acc_ref[h] = accs[h]
    save(rems, accs)

    def cond(state):
        kt, worst = state
        return jnp.logical_and(kt >= 0, worst > SB_LOG_FLOOR)

    def body(state):
        kt, _ = state
        rems = tuple(rem_ref[h] for h in range(SB_GROUP_HEADS))
        accs = tuple(acc_ref[h] for h in range(SB_GROUP_HEADS))
        rems, accs = absorb(kt, tile_terms(kt, False), rems, accs, False)
        save(rems, accs)
        return kt - 1, rem_max(rems)

    lax.while_loop(cond, body, (qi - SB_FIRST_TILES, rem_max(rems)))
    for hp in range(SB_GROUP_HEADS // 2):
        o_ref[0, :, hp * LANES:(hp + 1) * LANES] = acc_ref[2 * hp] + acc_ref[2 * hp + 1]


def _sb(qs, ks, vs, tri):
    b, s, _ = qs.shape
    tile = SB_TILE
    gw = SB_GROUP_HEADS * LANES
    return pl.pallas_call(
        _sb_kernel,
        grid=(b, SB_HEADS // SB_GROUP_HEADS, s // tile),
        in_specs=[pl.BlockSpec((1, tile, gw // 2), lambda bi, g, qi: (bi, qi, g)),
                  pl.BlockSpec((1, s, gw), lambda bi, g, qi: (bi, 0, g)),
                  pl.BlockSpec((1, s, gw), lambda bi, g, qi: (bi, 0, g)),
                  pl.BlockSpec((2 * tile, tile), lambda bi, g, qi: (0, 0))],
        out_specs=pl.BlockSpec((1, tile, gw // 2), lambda bi, g, qi: (bi, qi, g)),
        out_shape=jax.ShapeDtypeStruct((b, s, HEAD_PAIRS * LANES), _F32),
        scratch_shapes=[pltpu.VMEM((SB_GROUP_HEADS, tile, 1), _F32), pltpu.VMEM((SB_GROUP_HEADS, tile, LANES), _F32)],
        compiler_params=pltpu.CompilerParams(dimension_semantics=("parallel", "parallel", "arbitrary"),
                                             vmem_limit_bytes=VMEM_LIMIT),
        name="sb",
    )(qs, ks, vs, tri)


def _post_kernel(om_ref, os_ref, x_ref, g_om_ref, g_os_ref, wo_ref, g_moe_ref, wr_ref, br_ref, ltri_ref,
                 h1_ref, u2_ref, route_ref, counts_ref, seen_ref):
    rows = PROJ_ROWS

    @pl.when(pl.program_id(0) == 0)
    def _():
        seen_ref[...] = jnp.zeros_like(seen_ref)

    mixed = jnp.concatenate([_rms(om_ref[...], g_om_ref[...]), _rms(os_ref[...], g_os_ref[...])], axis=-1)
    h1 = x_ref[...] + _dot(mixed.astype(_BF16), wo_ref[...])
    h1_ref[...] = h1
    u2 = _rms(h1, g_moe_ref[...])
    _store_row_tiles(u2_ref, u2)
    w_hi = wr_ref[...].astype(_BF16)
    w_lo = (wr_ref[...] - w_hi.astype(_F32)).astype(_BF16)
    u_hi = u2.astype(_BF16)
    u_lo = (u2 - u_hi.astype(_F32)).astype(_BF16)
    logits = _dot(u_hi, w_hi) + _dot(u_hi, w_lo) + _dot(u_lo, w_hi) + br_ref[...]

    part = rows // POST_PARTS
    lane = lax.broadcasted_iota(jnp.int32, (part, N_EXPERTS), 1).astype(_F32)
    out_lane = lax.broadcasted_iota(jnp.int32, (part, LANES), 1)
    work = [logits[r0:r0 + part, :] for r0 in range(0, rows, part)]
    onehot = [jnp.zeros((part, N_EXPERTS), _F32) for _ in work]
    ids = [[] for _ in work]
    tops = [[] for _ in work]
    for _ in range(TOP_K):
        for n in range(POST_PARTS):
            top = jnp.max(work[n], axis=-1, keepdims=True)
            idx = jnp.min(jnp.where(work[n] == top, lane, float(N_EXPERTS)), axis=-1, keepdims=True)
            hit = lane == idx
            onehot[n] = jnp.where(hit, 1.0, onehot[n])
            work[n] = jnp.where(hit, -jnp.inf, work[n])
            ids[n].append(idx)
            tops[n].append(top)

    seen = seen_ref[...]
    for n in range(POST_PARTS):
        exps = [jnp.exp(tp - tops[n][0]) for tp in tops[n]]
        denom = exps[0] + exps[1] + exps[2] + exps[3]
        gates = [e / denom for e in exps]
        before = seen + _dot(ltri_ref[...], onehot[n].astype(_BF16))
        ranks = [jnp.sum(jnp.where(lane == idx, before, 0.0), axis=-1, keepdims=True) for idx in ids[n]]
        seen = seen + jnp.sum(onehot[n], axis=0, keepdims=True)
        route = jnp.zeros((part, LANES), _F32)
        for j, val in enumerate(ids[n] + gates + ranks):
            route = jnp.where(out_lane == j, val, route)
        route_ref[n * part:(n + 1) * part, :] = route
    seen_ref[...] = seen
    counts_ref[...] = seen


def _post(om, os_, x2, g_om, g_os, wo, g_moe, wr, br, ltri):
    t = x2.shape[0]
    rows = PROJ_ROWS

    def full(a):
        return pl.BlockSpec(a.shape, lambda i: (0,) * a.ndim)

    def tok(n):
        return pl.BlockSpec((rows, n), lambda i: (i, 0))

    return pl.pallas_call(
        _post_kernel,
        grid=(t // rows,),
        in_specs=[tok(om.shape[1]), tok(os_.shape[1]), tok(D_MODEL), full(g_om), full(g_os), full(wo), full(g_moe),
                  full(wr), full(br), full(ltri)],
        out_specs=[tok(D_MODEL), pl.BlockSpec((rows * ROW_TILE, LANES), lambda i: (i, 0)), tok(LANES),
                   pl.BlockSpec((1, N_EXPERTS), lambda i: (0, 0))],
        out_shape=[jax.ShapeDtypeStruct((t, D_MODEL), _F32), jax.ShapeDtypeStruct((t * ROW_TILE, LANES), _F32),
                   jax.ShapeDtypeStruct((t, LANES), _F32), jax.ShapeDtypeStruct((1, N_EXPERTS), _F32)],
        scratch_shapes=[pltpu.VMEM((1, N_EXPERTS), _F32)],
        compiler_params=pltpu.CompilerParams(dimension_semantics=("arbitrary",), vmem_limit_bytes=VMEM_LIMIT),
        name="post",
    )(om, os_, x2, g_om, g_os, wo, g_moe, wr, br, ltri)


def _padded_count(cnt_ref, e):
    shift = EXPERT_ROWS.bit_length() - 1
    return lax.shift_left(lax.shift_right_logical(cnt_ref[e] + (EXPERT_ROWS - 1), shift), shift)


def _layout_kernel(cnt_ref, route_ref, dest_ref, start_ref, be_ref, used_ref):
    shift = EXPERT_ROWS.bit_length() - 1
    rows = MOE_ROWS
    nb = be_ref.shape[0]

    def place(e, off):
        padded = _padded_count(cnt_ref, e)
        start_ref[e] = off

        def mark(j, c):
            be_ref[lax.shift_right_logical(off, shift) + j] = e
            return c
        lax.fori_loop(0, lax.shift_right_logical(padded, shift), mark, 0)
        return off + padded
    total = lax.fori_loop(0, N_EXPERTS, place, 0)
    used_ref[0] = lax.shift_right_logical(total, shift)

    def tail(b, c):
        be_ref[b] = N_EXPERTS - 1
        return c
    lax.fori_loop(lax.shift_right_logical(total, shift), nb, tail, 0)

    lane = lax.broadcasted_iota(jnp.int32, (1, LANES), 1)
    rank_lanes = jnp.logical_and(lane >= 2 * TOP_K, lane < 3 * TOP_K)
    pick = (lax.broadcasted_iota(jnp.int32, (2 * SUBLANES, LANES), 0)
            == lax.broadcasted_iota(jnp.int32, (2 * SUBLANES, LANES), 1)).astype(_BF16)

    def tile(i, c):
        r = route_ref[pl.ds(pl.multiple_of(i * rows, rows), rows), :]
        high = jnp.floor(r * (1.0 / 256.0))
        low_digits = jnp.where(rank_lanes, r - 256.0 * high, r)
        high_digits = jnp.where(rank_lanes, high, 0.0)
        low_t = _dot_nt(pick, low_digits.astype(_BF16))
        high_t = _dot_nt(pick, high_digits.astype(_BF16))
        ids = low_t[:TOP_K, :].astype(jnp.int32)
        dest = (256.0 * high_t[2 * TOP_K:3 * TOP_K, :] + low_t[2 * TOP_K:3 * TOP_K, :]).astype(jnp.int32)
        for e in range(N_EXPERTS):
            dest = dest + jnp.where(ids == e, start_ref[e], 0)
        lines = [dest[k:k + 1, half * LANES:(half + 1) * LANES] for k in range(TOP_K) for half in range(rows // LANES)]
        dest_ref[pl.ds(pl.multiple_of(i * SUBLANES, SUBLANES), SUBLANES), :] = jnp.concatenate(lines, axis=0)
        return c
    lax.fori_loop(0, route_ref.shape[0] // rows, tile, 0)


def _layout(counts, route, n_blocks):
    t = route.shape[0]
    pair_shape = (t * TOP_K // LANES, LANES)
    grid_spec = pltpu.PrefetchScalarGridSpec(
        num_scalar_prefetch=1,
        grid=(1,),
        in_specs=[pl.BlockSpec(route.shape, lambda i, c: (0, 0))],
        out_specs=[pl.BlockSpec(pair_shape, lambda i, c: (0, 0)), pl.BlockSpec(memory_space=pltpu.SMEM),
                   pl.BlockSpec(memory_space=pltpu.SMEM), pl.BlockSpec(memory_space=pltpu.SMEM)],
    )
    return pl.pallas_call(
        _layout_kernel,
        grid_spec=grid_spec,
        out_shape=[jax.ShapeDtypeStruct(pair_shape, jnp.int32), jax.ShapeDtypeStruct((N_EXPERTS,), jnp.int32),
                   jax.ShapeDtypeStruct((n_blocks,), jnp.int32), jax.ShapeDtypeStruct((1,), jnp.int32)],
        compiler_params=pltpu.CompilerParams(dimension_semantics=("arbitrary",), vmem_limit_bytes=VMEM_LIMIT),
        name="layout",
    )(counts, route)


def _tile_rows(row):
    return pl.ds(pl.multiple_of(row * ROW_TILE, ROW_TILE), ROW_TILE)


def _for_each_pair(idx, s, body):
    halves = MOE_ROWS // LANES

    def line(g, c):
        choice = g // halves
        first = lax.rem(g, halves) * LANES
        for t0 in range(0, LANES, DMA_GROUP):
            vals = [idx[s, g, t0 + j] for j in range(DMA_GROUP)]
            for j, row in enumerate(vals):
                body(first + t0 + j, choice, row, j % DMA_THREADS)
        return c
    lax.fori_loop(0, SUBLANES, line, 0)


def _dispatch_kernel(cnt_ref, start_ref, dest_hbm, u_hbm, xs_hbm, idx, stage, zrow, zblk, sem_d, sem_l, sem_i, sem_z):
    rows = MOE_ROWS
    shift = EXPERT_ROWS.bit_length() - 1
    i = pl.program_id(0)
    nt = pl.num_programs(0)
    nb = xs_hbm.shape[0] // (EXPERT_ROWS * ROW_TILE)
    slot = lax.rem(i, 2)
    stage_slot = lax.rem(i, STAGE_SLOTS)

    def idx_copy(tile, s):
        return pltpu.make_async_copy(dest_hbm.at[pl.ds(pl.multiple_of(tile * SUBLANES, SUBLANES), SUBLANES)],
                                     idx.at[s], sem_i.at[s])

    def load(tile):
        s = lax.rem(tile, STAGE_SLOTS)
        src = pl.ds(pl.multiple_of(tile * (rows * ROW_TILE), rows * ROW_TILE), rows * ROW_TILE)
        return pltpu.make_async_copy(u_hbm.at[src], stage.at[s], sem_l.at[s])

    def wait_rows(s):
        chunk = pl.ds(0, rows * ROW_TILE)
        for _ in range(TOP_K):
            pltpu.make_async_copy(stage.at[s], xs_hbm.at[chunk], sem_d.at[s]).wait()

    def zero_row(r):
        return pltpu.make_async_copy(zrow, xs_hbm.at[_tile_rows(r)], sem_z)

    def zero_block(b):
        dst = pl.ds(pl.multiple_of(b * (EXPERT_ROWS * ROW_TILE), EXPERT_ROWS * ROW_TILE), EXPERT_ROWS * ROW_TILE)
        return pltpu.make_async_copy(zblk, xs_hbm.at[dst], sem_z)

    def pad_rows(e, fn):
        def one(r, c):
            fn(zero_row(start_ref[e] + r))
            return c
        lax.fori_loop(cnt_ref[e], _padded_count(cnt_ref, e), one, 0)

    def tail_blocks(fn):
        used = start_ref[N_EXPERTS - 1] + _padded_count(cnt_ref, N_EXPERTS - 1)

        def one(b, c):
            fn(zero_block(b))
            return c
        lax.fori_loop(lax.shift_right_logical(used, shift), nb, one, 0)

    @pl.when(i == 0)
    def _():
        idx_copy(0, 0).start()
        load(0).start()

        @pl.when(nt > 1)
        def _():
            load(1).start()
        zrow[...] = jnp.zeros_like(zrow)
        zblk[...] = jnp.zeros_like(zblk)
        for fn in (lambda c: c.start(), lambda c: c.wait()):
            def per_expert(e, c, fn=fn):
                pad_rows(e, fn)
                return c
            lax.fori_loop(0, N_EXPERTS, per_expert, 0)
            tail_blocks(fn)

    idx_copy(i, slot).wait()

    @pl.when(i + 1 < nt)
    def _():
        idx_copy(i + 1, 1 - slot).start()

    @pl.when(i >= 2)
    def _():
        wait_rows(lax.rem(i + 2, STAGE_SLOTS))

    @pl.when(i + 2 < nt)
    def _():
        load(i + 2).start()

    load(i).wait()

    def copy_row(t, k, row, thread):
        del k
        pltpu.make_async_copy(stage.at[stage_slot, _tile_rows(t)], xs_hbm.at[_tile_rows(row)],
                              sem_d.at[stage_slot]).start(priority=thread)
    _for_each_pair(idx, slot, copy_row)

    @pl.when(i == nt - 1)
    def _():
        @pl.when(nt > 1)
        def _():
            wait_rows(lax.rem(i + STAGE_SLOTS - 1, STAGE_SLOTS))
        wait_rows(stage_slot)


def _dispatch(counts, start, dest, u2_tiles, n_blocks):
    nt = dest.shape[0] // SUBLANES
    grid_spec = pltpu.PrefetchScalarGridSpec(
        num_scalar_prefetch=2,
        grid=(nt,),
        in_specs=[pl.BlockSpec(memory_space=pl.ANY), pl.BlockSpec(memory_space=pl.ANY)],
        out_specs=pl.BlockSpec(memory_space=pl.ANY),
        scratch_shapes=[pltpu.SMEM((2, SUBLANES, LANES), jnp.int32),
                        pltpu.VMEM((STAGE_SLOTS, MOE_ROWS * ROW_TILE, LANES), _F32),
                        pltpu.VMEM((ROW_TILE, LANES), _F32), pltpu.VMEM((EXPERT_ROWS * ROW_TILE, LANES), _F32),
                        pltpu.SemaphoreType.DMA((STAGE_SLOTS,)), pltpu.SemaphoreType.DMA((STAGE_SLOTS,)),
                        pltpu.SemaphoreType.DMA((2,)), pltpu.SemaphoreType.DMA(())],
    )
    return pl.pallas_call(
        _dispatch_kernel,
        grid_spec=grid_spec,
        out_shape=jax.ShapeDtypeStruct((n_blocks * EXPERT_ROWS * ROW_TILE, LANES), _F32),
        compiler_params=pltpu.CompilerParams(dimension_semantics=("arbitrary",), vmem_limit_bytes=VMEM_LIMIT),
        name="dispatch",
    )(counts, start, dest, u2_tiles)


def _expert_kernel(be_ref, used_ref, xs_ref, wgu_hbm, bgu_ref, wdn_hbm, bdn_ref, ys_ref,
                   wgu_f32, wdn_f32, wgu_bf, wdn_bf, sem_w):
    rows = EXPERT_ROWS
    i = pl.program_id(0)
    nb = pl.num_programs(0)
    used = used_ref[0]
    e = be_ref[i]
    prev = be_ref[jnp.maximum(i - 1, 0)]
    in_use = i < used

    def fetch(expert):
        return (pltpu.make_async_copy(wgu_hbm.at[expert], wgu_f32, sem_w.at[0]),
                pltpu.make_async_copy(wdn_hbm.at[expert], wdn_f32, sem_w.at[1]))

    @pl.when(i == 0)
    def _():
        for c in fetch(e):
            c.start()

    @pl.when(jnp.logical_and(in_use, jnp.logical_or(i == 0, e != prev)))
    def _():
        for c in fetch(e):
            c.wait()
        chunk = 128

        def cast(c, carry):
            rs = pl.ds(pl.multiple_of(c * chunk, chunk), chunk)
            wgu_bf[rs, :] = wgu_f32[rs, :].astype(_BF16)
            wdn_bf[rs, :] = wdn_f32[rs, :].astype(_BF16)
            return carry
        lax.fori_loop(0, D_MODEL // chunk, cast, 0)

        def same_expert(j):
            return jnp.logical_and(j < used, be_ref[jnp.minimum(j, nb - 1)] == e)
        nxt = lax.while_loop(same_expert, lambda j: j + 1, i)

        @pl.when(nxt < used)
        def _():
            for c in fetch(be_ref[jnp.minimum(nxt, nb - 1)]):
                c.start()

    @pl.when(in_use)
    def _():
        x = _load_row_tiles(xs_ref, 0, rows).astype(_BF16)
        hh = _dot(x, wgu_bf[...]) + bgu_ref[0]
        glu = jnp.minimum(hh[:, :D_FF], SWIGLU_LIMIT)
        lin = jnp.clip(hh[:, D_FF:], -SWIGLU_LIMIT, SWIGLU_LIMIT)
        act = glu * jax.nn.sigmoid(SWIGLU_ALPHA * glu) * (lin + 1.0)
        _store_row_tiles(ys_ref, _dot(act.astype(_BF16), wdn_bf[...]) + bdn_ref[0])

    @pl.when(jnp.logical_not(in_use))
    def _():
        ys_ref[...] = jnp.zeros_like(ys_ref)


def _experts(block_e, used, xs, w_gu, b_gu, w_dn, b_dn):
    nb = block_e.shape[0]
    rows = EXPERT_ROWS
    grid_spec = pltpu.PrefetchScalarGridSpec(
        num_scalar_prefetch=2,
        grid=(nb,),
        in_specs=[pl.BlockSpec((rows * ROW_TILE, LANES), lambda i, be, u: (i, 0)),
                  pl.BlockSpec(memory_space=pl.ANY),
                  pl.BlockSpec((1, 1, 2 * D_FF), lambda i, be, u: (be[i], 0, 0)),
                  pl.BlockSpec(memory_space=pl.ANY),
                  pl.BlockSpec((1, 1, D_MODEL), lambda i, be, u: (be[i], 0, 0))],
        out_specs=pl.BlockSpec((rows * ROW_TILE, LANES), lambda i, be, u: (i, 0)),
        scratch_shapes=[pltpu.VMEM((D_MODEL, 2 * D_FF), _F32), pltpu.VMEM((D_FF, D_MODEL), _F32),
                        pltpu.VMEM((D_MODEL, 2 * D_FF), _BF16), pltpu.VMEM((D_FF, D_MODEL), _BF16),
                        pltpu.SemaphoreType.DMA((2,))],
    )
    return pl.pallas_call(
        _expert_kernel,
        grid_spec=grid_spec,
        out_shape=jax.ShapeDtypeStruct(xs.shape, _F32),
        compiler_params=pltpu.CompilerParams(dimension_semantics=("arbitrary",), vmem_limit_bytes=VMEM_LIMIT),
        name="experts",
    )(block_e, used, xs, w_gu, b_gu.reshape(N_EXPERTS, 1, 2 * D_FF), w_dn, b_dn.reshape(N_EXPERTS, 1, D_MODEL))


def _final_kernel(dest_hbm, ys_hbm, h1_ref, route_ref, p_ref, g_ple_ref, wpg_ref, wpp_ref, g_fin_ref,
                  out_ref, ybuf, idx, sem_y, sem_i, *, last_layer):
    rows = MOE_ROWS
    i = pl.program_id(0)
    nt = pl.num_programs(0)
    slot = lax.rem(i, 2)

    def idx_copy(tile, s):
        return pltpu.make_async_copy(dest_hbm.at[pl.ds(pl.multiple_of(tile * SUBLANES, SUBLANES), SUBLANES)],
                                     idx.at[s], sem_i.at[s])

    def gather(s):
        def copy_row(t, k, row, thread):
            pltpu.make_async_copy(ys_hbm.at[_tile_rows(row)], ybuf.at[s, _tile_rows(k * rows + t)],
                                  sem_y.at[s]).start(priority=thread)
        _for_each_pair(idx, s, copy_row)

    @pl.when(i == 0)
    def _():
        idx_copy(0, 0).start()
        idx_copy(0, 0).wait()
        gather(0)

        @pl.when(nt > 1)
        def _():
            idx_copy(1, 1).start()

    @pl.when(i + 1 < nt)
    def _():
        idx_copy(i + 1, 1 - slot).wait()
        gather(1 - slot)

        @pl.when(i + 2 < nt)
        def _():
            idx_copy(i + 2, slot).start()

    pltpu.make_async_copy(ys_hbm.at[pl.ds(0, TOP_K * rows * ROW_TILE)], ybuf.at[slot], sem_y.at[slot]).wait()
    part = rows // ROW_PARTS
    for r0 in range(0, rows, part):
        rs = slice(r0, r0 + part)
        route = route_ref[rs, :]
        y = jnp.zeros((part, D_MODEL), _F32)
        for k in range(TOP_K):
            y = y + _load_row_tiles(ybuf.at[slot], k * rows + r0, part) * route[:, TOP_K + k:TOP_K + k + 1]
        h2 = h1_ref[rs, :] + y
        u3 = _rms(h2, g_ple_ref[...]).astype(_BF16)
        gate = jax.nn.sigmoid(_dot(u3, wpg_ref[...]))
        h3 = h2 + gate * _dot(p_ref[rs, :].astype(_BF16), wpp_ref[...])
        out_ref[rs, :] = _rms(h3, g_fin_ref[...]) if last_layer else h3


def _final(dest, ys, h1, route, p2, g_ple, wpg, wpp, g_fin, last_layer):
    t = h1.shape[0]
    rows = MOE_ROWS

    def full(a):
        return pl.BlockSpec(a.shape, lambda i: (0,) * a.ndim)

    def tok(n):
        return pl.BlockSpec((rows, n), lambda i: (i, 0))

    return pl.pallas_call(
        functools.partial(_final_kernel, last_layer=last_layer),
        grid=(t // rows,),
        in_specs=[pl.BlockSpec(memory_space=pl.ANY), pl.BlockSpec(memory_space=pl.ANY), tok(D_MODEL), tok(LANES),
                  tok(PLE_DIM), full(g_ple), full(wpg), full(wpp), full(g_fin)],
        out_specs=tok(D_MODEL),
        out_shape=jax.ShapeDtypeStruct((t, D_MODEL), _F32),
        scratch_shapes=[pltpu.VMEM((2, TOP_K * rows * ROW_TILE, LANES), _F32),
                        pltpu.SMEM((2, SUBLANES, LANES), jnp.int32),
                        pltpu.SemaphoreType.DMA((2,)), pltpu.SemaphoreType.DMA((2,))],
        compiler_params=pltpu.CompilerParams(dimension_semantics=("arbitrary",), vmem_limit_bytes=VMEM_LIMIT),
        name="final",
    )(dest, ys, h1, route, p2, g_ple, wpg, wpp, g_fin)


def _rope_pad(w, rot):
    half = MLA_ROPE_DIM // 2
    body = jnp.concatenate([-w[:, half:], w[:, :half]], axis=1) if rot else w
    z = jnp.zeros((w.shape[0], MLA_NOPE_DIM), w.dtype)
    return jnp.concatenate([z, body, jnp.zeros((w.shape[0], LANES - MLA_NOPE_DIM - MLA_ROPE_DIM), w.dtype)], axis=1)


def _layer_weights(w_in, w_uq, w_ukv):
    o = Q_LORA_RANK + KV_LORA_RANK
    w_kr = w_in[:, o:o + MLA_ROPE_DIM]
    w1 = jnp.concatenate([w_in[:, :o], _rope_pad(w_kr, False), _rope_pad(w_kr, True), w_in[:, o + MLA_ROPE_DIM:]],
                         axis=1).astype(_BF16)
    uq = w_uq.reshape(Q_LORA_RANK, MLA_HEADS, MLA_NOPE_DIM + MLA_ROPE_DIM)
    zq = jnp.zeros((Q_LORA_RANK, MLA_HEADS, LANES - MLA_NOPE_DIM - MLA_ROPE_DIM), w_uq.dtype)
    wq = jnp.concatenate([uq, zq], axis=2).reshape(Q_LORA_RANK, MLA_HEADS * LANES).astype(_BF16)
    rope = uq[:, :, MLA_NOPE_DIM:]
    half = MLA_ROPE_DIM // 2
    rot = jnp.concatenate([jnp.zeros_like(uq[:, :, :MLA_NOPE_DIM]), -rope[:, :, half:], rope[:, :, :half], zq], axis=2)
    wqr = rot.reshape(Q_LORA_RANK, MLA_HEADS * LANES).astype(_BF16)
    ukv = w_ukv.reshape(KV_LORA_RANK, MLA_HEADS, MLA_NOPE_DIM + MLA_V_DIM)
    zk = jnp.zeros((KV_LORA_RANK, MLA_HEADS, LANES - MLA_NOPE_DIM), w_ukv.dtype)
    wk = jnp.concatenate([ukv[:, :, :MLA_NOPE_DIM], zk], axis=2).reshape(KV_LORA_RANK, MLA_HEADS * LANES).astype(_BF16)
    v = ukv[:, :, MLA_NOPE_DIM:].reshape(KV_LORA_RANK, HEAD_PAIRS, 2, MLA_V_DIM)
    zv = jnp.zeros((KV_LORA_RANK, HEAD_PAIRS, MLA_V_DIM), w_ukv.dtype)
    wv = jnp.stack([jnp.concatenate([v[:, :, 0], zv], axis=2), jnp.concatenate([zv, v[:, :, 1]], axis=2)], axis=2)
    wv = wv.reshape(KV_LORA_RANK, MLA_HEADS * LANES).astype(_BF16)
    return w1, wq, wqr, wk, wv


def kernel(x, p, positions, w_in, g_attn, g_cq, w_uq, g_ckv, w_ukv, g_out_mla, g_out_sb, w_o, g_moe, w_router,
           b_router, w_gu, b_gu, w_dn, b_dn, g_ple, w_ple_gate, w_ple_proj, g_final):
    b, s, d = x.shape
    t = b * s
    depth = w_in.shape[0]
    assert d == D_MODEL and ROW_TILE == SUBLANES and s % MLA_K_TILE == 0 and t % PROJ_ROWS == 0
    assert t % MOE_ROWS == 0 and LANES % DMA_GROUP == 0 and MOE_ROWS % LANES == 0 and (t * TOP_K) % EXPERT_ROWS == 0
    assert EXPERT_ROWS & (EXPERT_ROWS - 1) == 0

    freq = ROPE_THETA ** (-jnp.arange(0, MLA_ROPE_DIM, 2, dtype=_F32) / MLA_ROPE_DIM)
    invf = jnp.concatenate([jnp.zeros((MLA_NOPE_DIM,), _F32), freq, freq,
                            jnp.zeros((LANES - MLA_NOPE_DIM - MLA_ROPE_DIM,), _F32)]).reshape(1, LANES)
    pos = positions.reshape(t, 1)
    idx = jnp.arange(SB_TILE)
    tri = -(idx[:, None] >= idx[None, :]).astype(_BF16)
    tri = jnp.concatenate([tri, tri], axis=0)
    idx = jnp.arange(PROJ_ROWS // POST_PARTS)
    ltri = (idx[None, :] < idx[:, None]).astype(_BF16)
    n_blocks = t * TOP_K // EXPERT_ROWS + N_EXPERTS

    h = x.reshape(t, d)
    for i in range(depth):
        w1, wq, wqr, wk, wv = _layer_weights(w_in[i], w_uq[i], w_ukv[i])
        qm, km, vm, qs, ks, vs = _proj(pos, h, g_attn[i].reshape(1, d), w1, g_cq[i].reshape(1, -1), wq, wqr,
                                       g_ckv[i].reshape(1, -1), wk, wv, invf)

        def seq(a):
            return a.reshape(b, s, a.shape[1])

        om = _mla(seq(qm), seq(km), seq(vm)).reshape(t, -1)
        os_ = _sb(seq(qs), seq(ks), seq(vs), tri).reshape(t, -1)
        h1, u2_tiles, route, counts = _post(om, os_, h, g_out_mla[i].reshape(1, -1), g_out_sb[i].reshape(1, -1),
                                            w_o[i].astype(_BF16), g_moe[i].reshape(1, d), w_router[i],
                                            b_router[i].reshape(1, -1), ltri)
        counts = counts.reshape(-1).astype(jnp.int32)
        dest, start, block_e, used = _layout(counts, route, n_blocks)
        xs = _dispatch(counts, start, dest, u2_tiles, n_blocks)
        ys = _experts(block_e, used, xs, w_gu[i], b_gu[i], w_dn[i], b_dn[i])
        h = _final(dest, ys, h1, route, p[i].reshape(t, -1), g_ple[i].reshape(1, d), w_ple_gate[i].astype(_BF16),
                   w_ple_proj[i].astype(_BF16), g_final.reshape(1, d), i == depth - 1)
    return h.reshape(b, s, d)
```

```python
import functools
import math

import jax
import jax.numpy as jnp
from jax import lax
from jax.experimental import pallas as pl
from jax.experimental.pallas import tpu as pltpu

D_MODEL = 1024
PLE_DIM = 256
MLA_HEADS = 8
MLA_NOPE_DIM = 64
MLA_ROPE_DIM = 32
MLA_V_DIM = 64
Q_LORA_RANK = 384
KV_LORA_RANK = 256
SB_HEADS = 8
SB_HEAD_DIM = 64
SB_WIDTH = SB_HEADS * SB_HEAD_DIM
ROPE_THETA = 10000.0
N_EXPERTS = 32
TOP_K = 4
D_FF = 1024
SWIGLU_LIMIT = 7.0
SWIGLU_ALPHA = 1.702
RMS_EPS = 1e-6

LANES = 128
SUBLANES = 8
ROW_TILE = D_MODEL // LANES
HEAD_PAIRS = MLA_HEADS // 2
PROJ_ROWS = 256
GROUP_HEADS = 4
SB_GROUP_HEADS = 4
MLA_Q_TILE = 512
MLA_K_TILE = 512
SB_TILE = 256
SB_FIRST_TILES = 2
EXPERT_ROWS = 256
ROW_PARTS = 2
POST_PARTS = 4
MOE_ROWS = SUBLANES * LANES // TOP_K
DMA_GROUP = 32
STAGE_SLOTS = 4
DMA_THREADS = 2
VMEM_LIMIT = 56 * 1024 * 1024
SB_LOG_FLOOR = -105.0

_F32 = jnp.float32
_BF16 = jnp.bfloat16


def _rms(x, g):
    return x * lax.rsqrt(jnp.mean(x * x, axis=-1, keepdims=True) + RMS_EPS) * g


def _dot(a, b):
    return jnp.dot(a, b, preferred_element_type=_F32)


def _dot_nt(a, b):
    return lax.dot_general(a, b, (((1,), (1,)), ((), ())), preferred_element_type=_F32)


def _store_row_tiles(ref, value, first_row=0):
    rows = value.shape[0]
    for g in range(ROW_TILE):
        ref[pl.ds(first_row * ROW_TILE + g, rows, stride=ROW_TILE), :] = value[:, g * LANES:(g + 1) * LANES]


def _load_row_tiles(ref, first_row, rows):
    return jnp.concatenate([ref[pl.ds(first_row * ROW_TILE + g, rows, stride=ROW_TILE), :] for g in range(ROW_TILE)],
                           axis=-1)


def _proj_kernel(pos_ref, x_ref, g_attn_ref, w1_ref, g_cq_ref, wq_ref, wqr_ref, g_ckv_ref, wk_ref, wv_ref,
                 invf_ref, qm_ref, km_ref, vm_ref, qs_ref, ks_ref, vs_ref):
    u = _rms(x_ref[...], g_attn_ref[...]).astype(_BF16)
    y = _dot(u, w1_ref[...])
    c_q = y[:, :Q_LORA_RANK]
    c_kv = y[:, Q_LORA_RANK:Q_LORA_RANK + KV_LORA_RANK]
    o = Q_LORA_RANK + KV_LORA_RANK
    k_r = y[:, o:o + LANES]
    k_r_rot = y[:, o + LANES:o + 2 * LANES]
    o += 2 * LANES
    q_s = y[:, o:o + SB_WIDTH]
    k_s = y[:, o + SB_WIDTH:o + 2 * SB_WIDTH]
    v_s = y[:, o + 2 * SB_WIDTH:o + 3 * SB_WIDTH]

    ang = pos_ref[...].astype(_F32) * invf_ref[...]
    cos = jnp.cos(ang)
    sin = jnp.sin(ang)

    cq_n = _rms(c_q, g_cq_ref[...]).astype(_BF16)
    q = _dot(cq_n, wq_ref[...])
    q_rot = _dot(cq_n, wqr_ref[...])
    ckv_n = _rms(c_kv, g_ckv_ref[...]).astype(_BF16)
    k = _dot(ckv_n, wk_ref[...])
    wide_lane = lax.broadcasted_iota(jnp.int32, (1, MLA_HEADS * LANES), 1)
    ones_lane = (wide_lane % (2 * LANES) == MLA_V_DIM) | (wide_lane % (2 * LANES) == LANES)
    vm_ref[...] = jnp.where(ones_lane, 1.0, _dot(ckv_n, wv_ref[...])).astype(_BF16)
    k_rope = k_r * cos + k_r_rot * sin

    q_scale = (MLA_NOPE_DIM + MLA_ROPE_DIM) ** -0.5 * math.log2(math.e)
    lane = lax.broadcasted_iota(jnp.int32, (1, LANES), 1)
    low = lane < SB_HEAD_DIM
    for h in range(MLA_HEADS):
        sl = slice(h * LANES, (h + 1) * LANES)
        qm_ref[:, sl] = ((q[:, sl] * cos + q_rot[:, sl] * sin) * q_scale).astype(_BF16)
        km_ref[:, sl] = (k[:, sl] + k_rope).astype(_BF16)
    qs_ref[...] = (q_s * (SB_HEAD_DIM ** -0.5)).astype(_BF16)
    for hp in range(HEAD_PAIRS):
        sl = slice(hp * LANES, (hp + 1) * LANES)
        for half, keep in ((0, low), (1, jnp.logical_not(low))):
            dst = slice((2 * hp + half) * LANES, (2 * hp + half + 1) * LANES)
            ks_ref[:, dst] = jnp.where(keep, k_s[:, sl], 0.0).astype(_BF16)
            vs_ref[:, dst] = jnp.where(keep, v_s[:, sl], 0.0).astype(_BF16)


def _proj(pos, x2, g_attn, w1, g_cq, wq, wqr, g_ckv, wk, wv, invf):
    t = x2.shape[0]
    rows = PROJ_ROWS
    wide = MLA_HEADS * LANES

    def full(a):
        return pl.BlockSpec(a.shape, lambda i: (0,) * a.ndim)

    def tok(n):
        return pl.BlockSpec((rows, n), lambda i: (i, 0))

    outs = [jax.ShapeDtypeStruct((t, n), _BF16) for n in (wide, wide, wide, SB_WIDTH, wide, wide)]
    return pl.pallas_call(
        _proj_kernel,
        grid=(t // rows,),
        in_specs=[tok(1), tok(D_MODEL), full(g_attn), full(w1), full(g_cq), full(wq), full(wqr), full(g_ckv),
                  full(wk), full(wv), full(invf)],
        out_specs=[tok(wide), tok(wide), tok(wide), tok(SB_WIDTH), tok(wide), tok(wide)],
        out_shape=outs,
        compiler_params=pltpu.CompilerParams(dimension_semantics=("parallel",), vmem_limit_bytes=VMEM_LIMIT),
        name="proj",
    )(pos, x2, g_attn, w1, g_cq, wq, wqr, g_ckv, wk, wv, invf)


def _mla_kernel(q_ref, k_ref, v_ref, o_ref, s_a, s_b, m_ref, acc_ref):
    tq, tk = MLA_Q_TILE, MLA_K_TILE
    qi = pl.program_id(2)
    heads = [slice(h * LANES, (h + 1) * LANES) for h in range(GROUP_HEADS)]

    def key_rows(kt):
        return pl.ds(pl.multiple_of(kt * tk, tk), tk)

    def score(dst, kt, mask=None):
        for h, sl in enumerate(heads):
            s = _dot_nt(q_ref[0, :, sl], k_ref[0, key_rows(kt), sl])
            dst[h] = s if mask is None else jnp.where(mask, s, -jnp.inf)

    def absorb(src, kt):
        for h, sl in enumerate(heads):
            s = src[h]
            m = m_ref[h]
            m_new = jnp.maximum(m, jnp.max(s, axis=-1, keepdims=True))
            alpha = jnp.exp2(m - m_new)
            p = jnp.exp2(s - m_new)
            m_ref[h] = m_new
            acc_ref[h] = alpha * acc_ref[h] + _dot(p.astype(_BF16), v_ref[0, key_rows(kt), sl])

    m_ref[...] = jnp.full(m_ref.shape, -jnp.inf, _F32)
    acc_ref[...] = jnp.zeros(acc_ref.shape, _F32)
    last = (qi * tq) // tk
    row = lax.broadcasted_iota(jnp.int32, (tq, tk), 0)
    col = lax.broadcasted_iota(jnp.int32, (tq, tk), 1)
    score(s_a, last, mask=last * tk + col <= qi * tq + row)

    def tile_in_a(i):
        return jnp.where(i == 0, last, 2 * i - 1)

    def pair(i, c):
        score(s_b, 2 * i)
        absorb(s_a, tile_in_a(i))
        score(s_a, 2 * i + 1)
        absorb(s_b, 2 * i)
        return c

    pairs = last // 2
    lax.fori_loop(0, pairs, pair, 0)

    @pl.when(last % 2 == 1)
    def _():
        score(s_b, last - 1)
        absorb(s_a, tile_in_a(pairs))
        absorb(s_b, last - 1)

    @pl.when(last % 2 == 0)
    def _():
        absorb(s_a, tile_in_a(pairs))

    low = lax.broadcasted_iota(jnp.int32, (tq, LANES), 1) < MLA_V_DIM
    for hp in range(GROUP_HEADS // 2):
        even, odd = acc_ref[2 * hp], acc_ref[2 * hp + 1]
        o_ref[0, :, hp * LANES:(hp + 1) * LANES] = jnp.where(low, even / even[:, MLA_V_DIM:MLA_V_DIM + 1],
                                                             odd / odd[:, :1])


def _mla(qm, km, vm):
    b, s, _ = qm.shape
    tq = MLA_Q_TILE
    gw = GROUP_HEADS * LANES
    return pl.pallas_call(
        _mla_kernel,
        grid=(b, MLA_HEADS // GROUP_HEADS, s // tq),
        in_specs=[pl.BlockSpec((1, tq, gw), lambda bi, g, qi: (bi, qi, g)),
                  pl.BlockSpec((1, s, gw), lambda bi, g, qi: (bi, 0, g)),
                  pl.BlockSpec((1, s, gw), lambda bi, g, qi: (bi, 0, g))],
        out_specs=pl.BlockSpec((1, tq, gw // 2), lambda bi, g, qi: (bi, qi, g)),
        out_shape=jax.ShapeDtypeStruct((b, s, HEAD_PAIRS * LANES), _F32),
        scratch_shapes=[pltpu.VMEM((GROUP_HEADS, tq, MLA_K_TILE), _F32), pltpu.VMEM((GROUP_HEADS, tq, MLA_K_TILE), _F32),
                        pltpu.VMEM((GROUP_HEADS, tq, 1), _F32), pltpu.VMEM((GROUP_HEADS, tq, LANES), _F32)],
        compiler_params=pltpu.CompilerParams(dimension_semantics=("parallel", "parallel", "arbitrary"),
                                             vmem_limit_bytes=VMEM_LIMIT),
        name="mla",
    )(qm, km, vm)


def _sb_kernel(q_ref, k_ref, v_ref, tri_ref, o_ref, rem_ref, acc_ref):
    tile = SB_TILE
    qi = pl.program_id(2)
    row = lax.broadcasted_iota(jnp.int32, (tile, tile), 0)
    col = lax.broadcasted_iota(jnp.int32, (tile, tile), 1)
    strict = col < row
    heads = [slice(h * LANES, (h + 1) * LANES) for h in range(SB_GROUP_HEADS)]

    def key_rows(kt):
        return pl.ds(pl.multiple_of(kt * tile, tile), tile)

    def tile_terms(kt, diagonal):
        tri = tri_ref[...]
        out = []
        for h, sl in enumerate(heads):
            q = q_ref[0, :, (h // 2) * LANES:(h // 2 + 1) * LANES]
            z = _dot_nt(q, k_ref[0, key_rows(kt), sl])
            sp = jnp.maximum(z, 0.0) + jnp.log(1.0 + jnp.exp(-jnp.abs(z)))
            if diagonal:
                sp = jnp.where(strict, sp, 0.0)
            hi = sp.astype(_BF16)
            lo = (sp - hi.astype(_F32)).astype(_BF16)
            out.append((z, _dot(jnp.concatenate([hi, lo], axis=-1), tri)))
        return out

    def absorb(kt, terms, rems, accs, diagonal, live=None):
        new_rems, new_accs = [], []
        for sl, (z, incl), rem, acc in zip(heads, terms, rems, accs):
            p = jnp.exp(z + (rem + incl))
            if diagonal:
                p = jnp.where(strict, p, 0.0)
            step = incl[:, :1]
            if live is not None:
                p = jnp.where(live, p, 0.0)
                step = jnp.where(live, step, 0.0)
            new_accs.append(acc + _dot(p.astype(_BF16), v_ref[0, key_rows(kt), sl]))
            new_rems.append(rem + step)
        return tuple(new_rems), tuple(new_accs)

    def rem_max(rems):
        return functools.reduce(jnp.maximum, [jnp.max(r) for r in rems])

    terms = [tile_terms(qi, True)] + [tile_terms(jnp.maximum(qi - d, 0), False) for d in range(1, SB_FIRST_TILES)]
    rems = tuple(jnp.zeros((tile, 1), _F32) for _ in range(SB_GROUP_HEADS))
    accs = tuple(jnp.zeros((tile, LANES), _F32) for _ in range(SB_GROUP_HEADS))
    rems, accs = absorb(qi, terms[0], rems, accs, True)
    for d in range(1, SB_FIRST_TILES):
        rems, accs = absorb(jnp.maximum(qi - d, 0), terms[d], rems, accs, False, live=qi >= d)

    def save(rems, accs):
        for h in range(SB_GROUP_HEADS):
            rem_ref[h] = rems[h]
            acc_ref[h] = accs[h]
    save(rems, accs)

    def cond(state):
        kt, worst = state
        return jnp.logical_and(kt >= 0, worst > SB_LOG_FLOOR)

    def body(state):
        kt, _ = state
        rems = tuple(rem_ref[h] for h in range(SB_GROUP_HEADS))
        accs = tuple(acc_ref[h] for h in range(SB_GROUP_HEADS))
        rems, accs = absorb(kt, tile_terms(kt, False), rems, accs, False)
        save(rems, accs)
        return kt - 1, rem_max(rems)

    lax.while_loop(cond, body, (qi - SB_FIRST_TILES, rem_max(rems)))
    for hp in range(SB_GROUP_HEADS // 2):
        o_ref[0, :, hp * LANES:(hp + 1) * LANES] = acc_ref[2 * hp] + acc_ref[2 * hp + 1]


def _sb(qs, ks, vs, tri):
    b, s, _ = qs.shape
    tile = SB_TILE
    gw = SB_GROUP_HEADS * LANES
    return pl.pallas_call(
        _sb_kernel,
        grid=(b, SB_HEADS // SB_GROUP_HEADS, s // tile),
        in_specs=[pl.BlockSpec((1, tile, gw // 2), lambda bi, g, qi: (bi, qi, g)),
                  pl.BlockSpec((1, s, gw), lambda bi, g, qi: (bi, 0, g)),
                  pl.BlockSpec((1, s, gw), lambda bi, g, qi: (bi, 0, g)),
                  pl.BlockSpec((2 * tile, tile), lambda bi, g, qi: (0, 0))],
        out_specs=pl.BlockSpec((1, tile, gw // 2), lambda bi, g, qi: (bi, qi, g)),
        out_shape=jax.ShapeDtypeStruct((b, s, HEAD_PAIRS * LANES), _F32),
        scratch_shapes=[pltpu.VMEM((SB_GROUP_HEADS, tile, 1), _F32), pltpu.VMEM((SB_GROUP_HEADS, tile, LANES), _F32)],
        compiler_params=pltpu.CompilerParams(dimension_semantics=("parallel", "parallel", "arbitrary"),
                                             vmem_limit_bytes=VMEM_LIMIT),
        name="sb",
    )(qs, ks, vs, tri)


def _post_kernel(om_ref, os_ref, x_ref, g_om_ref, g_os_ref, wo_ref, g_moe_ref, wr_ref, br_ref, ltri_ref,
                 h1_ref, u2_ref, route_ref, counts_ref, seen_ref):
    rows = PROJ_ROWS

    @pl.when(pl.program_id(0) == 0)
    def _():
        seen_ref[...] = jnp.zeros_like(seen_ref)

    mixed = jnp.concatenate([_rms(om_ref[...], g_om_ref[...]), _rms(os_ref[...], g_os_ref[...])], axis=-1)
    h1 = x_ref[...] + _dot(mixed.astype(_BF16), wo_ref[...])
    h1_ref[...] = h1
    u2 = _rms(h1, g_moe_ref[...])
    _store_row_tiles(u2_ref, u2)
    w_hi = wr_ref[...].astype(_BF16)
    w_lo = (wr_ref[...] - w_hi.astype(_F32)).astype(_BF16)
    u_hi = u2.astype(_BF16)
    u_lo = (u2 - u_hi.astype(_F32)).astype(_BF16)
    logits = _dot(u_hi, w_hi) + _dot(u_hi, w_lo) + _dot(u_lo, w_hi) + br_ref[...]

    part = rows // POST_PARTS
    lane = lax.broadcasted_iota(jnp.int32, (part, N_EXPERTS), 1).astype(_F32)
    out_lane = lax.broadcasted_iota(jnp.int32, (part, LANES), 1)
    work = [logits[r0:r0 + part, :] for r0 in range(0, rows, part)]
    onehot = [jnp.zeros((part, N_EXPERTS), _F32) for _ in work]
    ids = [[] for _ in work]
    tops = [[] for _ in work]
    for _ in range(TOP_K):
        for n in range(POST_PARTS):
            top = jnp.max(work[n], axis=-1, keepdims=True)
            idx = jnp.min(jnp.where(work[n] == top, lane, float(N_EXPERTS)), axis=-1, keepdims=True)
            hit = lane == idx
            onehot[n] = jnp.where(hit, 1.0, onehot[n])
            work[n] = jnp.where(hit, -jnp.inf, work[n])
            ids[n].append(idx)
            tops[n].append(top)

    seen = seen_ref[...]
    for n in range(POST_PARTS):
        exps = [jnp.exp(tp - tops[n][0]) for tp in tops[n]]
        denom = exps[0] + exps[1] + exps[2] + exps[3]
        gates = [e / denom for e in exps]
        before = seen + _dot(ltri_ref[...], onehot[n].astype(_BF16))
        ranks = [jnp.sum(jnp.where(lane == idx, before, 0.0), axis=-1, keepdims=True) for idx in ids[n]]
        seen = seen + jnp.sum(onehot[n], axis=0, keepdims=True)
        route = jnp.zeros((part, LANES), _F32)
        for j, val in enumerate(ids[n] + gates + ranks):
            route = jnp.where(out_lane == j, val, route)
        route_ref[n * part:(n + 1) * part, :] = route
    seen_ref[...] = seen
    counts_ref[...] = seen


def _post(om, os_, x2, g_om, g_os, wo, g_moe, wr, br, ltri):
    t = x2.shape[0]
    rows = PROJ_ROWS

    def full(a):
        return pl.BlockSpec(a.shape, lambda i: (0,) * a.ndim)

    def tok(n):
        return pl.BlockSpec((rows, n), lambda i: (i, 0))

    return pl.pallas_call(
        _post_kernel,
        grid=(t // rows,),
        in_specs=[tok(om.shape[1]), tok(os_.shape[1]), tok(D_MODEL), full(g_om), full(g_os), full(wo), full(g_moe),
                  full(wr), full(br), full(ltri)],
        out_specs=[tok(D_MODEL), pl.BlockSpec((rows * ROW_TILE, LANES), lambda i: (i, 0)), tok(LANES),
                   pl.BlockSpec((1, N_EXPERTS), lambda i: (0, 0))],
        out_shape=[jax.ShapeDtypeStruct((t, D_MODEL), _F32), jax.ShapeDtypeStruct((t * ROW_TILE, LANES), _F32),
                   jax.ShapeDtypeStruct((t, LANES), _F32), jax.ShapeDtypeStruct((1, N_EXPERTS), _F32)],
        scratch_shapes=[pltpu.VMEM((1, N_EXPERTS), _F32)],
        compiler_params=pltpu.CompilerParams(dimension_semantics=("arbitrary",), vmem_limit_bytes=VMEM_LIMIT),
        name="post",
    )(om, os_, x2, g_om, g_os, wo, g_moe, wr, br, ltri)


def _padded_count(cnt_ref, e):
    shift = EXPERT_ROWS.bit_length() - 1
    return lax.shift_left(lax.shift_right_logical(cnt_ref[e] + (EXPERT_ROWS - 1), shift), shift)


def _layout_kernel(cnt_ref, route_ref, dest_ref, start_ref, be_ref, used_ref):
    shift = EXPERT_ROWS.bit_length() - 1
    rows = MOE_ROWS
    nb = be_ref.shape[0]

    def place(e, off):
        padded = _padded_count(cnt_ref, e)
        start_ref[e] = off

        def mark(j, c):
            be_ref[lax.shift_right_logical(off, shift) + j] = e
            return c
        lax.fori_loop(0, lax.shift_right_logical(padded, shift), mark, 0)
        return off + padded
    total = lax.fori_loop(0, N_EXPERTS, place, 0)
    used_ref[0] = lax.shift_right_logical(total, shift)

    def tail(b, c):
        be_ref[b] = N_EXPERTS - 1
        return c
    lax.fori_loop(lax.shift_right_logical(total, shift), nb, tail, 0)

    lane = lax.broadcasted_iota(jnp.int32, (1, LANES), 1)
    rank_lanes = jnp.logical_and(lane >= 2 * TOP_K, lane < 3 * TOP_K)
    pick = (lax.broadcasted_iota(jnp.int32, (2 * SUBLANES, LANES), 0)
            == lax.broadcasted_iota(jnp.int32, (2 * SUBLANES, LANES), 1)).astype(_BF16)

    def tile(i, c):
        r = route_ref[pl.ds(pl.multiple_of(i * rows, rows), rows), :]
        high = jnp.floor(r * (1.0 / 256.0))
        low_digits = jnp.where(rank_lanes, r - 256.0 * high, r)
        high_digits = jnp.where(rank_lanes, high, 0.0)
        low_t = _dot_nt(pick, low_digits.astype(_BF16))
        high_t = _dot_nt(pick, high_digits.astype(_BF16))
        ids = low_t[:TOP_K, :].astype(jnp.int32)
        dest = (256.0 * high_t[2 * TOP_K:3 * TOP_K, :] + low_t[2 * TOP_K:3 * TOP_K, :]).astype(jnp.int32)
        for e in range(N_EXPERTS):
            dest = dest + jnp.where(ids == e, start_ref[e], 0)
        lines = [dest[k:k + 1, half * LANES:(half + 1) * LANES] for k in range(TOP_K) for half in range(rows // LANES)]
        dest_ref[pl.ds(pl.multiple_of(i * SUBLANES, SUBLANES), SUBLANES), :] = jnp.concatenate(lines, axis=0)
        return c
    lax.fori_loop(0, route_ref.shape[0] // rows, tile, 0)


def _layout(counts, route, n_blocks):
    t = route.shape[0]
    pair_shape = (t * TOP_K // LANES, LANES)
    grid_spec = pltpu.PrefetchScalarGridSpec(
        num_scalar_prefetch=1,
        grid=(1,),
        in_specs=[pl.BlockSpec(route.shape, lambda i, c: (0, 0))],
        out_specs=[pl.BlockSpec(pair_shape, lambda i, c: (0, 0)), pl.BlockSpec(memory_space=pltpu.SMEM),
                   pl.BlockSpec(memory_space=pltpu.SMEM), pl.BlockSpec(memory_space=pltpu.SMEM)],
    )
    return pl.pallas_call(
        _layout_kernel,
        grid_spec=grid_spec,
        out_shape=[jax.ShapeDtypeStruct(pair_shape, jnp.int32), jax.ShapeDtypeStruct((N_EXPERTS,), jnp.int32),
                   jax.ShapeDtypeStruct((n_blocks,), jnp.int32), jax.ShapeDtypeStruct((1,), jnp.int32)],
        compiler_params=pltpu.CompilerParams(dimension_semantics=("arbitrary",), vmem_limit_bytes=VMEM_LIMIT),
        name="layout",
    )(counts, route)


def _tile_rows(row):
    return pl.ds(pl.multiple_of(row * ROW_TILE, ROW_TILE), ROW_TILE)


def _for_each_pair(idx, s, body):
    halves = MOE_ROWS // LANES

    def line(g, c):
        choice = g // halves
        first = lax.rem(g, halves) * LANES
        for t0 in range(0, LANES, DMA_GROUP):
            vals = [idx[s, g, t0 + j] for j in range(DMA_GROUP)]
            for j, row in enumerate(vals):
                body(first + t0 + j, choice, row, j % DMA_THREADS)
        return c
    lax.fori_loop(0, SUBLANES, line, 0)


def _dispatch_kernel(cnt_ref, start_ref, dest_hbm, u_hbm, xs_hbm, idx, stage, zblk, sem_d, sem_l, sem_i, sem_z):
    rows = MOE_ROWS
    shift = EXPERT_ROWS.bit_length() - 1
    i = pl.program_id(0)
    nt = pl.num_programs(0)
    nb = xs_hbm.shape[0] // (EXPERT_ROWS * ROW_TILE)
    slot = lax.rem(i, 2)
    stage_slot = lax.rem(i, STAGE_SLOTS)

    def idx_copy(tile, s):
        return pltpu.make_async_copy(dest_hbm.at[pl.ds(pl.multiple_of(tile * SUBLANES, SUBLANES), SUBLANES)],
                                     idx.at[s], sem_i.at[s])

    def load(tile):
        s = lax.rem(tile, STAGE_SLOTS)
        src = pl.ds(pl.multiple_of(tile * (rows * ROW_TILE), rows * ROW_TILE), rows * ROW_TILE)
        return pltpu.make_async_copy(u_hbm.at[src], stage.at[s], sem_l.at[s])

    def wait_rows(s):
        chunk = pl.ds(0, rows * ROW_TILE)
        for _ in range(TOP_K):
            pltpu.make_async_copy(stage.at[s], xs_hbm.at[chunk], sem_d.at[s]).wait()

    def zero_block(b):
        dst = pl.ds(pl.multiple_of(b * (EXPERT_ROWS * ROW_TILE), EXPERT_ROWS * ROW_TILE), EXPERT_ROWS * ROW_TILE)
        return pltpu.make_async_copy(zblk, xs_hbm.at[dst], sem_z)

    def pad_rows(e, fn):
        pad = _padded_count(cnt_ref, e) - cnt_ref[e]
        first = start_ref[e] + cnt_ref[e]
        for bit in reversed(range(shift)):
            size = 1 << bit
            before = lax.shift_left(lax.shift_right_logical(pad, bit + 1), bit + 1)

            @pl.when(lax.shift_right_logical(pad, bit) & 1 == 1)
            def _(size=size, before=before):
                dst = pl.ds(pl.multiple_of((first + before) * ROW_TILE, ROW_TILE), size * ROW_TILE)
                fn(pltpu.make_async_copy(zblk.at[pl.ds(0, size * ROW_TILE)], xs_hbm.at[dst], sem_z))

    def tail_blocks(fn):
        used = start_ref[N_EXPERTS - 1] + _padded_count(cnt_ref, N_EXPERTS - 1)

        def one(b, c):
            fn(zero_block(b))
            return c
        lax.fori_loop(lax.shift_right_logical(used, shift), nb, one, 0)

    @pl.when(i == 0)
    def _():
        idx_copy(0, 0).start()
        load(0).start()

        @pl.when(nt > 1)
        def _():
            load(1).start()
        zblk[...] = jnp.zeros_like(zblk)
        for fn in (lambda c: c.start(), lambda c: c.wait()):
            def per_expert(e, c, fn=fn):
                pad_rows(e, fn)
                return c
            lax.fori_loop(0, N_EXPERTS, per_expert, 0)
            tail_blocks(fn)

    idx_copy(i, slot).wait()

    @pl.when(i + 1 < nt)
    def _():
        idx_copy(i + 1, 1 - slot).start()

    @pl.when(i >= 2)
    def _():
        wait_rows(lax.rem(i + 2, STAGE_SLOTS))

    @pl.when(i + 2 < nt)
    def _():
        load(i + 2).start()

    load(i).wait()

    def copy_row(t, k, row, thread):
        del k
        pltpu.make_async_copy(stage.at[stage_slot, _tile_rows(t)], xs_hbm.at[_tile_rows(row)],
                              sem_d.at[stage_slot]).start(priority=thread)
    _for_each_pair(idx, slot, copy_row)

    @pl.when(i == nt - 1)
    def _():
        @pl.when(nt > 1)
        def _():
            wait_rows(lax.rem(i + STAGE_SLOTS - 1, STAGE_SLOTS))
        wait_rows(stage_slot)


def _dispatch(counts, start, dest, u2_tiles, n_blocks):
    nt = dest.shape[0] // SUBLANES
    grid_spec = pltpu.PrefetchScalarGridSpec(
        num_scalar_prefetch=2,
        grid=(nt,),
        in_specs=[pl.BlockSpec(memory_space=pl.ANY), pl.BlockSpec(memory_space=pl.ANY)],
        out_specs=pl.BlockSpec(memory_space=pl.ANY),
        scratch_shapes=[pltpu.SMEM((2, SUBLANES, LANES), jnp.int32),
                        pltpu.VMEM((STAGE_SLOTS, MOE_ROWS * ROW_TILE, LANES), _F32),
                        pltpu.VMEM((EXPERT_ROWS * ROW_TILE, LANES), _F32),
                        pltpu.SemaphoreType.DMA((STAGE_SLOTS,)), pltpu.SemaphoreType.DMA((STAGE_SLOTS,)),
                        pltpu.SemaphoreType.DMA((2,)), pltpu.SemaphoreType.DMA(())],
    )
    return pl.pallas_call(
        _dispatch_kernel,
        grid_spec=grid_spec,
        out_shape=jax.ShapeDtypeStruct((n_blocks * EXPERT_ROWS * ROW_TILE, LANES), _F32),
        compiler_params=pltpu.CompilerParams(dimension_semantics=("arbitrary",), vmem_limit_bytes=VMEM_LIMIT),
        name="dispatch",
    )(counts, start, dest, u2_tiles)


def _expert_kernel(be_ref, used_ref, xs_ref, wgu_hbm, bgu_ref, wdn_hbm, bdn_ref, ys_ref,
                   wgu_f32, wdn_f32, wgu_bf, wdn_bf, sem_w):
    rows = EXPERT_ROWS
    i = pl.program_id(0)
    nb = pl.num_programs(0)
    used = used_ref[0]
    e = be_ref[i]
    prev = be_ref[jnp.maximum(i - 1, 0)]
    in_use = i < used

    def fetch(expert):
        return (pltpu.make_async_copy(wgu_hbm.at[expert], wgu_f32, sem_w.at[0]),
                pltpu.make_async_copy(wdn_hbm.at[expert], wdn_f32, sem_w.at[1]))

    @pl.when(i == 0)
    def _():
        for c in fetch(e):
            c.start()

    @pl.when(jnp.logical_and(in_use, jnp.logical_or(i == 0, e != prev)))
    def _():
        for c in fetch(e):
            c.wait()
        chunk = 128

        def cast(c, carry):
            rs = pl.ds(pl.multiple_of(c * chunk, chunk), chunk)
            wgu_bf[rs, :] = wgu_f32[rs, :].astype(_BF16)
            wdn_bf[rs, :] = wdn_f32[rs, :].astype(_BF16)
            return carry
        lax.fori_loop(0, D_MODEL // chunk, cast, 0)

        def same_expert(j):
            return jnp.logical_and(j < used, be_ref[jnp.minimum(j, nb - 1)] == e)
        nxt = lax.while_loop(same_expert, lambda j: j + 1, i)

        @pl.when(nxt < used)
        def _():
            for c in fetch(be_ref[jnp.minimum(nxt, nb - 1)]):
                c.start()

    @pl.when(in_use)
    def _():
        x = _load_row_tiles(xs_ref, 0, rows).astype(_BF16)
        hh = _dot(x, wgu_bf[...]) + bgu_ref[0]
        glu = jnp.minimum(hh[:, :D_FF], SWIGLU_LIMIT)
        lin = jnp.clip(hh[:, D_FF:], -SWIGLU_LIMIT, SWIGLU_LIMIT)
        act = glu * jax.nn.sigmoid(SWIGLU_ALPHA * glu) * (lin + 1.0)
        _store_row_tiles(ys_ref, _dot(act.astype(_BF16), wdn_bf[...]) + bdn_ref[0])

    @pl.when(jnp.logical_not(in_use))
    def _():
        ys_ref[...] = jnp.zeros_like(ys_ref)


def _experts(block_e, used, xs, w_gu, b_gu, w_dn, b_dn):
    nb = block_e.shape[0]
    rows = EXPERT_ROWS
    grid_spec = pltpu.PrefetchScalarGridSpec(
        num_scalar_prefetch=2,
        grid=(nb,),
        in_specs=[pl.BlockSpec((rows * ROW_TILE, LANES), lambda i, be, u: (i, 0)),
                  pl.BlockSpec(memory_space=pl.ANY),
                  pl.BlockSpec((1, 1, 2 * D_FF), lambda i, be, u: (be[i], 0, 0)),
                  pl.BlockSpec(memory_space=pl.ANY),
                  pl.BlockSpec((1, 1, D_MODEL), lambda i, be, u: (be[i], 0, 0))],
        out_specs=pl.BlockSpec((rows * ROW_TILE, LANES), lambda i, be, u: (i, 0)),
        scratch_shapes=[pltpu.VMEM((D_MODEL, 2 * D_FF), _F32), pltpu.VMEM((D_FF, D_MODEL), _F32),
                        pltpu.VMEM((D_MODEL, 2 * D_FF), _BF16), pltpu.VMEM((D_FF, D_MODEL), _BF16),
                        pltpu.SemaphoreType.DMA((2,))],
    )
    return pl.pallas_call(
        _expert_kernel,
        grid_spec=grid_spec,
        out_shape=jax.ShapeDtypeStruct(xs.shape, _F32),
        compiler_params=pltpu.CompilerParams(dimension_semantics=("arbitrary",), vmem_limit_bytes=VMEM_LIMIT),
        name="experts",
    )(block_e, used, xs, w_gu, b_gu.reshape(N_EXPERTS, 1, 2 * D_FF), w_dn, b_dn.reshape(N_EXPERTS, 1, D_MODEL))


def _final_kernel(dest_hbm, ys_hbm, h1_ref, route_ref, p_ref, g_ple_ref, wpg_ref, wpp_ref, g_fin_ref,
                  out_ref, ybuf, idx, sem_y, sem_i, *, last_layer):
    rows = MOE_ROWS
    i = pl.program_id(0)
    nt = pl.num_programs(0)
    slot = lax.rem(i, 2)

    def idx_copy(tile, s):
        return pltpu.make_async_copy(dest_hbm.at[pl.ds(pl.multiple_of(tile * SUBLANES, SUBLANES), SUBLANES)],
                                     idx.at[s], sem_i.at[s])

    def gather(s):
        def copy_row(t, k, row, thread):
            pltpu.make_async_copy(ys_hbm.at[_tile_rows(row)], ybuf.at[s, _tile_rows(k * rows + t)],
                                  sem_y.at[s]).start(priority=thread)
        _for_each_pair(idx, s, copy_row)

    @pl.when(i == 0)
    def _():
        idx_copy(0, 0).start()
        idx_copy(0, 0).wait()
        gather(0)

        @pl.when(nt > 1)
        def _():
            idx_copy(1, 1).start()

    @pl.when(i + 1 < nt)
    def _():
        idx_copy(i + 1, 1 - slot).wait()
        gather(1 - slot)

        @pl.when(i + 2 < nt)
        def _():
            idx_copy(i + 2, slot).start()

    pltpu.make_async_copy(ys_hbm.at[pl.ds(0, TOP_K * rows * ROW_TILE)], ybuf.at[slot], sem_y.at[slot]).wait()
    part = rows // ROW_PARTS
    for r0 in range(0, rows, part):
        rs = slice(r0, r0 + part)
        route = route_ref[rs, :]
        y = jnp.zeros((part, D_MODEL), _F32)
        for k in range(TOP_K):
            y = y + _load_row_tiles(ybuf.at[slot], k * rows + r0, part) * route[:, TOP_K + k:TOP_K + k + 1]
        h2 = h1_ref[rs, :] + y
        u3 = _rms(h2, g_ple_ref[...]).astype(_BF16)
        gate = jax.nn.sigmoid(_dot(u3, wpg_ref[...]))
        h3 = h2 + gate * _dot(p_ref[rs, :].astype(_BF16), wpp_ref[...])
        out_ref[rs, :] = _rms(h3, g_fin_ref[...]) if last_layer else h3


def _final(dest, ys, h1, route, p2, g_ple, wpg, wpp, g_fin, last_layer):
    t = h1.shape[0]
    rows = MOE_ROWS

    def full(a):
        return pl.BlockSpec(a.shape, lambda i: (0,) * a.ndim)

    def tok(n):
        return pl.BlockSpec((rows, n), lambda i: (i, 0))

    return pl.pallas_call(
        functools.partial(_final_kernel, last_layer=last_layer),
        grid=(t // rows,),
        in_specs=[pl.BlockSpec(memory_space=pl.ANY), pl.BlockSpec(memory_space=pl.ANY), tok(D_MODEL), tok(LANES),
                  tok(PLE_DIM), full(g_ple), full(wpg), full(wpp), full(g_fin)],
        out_specs=tok(D_MODEL),
        out_shape=jax.ShapeDtypeStruct((t, D_MODEL), _F32),
        scratch_shapes=[pltpu.VMEM((2, TOP_K * rows * ROW_TILE, LANES), _F32),
                        pltpu.SMEM((2, SUBLANES, LANES), jnp.int32),
                        pltpu.SemaphoreType.DMA((2,)), pltpu.SemaphoreType.DMA((2,))],
        compiler_params=pltpu.CompilerParams(dimension_semantics=("arbitrary",), vmem_limit_bytes=VMEM_LIMIT),
        name="final",
    )(dest, ys, h1, route, p2, g_ple, wpg, wpp, g_fin)


def _rope_pad(w, rot):
    half = MLA_ROPE_DIM // 2
    body = jnp.concatenate([-w[:, half:], w[:, :half]], axis=1) if rot else w
    z = jnp.zeros((w.shape[0], MLA_NOPE_DIM), w.dtype)
    return jnp.concatenate([z, body, jnp.zeros((w.shape[0], LANES - MLA_NOPE_DIM - MLA_ROPE_DIM), w.dtype)], axis=1)


def _layer_weights(w_in, w_uq, w_ukv):
    o = Q_LORA_RANK + KV_LORA_RANK
    w_kr = w_in[:, o:o + MLA_ROPE_DIM]
    w1 = jnp.concatenate([w_in[:, :o], _rope_pad(w_kr, False), _rope_pad(w_kr, True), w_in[:, o + MLA_ROPE_DIM:]],
                         axis=1).astype(_BF16)
    uq = w_uq.reshape(Q_LORA_RANK, MLA_HEADS, MLA_NOPE_DIM + MLA_ROPE_DIM)
    zq = jnp.zeros((Q_LORA_RANK, MLA_HEADS, LANES - MLA_NOPE_DIM - MLA_ROPE_DIM), w_uq.dtype)
    wq = jnp.concatenate([uq, zq], axis=2).reshape(Q_LORA_RANK, MLA_HEADS * LANES).astype(_BF16)
    rope = uq[:, :, MLA_NOPE_DIM:]
    half = MLA_ROPE_DIM // 2
    rot = jnp.concatenate([jnp.zeros_like(uq[:, :, :MLA_NOPE_DIM]), -rope[:, :, half:], rope[:, :, :half], zq], axis=2)
    wqr = rot.reshape(Q_LORA_RANK, MLA_HEADS * LANES).astype(_BF16)
    ukv = w_ukv.reshape(KV_LORA_RANK, MLA_HEADS, MLA_NOPE_DIM + MLA_V_DIM)
    zk = jnp.zeros((KV_LORA_RANK, MLA_HEADS, LANES - MLA_NOPE_DIM), w_ukv.dtype)
    wk = jnp.concatenate([ukv[:, :, :MLA_NOPE_DIM], zk], axis=2).reshape(KV_LORA_RANK, MLA_HEADS * LANES).astype(_BF16)
    v = ukv[:, :, MLA_NOPE_DIM:].reshape(KV_LORA_RANK, HEAD_PAIRS, 2, MLA_V_DIM)
    zv = jnp.zeros((KV_LORA_RANK, HEAD_PAIRS, MLA_V_DIM), w_ukv.dtype)
    wv = jnp.stack([jnp.concatenate([v[:, :, 0], zv], axis=2), jnp.concatenate([zv, v[:, :, 1]], axis=2)], axis=2)
    wv = wv.reshape(KV_LORA_RANK, MLA_HEADS * LANES).astype(_BF16)
    return w1, wq, wqr, wk, wv


def kernel(x, p, positions, w_in, g_attn, g_cq, w_uq, g_ckv, w_ukv, g_out_mla, g_out_sb, w_o, g_moe, w_router,
           b_router, w_gu, b_gu, w_dn, b_dn, g_ple, w_ple_gate, w_ple_proj, g_final):
    b, s, d = x.shape
    t = b * s
    depth = w_in.shape[0]
    assert d == D_MODEL and ROW_TILE == SUBLANES and s % MLA_K_TILE == 0 and t % PROJ_ROWS == 0
    assert t % MOE_ROWS == 0 and LANES % DMA_GROUP == 0 and MOE_ROWS % LANES == 0 and (t * TOP_K) % EXPERT_ROWS == 0
    assert EXPERT_ROWS & (EXPERT_ROWS - 1) == 0

    freq = ROPE_THETA ** (-jnp.arange(0, MLA_ROPE_DIM, 2, dtype=_F32) / MLA_ROPE_DIM)
    invf = jnp.concatenate([jnp.zeros((MLA_NOPE_DIM,), _F32), freq, freq,
                            jnp.zeros((LANES - MLA_NOPE_DIM - MLA_ROPE_DIM,), _F32)]).reshape(1, LANES)
    pos = positions.reshape(t, 1)
    idx = jnp.arange(SB_TILE)
    tri = -(idx[:, None] >= idx[None, :]).astype(_BF16)
    tri = jnp.concatenate([tri, tri], axis=0)
    idx = jnp.arange(PROJ_ROWS // POST_PARTS)
    ltri = (idx[None, :] < idx[:, None]).astype(_BF16)
    n_blocks = t * TOP_K // EXPERT_ROWS + N_EXPERTS

    h = x.reshape(t, d)
    for i in range(depth):
        w1, wq, wqr, wk, wv = _layer_weights(w_in[i], w_uq[i], w_ukv[i])
        qm, km, vm, qs, ks, vs = _proj(pos, h, g_attn[i].reshape(1, d), w1, g_cq[i].reshape(1, -1), wq, wqr,
                                       g_ckv[i].reshape(1, -1), wk, wv, invf)

        def seq(a):
            return a.reshape(b, s, a.shape[1])

        om = _mla(seq(qm), seq(km), seq(vm)).reshape(t, -1)
        os_ = _sb(seq(qs), seq(ks), seq(vs), tri).reshape(t, -1)
        h1, u2_tiles, route, counts = _post(om, os_, h, g_out_mla[i].reshape(1, -1), g_out_sb[i].reshape(1, -1),
                                            w_o[i].astype(_BF16), g_moe[i].reshape(1, d), w_router[i],
                                            b_router[i].reshape(1, -1), ltri)
        counts = counts.reshape(-1).astype(jnp.int32)
        dest, start, block_e, used = _layout(counts, route, n_blocks)
        xs = _dispatch(counts, start, dest, u2_tiles, n_blocks)
        ys = _experts(block_e, used, xs, w_gu[i], b_gu[i], w_dn[i], b_dn[i])
        h = _final(dest, ys, h1, route, p[i].reshape(t, -1), g_ple[i].reshape(1, d), w_ple_gate[i].astype(_BF16),
                   w_ple_proj[i].astype(_BF16), g_final.reshape(1, d), i == depth - 1)
    return h.reshape(b, s, d)
```

```python
import functools
import math

import jax
import jax.numpy as jnp
from jax import lax
from jax.experimental import pallas as pl
from jax.experimental.pallas import tpu as pltpu

D_MODEL = 1024
PLE_DIM = 256
MLA_HEADS = 8
MLA_NOPE_DIM = 64
MLA_ROPE_DIM = 32
MLA_V_DIM = 64
Q_LORA_RANK = 384
KV_LORA_RANK = 256
SB_HEADS = 8
SB_HEAD_DIM = 64
SB_WIDTH = SB_HEADS * SB_HEAD_DIM
ROPE_THETA = 10000.0
N_EXPERTS = 32
TOP_K = 4
D_FF = 1024
SWIGLU_LIMIT = 7.0
SWIGLU_ALPHA = 1.702
RMS_EPS = 1e-6

LANES = 128
SUBLANES = 8
ROW_TILE = D_MODEL // LANES
HEAD_PAIRS = MLA_HEADS // 2
PROJ_ROWS = 512
POST_ROWS = 256
GROUP_HEADS = 4
SB_GROUP_HEADS = 4
MLA_Q_TILE = 512
MLA_K_TILE = 512
SB_TILE = 256
SB_FIRST_TILES = 2
EXPERT_ROWS = 256
ROW_PARTS = 2
POST_PARTS = 4
MOE_ROWS = SUBLANES * LANES // TOP_K
DMA_GROUP = 32
STAGE_SLOTS = 4
DMA_THREADS = 2
VMEM_LIMIT = 56 * 1024 * 1024
SB_LOG_FLOOR = -105.0

_F32 = jnp.float32
_BF16 = jnp.bfloat16


def _rms(x, g):
    return x * lax.rsqrt(jnp.mean(x * x, axis=-1, keepdims=True) + RMS_EPS) * g


def _dot(a, b):
    return jnp.dot(a, b, preferred_element_type=_F32)


def _dot_nt(a, b):
    return lax.dot_general(a, b, (((1,), (1,)), ((), ())), preferred_element_type=_F32)


def _store_row_tiles(ref, value, first_row=0):
    rows = value.shape[0]
    for g in range(ROW_TILE):
        ref[pl.ds(first_row * ROW_TILE + g, rows, stride=ROW_TILE), :] = value[:, g * LANES:(g + 1) * LANES]


def _load_row_tiles(ref, first_row, rows):
    return jnp.concatenate([ref[pl.ds(first_row * ROW_TILE + g, rows, stride=ROW_TILE), :] for g in range(ROW_TILE)],
                           axis=-1)


def _proj_kernel(pos_ref, x_ref, g_attn_ref, w1_ref, g_cq_ref, wq_ref, wqr_ref, g_ckv_ref, wk_ref, wv_ref,
                 invf_ref, qm_ref, km_ref, vm_ref, qs_ref, ks_ref, vs_ref):
    u = _rms(x_ref[...], g_attn_ref[...]).astype(_BF16)
    y = _dot(u, w1_ref[...])
    c_q = y[:, :Q_LORA_RANK]
    c_kv = y[:, Q_LORA_RANK:Q_LORA_RANK + KV_LORA_RANK]
    o = Q_LORA_RANK + KV_LORA_RANK
    k_r = y[:, o:o + LANES]
    k_r_rot = y[:, o + LANES:o + 2 * LANES]
    o += 2 * LANES
    q_s = y[:, o:o + SB_WIDTH]
    k_s = y[:, o + SB_WIDTH:o + 2 * SB_WIDTH]
    v_s = y[:, o + 2 * SB_WIDTH:o + 3 * SB_WIDTH]

    ang = pos_ref[...].astype(_F32) * invf_ref[...]
    cos = jnp.cos(ang)
    sin = jnp.sin(ang)

    cq_n = _rms(c_q, g_cq_ref[...]).astype(_BF16)
    q = _dot(cq_n, wq_ref[...])
    q_rot = _dot(cq_n, wqr_ref[...])
    ckv_n = _rms(c_kv, g_ckv_ref[...]).astype(_BF16)
    k = _dot(ckv_n, wk_ref[...])
    wide_lane = lax.broadcasted_iota(jnp.int32, (1, MLA_HEADS * LANES), 1)
    ones_lane = (wide_lane % (2 * LANES) == MLA_V_DIM) | (wide_lane % (2 * LANES) == LANES)
    vm_ref[...] = jnp.where(ones_lane, 1.0, _dot(ckv_n, wv_ref[...])).astype(_BF16)
    k_rope = k_r * cos + k_r_rot * sin

    q_scale = (MLA_NOPE_DIM + MLA_ROPE_DIM) ** -0.5 * math.log2(math.e)
    lane = lax.broadcasted_iota(jnp.int32, (1, LANES), 1)
    low = lane < SB_HEAD_DIM
    for h in range(MLA_HEADS):
        sl = slice(h * LANES, (h + 1) * LANES)
        qm_ref[:, sl] = ((q[:, sl] * cos + q_rot[:, sl] * sin) * q_scale).astype(_BF16)
        km_ref[:, sl] = (k[:, sl] + k_rope).astype(_BF16)
    qs_ref[...] = (q_s * (SB_HEAD_DIM ** -0.5)).astype(_BF16)
    for hp in range(HEAD_PAIRS):
        sl = slice(hp * LANES, (hp + 1) * LANES)
        for half, keep in ((0, low), (1, jnp.logical_not(low))):
            dst = slice((2 * hp + half) * LANES, (2 * hp + half + 1) * LANES)
            ks_ref[:, dst] = jnp.where(keep, k_s[:, sl], 0.0).astype(_BF16)
            vs_ref[:, dst] = jnp.where(keep, v_s[:, sl], 0.0).astype(_BF16)


def _proj(pos, x2, g_attn, w1, g_cq, wq, wqr, g_ckv, wk, wv, invf):
    t = x2.shape[0]
    rows = PROJ_ROWS
    wide = MLA_HEADS * LANES

    def full(a):
        return pl.BlockSpec(a.shape, lambda i: (0,) * a.ndim)

    def tok(n):
        return pl.BlockSpec((rows, n), lambda i: (i, 0))

    outs = [jax.ShapeDtypeStruct((t, n), _BF16) for n in (wide, wide, wide, SB_WIDTH, wide, wide)]
    return pl.pallas_call(
        _proj_kernel,
        grid=(t // rows,),
        in_specs=[tok(1), tok(D_MODEL), full(g_attn), full(w1), full(g_cq), full(wq), full(wqr), full(g_ckv),
                  full(wk), full(wv), full(invf)],
        out_specs=[tok(wide), tok(wide), tok(wide), tok(SB_WIDTH), tok(wide), tok(wide)],
        out_shape=outs,
        compiler_params=pltpu.CompilerParams(dimension_semantics=("parallel",), vmem_limit_bytes=VMEM_LIMIT),
        name="proj",
    )(pos, x2, g_attn, w1, g_cq, wq, wqr, g_ckv, wk, wv, invf)


def _mla_kernel(q_ref, k_ref, v_ref, o_ref, s_a, s_b, m_ref, acc_ref):
    tq, tk = MLA_Q_TILE, MLA_K_TILE
    qi = pl.program_id(2)
    heads = [slice(h * LANES, (h + 1) * LANES) for h in range(GROUP_HEADS)]

    def key_rows(kt):
        return pl.ds(pl.multiple_of(kt * tk, tk), tk)

    def score(dst, kt, mask=None):
        for h, sl in enumerate(heads):
            s = _dot_nt(q_ref[0, :, sl], k_ref[0, key_rows(kt), sl])
            dst[h] = s if mask is None else jnp.where(mask, s, -jnp.inf)

    def absorb(src, kt):
        for h, sl in enumerate(heads):
            s = src[h]
            m = m_ref[h]
            m_new = jnp.maximum(m, jnp.max(s, axis=-1, keepdims=True))
            alpha = jnp.exp2(m - m_new)
            p = jnp.exp2(s - m_new)
            m_ref[h] = m_new
            acc_ref[h] = alpha * acc_ref[h] + _dot(p.astype(_BF16), v_ref[0, key_rows(kt), sl])

    m_ref[...] = jnp.full(m_ref.shape, -jnp.inf, _F32)
    acc_ref[...] = jnp.zeros(acc_ref.shape, _F32)
    last = (qi * tq) // tk
    row = lax.broadcasted_iota(jnp.int32, (tq, tk), 0)
    col = lax.broadcasted_iota(jnp.int32, (tq, tk), 1)
    score(s_a, last, mask=last * tk + col <= qi * tq + row)

    def tile_in_a(i):
        return jnp.where(i == 0, last, 2 * i - 1)

    def pair(i, c):
        score(s_b, 2 * i)
        absorb(s_a, tile_in_a(i))
        score(s_a, 2 * i + 1)
        absorb(s_b, 2 * i)
        return c

    pairs = last // 2
    lax.fori_loop(0, pairs, pair, 0)

    @pl.when(last % 2 == 1)
    def _():
        score(s_b, last - 1)
        absorb(s_a, tile_in_a(pairs))
        absorb(s_b, last - 1)

    @pl.when(last % 2 == 0)
    def _():
        absorb(s_a, tile_in_a(pairs))

    low = lax.broadcasted_iota(jnp.int32, (tq, LANES), 1) < MLA_V_DIM
    for hp in range(GROUP_HEADS // 2):
        even, odd = acc_ref[2 * hp], acc_ref[2 * hp + 1]
        o_ref[0, :, hp * LANES:(hp + 1) * LANES] = jnp.where(low, even / even[:, MLA_V_DIM:MLA_V_DIM + 1],
                                                             odd / odd[:, :1])


def _mla(qm, km, vm):
    b, s, _ = qm.shape
    tq = MLA_Q_TILE
    gw = GROUP_HEADS * LANES
    return pl.pallas_call(
        _mla_kernel,
        grid=(b, MLA_HEADS // GROUP_HEADS, s // tq),
        in_specs=[pl.BlockSpec((1, tq, gw), lambda bi, g, qi: (bi, qi, g)),
                  pl.BlockSpec((1, s, gw), lambda bi, g, qi: (bi, 0, g)),
                  pl.BlockSpec((1, s, gw), lambda bi, g, qi: (bi, 0, g))],
        out_specs=pl.BlockSpec((1, tq, gw // 2), lambda bi, g, qi: (bi, qi, g)),
        out_shape=jax.ShapeDtypeStruct((b, s, HEAD_PAIRS * LANES), _F32),
        scratch_shapes=[pltpu.VMEM((GROUP_HEADS, tq, MLA_K_TILE), _F32), pltpu.VMEM((GROUP_HEADS, tq, MLA_K_TILE), _F32),
                        pltpu.VMEM((GROUP_HEADS, tq, 1), _F32), pltpu.VMEM((GROUP_HEADS, tq, LANES), _F32)],
        compiler_params=pltpu.CompilerParams(dimension_semantics=("parallel", "parallel", "arbitrary"),
                                             vmem_limit_bytes=VMEM_LIMIT),
        name="mla",
    )(qm, km, vm)


def _sb_kernel(q_ref, k_ref, v_ref, tri_ref, o_ref, rem_ref, acc_ref):
    tile = SB_TILE
    qi = pl.program_id(2)
    row = lax.broadcasted_iota(jnp.int32, (tile, tile), 0)
    col = lax.broadcasted_iota(jnp.int32, (tile, tile), 1)
    strict = col < row
    heads = [slice(h * LANES, (h + 1) * LANES) for h in range(SB_GROUP_HEADS)]

    def key_rows(kt):
        return pl.ds(pl.multiple_of(kt * tile, tile), tile)

    def tile_terms(kt, diagonal):
        tri = tri_ref[...]
        out = []
        for h, sl in enumerate(heads):
            q = q_ref[0, :, (h // 2) * LANES:(h // 2 + 1) * LANES]
            z = _dot_nt(q, k_ref[0, key_rows(kt), sl])
            sp = jnp.maximum(z, 0.0) + jnp.log(1.0 + jnp.exp(-jnp.abs(z)))
            if diagonal:
                sp = jnp.where(strict, sp, 0.0)
            hi = sp.astype(_BF16)
            lo = (sp - hi.astype(_F32)).astype(_BF16)
            out.append((z, _dot(jnp.concatenate([hi, lo], axis=-1), tri)))
        return out

    def absorb(kt, terms, rems, accs, diagonal, live=None):
        new_rems, new_accs = [], []
        for sl, (z, incl), rem, acc in zip(heads, terms, rems, accs):
            p = jnp.exp(z + (rem + incl))
            if diagonal:
                p = jnp.where(strict, p, 0.0)
            step = incl[:, :1]
            if live is not None:
                p = jnp.where(live, p, 0.0)
                step = jnp.where(live, step, 0.0)
            new_accs.append(acc + _dot(p.astype(_BF16), v_ref[0, key_rows(kt), sl]))
            new_rems.append(rem + step)
        return tuple(new_rems), tuple(new_accs)

    def rem_max(rems):
        return functools.reduce(jnp.maximum, [jnp.max(r) for r in rems])

    terms = [tile_terms(qi, True)] + [tile_terms(jnp.maximum(qi - d, 0), False) for d in range(1, SB_FIRST_TILES)]
    rems = tuple(jnp.zeros((tile, 1), _F32) for _ in range(SB_GROUP_HEADS))
    accs = tuple(jnp.zeros((tile, LANES), _F32) for _ in range(SB_GROUP_HEADS))
    rems, accs = absorb(qi, terms[0], rems, accs, True)
    for d in range(1, SB_FIRST_TILES):
        rems, accs = absorb(jnp.maximum(qi - d, 0), terms[d], rems, accs, False, live=qi >= d)

    def save(rems, accs):
        for h in range(SB_GROUP_HEADS):
            rem_ref[h] = rems[h]
            acc_ref[h] = accs[h]
    save(rems, accs)

    def cond(state):
        kt, worst = state
        return jnp.logical_and(kt >= 0, worst > SB_LOG_FLOOR)

    def body(state):
        kt, _ = state
        rems = tuple(rem_ref[h] for h in range(SB_GROUP_HEADS))
        accs = tuple(acc_ref[h] for h in range(SB_GROUP_HEADS))
        rems, accs = absorb(kt, tile_terms(kt, False), rems, accs, False)
        save(rems, accs)
        return kt - 1, rem_max(rems)

    lax.while_loop(cond, body, (qi - SB_FIRST_TILES, rem_max(rems)))
    for hp in range(SB_GROUP_HEADS // 2):
        o_ref[0, :, hp * LANES:(hp + 1) * LANES] = acc_ref[2 * hp] + acc_ref[2 * hp + 1]


def _sb(qs, ks, vs, tri):
    b, s, _ = qs.shape
    tile = SB_TILE
    gw = SB_GROUP_HEADS * LANES
    return pl.pallas_call(
        _sb_kernel,
        grid=(b, SB_HEADS // SB_GROUP_HEADS, s // tile),
        in_specs=[pl.BlockSpec((1, tile, gw // 2), lambda bi, g, qi: (bi, qi, g)),
                  pl.BlockSpec((1, s, gw), lambda bi, g, qi: (bi, 0, g)),
                  pl.BlockSpec((1, s, gw), lambda bi, g, qi: (bi, 0, g)),
                  pl.BlockSpec((2 * tile, tile), lambda bi, g, qi: (0, 0))],
        out_specs=pl.BlockSpec((1, tile, gw // 2), lambda bi, g, qi: (bi, qi, g)),
        out_shape=jax.ShapeDtypeStruct((b, s, HEAD_PAIRS * LANES), _F32),
        scratch_shapes=[pltpu.VMEM((SB_GROUP_HEADS, tile, 1), _F32), pltpu.VMEM((SB_GROUP_HEADS, tile, LANES), _F32)],
        compiler_params=pltpu.CompilerParams(dimension_semantics=("parallel", "parallel", "arbitrary"),
                                             vmem_limit_bytes=VMEM_LIMIT),
        name="sb",
    )(qs, ks, vs, tri)


def _post_kernel(om_ref, os_ref, x_ref, g_om_ref, g_os_ref, wo_ref, g_moe_ref, wr_ref, br_ref, ltri_ref,
                 h1_ref, u2_ref, route_ref, counts_ref, seen_ref):
    rows = POST_ROWS

    @pl.when(pl.program_id(0) == 0)
    def _():
        seen_ref[...] = jnp.zeros_like(seen_ref)

    mixed = jnp.concatenate([_rms(om_ref[...], g_om_ref[...]), _rms(os_ref[...], g_os_ref[...])], axis=-1)
    h1 = x_ref[...] + _dot(mixed.astype(_BF16), wo_ref[...])
    h1_ref[...] = h1
    u2 = _rms(h1, g_moe_ref[...])
    _store_row_tiles(u2_ref, u2)
    w_hi = wr_ref[...].astype(_BF16)
    w_lo = (wr_ref[...] - w_hi.astype(_F32)).astype(_BF16)
    u_hi = u2.astype(_BF16)
    u_lo = (u2 - u_hi.astype(_F32)).astype(_BF16)
    logits = _dot(u_hi, w_hi) + _dot(u_hi, w_lo) + _dot(u_lo, w_hi) + br_ref[...]

    part = rows // POST_PARTS
    lane = lax.broadcasted_iota(jnp.int32, (part, N_EXPERTS), 1).astype(_F32)
    out_lane = lax.broadcasted_iota(jnp.int32, (part, LANES), 1)
    work = [logits[r0:r0 + part, :] for r0 in range(0, rows, part)]
    onehot = [jnp.zeros((part, N_EXPERTS), _F32) for _ in work]
    ids = [[] for _ in work]
    tops = [[] for _ in work]
    for _ in range(TOP_K):
        for n in range(POST_PARTS):
            top = jnp.max(work[n], axis=-1, keepdims=True)
            idx = jnp.min(jnp.where(work[n] == top, lane, float(N_EXPERTS)), axis=-1, keepdims=True)
            hit = lane == idx
            onehot[n] = jnp.where(hit, 1.0, onehot[n])
            work[n] = jnp.where(hit, -jnp.inf, work[n])
            ids[n].append(idx)
            tops[n].append(top)

    seen = seen_ref[...]
    for n in range(POST_PARTS):
        exps = [jnp.exp(tp - tops[n][0]) for tp in tops[n]]
        denom = exps[0] + exps[1] + exps[2] + exps[3]
        gates = [e / denom for e in exps]
        before = seen + _dot(ltri_ref[...], onehot[n].astype(_BF16))
        ranks = [jnp.sum(jnp.where(lane == idx, before, 0.0), axis=-1, keepdims=True) for idx in ids[n]]
        seen = seen + jnp.sum(onehot[n], axis=0, keepdims=True)
        route = jnp.zeros((part, LANES), _F32)
        for j, val in enumerate(ids[n] + gates + ranks):
            route = jnp.where(out_lane == j, val, route)
        route_ref[n * part:(n + 1) * part, :] = route
    seen_ref[...] = seen
    counts_ref[...] = seen


def _post(om, os_, x2, g_om, g_os, wo, g_moe, wr, br, ltri):
    t = x2.shape[0]
    rows = POST_ROWS

    def full(a):
        return pl.BlockSpec(a.shape, lambda i: (0,) * a.ndim)

    def tok(n):
        return pl.BlockSpec((rows, n), lambda i: (i, 0))

    return pl.pallas_call(
        _post_kernel,
        grid=(t // rows,),
        in_specs=[tok(om.shape[1]), tok(os_.shape[1]), tok(D_MODEL), full(g_om), full(g_os), full(wo), full(g_moe),
                  full(wr), full(br), full(ltri)],
        out_specs=[tok(D_MODEL), pl.BlockSpec((rows * ROW_TILE, LANES), lambda i: (i, 0)), tok(LANES),
                   pl.BlockSpec((1, N_EXPERTS), lambda i: (0, 0))],
        out_shape=[jax.ShapeDtypeStruct((t, D_MODEL), _F32), jax.ShapeDtypeStruct((t * ROW_TILE, LANES), _F32),
                   jax.ShapeDtypeStruct((t, LANES), _F32), jax.ShapeDtypeStruct((1, N_EXPERTS), _F32)],
        scratch_shapes=[pltpu.VMEM((1, N_EXPERTS), _F32)],
        compiler_params=pltpu.CompilerParams(dimension_semantics=("arbitrary",), vmem_limit_bytes=VMEM_LIMIT),
        name="post",
    )(om, os_, x2, g_om, g_os, wo, g_moe, wr, br, ltri)


def _padded_count(cnt_ref, e):
    shift = EXPERT_ROWS.bit_length() - 1
    return lax.shift_left(lax.shift_right_logical(cnt_ref[e] + (EXPERT_ROWS - 1), shift), shift)


def _layout_kernel(cnt_ref, route_ref, dest_ref, start_ref, be_ref, used_ref):
    shift = EXPERT_ROWS.bit_length() - 1
    rows = MOE_ROWS
    nb = be_ref.shape[0]

    def place(e, off):
        padded = _padded_count(cnt_ref, e)
        start_ref[e] = off

        def mark(j, c):
            be_ref[lax.shift_right_logical(off, shift) + j] = e
            return c
        lax.fori_loop(0, lax.shift_right_logical(padded, shift), mark, 0)
        return off + padded
    total = lax.fori_loop(0, N_EXPERTS, place, 0)
    used_ref[0] = lax.shift_right_logical(total, shift)

    def tail(b, c):
        be_ref[b] = N_EXPERTS - 1
        return c
    lax.fori_loop(lax.shift_right_logical(total, shift), nb, tail, 0)

    lane = lax.broadcasted_iota(jnp.int32, (1, LANES), 1)
    rank_lanes = jnp.logical_and(lane >= 2 * TOP_K, lane < 3 * TOP_K)
    pick = (lax.broadcasted_iota(jnp.int32, (2 * SUBLANES, LANES), 0)
            == lax.broadcasted_iota(jnp.int32, (2 * SUBLANES, LANES), 1)).astype(_BF16)

    def tile(i, c):
        r = route_ref[pl.ds(pl.multiple_of(i * rows, rows), rows), :]
        high = jnp.floor(r * (1.0 / 256.0))
        low_digits = jnp.where(rank_lanes, r - 256.0 * high, r)
        high_digits = jnp.where(rank_lanes, high, 0.0)
        low_t = _dot_nt(pick, low_digits.astype(_BF16))
        high_t = _dot_nt(pick, high_digits.astype(_BF16))
        ids = low_t[:TOP_K, :].astype(jnp.int32)
        dest = (256.0 * high_t[2 * TOP_K:3 * TOP_K, :] + low_t[2 * TOP_K:3 * TOP_K, :]).astype(jnp.int32)
        for e in range(N_EXPERTS):
            dest = dest + jnp.where(ids == e, start_ref[e], 0)
        lines = [dest[k:k + 1, half * LANES:(half + 1) * LANES] for k in range(TOP_K) for half in range(rows // LANES)]
        dest_ref[pl.ds(pl.multiple_of(i * SUBLANES, SUBLANES), SUBLANES), :] = jnp.concatenate(lines, axis=0)
        return c
    lax.fori_loop(0, route_ref.shape[0] // rows, tile, 0)


def _layout(counts, route, n_blocks):
    t = route.shape[0]
    pair_shape = (t * TOP_K // LANES, LANES)
    grid_spec = pltpu.PrefetchScalarGridSpec(
        num_scalar_prefetch=1,
        grid=(1,),
        in_specs=[pl.BlockSpec(route.shape, lambda i, c: (0, 0))],
        out_specs=[pl.BlockSpec(pair_shape, lambda i, c: (0, 0)), pl.BlockSpec(memory_space=pltpu.SMEM),
                   pl.BlockSpec(memory_space=pltpu.SMEM), pl.BlockSpec(memory_space=pltpu.SMEM)],
    )
    return pl.pallas_call(
        _layout_kernel,
        grid_spec=grid_spec,
        out_shape=[jax.ShapeDtypeStruct(pair_shape, jnp.int32), jax.ShapeDtypeStruct((N_EXPERTS,), jnp.int32),
                   jax.ShapeDtypeStruct((n_blocks,), jnp.int32), jax.ShapeDtypeStruct((1,), jnp.int32)],
        compiler_params=pltpu.CompilerParams(dimension_semantics=("arbitrary",), vmem_limit_bytes=VMEM_LIMIT),
        name="layout",
    )(counts, route)


def _tile_rows(row):
    return pl.ds(pl.multiple_of(row * ROW_TILE, ROW_TILE), ROW_TILE)


def _for_each_pair(idx, s, body):
    halves = MOE_ROWS // LANES

    def line(g, c):
        choice = g // halves
        first = lax.rem(g, halves) * LANES
        for t0 in range(0, LANES, DMA_GROUP):
            vals = [idx[s, g, t0 + j] for j in range(DMA_GROUP)]
            for j, row in enumerate(vals):
                body(first + t0 + j, choice, row, j % DMA_THREADS)
        return c
    lax.fori_loop(0, SUBLANES, line, 0)


def _dispatch_kernel(cnt_ref, start_ref, dest_hbm, u_hbm, xs_hbm, idx, stage, zblk, sem_d, sem_l, sem_i, sem_z):
    rows = MOE_ROWS
    shift = EXPERT_ROWS.bit_length() - 1
    i = pl.program_id(0)
    nt = pl.num_programs(0)
    nb = xs_hbm.shape[0] // (EXPERT_ROWS * ROW_TILE)
    slot = lax.rem(i, 2)
    stage_slot = lax.rem(i, STAGE_SLOTS)

    def idx_copy(tile, s):
        return pltpu.make_async_copy(dest_hbm.at[pl.ds(pl.multiple_of(tile * SUBLANES, SUBLANES), SUBLANES)],
                                     idx.at[s], sem_i.at[s])

    def load(tile):
        s = lax.rem(tile, STAGE_SLOTS)
        src = pl.ds(pl.multiple_of(tile * (rows * ROW_TILE), rows * ROW_TILE), rows * ROW_TILE)
        return pltpu.make_async_copy(u_hbm.at[src], stage.at[s], sem_l.at[s])

    def wait_rows(s):
        chunk = pl.ds(0, rows * ROW_TILE)
        for _ in range(TOP_K):
            pltpu.make_async_copy(stage.at[s], xs_hbm.at[chunk], sem_d.at[s]).wait()

    def zero_block(b):
        dst = pl.ds(pl.multiple_of(b * (EXPERT_ROWS * ROW_TILE), EXPERT_ROWS * ROW_TILE), EXPERT_ROWS * ROW_TILE)
        return pltpu.make_async_copy(zblk, xs_hbm.at[dst], sem_z)

    def pad_rows(e, fn):
        pad = _padded_count(cnt_ref, e) - cnt_ref[e]
        first = start_ref[e] + cnt_ref[e]
        for bit in reversed(range(shift)):
            size = 1 << bit
            before = lax.shift_left(lax.shift_right_logical(pad, bit + 1), bit + 1)

            @pl.when(lax.shift_right_logical(pad, bit) & 1 == 1)
            def _(size=size, before=before):
                dst = pl.ds(pl.multiple_of((first + before) * ROW_TILE, ROW_TILE), size * ROW_TILE)
                fn(pltpu.make_async_copy(zblk.at[pl.ds(0, size * ROW_TILE)], xs_hbm.at[dst], sem_z))

    def tail_blocks(fn):
        used = start_ref[N_EXPERTS - 1] + _padded_count(cnt_ref, N_EXPERTS - 1)

        def one(b, c):
            fn(zero_block(b))
            return c
        lax.fori_loop(lax.shift_right_logical(used, shift), nb, one, 0)

    @pl.when(i == 0)
    def _():
        idx_copy(0, 0).start()
        load(0).start()

        @pl.when(nt > 1)
        def _():
            load(1).start()
        zblk[...] = jnp.zeros_like(zblk)
        for fn in (lambda c: c.start(), lambda c: c.wait()):
            def per_expert(e, c, fn=fn):
                pad_rows(e, fn)
                return c
            lax.fori_loop(0, N_EXPERTS, per_expert, 0)
            tail_blocks(fn)

    idx_copy(i, slot).wait()

    @pl.when(i + 1 < nt)
    def _():
        idx_copy(i + 1, 1 - slot).start()

    @pl.when(i >= 2)
    def _():
        wait_rows(lax.rem(i + 2, STAGE_SLOTS))

    @pl.when(i + 2 < nt)
    def _():
        load(i + 2).start()

    load(i).wait()

    def copy_row(t, k, row, thread):
        del k
        pltpu.make_async_copy(stage.at[stage_slot, _tile_rows(t)], xs_hbm.at[_tile_rows(row)],
                              sem_d.at[stage_slot]).start(priority=thread)
    _for_each_pair(idx, slot, copy_row)

    @pl.when(i == nt - 1)
    def _():
        @pl.when(nt > 1)
        def _():
            wait_rows(lax.rem(i + STAGE_SLOTS - 1, STAGE_SLOTS))
        wait_rows(stage_slot)


def _dispatch(counts, start, dest, u2_tiles, n_blocks):
    nt = dest.shape[0] // SUBLANES
    grid_spec = pltpu.PrefetchScalarGridSpec(
        num_scalar_prefetch=2,
        grid=(nt,),
        in_specs=[pl.BlockSpec(memory_space=pl.ANY), pl.BlockSpec(memory_space=pl.ANY)],
        out_specs=pl.BlockSpec(memory_space=pl.ANY),
        scratch_shapes=[pltpu.SMEM((2, SUBLANES, LANES), jnp.int32),
                        pltpu.VMEM((STAGE_SLOTS, MOE_ROWS * ROW_TILE, LANES), _F32),
                        pltpu.VMEM((EXPERT_ROWS * ROW_TILE, LANES), _F32),
                        pltpu.SemaphoreType.DMA((STAGE_SLOTS,)), pltpu.SemaphoreType.DMA((STAGE_SLOTS,)),
                        pltpu.SemaphoreType.DMA((2,)), pltpu.SemaphoreType.DMA(())],
    )
    return pl.pallas_call(
        _dispatch_kernel,
        grid_spec=grid_spec,
        out_shape=jax.ShapeDtypeStruct((n_blocks * EXPERT_ROWS * ROW_TILE, LANES), _F32),
        compiler_params=pltpu.CompilerParams(dimension_semantics=("arbitrary",), vmem_limit_bytes=VMEM_LIMIT),
        name="dispatch",
    )(counts, start, dest, u2_tiles)


def _expert_kernel(be_ref, used_ref, xs_ref, wgu_hbm, bgu_ref, wdn_hbm, bdn_ref, ys_ref,
                   wgu_f32, wdn_f32, wgu_bf, wdn_bf, sem_w):
    rows = EXPERT_ROWS
    i = pl.program_id(0)
    nb = pl.num_programs(0)
    used = used_ref[0]
    e = be_ref[i]
    prev = be_ref[jnp.maximum(i - 1, 0)]
    in_use = i < used

    def fetch(expert):
        return (pltpu.make_async_copy(wgu_hbm.at[expert], wgu_f32, sem_w.at[0]),
                pltpu.make_async_copy(wdn_hbm.at[expert], wdn_f32, sem_w.at[1]))

    @pl.when(i == 0)
    def _():
        for c in fetch(e):
            c.start()

    @pl.when(jnp.logical_and(in_use, jnp.logical_or(i == 0, e != prev)))
    def _():
        for c in fetch(e):
            c.wait()
        chunk = 128

        def cast(c, carry):
            rs = pl.ds(pl.multiple_of(c * chunk, chunk), chunk)
            wgu_bf[rs, :] = wgu_f32[rs, :].astype(_BF16)
            wdn_bf[rs, :] = wdn_f32[rs, :].astype(_BF16)
            return carry
        lax.fori_loop(0, D_MODEL // chunk, cast, 0)

        def same_expert(j):
            return jnp.logical_and(j < used, be_ref[jnp.minimum(j, nb - 1)] == e)
        nxt = lax.while_loop(same_expert, lambda j: j + 1, i)

        @pl.when(nxt < used)
        def _():
            for c in fetch(be_ref[jnp.minimum(nxt, nb - 1)]):
                c.start()

    @pl.when(in_use)
    def _():
        x = _load_row_tiles(xs_ref, 0, rows).astype(_BF16)
        hh = _dot(x, wgu_bf[...]) + bgu_ref[0]
        glu = jnp.minimum(hh[:, :D_FF], SWIGLU_LIMIT)
        lin = jnp.clip(hh[:, D_FF:], -SWIGLU_LIMIT, SWIGLU_LIMIT)
        act = glu * jax.nn.sigmoid(SWIGLU_ALPHA * glu) * (lin + 1.0)
        _store_row_tiles(ys_ref, _dot(act.astype(_BF16), wdn_bf[...]) + bdn_ref[0])

    @pl.when(jnp.logical_not(in_use))
    def _():
        ys_ref[...] = jnp.zeros_like(ys_ref)


def _experts(block_e, used, xs, w_gu, b_gu, w_dn, b_dn):
    nb = block_e.shape[0]
    rows = EXPERT_ROWS
    grid_spec = pltpu.PrefetchScalarGridSpec(
        num_scalar_prefetch=2,
        grid=(nb,),
        in_specs=[pl.BlockSpec((rows * ROW_TILE, LANES), lambda i, be, u: (i, 0)),
                  pl.BlockSpec(memory_space=pl.ANY),
                  pl.BlockSpec((1, 1, 2 * D_FF), lambda i, be, u: (be[i], 0, 0)),
                  pl.BlockSpec(memory_space=pl.ANY),
                  pl.BlockSpec((1, 1, D_MODEL), lambda i, be, u: (be[i], 0, 0))],
        out_specs=pl.BlockSpec((rows * ROW_TILE, LANES), lambda i, be, u: (i, 0)),
        scratch_shapes=[pltpu.VMEM((D_MODEL, 2 * D_FF), _F32), pltpu.VMEM((D_FF, D_MODEL), _F32),
                        pltpu.VMEM((D_MODEL, 2 * D_FF), _BF16), pltpu.VMEM((D_FF, D_MODEL), _BF16),
                        pltpu.SemaphoreType.DMA((2,))],
    )
    return pl.pallas_call(
        _expert_kernel,
        grid_spec=grid_spec,
        out_shape=jax.ShapeDtypeStruct(xs.shape, _F32),
        compiler_params=pltpu.CompilerParams(dimension_semantics=("arbitrary",), vmem_limit_bytes=VMEM_LIMIT),
        name="experts",
    )(block_e, used, xs, w_gu, b_gu.reshape(N_EXPERTS, 1, 2 * D_FF), w_dn, b_dn.reshape(N_EXPERTS, 1, D_MODEL))


def _final_kernel(dest_hbm, ys_hbm, h1_ref, route_ref, p_ref, g_ple_ref, wpg_ref, wpp_ref, g_fin_ref,
                  out_ref, ybuf, idx, sem_y, sem_i, *, last_layer):
    rows = MOE_ROWS
    i = pl.program_id(0)
    nt = pl.num_programs(0)
    slot = lax.rem(i, 2)

    def idx_copy(tile, s):
        return pltpu.make_async_copy(dest_hbm.at[pl.ds(pl.multiple_of(tile * SUBLANES, SUBLANES), SUBLANES)],
                                     idx.at[s], sem_i.at[s])

    def gather(s):
        def copy_row(t, k, row, thread):
            pltpu.make_async_copy(ys_hbm.at[_tile_rows(row)], ybuf.at[s, _tile_rows(k * rows + t)],
                                  sem_y.at[s]).start(priority=thread)
        _for_each_pair(idx, s, copy_row)

    @pl.when(i == 0)
    def _():
        idx_copy(0, 0).start()
        idx_copy(0, 0).wait()
        gather(0)

        @pl.when(nt > 1)
        def _():
            idx_copy(1, 1).start()

    @pl.when(i + 1 < nt)
    def _():
        idx_copy(i + 1, 1 - slot).wait()
        gather(1 - slot)

        @pl.when(i + 2 < nt)
        def _():
            idx_copy(i + 2, slot).start()

    pltpu.make_async_copy(ys_hbm.at[pl.ds(0, TOP_K * rows * ROW_TILE)], ybuf.at[slot], sem_y.at[slot]).wait()
    part = rows // ROW_PARTS
    for r0 in range(0, rows, part):
        rs = slice(r0, r0 + part)
        route = route_ref[rs, :]
        y = jnp.zeros((part, D_MODEL), _F32)
        for k in range(TOP_K):
            y = y + _load_row_tiles(ybuf.at[slot], k * rows + r0, part) * route[:, TOP_K + k:TOP_K + k + 1]
        h2 = h1_ref[rs, :] + y
        u3 = _rms(h2, g_ple_ref[...]).astype(_BF16)
        gate = jax.nn.sigmoid(_dot(u3, wpg_ref[...]))
        h3 = h2 + gate * _dot(p_ref[rs, :].astype(_BF16), wpp_ref[...])
        out_ref[rs, :] = _rms(h3, g_fin_ref[...]) if last_layer else h3


def _final(dest, ys, h1, route, p2, g_ple, wpg, wpp, g_fin, last_layer):
    t = h1.shape[0]
    rows = MOE_ROWS

    def full(a):
        return pl.BlockSpec(a.shape, lambda i: (0,) * a.ndim)

    def tok(n):
        return pl.BlockSpec((rows, n), lambda i: (i, 0))

    return pl.pallas_call(
        functools.partial(_final_kernel, last_layer=last_layer),
        grid=(t // rows,),
        in_specs=[pl.BlockSpec(memory_space=pl.ANY), pl.BlockSpec(memory_space=pl.ANY), tok(D_MODEL), tok(LANES),
                  tok(PLE_DIM), full(g_ple), full(wpg), full(wpp), full(g_fin)],
        out_specs=tok(D_MODEL),
        out_shape=jax.ShapeDtypeStruct((t, D_MODEL), _F32),
        scratch_shapes=[pltpu.VMEM((2, TOP_K * rows * ROW_TILE, LANES), _F32),
                        pltpu.SMEM((2, SUBLANES, LANES), jnp.int32),
                        pltpu.SemaphoreType.DMA((2,)), pltpu.SemaphoreType.DMA((2,))],
        compiler_params=pltpu.CompilerParams(dimension_semantics=("arbitrary",), vmem_limit_bytes=VMEM_LIMIT),
        name="final",
    )(dest, ys, h1, route, p2, g_ple, wpg, wpp, g_fin)


def _rope_pad(w, rot):
    half = MLA_ROPE_DIM // 2
    body = jnp.concatenate([-w[:, half:], w[:, :half]], axis=1) if rot else w
    z = jnp.zeros((w.shape[0], MLA_NOPE_DIM), w.dtype)
    return jnp.concatenate([z, body, jnp.zeros((w.shape[0], LANES - MLA_NOPE_DIM - MLA_ROPE_DIM), w.dtype)], axis=1)


def _layer_weights(w_in, w_uq, w_ukv):
    o = Q_LORA_RANK + KV_LORA_RANK
    w_kr = w_in[:, o:o + MLA_ROPE_DIM]
    w1 = jnp.concatenate([w_in[:, :o], _rope_pad(w_kr, False), _rope_pad(w_kr, True), w_in[:, o + MLA_ROPE_DIM:]],
                         axis=1).astype(_BF16)
    uq = w_uq.reshape(Q_LORA_RANK, MLA_HEADS, MLA_NOPE_DIM + MLA_ROPE_DIM)
    zq = jnp.zeros((Q_LORA_RANK, MLA_HEADS, LANES - MLA_NOPE_DIM - MLA_ROPE_DIM), w_uq.dtype)
    wq = jnp.concatenate([uq, zq], axis=2).reshape(Q_LORA_RANK, MLA_HEADS * LANES).astype(_BF16)
    rope = uq[:, :, MLA_NOPE_DIM:]
    half = MLA_ROPE_DIM // 2
    rot = jnp.concatenate([jnp.zeros_like(uq[:, :, :MLA_NOPE_DIM]), -rope[:, :, half:], rope[:, :, :half], zq], axis=2)
    wqr = rot.reshape(Q_LORA_RANK, MLA_HEADS * LANES).astype(_BF16)
    ukv = w_ukv.reshape(KV_LORA_RANK, MLA_HEADS, MLA_NOPE_DIM + MLA_V_DIM)
    zk = jnp.zeros((KV_LORA_RANK, MLA_HEADS, LANES - MLA_NOPE_DIM), w_ukv.dtype)
    wk = jnp.concatenate([ukv[:, :, :MLA_NOPE_DIM], zk], axis=2).reshape(KV_LORA_RANK, MLA_HEADS * LANES).astype(_BF16)
    v = ukv[:, :, MLA_NOPE_DIM:].reshape(KV_LORA_RANK, HEAD_PAIRS, 2, MLA_V_DIM)
    zv = jnp.zeros((KV_LORA_RANK, HEAD_PAIRS, MLA_V_DIM), w_ukv.dtype)
    wv = jnp.stack([jnp.concatenate([v[:, :, 0], zv], axis=2), jnp.concatenate([zv, v[:, :, 1]], axis=2)], axis=2)
    wv = wv.reshape(KV_LORA_RANK, MLA_HEADS * LANES).astype(_BF16)
    return w1, wq, wqr, wk, wv


def kernel(x, p, positions, w_in, g_attn, g_cq, w_uq, g_ckv, w_ukv, g_out_mla, g_out_sb, w_o, g_moe, w_router,
           b_router, w_gu, b_gu, w_dn, b_dn, g_ple, w_ple_gate, w_ple_proj, g_final):
    b, s, d = x.shape
    t = b * s
    depth = w_in.shape[0]
    assert d == D_MODEL and ROW_TILE == SUBLANES and s % MLA_K_TILE == 0 and t % PROJ_ROWS == 0 and t % POST_ROWS == 0
    assert t % MOE_ROWS == 0 and LANES % DMA_GROUP == 0 and MOE_ROWS % LANES == 0 and (t * TOP_K) % EXPERT_ROWS == 0
    assert EXPERT_ROWS & (EXPERT_ROWS - 1) == 0

    freq = ROPE_THETA ** (-jnp.arange(0, MLA_ROPE_DIM, 2, dtype=_F32) / MLA_ROPE_DIM)
    invf = jnp.concatenate([jnp.zeros((MLA_NOPE_DIM,), _F32), freq, freq,
                            jnp.zeros((LANES - MLA_NOPE_DIM - MLA_ROPE_DIM,), _F32)]).reshape(1, LANES)
    pos = positions.reshape(t, 1)
    idx = jnp.arange(SB_TILE)
    tri = -(idx[:, None] >= idx[None, :]).astype(_BF16)
    tri = jnp.concatenate([tri, tri], axis=0)
    idx = jnp.arange(POST_ROWS // POST_PARTS)
    ltri = (idx[None, :] < idx[:, None]).astype(_BF16)
    n_blocks = t * TOP_K // EXPERT_ROWS + N_EXPERTS

    h = x.reshape(t, d)
    for i in range(depth):
        w1, wq, wqr, wk, wv = _layer_weights(w_in[i], w_uq[i], w_ukv[i])
        qm, km, vm, qs, ks, vs = _proj(pos, h, g_attn[i].reshape(1, d), w1, g_cq[i].reshape(1, -1), wq, wqr,
                                       g_ckv[i].reshape(1, -1), wk, wv, invf)

        def seq(a):
            return a.reshape(b, s, a.shape[1])

        om = _mla(seq(qm), seq(km), seq(vm)).reshape(t, -1)
        os_ = _sb(seq(qs), seq(ks), seq(vs), tri).reshape(t, -1)
        h1, u2_tiles, route, counts = _post(om, os_, h, g_out_mla[i].reshape(1, -1), g_out_sb[i].reshape(1, -1),
                                            w_o[i].astype(_BF16), g_moe[i].reshape(1, d), w_router[i],
                                            b_router[i].reshape(1, -1), ltri)
        counts = counts.reshape(-1).astype(jnp.int32)
        dest, start, block_e, used = _layout(counts, route, n_blocks)
        xs = _dispatch(counts, start, dest, u2_tiles, n_blocks)
        ys = _experts(block_e, used, xs, w_gu[i], b_gu[i], w_dn[i], b_dn[i])
        h = _final(dest, ys, h1, route, p[i].reshape(t, -1), g_ple[i].reshape(1, d), w_ple_gate[i].astype(_BF16),
                   w_ple_proj[i].astype(_BF16), g_final.reshape(1, d), i == depth - 1)
    return h.reshape(b, s, d)
```

```python
import functools
import math

import jax
import jax.numpy as jnp
from jax import lax
from jax.experimental import pallas as pl
from jax.experimental.pallas import tpu as pltpu

D_MODEL = 1024
PLE_DIM = 256
MLA_HEADS = 8
MLA_NOPE_DIM = 64
MLA_ROPE_DIM = 32
MLA_V_DIM = 64
Q_LORA_RANK = 384
KV_LORA_RANK = 256
SB_HEADS = 8
SB_HEAD_DIM = 64
SB_WIDTH = SB_HEADS * SB_HEAD_DIM
ROPE_THETA = 10000.0
N_EXPERTS = 32
TOP_K = 4
D_FF = 1024
SWIGLU_LIMIT = 7.0
SWIGLU_ALPHA = 1.702
RMS_EPS = 1e-6

LANES = 128
SUBLANES = 8
ROW_TILE = D_MODEL // LANES
HEAD_PAIRS = MLA_HEADS // 2
PROJ_ROWS = 512
POST_ROWS = 256
GROUP_HEADS = 4
SB_GROUP_HEADS = 4
MLA_TILE = 512
SB_TILE = 256
SB_FIRST_TILES = 2
EXPERT_ROWS = 256
ROW_PARTS = 2
POST_PARTS = 4
MOE_ROWS = SUBLANES * LANES // TOP_K
DMA_GROUP = 32
STAGE_SLOTS = 4
DMA_THREADS = 2
VMEM_LIMIT = 56 * 1024 * 1024
SB_LOG_FLOOR = -105.0

_F32 = jnp.float32
_BF16 = jnp.bfloat16


def _rms(x, g):
    return x * lax.rsqrt(jnp.mean(x * x, axis=-1, keepdims=True) + RMS_EPS) * g


def _dot(a, b):
    return jnp.dot(a, b, preferred_element_type=_F32)


def _dot_nt(a, b):
    return lax.dot_general(a, b, (((1,), (1,)), ((), ())), preferred_element_type=_F32)


def _store_row_tiles(ref, value, first_row=0):
    rows = value.shape[0]
    for g in range(ROW_TILE):
        ref[pl.ds(first_row * ROW_TILE + g, rows, stride=ROW_TILE), :] = value[:, g * LANES:(g + 1) * LANES]


def _load_row_tiles(ref, first_row, rows):
    return jnp.concatenate([ref[pl.ds(first_row * ROW_TILE + g, rows, stride=ROW_TILE), :] for g in range(ROW_TILE)],
                           axis=-1)


def _proj_kernel(pos_ref, x_ref, g_attn_ref, w1_ref, g_cq_ref, wq_ref, wqr_ref, g_ckv_ref, wk_ref, wv_ref,
                 invf_ref, qm_ref, km_ref, vm_ref, qs_ref, ks_ref, vs_ref):
    u = _rms(x_ref[...], g_attn_ref[...]).astype(_BF16)
    y = _dot(u, w1_ref[...])
    c_q = y[:, :Q_LORA_RANK]
    c_kv = y[:, Q_LORA_RANK:Q_LORA_RANK + KV_LORA_RANK]
    o = Q_LORA_RANK + KV_LORA_RANK
    k_r = y[:, o:o + LANES]
    k_r_rot = y[:, o + LANES:o + 2 * LANES]
    o += 2 * LANES
    q_s = y[:, o:o + SB_WIDTH]
    k_s = y[:, o + SB_WIDTH:o + 2 * SB_WIDTH]
    v_s = y[:, o + 2 * SB_WIDTH:o + 3 * SB_WIDTH]

    ang = pos_ref[...].astype(_F32) * invf_ref[...]
    cos = jnp.cos(ang)
    sin = jnp.sin(ang)

    cq_n = _rms(c_q, g_cq_ref[...]).astype(_BF16)
    q = _dot(cq_n, wq_ref[...])
    q_rot = _dot(cq_n, wqr_ref[...])
    ckv_n = _rms(c_kv, g_ckv_ref[...]).astype(_BF16)
    k = _dot(ckv_n, wk_ref[...])
    wide_lane = lax.broadcasted_iota(jnp.int32, (1, MLA_HEADS * LANES), 1)
    ones_lane = (wide_lane % (2 * LANES) == MLA_V_DIM) | (wide_lane % (2 * LANES) == LANES)
    vm_ref[...] = jnp.where(ones_lane, 1.0, _dot(ckv_n, wv_ref[...])).astype(_BF16)
    k_rope = k_r * cos + k_r_rot * sin

    q_scale = (MLA_NOPE_DIM + MLA_ROPE_DIM) ** -0.5 * math.log2(math.e)
    lane = lax.broadcasted_iota(jnp.int32, (1, LANES), 1)
    low = lane < SB_HEAD_DIM
    for h in range(MLA_HEADS):
        sl = slice(h * LANES, (h + 1) * LANES)
        qm_ref[:, sl] = ((q[:, sl] * cos + q_rot[:, sl] * sin) * q_scale).astype(_BF16)
        km_ref[:, sl] = (k[:, sl] + k_rope).astype(_BF16)
    qs_ref[...] = (q_s * (SB_HEAD_DIM ** -0.5)).astype(_BF16)
    for hp in range(HEAD_PAIRS):
        sl = slice(hp * LANES, (hp + 1) * LANES)
        for half, keep in ((0, low), (1, jnp.logical_not(low))):
            dst = slice((2 * hp + half) * LANES, (2 * hp + half + 1) * LANES)
            ks_ref[:, dst] = jnp.where(keep, k_s[:, sl], 0.0).astype(_BF16)
            vs_ref[:, dst] = jnp.where(keep, v_s[:, sl], 0.0).astype(_BF16)


def _proj(pos, x2, g_attn, w1, g_cq, wq, wqr, g_ckv, wk, wv, invf):
    t = x2.shape[0]
    rows = PROJ_ROWS
    wide = MLA_HEADS * LANES

    def full(a):
        return pl.BlockSpec(a.shape, lambda i: (0,) * a.ndim)

    def tok(n):
        return pl.BlockSpec((rows, n), lambda i: (i, 0))

    outs = [jax.ShapeDtypeStruct((t, n), _BF16) for n in (wide, wide, wide, SB_WIDTH, wide, wide)]
    return pl.pallas_call(
        _proj_kernel,
        grid=(t // rows,),
        in_specs=[tok(1), tok(D_MODEL), full(g_attn), full(w1), full(g_cq), full(wq), full(wqr), full(g_ckv),
                  full(wk), full(wv), full(invf)],
        out_specs=[tok(wide), tok(wide), tok(wide), tok(SB_WIDTH), tok(wide), tok(wide)],
        out_shape=outs,
        compiler_params=pltpu.CompilerParams(dimension_semantics=("parallel",), vmem_limit_bytes=VMEM_LIMIT),
        name="proj",
    )(pos, x2, g_attn, w1, g_cq, wq, wqr, g_ckv, wk, wv, invf)


def _mla_kernel(q_ref, k_ref, v_ref, o_ref, s_a, s_b, m_ref, acc_ref):
    tile, half = MLA_TILE, MLA_TILE // 2
    qi = pl.program_id(2)
    heads = [slice(h * LANES, (h + 1) * LANES) for h in range(GROUP_HEADS)]
    everything = slice(0, tile)

    def key_rows(kt, n=tile):
        return pl.ds(pl.multiple_of(kt * tile, tile), n)

    def score(dst, kt):
        for h, sl in enumerate(heads):
            dst[h] = _dot_nt(q_ref[0, :, sl], k_ref[0, key_rows(kt), sl])

    def score_diagonal(dst):
        upper_mask = (lax.broadcasted_iota(jnp.int32, (half, half), 1)
                      <= lax.broadcasted_iota(jnp.int32, (half, half), 0))
        lower_mask = (lax.broadcasted_iota(jnp.int32, (half, tile), 1)
                      <= lax.broadcasted_iota(jnp.int32, (half, tile), 0) + half)
        for h, sl in enumerate(heads):
            upper = _dot_nt(q_ref[0, :half, sl], k_ref[0, key_rows(qi, half), sl])
            dst[h, :half, :half] = jnp.where(upper_mask, upper, -jnp.inf)
            lower = _dot_nt(q_ref[0, half:, sl], k_ref[0, key_rows(qi), sl])
            dst[h, half:, :] = jnp.where(lower_mask, lower, -jnp.inf)

    def absorb_block(src, h, sl, part, cols, keys):
        s = src[h, part * half:(part + 1) * half, cols]
        m = m_ref[h, part]
        row_max = jnp.max(s, axis=-1, keepdims=True)
        m_new = jnp.maximum(m, row_max)
        alpha = jnp.exp2(m - m_new)
        p = jnp.exp2(s - jnp.maximum(m[:, :1], row_max))
        m_ref[h, part] = m_new
        acc_ref[h, part] = alpha * acc_ref[h, part] + _dot(p.astype(_BF16), v_ref[0, keys, sl])

    def absorb(src, kt):
        for h, sl in enumerate(heads):
            for part in range(2):
                absorb_block(src, h, sl, part, everything, key_rows(kt))

    def absorb_diagonal(src):
        for h, sl in enumerate(heads):
            absorb_block(src, h, sl, 0, slice(0, half), key_rows(qi, half))
            absorb_block(src, h, sl, 1, everything, key_rows(qi))

    m_ref[...] = jnp.full(m_ref.shape, -jnp.inf, _F32)
    acc_ref[...] = jnp.zeros(acc_ref.shape, _F32)

    @pl.when(qi == 0)
    def _():
        score_diagonal(s_a)
        absorb_diagonal(s_a)

    @pl.when(qi > 0)
    def _():
        score(s_a, 0)

        def pair(i, c):
            score(s_b, 2 * i + 1)
            absorb(s_a, 2 * i)
            score(s_a, 2 * i + 2)
            absorb(s_b, 2 * i + 1)
            return c
        lax.fori_loop(0, (qi - 1) // 2, pair, 0)

        @pl.when(qi % 2 == 1)
        def _():
            score_diagonal(s_b)
            absorb(s_a, qi - 1)
            absorb_diagonal(s_b)

        @pl.when(qi % 2 == 0)
        def _():
            score(s_b, qi - 1)
            absorb(s_a, qi - 2)
            score_diagonal(s_a)
            absorb(s_b, qi - 1)
            absorb_diagonal(s_a)

    low = lax.broadcasted_iota(jnp.int32, (half, LANES), 1) < MLA_V_DIM
    for hp in range(GROUP_HEADS // 2):
        for part in range(2):
            even, odd = acc_ref[2 * hp, part], acc_ref[2 * hp + 1, part]
            o_ref[0, part * half:(part + 1) * half, hp * LANES:(hp + 1) * LANES] = jnp.where(
                low, even / even[:, MLA_V_DIM:MLA_V_DIM + 1], odd / odd[:, :1])


def _mla(qm, km, vm):
    b, s, _ = qm.shape
    tq = MLA_TILE
    gw = GROUP_HEADS * LANES
    return pl.pallas_call(
        _mla_kernel,
        grid=(b, MLA_HEADS // GROUP_HEADS, s // tq),
        in_specs=[pl.BlockSpec((1, tq, gw), lambda bi, g, qi: (bi, qi, g)),
                  pl.BlockSpec((1, s, gw), lambda bi, g, qi: (bi, 0, g)),
                  pl.BlockSpec((1, s, gw), lambda bi, g, qi: (bi, 0, g))],
        out_specs=pl.BlockSpec((1, tq, gw // 2), lambda bi, g, qi: (bi, qi, g)),
        out_shape=jax.ShapeDtypeStruct((b, s, HEAD_PAIRS * LANES), _F32),
        scratch_shapes=[pltpu.VMEM((GROUP_HEADS, tq, tq), _F32), pltpu.VMEM((GROUP_HEADS, tq, tq), _F32),
                        pltpu.VMEM((GROUP_HEADS, 2, tq // 2, LANES), _F32),
                        pltpu.VMEM((GROUP_HEADS, 2, tq // 2, LANES), _F32)],
        compiler_params=pltpu.CompilerParams(dimension_semantics=("parallel", "parallel", "arbitrary"),
                                             vmem_limit_bytes=VMEM_LIMIT),
        name="mla",
    )(qm, km, vm)


def _sb_kernel(q_ref, k_ref, v_ref, tri_ref, o_ref, rem_ref, acc_ref):
    tile = SB_TILE
    qi = pl.program_id(2)
    row = lax.broadcasted_iota(jnp.int32, (tile, tile), 0)
    col = lax.broadcasted_iota(jnp.int32, (tile, tile), 1)
    strict = col < row
    heads = [slice(h * LANES, (h + 1) * LANES) for h in range(SB_GROUP_HEADS)]

    def key_rows(kt):
        return pl.ds(pl.multiple_of(kt * tile, tile), tile)

    def tile_terms(kt, diagonal):
        tri = tri_ref[...]
        out = []
        for h, sl in enumerate(heads):
            q = q_ref[0, :, (h // 2) * LANES:(h // 2 + 1) * LANES]
            z = _dot_nt(q, k_ref[0, key_rows(kt), sl])
            sp = jnp.maximum(z, 0.0) + jnp.log(1.0 + jnp.exp(-jnp.abs(z)))
            if diagonal:
                sp = jnp.where(strict, sp, 0.0)
            hi = sp.astype(_BF16)
            lo = (sp - hi.astype(_F32)).astype(_BF16)
            out.append((z, _dot(jnp.concatenate([hi, lo], axis=-1), tri)))
        return out

    def absorb(kt, terms, rems, accs, diagonal, live=None):
        new_rems, new_accs = [], []
        for sl, (z, incl), rem, acc in zip(heads, terms, rems, accs):
            p = jnp.exp(z + (rem + incl))
            if diagonal:
                p = jnp.where(strict, p, 0.0)
            step = incl[:, :1]
            if live is not None:
                p = jnp.where(live, p, 0.0)
                step = jnp.where(live, step, 0.0)
            new_accs.append(acc + _dot(p.astype(_BF16), v_ref[0, key_rows(kt), sl]))
            new_rems.append(rem + step)
        return tuple(new_rems), tuple(new_accs)

    def rem_max(rems):
        return functools.reduce(jnp.maximum, [jnp.max(r) for r in rems])

    terms = [tile_terms(qi, True)] + [tile_terms(jnp.maximum(qi - d, 0), False) for d in range(1, SB_FIRST_TILES)]
    rems = tuple(jnp.zeros((tile, 1), _F32) for _ in range(SB_GROUP_HEADS))
    accs = tuple(jnp.zeros((tile, LANES), _F32) for _ in range(SB_GROUP_HEADS))
    rems, accs = absorb(qi, terms[0], rems, accs, True)
    for d in range(1, SB_FIRST_TILES):
        rems, accs = absorb(jnp.maximum(qi - d, 0), terms[d], rems, accs, False, live=qi >= d)

    def save(rems, accs):
        for h in range(SB_GROUP_HEADS):
            rem_ref[h] = rems[h]
            acc_ref[h] = accs[h]
    save(rems, accs)

    def cond(state):
        kt, worst = state
        return jnp.logical_and(kt >= 0, worst > SB_LOG_FLOOR)

    def body(state):
        kt, _ = state
        rems = tuple(rem_ref[h] for h in range(SB_GROUP_HEADS))
        accs = tuple(acc_ref[h] for h in range(SB_GROUP_HEADS))
        rems, accs = absorb(kt, tile_terms(kt, False), rems, accs, False)
        save(rems, accs)
        return kt - 1, rem_max(rems)

    lax.while_loop(cond, body, (qi - SB_FIRST_TILES, rem_max(rems)))
    for hp in range(SB_GROUP_HEADS // 2):
        o_ref[0, :, hp * LANES:(hp + 1) * LANES] = acc_ref[2 * hp] + acc_ref[2 * hp + 1]


def _sb(qs, ks, vs, tri):
    b, s, _ = qs.shape
    tile = SB_TILE
    gw = SB_GROUP_HEADS * LANES
    return pl.pallas_call(
        _sb_kernel,
        grid=(b, SB_HEADS // SB_GROUP_HEADS, s // tile),
        in_specs=[pl.BlockSpec((1, tile, gw // 2), lambda bi, g, qi: (bi, qi, g)),
                  pl.BlockSpec((1, s, gw), lambda bi, g, qi: (bi, 0, g)),
                  pl.BlockSpec((1, s, gw), lambda bi, g, qi: (bi, 0, g)),
                  pl.BlockSpec((2 * tile, tile), lambda bi, g, qi: (0, 0))],
        out_specs=pl.BlockSpec((1, tile, gw // 2), lambda bi, g, qi: (bi, qi, g)),
        out_shape=jax.ShapeDtypeStruct((b, s, HEAD_PAIRS * LANES), _F32),
        scratch_shapes=[pltpu.VMEM((SB_GROUP_HEADS, tile, 1), _F32), pltpu.VMEM((SB_GROUP_HEADS, tile, LANES), _F32)],
        compiler_params=pltpu.CompilerParams(dimension_semantics=("parallel", "parallel", "arbitrary"),
                                             vmem_limit_bytes=VMEM_LIMIT),
        name="sb",
    )(qs, ks, vs, tri)


def _post_kernel(om_ref, os_ref, x_ref, g_om_ref, g_os_ref, wo_ref, g_moe_ref, wr_ref, br_ref, ltri_ref,
                 h1_ref, u2_ref, route_ref, counts_ref, seen_ref):
    rows = POST_ROWS

    @pl.when(pl.program_id(0) == 0)
    def _():
        seen_ref[...] = jnp.zeros_like(seen_ref)

    mixed = jnp.concatenate([_rms(om_ref[...], g_om_ref[...]), _rms(os_ref[...], g_os_ref[...])], axis=-1)
    h1 = x_ref[...] + _dot(mixed.astype(_BF16), wo_ref[...])
    h1_ref[...] = h1
    u2 = _rms(h1, g_moe_ref[...])
    _store_row_tiles(u2_ref, u2)
    w_hi = wr_ref[...].astype(_BF16)
    w_lo = (wr_ref[...] - w_hi.astype(_F32)).astype(_BF16)
    u_hi = u2.astype(_BF16)
    u_lo = (u2 - u_hi.astype(_F32)).astype(_BF16)
    logits = _dot(u_hi, w_hi) + _dot(u_hi, w_lo) + _dot(u_lo, w_hi) + br_ref[...]

    part = rows // POST_PARTS
    lane = lax.broadcasted_iota(jnp.int32, (part, N_EXPERTS), 1).astype(_F32)
    out_lane = lax.broadcasted_iota(jnp.int32, (part, LANES), 1)
    work = [logits[r0:r0 + part, :] for r0 in range(0, rows, part)]
    onehot = [jnp.zeros((part, N_EXPERTS), _F32) for _ in work]
    ids = [[] for _ in work]
    tops = [[] for _ in work]
    for _ in range(TOP_K):
        for n in range(POST_PARTS):
            top = jnp.max(work[n], axis=-1, keepdims=True)
            idx = jnp.min(jnp.where(work[n] == top, lane, float(N_EXPERTS)), axis=-1, keepdims=True)
            hit = lane == idx
            onehot[n] = jnp.where(hit, 1.0, onehot[n])
            work[n] = jnp.where(hit, -jnp.inf, work[n])
            ids[n].append(idx)
            tops[n].append(top)

    seen = seen_ref[...]
    for n in range(POST_PARTS):
        exps = [jnp.exp(tp - tops[n][0]) for tp in tops[n]]
        denom = exps[0] + exps[1] + exps[2] + exps[3]
        gates = [e / denom for e in exps]
        before = seen + _dot(ltri_ref[...], onehot[n].astype(_BF16))
        ranks = [jnp.sum(jnp.where(lane == idx, before, 0.0), axis=-1, keepdims=True) for idx in ids[n]]
        seen = seen + jnp.sum(onehot[n], axis=0, keepdims=True)
        route = jnp.zeros((part, LANES), _F32)
        for j, val in enumerate(ids[n] + gates + ranks):
            route = jnp.where(out_lane == j, val, route)
        route_ref[n * part:(n + 1) * part, :] = route
    seen_ref[...] = seen
    counts_ref[...] = seen


def _post(om, os_, x2, g_om, g_os, wo, g_moe, wr, br, ltri):
    t = x2.shape[0]
    rows = POST_ROWS

    def full(a):
        return pl.BlockSpec(a.shape, lambda i: (0,) * a.ndim)

    def tok(n):
        return pl.BlockSpec((rows, n), lambda i: (i, 0))

    return pl.pallas_call(
        _post_kernel,
        grid=(t // rows,),
        in_specs=[tok(om.shape[1]), tok(os_.shape[1]), tok(D_MODEL), full(g_om), full(g_os), full(wo), full(g_moe),
                  full(wr), full(br), full(ltri)],
        out_specs=[tok(D_MODEL), pl.BlockSpec((rows * ROW_TILE, LANES), lambda i: (i, 0)), tok(LANES),
                   pl.BlockSpec((1, N_EXPERTS), lambda i: (0, 0))],
        out_shape=[jax.ShapeDtypeStruct((t, D_MODEL), _F32), jax.ShapeDtypeStruct((t * ROW_TILE, LANES), _F32),
                   jax.ShapeDtypeStruct((t, LANES), _F32), jax.ShapeDtypeStruct((1, N_EXPERTS), _F32)],
        scratch_shapes=[pltpu.VMEM((1, N_EXPERTS), _F32)],
        compiler_params=pltpu.CompilerParams(dimension_semantics=("arbitrary",), vmem_limit_bytes=VMEM_LIMIT),
        name="post",
    )(om, os_, x2, g_om, g_os, wo, g_moe, wr, br, ltri)


def _padded_count(cnt_ref, e):
    shift = EXPERT_ROWS.bit_length() - 1
    return lax.shift_left(lax.shift_right_logical(cnt_ref[e] + (EXPERT_ROWS - 1), shift), shift)


def _layout_kernel(cnt_ref, route_ref, dest_ref, start_ref, be_ref, used_ref):
    shift = EXPERT_ROWS.bit_length() - 1
    rows = MOE_ROWS
    nb = be_ref.shape[0]

    def place(e, off):
        padded = _padded_count(cnt_ref, e)
        start_ref[e] = off

        def mark(j, c):
            be_ref[lax.shift_right_logical(off, shift) + j] = e
            return c
        lax.fori_loop(0, lax.shift_right_logical(padded, shift), mark, 0)
        return off + padded
    total = lax.fori_loop(0, N_EXPERTS, place, 0)
    used_ref[0] = lax.shift_right_logical(total, shift)

    def tail(b, c):
        be_ref[b] = N_EXPERTS - 1
        return c
    lax.fori_loop(lax.shift_right_logical(total, shift), nb, tail, 0)

    lane = lax.broadcasted_iota(jnp.int32, (1, LANES), 1)
    rank_lanes = jnp.logical_and(lane >= 2 * TOP_K, lane < 3 * TOP_K)
    pick = (lax.broadcasted_iota(jnp.int32, (2 * SUBLANES, LANES), 0)
            == lax.broadcasted_iota(jnp.int32, (2 * SUBLANES, LANES), 1)).astype(_BF16)

    def tile(i, c):
        r = route_ref[pl.ds(pl.multiple_of(i * rows, rows), rows), :]
        high = jnp.floor(r * (1.0 / 256.0))
        low_digits = jnp.where(rank_lanes, r - 256.0 * high, r)
        high_digits = jnp.where(rank_lanes, high, 0.0)
        low_t = _dot_nt(pick, low_digits.astype(_BF16))
        high_t = _dot_nt(pick, high_digits.astype(_BF16))
        ids = low_t[:TOP_K, :].astype(jnp.int32)
        dest = (256.0 * high_t[2 * TOP_K:3 * TOP_K, :] + low_t[2 * TOP_K:3 * TOP_K, :]).astype(jnp.int32)
        for e in range(N_EXPERTS):
            dest = dest + jnp.where(ids == e, start_ref[e], 0)
        lines = [dest[k:k + 1, half * LANES:(half + 1) * LANES] for k in range(TOP_K) for half in range(rows // LANES)]
        dest_ref[pl.ds(pl.multiple_of(i * SUBLANES, SUBLANES), SUBLANES), :] = jnp.concatenate(lines, axis=0)
        return c
    lax.fori_loop(0, route_ref.shape[0] // rows, tile, 0)


def _layout(counts, route, n_blocks):
    t = route.shape[0]
    pair_shape = (t * TOP_K // LANES, LANES)
    grid_spec = pltpu.PrefetchScalarGridSpec(
        num_scalar_prefetch=1,
        grid=(1,),
        in_specs=[pl.BlockSpec(route.shape, lambda i, c: (0, 0))],
        out_specs=[pl.BlockSpec(pair_shape, lambda i, c: (0, 0)), pl.BlockSpec(memory_space=pltpu.SMEM),
                   pl.BlockSpec(memory_space=pltpu.SMEM), pl.BlockSpec(memory_space=pltpu.SMEM)],
    )
    return pl.pallas_call(
        _layout_kernel,
        grid_spec=grid_spec,
        out_shape=[jax.ShapeDtypeStruct(pair_shape, jnp.int32), jax.ShapeDtypeStruct((N_EXPERTS,), jnp.int32),
                   jax.ShapeDtypeStruct((n_blocks,), jnp.int32), jax.ShapeDtypeStruct((1,), jnp.int32)],
        compiler_params=pltpu.CompilerParams(dimension_semantics=("arbitrary",), vmem_limit_bytes=VMEM_LIMIT),
        name="layout",
    )(counts, route)


def _tile_rows(row):
    return pl.ds(pl.multiple_of(row * ROW_TILE, ROW_TILE), ROW_TILE)


def _for_each_pair(idx, s, body):
    halves = MOE_ROWS // LANES

    def line(g, c):
        choice = g // halves
        first = lax.rem(g, halves) * LANES
        for t0 in range(0, LANES, DMA_GROUP):
            vals = [idx[s, g, t0 + j] for j in range(DMA_GROUP)]
            for j, row in enumerate(vals):
                body(first + t0 + j, choice, row, j % DMA_THREADS)
        return c
    lax.fori_loop(0, SUBLANES, line, 0)


def _dispatch_kernel(cnt_ref, start_ref, dest_hbm, u_hbm, xs_hbm, idx, stage, zblk, sem_d, sem_l, sem_i, sem_z):
    rows = MOE_ROWS
    shift = EXPERT_ROWS.bit_length() - 1
    i = pl.program_id(0)
    nt = pl.num_programs(0)
    nb = xs_hbm.shape[0] // (EXPERT_ROWS * ROW_TILE)
    slot = lax.rem(i, 2)
    stage_slot = lax.rem(i, STAGE_SLOTS)

    def idx_copy(tile, s):
        return pltpu.make_async_copy(dest_hbm.at[pl.ds(pl.multiple_of(tile * SUBLANES, SUBLANES), SUBLANES)],
                                     idx.at[s], sem_i.at[s])

    def load(tile):
        s = lax.rem(tile, STAGE_SLOTS)
        src = pl.ds(pl.multiple_of(tile * (rows * ROW_TILE), rows * ROW_TILE), rows * ROW_TILE)
        return pltpu.make_async_copy(u_hbm.at[src], stage.at[s], sem_l.at[s])

    def wait_rows(s):
        chunk = pl.ds(0, rows * ROW_TILE)
        for _ in range(TOP_K):
            pltpu.make_async_copy(stage.at[s], xs_hbm.at[chunk], sem_d.at[s]).wait()

    def zero_block(b):
        dst = pl.ds(pl.multiple_of(b * (EXPERT_ROWS * ROW_TILE), EXPERT_ROWS * ROW_TILE), EXPERT_ROWS * ROW_TILE)
        return pltpu.make_async_copy(zblk, xs_hbm.at[dst], sem_z)

    def pad_rows(e, fn):
        pad = _padded_count(cnt_ref, e) - cnt_ref[e]
        first = start_ref[e] + cnt_ref[e]
        for bit in reversed(range(shift)):
            size = 1 << bit
            before = lax.shift_left(lax.shift_right_logical(pad, bit + 1), bit + 1)

            @pl.when(lax.shift_right_logical(pad, bit) & 1 == 1)
            def _(size=size, before=before):
                dst = pl.ds(pl.multiple_of((first + before) * ROW_TILE, ROW_TILE), size * ROW_TILE)
                fn(pltpu.make_async_copy(zblk.at[pl.ds(0, size * ROW_TILE)], xs_hbm.at[dst], sem_z))

    def tail_blocks(fn):
        used = start_ref[N_EXPERTS - 1] + _padded_count(cnt_ref, N_EXPERTS - 1)

        def one(b, c):
            fn(zero_block(b))
            return c
        lax.fori_loop(lax.shift_right_logical(used, shift), nb, one, 0)

    @pl.when(i == 0)
    def _():
        idx_copy(0, 0).start()
        load(0).start()

        @pl.when(nt > 1)
        def _():
            load(1).start()
        zblk[...] = jnp.zeros_like(zblk)
        for fn in (lambda c: c.start(), lambda c: c.wait()):
            def per_expert(e, c, fn=fn):
                pad_rows(e, fn)
                return c
            lax.fori_loop(0, N_EXPERTS, per_expert, 0)
            tail_blocks(fn)

    idx_copy(i, slot).wait()

    @pl.when(i + 1 < nt)
    def _():
        idx_copy(i + 1, 1 - slot).start()

    @pl.when(i >= 2)
    def _():
        wait_rows(lax.rem(i + 2, STAGE_SLOTS))

    @pl.when(i + 2 < nt)
    def _():
        load(i + 2).start()

    load(i).wait()

    def copy_row(t, k, row, thread):
        del k
        pltpu.make_async_copy(stage.at[stage_slot, _tile_rows(t)], xs_hbm.at[_tile_rows(row)],
                              sem_d.at[stage_slot]).start(priority=thread)
    _for_each_pair(idx, slot, copy_row)

    @pl.when(i == nt - 1)
    def _():
        @pl.when(nt > 1)
        def _():
            wait_rows(lax.rem(i + STAGE_SLOTS - 1, STAGE_SLOTS))
        wait_rows(stage_slot)


def _dispatch(counts, start, dest, u2_tiles, n_blocks):
    nt = dest.shape[0] // SUBLANES
    grid_spec = pltpu.PrefetchScalarGridSpec(
        num_scalar_prefetch=2,
        grid=(nt,),
        in_specs=[pl.BlockSpec(memory_space=pl.ANY), pl.BlockSpec(memory_space=pl.ANY)],
        out_specs=pl.BlockSpec(memory_space=pl.ANY),
        scratch_shapes=[pltpu.SMEM((2, SUBLANES, LANES), jnp.int32),
                        pltpu.VMEM((STAGE_SLOTS, MOE_ROWS * ROW_TILE, LANES), _F32),
                        pltpu.VMEM((EXPERT_ROWS * ROW_TILE, LANES), _F32),
                        pltpu.SemaphoreType.DMA((STAGE_SLOTS,)), pltpu.SemaphoreType.DMA((STAGE_SLOTS,)),
                        pltpu.SemaphoreType.DMA((2,)), pltpu.SemaphoreType.DMA(())],
    )
    return pl.pallas_call(
        _dispatch_kernel,
        grid_spec=grid_spec,
        out_shape=jax.ShapeDtypeStruct((n_blocks * EXPERT_ROWS * ROW_TILE, LANES), _F32),
        compiler_params=pltpu.CompilerParams(dimension_semantics=("arbitrary",), vmem_limit_bytes=VMEM_LIMIT),
        name="dispatch",
    )(counts, start, dest, u2_tiles)


def _expert_kernel(be_ref, used_ref, xs_ref, wgu_hbm, bgu_ref, wdn_hbm, bdn_ref, ys_ref,
                   wgu_f32, wdn_f32, wgu_bf, wdn_bf, sem_w):
    rows = EXPERT_ROWS
    i = pl.program_id(0)
    nb = pl.num_programs(0)
    used = used_ref[0]
    e = be_ref[i]
    prev = be_ref[jnp.maximum(i - 1, 0)]
    in_use = i < used

    def fetch(expert):
        return (pltpu.make_async_copy(wgu_hbm.at[expert], wgu_f32, sem_w.at[0]),
                pltpu.make_async_copy(wdn_hbm.at[expert], wdn_f32, sem_w.at[1]))

    @pl.when(i == 0)
    def _():
        for c in fetch(e):
            c.start()

    @pl.when(jnp.logical_and(in_use, jnp.logical_or(i == 0, e != prev)))
    def _():
        for c in fetch(e):
            c.wait()
        chunk = 128

        def cast(c, carry):
            rs = pl.ds(pl.multiple_of(c * chunk, chunk), chunk)
            wgu_bf[rs, :] = wgu_f32[rs, :].astype(_BF16)
            wdn_bf[rs, :] = wdn_f32[rs, :].astype(_BF16)
            return carry
        lax.fori_loop(0, D_MODEL // chunk, cast, 0)

        def same_expert(j):
            return jnp.logical_and(j < used, be_ref[jnp.minimum(j, nb - 1)] == e)
        nxt = lax.while_loop(same_expert, lambda j: j + 1, i)

        @pl.when(nxt < used)
        def _():
            for c in fetch(be_ref[jnp.minimum(nxt, nb - 1)]):
                c.start()

    @pl.when(in_use)
    def _():
        x = _load_row_tiles(xs_ref, 0, rows).astype(_BF16)
        hh = _dot(x, wgu_bf[...]) + bgu_ref[0]
        glu = jnp.minimum(hh[:, :D_FF], SWIGLU_LIMIT)
        lin = jnp.clip(hh[:, D_FF:], -SWIGLU_LIMIT, SWIGLU_LIMIT)
        act = glu * jax.nn.sigmoid(SWIGLU_ALPHA * glu) * (lin + 1.0)
        _store_row_tiles(ys_ref, _dot(act.astype(_BF16), wdn_bf[...]) + bdn_ref[0])

    @pl.when(jnp.logical_not(in_use))
    def _():
        ys_ref[...] = jnp.zeros_like(ys_ref)


def _experts(block_e, used, xs, w_gu, b_gu, w_dn, b_dn):
    nb = block_e.shape[0]
    rows = EXPERT_ROWS
    grid_spec = pltpu.PrefetchScalarGridSpec(
        num_scalar_prefetch=2,
        grid=(nb,),
        in_specs=[pl.BlockSpec((rows * ROW_TILE, LANES), lambda i, be, u: (i, 0)),
                  pl.BlockSpec(memory_space=pl.ANY),
                  pl.BlockSpec((1, 1, 2 * D_FF), lambda i, be, u: (be[i], 0, 0)),
                  pl.BlockSpec(memory_space=pl.ANY),
                  pl.BlockSpec((1, 1, D_MODEL), lambda i, be, u: (be[i], 0, 0))],
        out_specs=pl.BlockSpec((rows * ROW_TILE, LANES), lambda i, be, u: (i, 0)),
        scratch_shapes=[pltpu.VMEM((D_MODEL, 2 * D_FF), _F32), pltpu.VMEM((D_FF, D_MODEL), _F32),
                        pltpu.VMEM((D_MODEL, 2 * D_FF), _BF16), pltpu.VMEM((D_FF, D_MODEL), _BF16),
                        pltpu.SemaphoreType.DMA((2,))],
    )
    return pl.pallas_call(
        _expert_kernel,
        grid_spec=grid_spec,
        out_shape=jax.ShapeDtypeStruct(xs.shape, _F32),
        compiler_params=pltpu.CompilerParams(dimension_semantics=("arbitrary",), vmem_limit_bytes=VMEM_LIMIT),
        name="experts",
    )(block_e, used, xs, w_gu, b_gu.reshape(N_EXPERTS, 1, 2 * D_FF), w_dn, b_dn.reshape(N_EXPERTS, 1, D_MODEL))


def _final_kernel(dest_hbm, ys_hbm, h1_ref, route_ref, p_ref, g_ple_ref, wpg_ref, wpp_ref, g_fin_ref,
                  out_ref, ybuf, idx, sem_y, sem_i, *, last_layer):
    rows = MOE_ROWS
    i = pl.program_id(0)
    nt = pl.num_programs(0)
    slot = lax.rem(i, 2)

    def idx_copy(tile, s):
        return pltpu.make_async_copy(dest_hbm.at[pl.ds(pl.multiple_of(tile * SUBLANES, SUBLANES), SUBLANES)],
                                     idx.at[s], sem_i.at[s])

    def gather(s):
        def copy_row(t, k, row, thread):
            pltpu.make_async_copy(ys_hbm.at[_tile_rows(row)], ybuf.at[s, _tile_rows(k * rows + t)],
                                  sem_y.at[s]).start(priority=thread)
        _for_each_pair(idx, s, copy_row)

    @pl.when(i == 0)
    def _():
        idx_copy(0, 0).start()
        idx_copy(0, 0).wait()
        gather(0)

        @pl.when(nt > 1)
        def _():
            idx_copy(1, 1).start()

    @pl.when(i + 1 < nt)
    def _():
        idx_copy(i + 1, 1 - slot).wait()
        gather(1 - slot)

        @pl.when(i + 2 < nt)
        def _():
            idx_copy(i + 2, slot).start()

    pltpu.make_async_copy(ys_hbm.at[pl.ds(0, TOP_K * rows * ROW_TILE)], ybuf.at[slot], sem_y.at[slot]).wait()
    part = rows // ROW_PARTS
    for r0 in range(0, rows, part):
        rs = slice(r0, r0 + part)
        route = route_ref[rs, :]
        y = jnp.zeros((part, D_MODEL), _F32)
        for k in range(TOP_K):
            y = y + _load_row_tiles(ybuf.at[slot], k * rows + r0, part) * route[:, TOP_K + k:TOP_K + k + 1]
        h2 = h1_ref[rs, :] + y
        u3 = _rms(h2, g_ple_ref[...]).astype(_BF16)
        gate = jax.nn.sigmoid(_dot(u3, wpg_ref[...]))
        h3 = h2 + gate * _dot(p_ref[rs, :].astype(_BF16), wpp_ref[...])
        out_ref[rs, :] = _rms(h3, g_fin_ref[...]) if last_layer else h3


def _final(dest, ys, h1, route, p2, g_ple, wpg, wpp, g_fin, last_layer):
    t = h1.shape[0]
    rows = MOE_ROWS

    def full(a):
        return pl.BlockSpec(a.shape, lambda i: (0,) * a.ndim)

    def tok(n):
        return pl.BlockSpec((rows, n), lambda i: (i, 0))

    return pl.pallas_call(
        functools.partial(_final_kernel, last_layer=last_layer),
        grid=(t // rows,),
        in_specs=[pl.BlockSpec(memory_space=pl.ANY), pl.BlockSpec(memory_space=pl.ANY), tok(D_MODEL), tok(LANES),
                  tok(PLE_DIM), full(g_ple), full(wpg), full(wpp), full(g_fin)],
        out_specs=tok(D_MODEL),
        out_shape=jax.ShapeDtypeStruct((t, D_MODEL), _F32),
        scratch_shapes=[pltpu.VMEM((2, TOP_K * rows * ROW_TILE, LANES), _F32),
                        pltpu.SMEM((2, SUBLANES, LANES), jnp.int32),
                        pltpu.SemaphoreType.DMA((2,)), pltpu.SemaphoreType.DMA((2,))],
        compiler_params=pltpu.CompilerParams(dimension_semantics=("arbitrary",), vmem_limit_bytes=VMEM_LIMIT),
        name="final",
    )(dest, ys, h1, route, p2, g_ple, wpg, wpp, g_fin)


def _rope_pad(w, rot):
    half = MLA_ROPE_DIM // 2
    body = jnp.concatenate([-w[:, half:], w[:, :half]], axis=1) if rot else w
    z = jnp.zeros((w.shape[0], MLA_NOPE_DIM), w.dtype)
    return jnp.concatenate([z, body, jnp.zeros((w.shape[0], LANES - MLA_NOPE_DIM - MLA_ROPE_DIM), w.dtype)], axis=1)


def _layer_weights(w_in, w_uq, w_ukv):
    o = Q_LORA_RANK + KV_LORA_RANK
    w_kr = w_in[:, o:o + MLA_ROPE_DIM]
    w1 = jnp.concatenate([w_in[:, :o], _rope_pad(w_kr, False), _rope_pad(w_kr, True), w_in[:, o + MLA_ROPE_DIM:]],
                         axis=1).astype(_BF16)
    uq = w_uq.reshape(Q_LORA_RANK, MLA_HEADS, MLA_NOPE_DIM + MLA_ROPE_DIM)
    zq = jnp.zeros((Q_LORA_RANK, MLA_HEADS, LANES - MLA_NOPE_DIM - MLA_ROPE_DIM), w_uq.dtype)
    wq = jnp.concatenate([uq, zq], axis=2).reshape(Q_LORA_RANK, MLA_HEADS * LANES).astype(_BF16)
    rope = uq[:, :, MLA_NOPE_DIM:]
    half = MLA_ROPE_DIM // 2
    rot = jnp.concatenate([jnp.zeros_like(uq[:, :, :MLA_NOPE_DIM]), -rope[:, :, half:], rope[:, :, :half], zq], axis=2)
    wqr = rot.reshape(Q_LORA_RANK, MLA_HEADS * LANES).astype(_BF16)
    ukv = w_ukv.reshape(KV_LORA_RANK, MLA_HEADS, MLA_NOPE_DIM + MLA_V_DIM)
    zk = jnp.zeros((KV_LORA_RANK, MLA_HEADS, LANES - MLA_NOPE_DIM), w_ukv.dtype)
    wk = jnp.concatenate([ukv[:, :, :MLA_NOPE_DIM], zk], axis=2).reshape(KV_LORA_RANK, MLA_HEADS * LANES).astype(_BF16)
    v = ukv[:, :, MLA_NOPE_DIM:].reshape(KV_LORA_RANK, HEAD_PAIRS, 2, MLA_V_DIM)
    zv = jnp.zeros((KV_LORA_RANK, HEAD_PAIRS, MLA_V_DIM), w_ukv.dtype)
    wv = jnp.stack([jnp.concatenate([v[:, :, 0], zv], axis=2), jnp.concatenate([zv, v[:, :, 1]], axis=2)], axis=2)
    wv = wv.reshape(KV_LORA_RANK, MLA_HEADS * LANES).astype(_BF16)
    return w1, wq, wqr, wk, wv


def kernel(x, p, positions, w_in, g_attn, g_cq, w_uq, g_ckv, w_ukv, g_out_mla, g_out_sb, w_o, g_moe, w_router,
           b_router, w_gu, b_gu, w_dn, b_dn, g_ple, w_ple_gate, w_ple_proj, g_final):
    b, s, d = x.shape
    t = b * s
    depth = w_in.shape[0]
    assert d == D_MODEL and ROW_TILE == SUBLANES and s % MLA_TILE == 0 and t % PROJ_ROWS == 0 and t % POST_ROWS == 0
    assert t % MOE_ROWS == 0 and LANES % DMA_GROUP == 0 and MOE_ROWS % LANES == 0 and (t * TOP_K) % EXPERT_ROWS == 0
    assert EXPERT_ROWS & (EXPERT_ROWS - 1) == 0

    freq = ROPE_THETA ** (-jnp.arange(0, MLA_ROPE_DIM, 2, dtype=_F32) / MLA_ROPE_DIM)
    invf = jnp.concatenate([jnp.zeros((MLA_NOPE_DIM,), _F32), freq, freq,
                            jnp.zeros((LANES - MLA_NOPE_DIM - MLA_ROPE_DIM,), _F32)]).reshape(1, LANES)
    pos = positions.reshape(t, 1)
    idx = jnp.arange(SB_TILE)
    tri = -(idx[:, None] >= idx[None, :]).astype(_BF16)
    tri = jnp.concatenate([tri, tri], axis=0)
    idx = jnp.arange(POST_ROWS // POST_PARTS)
    ltri = (idx[None, :] < idx[:, None]).astype(_BF16)
    n_blocks = t * TOP_K // EXPERT_ROWS + N_EXPERTS

    h = x.reshape(t, d)
    for i in range(depth):
        w1, wq, wqr, wk, wv = _layer_weights(w_in[i], w_uq[i], w_ukv[i])
        qm, km, vm, qs, ks, vs = _proj(pos, h, g_attn[i].reshape(1, d), w1, g_cq[i].reshape(1, -1), wq, wqr,
                                       g_ckv[i].reshape(1, -1), wk, wv, invf)

        def seq(a):
            return a.reshape(b, s, a.shape[1])

        om = _mla(seq(qm), seq(km), seq(vm)).reshape(t, -1)
        os_ = _sb(seq(qs), seq(ks), seq(vs), tri).reshape(t, -1)
        h1, u2_tiles, route, counts = _post(om, os_, h, g_out_mla[i].reshape(1, -1), g_out_sb[i].reshape(1, -1),
                                            w_o[i].astype(_BF16), g_moe[i].reshape(1, d), w_router[i],
                                            b_router[i].reshape(1, -1), ltri)
        counts = counts.reshape(-1).astype(jnp.int32)
        dest, start, block_e, used = _layout(counts, route, n_blocks)
        xs = _dispatch(counts, start, dest, u2_tiles, n_blocks)
        ys = _experts(block_e, used, xs, w_gu[i], b_gu[i], w_dn[i], b_dn[i])
        h = _final(dest, ys, h1, route, p[i].reshape(t, -1), g_ple[i].reshape(1, d), w_ple_gate[i].astype(_BF16),
                   w_ple_proj[i].astype(_BF16), g_final.reshape(1, d), i == depth - 1)
    return h.reshape(b, s, d)
```

```python
import functools
import math

import jax
import jax.numpy as jnp
from jax import lax
from jax.experimental import pallas as pl
from jax.experimental.pallas import tpu as pltpu

D_MODEL = 1024
PLE_DIM = 256
MLA_HEADS = 8
MLA_NOPE_DIM = 64
MLA_ROPE_DIM = 32
MLA_V_DIM = 64
Q_LORA_RANK = 384
KV_LORA_RANK = 256
SB_HEADS = 8
SB_HEAD_DIM = 64
SB_WIDTH = SB_HEADS * SB_HEAD_DIM
ROPE_THETA = 10000.0
N_EXPERTS = 32
TOP_K = 4
D_FF = 1024
SWIGLU_LIMIT = 7.0
SWIGLU_ALPHA = 1.702
RMS_EPS = 1e-6

LANES = 128
SUBLANES = 8
ROW_TILE = D_MODEL // LANES
HEAD_PAIRS = MLA_HEADS // 2
PROJ_ROWS = 512
POST_ROWS = 512
GROUP_HEADS = 4
SB_GROUP_HEADS = 4
MLA_TILE = 512
SB_TILE = 256
SB_FIRST_TILES = 2
EXPERT_ROWS = 256
ROW_PARTS = 2
POST_PARTS = 8
MOE_ROWS = SUBLANES * LANES // TOP_K
DMA_GROUP = 32
STAGE_SLOTS = 4
DMA_THREADS = 2
VMEM_LIMIT = 56 * 1024 * 1024
SB_LOG_FLOOR = -105.0

_F32 = jnp.float32
_BF16 = jnp.bfloat16


def _rms(x, g):
    return x * lax.rsqrt(jnp.mean(x * x, axis=-1, keepdims=True) + RMS_EPS) * g


def _dot(a, b):
    return jnp.dot(a, b, preferred_element_type=_F32)


def _dot_nt(a, b):
    return lax.dot_general(a, b, (((1,), (1,)), ((), ())), preferred_element_type=_F32)


def _store_row_tiles(ref, value, first_row=0):
    rows = value.shape[0]
    for g in range(ROW_TILE):
        ref[pl.ds(first_row * ROW_TILE + g, rows, stride=ROW_TILE), :] = value[:, g * LANES:(g + 1) * LANES]


def _load_row_tiles(ref, first_row, rows):
    return jnp.concatenate([ref[pl.ds(first_row * ROW_TILE + g, rows, stride=ROW_TILE), :] for g in range(ROW_TILE)],
                           axis=-1)


def _proj_kernel(pos_ref, x_ref, g_attn_ref, w1_ref, g_cq_ref, wq_ref, wqr_ref, g_ckv_ref, wk_ref, wv_ref,
                 invf_ref, qm_ref, km_ref, vm_ref, qs_ref, ks_ref, vs_ref):
    u = _rms(x_ref[...], g_attn_ref[...]).astype(_BF16)
    y = _dot(u, w1_ref[...])
    c_q = y[:, :Q_LORA_RANK]
    c_kv = y[:, Q_LORA_RANK:Q_LORA_RANK + KV_LORA_RANK]
    o = Q_LORA_RANK + KV_LORA_RANK
    k_r = y[:, o:o + LANES]
    k_r_rot = y[:, o + LANES:o + 2 * LANES]
    o += 2 * LANES
    q_s = y[:, o:o + SB_WIDTH]
    k_s = y[:, o + SB_WIDTH:o + 2 * SB_WIDTH]
    v_s = y[:, o + 2 * SB_WIDTH:o + 3 * SB_WIDTH]

    ang = pos_ref[...].astype(_F32) * invf_ref[...]
    cos = jnp.cos(ang)
    sin = jnp.sin(ang)

    cq_n = _rms(c_q, g_cq_ref[...]).astype(_BF16)
    q = _dot(cq_n, wq_ref[...])
    q_rot = _dot(cq_n, wqr_ref[...])
    ckv_n = _rms(c_kv, g_ckv_ref[...]).astype(_BF16)
    k = _dot(ckv_n, wk_ref[...])
    wide_lane = lax.broadcasted_iota(jnp.int32, (1, MLA_HEADS * LANES), 1)
    ones_lane = (wide_lane % (2 * LANES) == MLA_V_DIM) | (wide_lane % (2 * LANES) == LANES)
    vm_ref[...] = jnp.where(ones_lane, 1.0, _dot(ckv_n, wv_ref[...])).astype(_BF16)
    k_rope = k_r * cos + k_r_rot * sin

    q_scale = (MLA_NOPE_DIM + MLA_ROPE_DIM) ** -0.5 * math.log2(math.e)
    lane = lax.broadcasted_iota(jnp.int32, (1, LANES), 1)
    low = lane < SB_HEAD_DIM
    for h in range(MLA_HEADS):
        sl = slice(h * LANES, (h + 1) * LANES)
        qm_ref[:, sl] = ((q[:, sl] * cos + q_rot[:, sl] * sin) * q_scale).astype(_BF16)
        km_ref[:, sl] = (k[:, sl] + k_rope).astype(_BF16)
    qs_ref[...] = (q_s * (SB_HEAD_DIM ** -0.5)).astype(_BF16)
    for hp in range(HEAD_PAIRS):
        sl = slice(hp * LANES, (hp + 1) * LANES)
        for half, keep in ((0, low), (1, jnp.logical_not(low))):
            dst = slice((2 * hp + half) * LANES, (2 * hp + half + 1) * LANES)
            ks_ref[:, dst] = jnp.where(keep, k_s[:, sl], 0.0).astype(_BF16)
            vs_ref[:, dst] = jnp.where(keep, v_s[:, sl], 0.0).astype(_BF16)


def _proj(pos, x2, g_attn, w1, g_cq, wq, wqr, g_ckv, wk, wv, invf):
    t = x2.shape[0]
    rows = PROJ_ROWS
    wide = MLA_HEADS * LANES

    def full(a):
        return pl.BlockSpec(a.shape, lambda i: (0,) * a.ndim)

    def tok(n):
        return pl.BlockSpec((rows, n), lambda i: (i, 0))

    outs = [jax.ShapeDtypeStruct((t, n), _BF16) for n in (wide, wide, wide, SB_WIDTH, wide, wide)]
    return pl.pallas_call(
        _proj_kernel,
        grid=(t // rows,),
        in_specs=[tok(1), tok(D_MODEL), full(g_attn), full(w1), full(g_cq), full(wq), full(wqr), full(g_ckv),
                  full(wk), full(wv), full(invf)],
        out_specs=[tok(wide), tok(wide), tok(wide), tok(SB_WIDTH), tok(wide), tok(wide)],
        out_shape=outs,
        compiler_params=pltpu.CompilerParams(dimension_semantics=("parallel",), vmem_limit_bytes=VMEM_LIMIT),
        name="proj",
    )(pos, x2, g_attn, w1, g_cq, wq, wqr, g_ckv, wk, wv, invf)


def _mla_kernel(q_ref, k_ref, v_ref, o_ref, s_a, s_b, m_ref, acc_ref):
    tile, half = MLA_TILE, MLA_TILE // 2
    qi = pl.program_id(2)
    heads = [slice(h * LANES, (h + 1) * LANES) for h in range(GROUP_HEADS)]
    everything = slice(0, tile)

    def key_rows(kt, n=tile):
        return pl.ds(pl.multiple_of(kt * tile, tile), n)

    def score(dst, kt):
        for h, sl in enumerate(heads):
            dst[h] = _dot_nt(q_ref[0, :, sl], k_ref[0, key_rows(kt), sl])

    def score_diagonal(dst):
        upper_mask = (lax.broadcasted_iota(jnp.int32, (half, half), 1)
                      <= lax.broadcasted_iota(jnp.int32, (half, half), 0))
        lower_mask = (lax.broadcasted_iota(jnp.int32, (half, tile), 1)
                      <= lax.broadcasted_iota(jnp.int32, (half, tile), 0) + half)
        for h, sl in enumerate(heads):
            upper = _dot_nt(q_ref[0, :half, sl], k_ref[0, key_rows(qi, half), sl])
            dst[h, :half, :half] = jnp.where(upper_mask, upper, -jnp.inf)
            lower = _dot_nt(q_ref[0, half:, sl], k_ref[0, key_rows(qi), sl])
            dst[h, half:, :] = jnp.where(lower_mask, lower, -jnp.inf)

    def absorb_block(src, h, sl, part, cols, keys):
        s = src[h, part * half:(part + 1) * half, cols]
        m = m_ref[h, part]
        row_max = jnp.max(s, axis=-1, keepdims=True)
        m_new = jnp.maximum(m, row_max)
        alpha = jnp.exp2(m - m_new)
        p = jnp.exp2(s - jnp.maximum(m[:, :1], row_max))
        m_ref[h, part] = m_new
        acc_ref[h, part] = alpha * acc_ref[h, part] + _dot(p.astype(_BF16), v_ref[0, keys, sl])

    def absorb(src, kt):
        for h, sl in enumerate(heads):
            for part in range(2):
                absorb_block(src, h, sl, part, everything, key_rows(kt))

    def absorb_diagonal(src):
        for h, sl in enumerate(heads):
            absorb_block(src, h, sl, 0, slice(0, half), key_rows(qi, half))
            absorb_block(src, h, sl, 1, everything, key_rows(qi))

    m_ref[...] = jnp.full(m_ref.shape, -jnp.inf, _F32)
    acc_ref[...] = jnp.zeros(acc_ref.shape, _F32)

    @pl.when(qi == 0)
    def _():
        score_diagonal(s_a)
        absorb_diagonal(s_a)

    @pl.when(qi > 0)
    def _():
        score(s_a, 0)

        def pair(i, c):
            score(s_b, 2 * i + 1)
            absorb(s_a, 2 * i)
            score(s_a, 2 * i + 2)
            absorb(s_b, 2 * i + 1)
            return c
        lax.fori_loop(0, (qi - 1) // 2, pair, 0)

        @pl.when(qi % 2 == 1)
        def _():
            score_diagonal(s_b)
            absorb(s_a, qi - 1)
            absorb_diagonal(s_b)

        @pl.when(qi % 2 == 0)
        def _():
            score(s_b, qi - 1)
            absorb(s_a, qi - 2)
            score_diagonal(s_a)
            absorb(s_b, qi - 1)
            absorb_diagonal(s_a)

    low = lax.broadcasted_iota(jnp.int32, (half, LANES), 1) < MLA_V_DIM
    for hp in range(GROUP_HEADS // 2):
        for part in range(2):
            even, odd = acc_ref[2 * hp, part], acc_ref[2 * hp + 1, part]
            o_ref[0, part * half:(part + 1) * half, hp * LANES:(hp + 1) * LANES] = jnp.where(
                low, even / even[:, MLA_V_DIM:MLA_V_DIM + 1], odd / odd[:, :1])


def _mla(qm, km, vm):
    b, s, _ = qm.shape
    tq = MLA_TILE
    gw = GROUP_HEADS * LANES
    return pl.pallas_call(
        _mla_kernel,
        grid=(b, MLA_HEADS // GROUP_HEADS, s // tq),
        in_specs=[pl.BlockSpec((1, tq, gw), lambda bi, g, qi: (bi, qi, g)),
                  pl.BlockSpec((1, s, gw), lambda bi, g, qi: (bi, 0, g)),
                  pl.BlockSpec((1, s, gw), lambda bi, g, qi: (bi, 0, g))],
        out_specs=pl.BlockSpec((1, tq, gw // 2), lambda bi, g, qi: (bi, qi, g)),
        out_shape=jax.ShapeDtypeStruct((b, s, HEAD_PAIRS * LANES), _F32),
        scratch_shapes=[pltpu.VMEM((GROUP_HEADS, tq, tq), _F32), pltpu.VMEM((GROUP_HEADS, tq, tq), _F32),
                        pltpu.VMEM((GROUP_HEADS, 2, tq // 2, LANES), _F32),
                        pltpu.VMEM((GROUP_HEADS, 2, tq // 2, LANES), _F32)],
        compiler_params=pltpu.CompilerParams(dimension_semantics=("parallel", "parallel", "arbitrary"),
                                             vmem_limit_bytes=VMEM_LIMIT),
        name="mla",
    )(qm, km, vm)


def _sb_kernel(q_ref, k_ref, v_ref, tri_ref, o_ref, rem_ref, acc_ref):
    tile = SB_TILE
    qi = pl.program_id(2)
    row = lax.broadcasted_iota(jnp.int32, (tile, tile), 0)
    col = lax.broadcasted_iota(jnp.int32, (tile, tile), 1)
    strict = col < row
    heads = [slice(h * LANES, (h + 1) * LANES) for h in range(SB_GROUP_HEADS)]

    def key_rows(kt):
        return pl.ds(pl.multiple_of(kt * tile, tile), tile)

    def tile_terms(kt, diagonal):
        tri = tri_ref[...]
        out = []
        for h, sl in enumerate(heads):
            q = q_ref[0, :, (h // 2) * LANES:(h // 2 + 1) * LANES]
            z = _dot_nt(q, k_ref[0, key_rows(kt), sl])
            sp = jnp.maximum(z, 0.0) + jnp.log(1.0 + jnp.exp(-jnp.abs(z)))
            if diagonal:
                sp = jnp.where(strict, sp, 0.0)
            hi = sp.astype(_BF16)
            lo = (sp - hi.astype(_F32)).astype(_BF16)
            out.append((z, _dot(jnp.concatenate([hi, lo], axis=-1), tri)))
        return out

    def absorb(kt, terms, rems, accs, diagonal, live=None):
        new_rems, new_accs = [], []
        for sl, (z, incl), rem, acc in zip(heads, terms, rems, accs):
            p = jnp.exp(z + (rem + incl))
            if diagonal:
                p = jnp.where(strict, p, 0.0)
            step = incl[:, :1]
            if live is not None:
                p = jnp.where(live, p, 0.0)
                step = jnp.where(live, step, 0.0)
            new_accs.append(acc + _dot(p.astype(_BF16), v_ref[0, key_rows(kt), sl]))
            new_rems.append(rem + step)
        return tuple(new_rems), tuple(new_accs)

    def rem_max(rems):
        return functools.reduce(jnp.maximum, [jnp.max(r) for r in rems])

    terms = [tile_terms(qi, True)] + [tile_terms(jnp.maximum(qi - d, 0), False) for d in range(1, SB_FIRST_TILES)]
    rems = tuple(jnp.zeros((tile, 1), _F32) for _ in range(SB_GROUP_HEADS))
    accs = tuple(jnp.zeros((tile, LANES), _F32) for _ in range(SB_GROUP_HEADS))
    rems, accs = absorb(qi, terms[0], rems, accs, True)
    for d in range(1, SB_FIRST_TILES):
        rems, accs = absorb(jnp.maximum(qi - d, 0), terms[d], rems, accs, False, live=qi >= d)

    def save(rems, accs):
        for h in range(SB_GROUP_HEADS):
            rem_ref[h] = rems[h]
            acc_ref[h] = accs[h]
    save(rems, accs)

    def cond(state):
        kt, worst = state
        return jnp.logical_and(kt >= 0, worst > SB_LOG_FLOOR)

    def body(state):
        kt, _ = state
        rems = tuple(rem_ref[h] for h in range(SB_GROUP_HEADS))
        accs = tuple(acc_ref[h] for h in range(SB_GROUP_HEADS))
        rems, accs = absorb(kt, tile_terms(kt, False), rems, accs, False)
        save(rems, accs)
        return kt - 1, rem_max(rems)

    lax.while_loop(cond, body, (qi - SB_FIRST_TILES, rem_max(rems)))
    for hp in range(SB_GROUP_HEADS // 2):
        o_ref[0, :, hp * LANES:(hp + 1) * LANES] = acc_ref[2 * hp] + acc_ref[2 * hp + 1]


def _sb(qs, ks, vs, tri):
    b, s, _ = qs.shape
    tile = SB_TILE
    gw = SB_GROUP_HEADS * LANES
    return pl.pallas_call(
        _sb_kernel,
        grid=(b, SB_HEADS // SB_GROUP_HEADS, s // tile),
        in_specs=[pl.BlockSpec((1, tile, gw // 2), lambda bi, g, qi: (bi, qi, g)),
                  pl.BlockSpec((1, s, gw), lambda bi, g, qi: (bi, 0, g)),
                  pl.BlockSpec((1, s, gw), lambda bi, g, qi: (bi, 0, g)),
                  pl.BlockSpec((2 * tile, tile), lambda bi, g, qi: (0, 0))],
        out_specs=pl.BlockSpec((1, tile, gw // 2), lambda bi, g, qi: (bi, qi, g)),
        out_shape=jax.ShapeDtypeStruct((b, s, HEAD_PAIRS * LANES), _F32),
        scratch_shapes=[pltpu.VMEM((SB_GROUP_HEADS, tile, 1), _F32), pltpu.VMEM((SB_GROUP_HEADS, tile, LANES), _F32)],
        compiler_params=pltpu.CompilerParams(dimension_semantics=("parallel", "parallel", "arbitrary"),
                                             vmem_limit_bytes=VMEM_LIMIT),
        name="sb",
    )(qs, ks, vs, tri)


def _post_kernel(om_ref, os_ref, x_ref, g_om_ref, g_os_ref, wo_ref, g_moe_ref, wr_ref, br_ref, ltri_ref,
                 h1_ref, u2_ref, route_ref, counts_ref, seen_ref):
    rows = POST_ROWS

    @pl.when(pl.program_id(0) == 0)
    def _():
        seen_ref[...] = jnp.zeros_like(seen_ref)

    mixed = jnp.concatenate([_rms(om_ref[...], g_om_ref[...]), _rms(os_ref[...], g_os_ref[...])], axis=-1)
    h1 = x_ref[...] + _dot(mixed.astype(_BF16), wo_ref[...])
    h1_ref[...] = h1
    u2 = _rms(h1, g_moe_ref[...])
    _store_row_tiles(u2_ref, u2)
    w_hi = wr_ref[...].astype(_BF16)
    w_lo = (wr_ref[...] - w_hi.astype(_F32)).astype(_BF16)
    u_hi = u2.astype(_BF16)
    u_lo = (u2 - u_hi.astype(_F32)).astype(_BF16)
    logits = _dot(u_hi, w_hi) + _dot(u_hi, w_lo) + _dot(u_lo, w_hi) + br_ref[...]

    part = rows // POST_PARTS
    lane = lax.broadcasted_iota(jnp.int32, (part, N_EXPERTS), 1).astype(_F32)
    out_lane = lax.broadcasted_iota(jnp.int32, (part, LANES), 1)
    work = [logits[r0:r0 + part, :] for r0 in range(0, rows, part)]
    onehot = [jnp.zeros((part, N_EXPERTS), _F32) for _ in work]
    ids = [[] for _ in work]
    tops = [[] for _ in work]
    for _ in range(TOP_K):
        for n in range(POST_PARTS):
            top = jnp.max(work[n], axis=-1, keepdims=True)
            idx = jnp.min(jnp.where(work[n] == top, lane, float(N_EXPERTS)), axis=-1, keepdims=True)
            hit = lane == idx
            onehot[n] = jnp.where(hit, 1.0, onehot[n])
            work[n] = jnp.where(hit, -jnp.inf, work[n])
            ids[n].append(idx)
            tops[n].append(top)

    seen = seen_ref[...]
    for n in range(POST_PARTS):
        exps = [jnp.exp(tp - tops[n][0]) for tp in tops[n]]
        denom = exps[0] + exps[1] + exps[2] + exps[3]
        gates = [e / denom for e in exps]
        before = seen + _dot(ltri_ref[...], onehot[n].astype(_BF16))
        ranks = [jnp.sum(jnp.where(lane == idx, before, 0.0), axis=-1, keepdims=True) for idx in ids[n]]
        seen = seen + jnp.sum(onehot[n], axis=0, keepdims=True)
        route = jnp.zeros((part, LANES), _F32)
        for j, val in enumerate(ids[n] + gates + ranks):
            route = jnp.where(out_lane == j, val, route)
        route_ref[n * part:(n + 1) * part, :] = route
    seen_ref[...] = seen
    counts_ref[...] = seen


def _post(om, os_, x2, g_om, g_os, wo, g_moe, wr, br, ltri):
    t = x2.shape[0]
    rows = POST_ROWS

    def full(a):
        return pl.BlockSpec(a.shape, lambda i: (0,) * a.ndim)

    def tok(n):
        return pl.BlockSpec((rows, n), lambda i: (i, 0))

    return pl.pallas_call(
        _post_kernel,
        grid=(t // rows,),
        in_specs=[tok(om.shape[1]), tok(os_.shape[1]), tok(D_MODEL), full(g_om), full(g_os), full(wo), full(g_moe),
                  full(wr), full(br), full(ltri)],
        out_specs=[tok(D_MODEL), pl.BlockSpec((rows * ROW_TILE, LANES), lambda i: (i, 0)), tok(LANES),
                   pl.BlockSpec((1, N_EXPERTS), lambda i: (0, 0))],
        out_shape=[jax.ShapeDtypeStruct((t, D_MODEL), _F32), jax.ShapeDtypeStruct((t * ROW_TILE, LANES), _F32),
                   jax.ShapeDtypeStruct((t, LANES), _F32), jax.ShapeDtypeStruct((1, N_EXPERTS), _F32)],
        scratch_shapes=[pltpu.VMEM((1, N_EXPERTS), _F32)],
        compiler_params=pltpu.CompilerParams(dimension_semantics=("arbitrary",), vmem_limit_bytes=VMEM_LIMIT),
        name="post",
    )(om, os_, x2, g_om, g_os, wo, g_moe, wr, br, ltri)


def _padded_count(cnt_ref, e):
    shift = EXPERT_ROWS.bit_length() - 1
    return lax.shift_left(lax.shift_right_logical(cnt_ref[e] + (EXPERT_ROWS - 1), shift), shift)


def _layout_kernel(cnt_ref, route_ref, dest_ref, start_ref, be_ref, used_ref):
    shift = EXPERT_ROWS.bit_length() - 1
    rows = MOE_ROWS
    nb = be_ref.shape[0]

    def place(e, off):
        padded = _padded_count(cnt_ref, e)
        start_ref[e] = off

        def mark(j, c):
            be_ref[lax.shift_right_logical(off, shift) + j] = e
            return c
        lax.fori_loop(0, lax.shift_right_logical(padded, shift), mark, 0)
        return off + padded
    total = lax.fori_loop(0, N_EXPERTS, place, 0)
    used_ref[0] = lax.shift_right_logical(total, shift)

    def tail(b, c):
        be_ref[b] = N_EXPERTS - 1
        return c
    lax.fori_loop(lax.shift_right_logical(total, shift), nb, tail, 0)

    lane = lax.broadcasted_iota(jnp.int32, (1, LANES), 1)
    rank_lanes = jnp.logical_and(lane >= 2 * TOP_K, lane < 3 * TOP_K)
    pick = (lax.broadcasted_iota(jnp.int32, (2 * SUBLANES, LANES), 0)
            == lax.broadcasted_iota(jnp.int32, (2 * SUBLANES, LANES), 1)).astype(_BF16)

    def tile(i, c):
        r = route_ref[pl.ds(pl.multiple_of(i * rows, rows), rows), :]
        high = jnp.floor(r * (1.0 / 256.0))
        low_digits = jnp.where(rank_lanes, r - 256.0 * high, r)
        high_digits = jnp.where(rank_lanes, high, 0.0)
        low_t = _dot_nt(pick, low_digits.astype(_BF16))
        high_t = _dot_nt(pick, high_digits.astype(_BF16))
        ids = low_t[:TOP_K, :].astype(jnp.int32)
        dest = (256.0 * high_t[2 * TOP_K:3 * TOP_K, :] + low_t[2 * TOP_K:3 * TOP_K, :]).astype(jnp.int32)
        for e in range(N_EXPERTS):
            dest = dest + jnp.where(ids == e, start_ref[e], 0)
        lines = [dest[k:k + 1, half * LANES:(half + 1) * LANES] for k in range(TOP_K) for half in range(rows // LANES)]
        dest_ref[pl.ds(pl.multiple_of(i * SUBLANES, SUBLANES), SUBLANES), :] = jnp.concatenate(lines, axis=0)
        return c
    lax.fori_loop(0, route_ref.shape[0] // rows, tile, 0)


def _layout(counts, route, n_blocks):
    t = route.shape[0]
    pair_shape = (t * TOP_K // LANES, LANES)
    grid_spec = pltpu.PrefetchScalarGridSpec(
        num_scalar_prefetch=1,
        grid=(1,),
        in_specs=[pl.BlockSpec(route.shape, lambda i, c: (0, 0))],
        out_specs=[pl.BlockSpec(pair_shape, lambda i, c: (0, 0)), pl.BlockSpec(memory_space=pltpu.SMEM),
                   pl.BlockSpec(memory_space=pltpu.SMEM), pl.BlockSpec(memory_space=pltpu.SMEM)],
    )
    return pl.pallas_call(
        _layout_kernel,
        grid_spec=grid_spec,
        out_shape=[jax.ShapeDtypeStruct(pair_shape, jnp.int32), jax.ShapeDtypeStruct((N_EXPERTS,), jnp.int32),
                   jax.ShapeDtypeStruct((n_blocks,), jnp.int32), jax.ShapeDtypeStruct((1,), jnp.int32)],
        compiler_params=pltpu.CompilerParams(dimension_semantics=("arbitrary",), vmem_limit_bytes=VMEM_LIMIT),
        name="layout",
    )(counts, route)


def _tile_rows(row):
    return pl.ds(pl.multiple_of(row * ROW_TILE, ROW_TILE), ROW_TILE)


def _for_each_pair(idx, s, body):
    halves = MOE_ROWS // LANES

    def line(g, c):
        choice = g // halves
        first = lax.rem(g, halves) * LANES
        for t0 in range(0, LANES, DMA_GROUP):
            vals = [idx[s, g, t0 + j] for j in range(DMA_GROUP)]
            for j, row in enumerate(vals):
                body(first + t0 + j, choice, row, j % DMA_THREADS)
        return c
    lax.fori_loop(0, SUBLANES, line, 0)


def _dispatch_kernel(cnt_ref, start_ref, dest_hbm, u_hbm, xs_hbm, idx, stage, zblk, sem_d, sem_l, sem_i, sem_z):
    rows = MOE_ROWS
    shift = EXPERT_ROWS.bit_length() - 1
    i = pl.program_id(0)
    nt = pl.num_programs(0)
    nb = xs_hbm.shape[0] // (EXPERT_ROWS * ROW_TILE)
    slot = lax.rem(i, 2)
    stage_slot = lax.rem(i, STAGE_SLOTS)

    def idx_copy(tile, s):
        return pltpu.make_async_copy(dest_hbm.at[pl.ds(pl.multiple_of(tile * SUBLANES, SUBLANES), SUBLANES)],
                                     idx.at[s], sem_i.at[s])

    def load(tile):
        s = lax.rem(tile, STAGE_SLOTS)
        src = pl.ds(pl.multiple_of(tile * (rows * ROW_TILE), rows * ROW_TILE), rows * ROW_TILE)
        return pltpu.make_async_copy(u_hbm.at[src], stage.at[s], sem_l.at[s])

    def wait_rows(s):
        chunk = pl.ds(0, rows * ROW_TILE)
        for _ in range(TOP_K):
            pltpu.make_async_copy(stage.at[s], xs_hbm.at[chunk], sem_d.at[s]).wait()

    def zero_block(b):
        dst = pl.ds(pl.multiple_of(b * (EXPERT_ROWS * ROW_TILE), EXPERT_ROWS * ROW_TILE), EXPERT_ROWS * ROW_TILE)
        return pltpu.make_async_copy(zblk, xs_hbm.at[dst], sem_z)

    def pad_rows(e, fn):
        pad = _padded_count(cnt_ref, e) - cnt_ref[e]
        first = start_ref[e] + cnt_ref[e]
        for bit in reversed(range(shift)):
            size = 1 << bit
            before = lax.shift_left(lax.shift_right_logical(pad, bit + 1), bit + 1)

            @pl.when(lax.shift_right_logical(pad, bit) & 1 == 1)
            def _(size=size, before=before):
                dst = pl.ds(pl.multiple_of((first + before) * ROW_TILE, ROW_TILE), size * ROW_TILE)
                fn(pltpu.make_async_copy(zblk.at[pl.ds(0, size * ROW_TILE)], xs_hbm.at[dst], sem_z))

    def tail_blocks(fn):
        used = start_ref[N_EXPERTS - 1] + _padded_count(cnt_ref, N_EXPERTS - 1)

        def one(b, c):
            fn(zero_block(b))
            return c
        lax.fori_loop(lax.shift_right_logical(used, shift), nb, one, 0)

    @pl.when(i == 0)
    def _():
        idx_copy(0, 0).start()
        load(0).start()

        @pl.when(nt > 1)
        def _():
            load(1).start()
        zblk[...] = jnp.zeros_like(zblk)
        for fn in (lambda c: c.start(), lambda c: c.wait()):
            def per_expert(e, c, fn=fn):
                pad_rows(e, fn)
                return c
            lax.fori_loop(0, N_EXPERTS, per_expert, 0)
            tail_blocks(fn)

    idx_copy(i, slot).wait()

    @pl.when(i + 1 < nt)
    def _():
        idx_copy(i + 1, 1 - slot).start()

    @pl.when(i >= 2)
    def _():
        wait_rows(lax.rem(i + 2, STAGE_SLOTS))

    @pl.when(i + 2 < nt)
    def _():
        load(i + 2).start()

    load(i).wait()

    def copy_row(t, k, row, thread):
        del k
        pltpu.make_async_copy(stage.at[stage_slot, _tile_rows(t)], xs_hbm.at[_tile_rows(row)],
                              sem_d.at[stage_slot]).start(priority=thread)
    _for_each_pair(idx, slot, copy_row)

    @pl.when(i == nt - 1)
    def _():
        @pl.when(nt > 1)
        def _():
            wait_rows(lax.rem(i + STAGE_SLOTS - 1, STAGE_SLOTS))
        wait_rows(stage_slot)


def _dispatch(counts, start, dest, u2_tiles, n_blocks):
    nt = dest.shape[0] // SUBLANES
    grid_spec = pltpu.PrefetchScalarGridSpec(
        num_scalar_prefetch=2,
        grid=(nt,),
        in_specs=[pl.BlockSpec(memory_space=pl.ANY), pl.BlockSpec(memory_space=pl.ANY)],
        out_specs=pl.BlockSpec(memory_space=pl.ANY),
        scratch_shapes=[pltpu.SMEM((2, SUBLANES, LANES), jnp.int32),
                        pltpu.VMEM((STAGE_SLOTS, MOE_ROWS * ROW_TILE, LANES), _F32),
                        pltpu.VMEM((EXPERT_ROWS * ROW_TILE, LANES), _F32),
                        pltpu.SemaphoreType.DMA((STAGE_SLOTS,)), pltpu.SemaphoreType.DMA((STAGE_SLOTS,)),
                        pltpu.SemaphoreType.DMA((2,)), pltpu.SemaphoreType.DMA(())],
    )
    return pl.pallas_call(
        _dispatch_kernel,
        grid_spec=grid_spec,
        out_shape=jax.ShapeDtypeStruct((n_blocks * EXPERT_ROWS * ROW_TILE, LANES), _F32),
        compiler_params=pltpu.CompilerParams(dimension_semantics=("arbitrary",), vmem_limit_bytes=VMEM_LIMIT),
        name="dispatch",
    )(counts, start, dest, u2_tiles)


def _expert_kernel(be_ref, used_ref, xs_ref, wgu_hbm, bgu_ref, wdn_hbm, bdn_ref, ys_ref,
                   wgu_f32, wdn_f32, wgu_bf, wdn_bf, sem_w):
    rows = EXPERT_ROWS
    i = pl.program_id(0)
    nb = pl.num_programs(0)
    used = used_ref[0]
    e = be_ref[i]
    prev = be_ref[jnp.maximum(i - 1, 0)]
    in_use = i < used

    def fetch(expert):
        return (pltpu.make_async_copy(wgu_hbm.at[expert], wgu_f32, sem_w.at[0]),
                pltpu.make_async_copy(wdn_hbm.at[expert], wdn_f32, sem_w.at[1]))

    @pl.when(i == 0)
    def _():
        for c in fetch(e):
            c.start()

    @pl.when(jnp.logical_and(in_use, jnp.logical_or(i == 0, e != prev)))
    def _():
        for c in fetch(e):
            c.wait()
        chunk = 128

        def cast(c, carry):
            rs = pl.ds(pl.multiple_of(c * chunk, chunk), chunk)
            wgu_bf[rs, :] = wgu_f32[rs, :].astype(_BF16)
            wdn_bf[rs, :] = wdn_f32[rs, :].astype(_BF16)
            return carry
        lax.fori_loop(0, D_MODEL // chunk, cast, 0)

        def same_expert(j):
            return jnp.logical_and(j < used, be_ref[jnp.minimum(j, nb - 1)] == e)
        nxt = lax.while_loop(same_expert, lambda j: j + 1, i)

        @pl.when(nxt < used)
        def _():
            for c in fetch(be_ref[jnp.minimum(nxt, nb - 1)]):
                c.start()

    @pl.when(in_use)
    def _():
        x = _load_row_tiles(xs_ref, 0, rows).astype(_BF16)
        hh = _dot(x, wgu_bf[...]) + bgu_ref[0]
        glu = jnp.minimum(hh[:, :D_FF], SWIGLU_LIMIT)
        lin = jnp.clip(hh[:, D_FF:], -SWIGLU_LIMIT, SWIGLU_LIMIT)
        act = glu * jax.nn.sigmoid(SWIGLU_ALPHA * glu) * (lin + 1.0)
        _store_row_tiles(ys_ref, _dot(act.astype(_BF16), wdn_bf[...]) + bdn_ref[0])

    @pl.when(jnp.logical_not(in_use))
    def _():
        ys_ref[...] = jnp.zeros_like(ys_ref)


def _experts(block_e, used, xs, w_gu, b_gu, w_dn, b_dn):
    nb = block_e.shape[0]
    rows = EXPERT_ROWS
    grid_spec = pltpu.PrefetchScalarGridSpec(
        num_scalar_prefetch=2,
        grid=(nb,),
        in_specs=[pl.BlockSpec((rows * ROW_TILE, LANES), lambda i, be, u: (i, 0)),
                  pl.BlockSpec(memory_space=pl.ANY),
                  pl.BlockSpec((1, 1, 2 * D_FF), lambda i, be, u: (be[i], 0, 0)),
                  pl.BlockSpec(memory_space=pl.ANY),
                  pl.BlockSpec((1, 1, D_MODEL), lambda i, be, u: (be[i], 0, 0))],
        out_specs=pl.BlockSpec((rows * ROW_TILE, LANES), lambda i, be, u: (i, 0)),
        scratch_shapes=[pltpu.VMEM((D_MODEL, 2 * D_FF), _F32), pltpu.VMEM((D_FF, D_MODEL), _F32),
                        pltpu.VMEM((D_MODEL, 2 * D_FF), _BF16), pltpu.VMEM((D_FF, D_MODEL), _BF16),
                        pltpu.SemaphoreType.DMA((2,))],
    )
    return pl.pallas_call(
        _expert_kernel,
        grid_spec=grid_spec,
        out_shape=jax.ShapeDtypeStruct(xs.shape, _F32),
        compiler_params=pltpu.CompilerParams(dimension_semantics=("arbitrary",), vmem_limit_bytes=VMEM_LIMIT),
        name="experts",
    )(block_e, used, xs, w_gu, b_gu.reshape(N_EXPERTS, 1, 2 * D_FF), w_dn, b_dn.reshape(N_EXPERTS, 1, D_MODEL))


def _final_kernel(dest_hbm, ys_hbm, h1_ref, route_ref, p_ref, g_ple_ref, wpg_ref, wpp_ref, g_fin_ref,
                  out_ref, ybuf, idx, sem_y, sem_i, *, last_layer):
    rows = MOE_ROWS
    i = pl.program_id(0)
    nt = pl.num_programs(0)
    slot = lax.rem(i, 2)

    def idx_copy(tile, s):
        return pltpu.make_async_copy(dest_hbm.at[pl.ds(pl.multiple_of(tile * SUBLANES, SUBLANES), SUBLANES)],
                                     idx.at[s], sem_i.at[s])

    def gather(s):
        def copy_row(t, k, row, thread):
            pltpu.make_async_copy(ys_hbm.at[_tile_rows(row)], ybuf.at[s, _tile_rows(k * rows + t)],
                                  sem_y.at[s]).start(priority=thread)
        _for_each_pair(idx, s, copy_row)

    @pl.when(i == 0)
    def _():
        idx_copy(0, 0).start()
        idx_copy(0, 0).wait()
        gather(0)

        @pl.when(nt > 1)
        def _():
            idx_copy(1, 1).start()

    @pl.when(i + 1 < nt)
    def _():
        idx_copy(i + 1, 1 - slot).wait()
        gather(1 - slot)

        @pl.when(i + 2 < nt)
        def _():
            idx_copy(i + 2, slot).start()

    pltpu.make_async_copy(ys_hbm.at[pl.ds(0, TOP_K * rows * ROW_TILE)], ybuf.at[slot], sem_y.at[slot]).wait()
    part = rows // ROW_PARTS
    for r0 in range(0, rows, part):
        rs = slice(r0, r0 + part)
        route = route_ref[rs, :]
        y = jnp.zeros((part, D_MODEL), _F32)
        for k in range(TOP_K):
            y = y + _load_row_tiles(ybuf.at[slot], k * rows + r0, part) * route[:, TOP_K + k:TOP_K + k + 1]
        h2 = h1_ref[rs, :] + y
        u3 = _rms(h2, g_ple_ref[...]).astype(_BF16)
        gate = jax.nn.sigmoid(_dot(u3, wpg_ref[...]))
        h3 = h2 + gate * _dot(p_ref[rs, :].astype(_BF16), wpp_ref[...])
        out_ref[rs, :] = _rms(h3, g_fin_ref[...]) if last_layer else h3


def _final(dest, ys, h1, route, p2, g_ple, wpg, wpp, g_fin, last_layer):
    t = h1.shape[0]
    rows = MOE_ROWS

    def full(a):
        return pl.BlockSpec(a.shape, lambda i: (0,) * a.ndim)

    def tok(n):
        return pl.BlockSpec((rows, n), lambda i: (i, 0))

    return pl.pallas_call(
        functools.partial(_final_kernel, last_layer=last_layer),
        grid=(t // rows,),
        in_specs=[pl.BlockSpec(memory_space=pl.ANY), pl.BlockSpec(memory_space=pl.ANY), tok(D_MODEL), tok(LANES),
                  tok(PLE_DIM), full(g_ple), full(wpg), full(wpp), full(g_fin)],
        out_specs=tok(D_MODEL),
        out_shape=jax.ShapeDtypeStruct((t, D_MODEL), _F32),
        scratch_shapes=[pltpu.VMEM((2, TOP_K * rows * ROW_TILE, LANES), _F32),
                        pltpu.SMEM((2, SUBLANES, LANES), jnp.int32),
                        pltpu.SemaphoreType.DMA((2,)), pltpu.SemaphoreType.DMA((2,))],
        compiler_params=pltpu.CompilerParams(dimension_semantics=("arbitrary",), vmem_limit_bytes=VMEM_LIMIT),
        name="final",
    )(dest, ys, h1, route, p2, g_ple, wpg, wpp, g_fin)


def _rope_pad(w, rot):
    half = MLA_ROPE_DIM // 2
    body = jnp.concatenate([-w[:, half:], w[:, :half]], axis=1) if rot else w
    z = jnp.zeros((w.shape[0], MLA_NOPE_DIM), w.dtype)
    return jnp.concatenate([z, body, jnp.zeros((w.shape[0], LANES - MLA_NOPE_DIM - MLA_ROPE_DIM), w.dtype)], axis=1)


def _layer_weights(w_in, w_uq, w_ukv):
    o = Q_LORA_RANK + KV_LORA_RANK
    w_kr = w_in[:, o:o + MLA_ROPE_DIM]
    w1 = jnp.concatenate([w_in[:, :o], _rope_pad(w_kr, False), _rope_pad(w_kr, True), w_in[:, o + MLA_ROPE_DIM:]],
                         axis=1).astype(_BF16)
    uq = w_uq.reshape(Q_LORA_RANK, MLA_HEADS, MLA_NOPE_DIM + MLA_ROPE_DIM)
    zq = jnp.zeros((Q_LORA_RANK, MLA_HEADS, LANES - MLA_NOPE_DIM - MLA_ROPE_DIM), w_uq.dtype)
    wq = jnp.concatenate([uq, zq], axis=2).reshape(Q_LORA_RANK, MLA_HEADS * LANES).astype(_BF16)
    rope = uq[:, :, MLA_NOPE_DIM:]
    half = MLA_ROPE_DIM // 2
    rot = jnp.concatenate([jnp.zeros_like(uq[:, :, :MLA_NOPE_DIM]), -rope[:, :, half:], rope[:, :, :half], zq], axis=2)
    wqr = rot.reshape(Q_LORA_RANK, MLA_HEADS * LANES).astype(_BF16)
    ukv = w_ukv.reshape(KV_LORA_RANK, MLA_HEADS, MLA_NOPE_DIM + MLA_V_DIM)
    zk = jnp.zeros((KV_LORA_RANK, MLA_HEADS, LANES - MLA_NOPE_DIM), w_ukv.dtype)
    wk = jnp.concatenate([ukv[:, :, :MLA_NOPE_DIM], zk], axis=2).reshape(KV_LORA_RANK, MLA_HEADS * LANES).astype(_BF16)
    v = ukv[:, :, MLA_NOPE_DIM:].reshape(KV_LORA_RANK, HEAD_PAIRS, 2, MLA_V_DIM)
    zv = jnp.zeros((KV_LORA_RANK, HEAD_PAIRS, MLA_V_DIM), w_ukv.dtype)
    wv = jnp.stack([jnp.concatenate([v[:, :, 0], zv], axis=2), jnp.concatenate([zv, v[:, :, 1]], axis=2)], axis=2)
    wv = wv.reshape(KV_LORA_RANK, MLA_HEADS * LANES).astype(_BF16)
    return w1, wq, wqr, wk, wv


def kernel(x, p, positions, w_in, g_attn, g_cq, w_uq, g_ckv, w_ukv, g_out_mla, g_out_sb, w_o, g_moe, w_router,
           b_router, w_gu, b_gu, w_dn, b_dn, g_ple, w_ple_gate, w_ple_proj, g_final):
    b, s, d = x.shape
    t = b * s
    depth = w_in.shape[0]
    assert d == D_MODEL and ROW_TILE == SUBLANES and s % MLA_TILE == 0 and t % PROJ_ROWS == 0 and t % POST_ROWS == 0
    assert t % MOE_ROWS == 0 and LANES % DMA_GROUP == 0 and MOE_ROWS % LANES == 0 and (t * TOP_K) % EXPERT_ROWS == 0
    assert EXPERT_ROWS & (EXPERT_ROWS - 1) == 0

    freq = ROPE_THETA ** (-jnp.arange(0, MLA_ROPE_DIM, 2, dtype=_F32) / MLA_ROPE_DIM)
    invf = jnp.concatenate([jnp.zeros((MLA_NOPE_DIM,), _F32), freq, freq,
                            jnp.zeros((LANES - MLA_NOPE_DIM - MLA_ROPE_DIM,), _F32)]).reshape(1, LANES)
    pos = positions.reshape(t, 1)
    idx = jnp.arange(SB_TILE)
    tri = -(idx[:, None] >= idx[None, :]).astype(_BF16)
    tri = jnp.concatenate([tri, tri], axis=0)
    idx = jnp.arange(POST_ROWS // POST_PARTS)
    ltri = (idx[None, :] < idx[:, None]).astype(_BF16)
    n_blocks = t * TOP_K // EXPERT_ROWS + N_EXPERTS

    h = x.reshape(t, d)
    for i in range(depth):
        w1, wq, wqr, wk, wv = _layer_weights(w_in[i], w_uq[i], w_ukv[i])
        qm, km, vm, qs, ks, vs = _proj(pos, h, g_attn[i].reshape(1, d), w1, g_cq[i].reshape(1, -1), wq, wqr,
                                       g_ckv[i].reshape(1, -1), wk, wv, invf)

        def seq(a):
            return a.reshape(b, s, a.shape[1])

        om = _mla(seq(qm), seq(km), seq(vm)).reshape(t, -1)
        os_ = _sb(seq(qs), seq(ks), seq(vs), tri).reshape(t, -1)
        h1, u2_tiles, route, counts = _post(om, os_, h, g_out_mla[i].reshape(1, -1), g_out_sb[i].reshape(1, -1),
                                            w_o[i].astype(_BF16), g_moe[i].reshape(1, d), w_router[i],
                                            b_router[i].reshape(1, -1), ltri)
        counts = counts.reshape(-1).astype(jnp.int32)
        dest, start, block_e, used = _layout(counts, route, n_blocks)
        xs = _dispatch(counts, start, dest, u2_tiles, n_blocks)
        ys = _experts(block_e, used, xs, w_gu[i], b_gu[i], w_dn[i], b_dn[i])
        h = _final(dest, ys, h1, route, p[i].reshape(t, -1), g_ple[i].reshape(1, d), w_ple_gate[i].astype(_BF16),
                   w_ple_proj[i].astype(_BF16), g_final.reshape(1, d), i == depth - 1)
    return h.reshape(b, s, d)
```

```python
import functools
import math

import jax
import jax.numpy as jnp
from jax import lax
from jax.experimental import pallas as pl
from jax.experimental.pallas import tpu as pltpu

D_MODEL = 1024
PLE_DIM = 256
MLA_HEADS = 8
MLA_NOPE_DIM = 64
MLA_ROPE_DIM = 32
MLA_V_DIM = 64
Q_LORA_RANK = 384
KV_LORA_RANK = 256
SB_HEADS = 8
SB_HEAD_DIM = 64
SB_WIDTH = SB_HEADS * SB_HEAD_DIM
ROPE_THETA = 10000.0
N_EXPERTS = 32
TOP_K = 4
D_FF = 1024
SWIGLU_LIMIT = 7.0
SWIGLU_ALPHA = 1.702
RMS_EPS = 1e-6

LANES = 128
SUBLANES = 8
ROW_TILE = D_MODEL // LANES
HEAD_PAIRS = MLA_HEADS // 2
PROJ_ROWS = 512
POST_ROWS = 512
GROUP_HEADS = 4
SB_GROUP_HEADS = 4
MLA_TILE = 512
SB_TILE = 256
SB_FIRST_TILES = 2
EXPERT_ROWS = 256
ROW_PARTS = 2
POST_PARTS = 8
MOE_ROWS = SUBLANES * LANES // TOP_K
DMA_GROUP = 32
STAGE_SLOTS = 4
DMA_THREADS = 2
VMEM_LIMIT = 56 * 1024 * 1024
SB_LOG_FLOOR = -105.0

_F32 = jnp.float32
_BF16 = jnp.bfloat16


def _rms(x, g):
    return x * lax.rsqrt(jnp.mean(x * x, axis=-1, keepdims=True) + RMS_EPS) * g


def _dot(a, b):
    return jnp.dot(a, b, preferred_element_type=_F32)


def _dot_nt(a, b):
    return lax.dot_general(a, b, (((1,), (1,)), ((), ())), preferred_element_type=_F32)


def _store_row_tiles(ref, value, first_row=0):
    rows = value.shape[0]
    for g in range(ROW_TILE):
        ref[pl.ds(first_row * ROW_TILE + g, rows, stride=ROW_TILE), :] = value[:, g * LANES:(g + 1) * LANES]


def _load_row_tiles(ref, first_row, rows):
    return jnp.concatenate([ref[pl.ds(first_row * ROW_TILE + g, rows, stride=ROW_TILE), :] for g in range(ROW_TILE)],
                           axis=-1)


def _proj_kernel(pos_ref, x_ref, g_attn_ref, w1_ref, g_cq_ref, wq_ref, wqr_ref, g_ckv_ref, wk_ref, wv_ref,
                 invf_ref, qm_ref, km_ref, vm_ref, qs_ref, ks_ref, vs_ref):
    u = _rms(x_ref[...], g_attn_ref[...]).astype(_BF16)
    y = _dot(u, w1_ref[...])
    c_q = y[:, :Q_LORA_RANK]
    c_kv = y[:, Q_LORA_RANK:Q_LORA_RANK + KV_LORA_RANK]
    o = Q_LORA_RANK + KV_LORA_RANK
    k_r = y[:, o:o + LANES]
    k_r_rot = y[:, o + LANES:o + 2 * LANES]
    o += 2 * LANES
    q_s = y[:, o:o + SB_WIDTH]
    k_s = y[:, o + SB_WIDTH:o + 2 * SB_WIDTH]
    v_s = y[:, o + 2 * SB_WIDTH:o + 3 * SB_WIDTH]

    ang = pos_ref[...].astype(_F32) * invf_ref[...]
    cos = jnp.cos(ang)
    sin = jnp.sin(ang)

    cq_n = _rms(c_q, g_cq_ref[...]).astype(_BF16)
    q = _dot(cq_n, wq_ref[...])
    q_rot = _dot(cq_n, wqr_ref[...])
    ckv_n = _rms(c_kv, g_ckv_ref[...]).astype(_BF16)
    k = _dot(ckv_n, wk_ref[...])
    wide_lane = lax.broadcasted_iota(jnp.int32, (1, MLA_HEADS * LANES), 1)
    ones_lane = (wide_lane % (2 * LANES) == MLA_V_DIM) | (wide_lane % (2 * LANES) == LANES)
    vm_ref[...] = jnp.where(ones_lane, 1.0, _dot(ckv_n, wv_ref[...])).astype(_BF16)
    k_rope = k_r * cos + k_r_rot * sin

    q_scale = (MLA_NOPE_DIM + MLA_ROPE_DIM) ** -0.5 * math.log2(math.e)
    lane = lax.broadcasted_iota(jnp.int32, (1, LANES), 1)
    low = lane < SB_HEAD_DIM
    for h in range(MLA_HEADS):
        sl = slice(h * LANES, (h + 1) * LANES)
        qm_ref[:, sl] = ((q[:, sl] * cos + q_rot[:, sl] * sin) * q_scale).astype(_BF16)
        km_ref[:, sl] = (k[:, sl] + k_rope).astype(_BF16)
    qs_ref[...] = (q_s * (SB_HEAD_DIM ** -0.5)).astype(_BF16)
    for hp in range(HEAD_PAIRS):
        sl = slice(hp * LANES, (hp + 1) * LANES)
        for half, keep in ((0, low), (1, jnp.logical_not(low))):
            dst = slice((2 * hp + half) * LANES, (2 * hp + half + 1) * LANES)
            ks_ref[:, dst] = jnp.where(keep, k_s[:, sl], 0.0).astype(_BF16)
            vs_ref[:, dst] = jnp.where(keep, v_s[:, sl], 0.0).astype(_BF16)


def _proj(pos, x2, g_attn, w1, g_cq, wq, wqr, g_ckv, wk, wv, invf):
    t = x2.shape[0]
    rows = PROJ_ROWS
    wide = MLA_HEADS * LANES

    def full(a):
        return pl.BlockSpec(a.shape, lambda i: (0,) * a.ndim)

    def tok(n):
        return pl.BlockSpec((rows, n), lambda i: (i, 0))

    outs = [jax.ShapeDtypeStruct((t, n), _BF16) for n in (wide, wide, wide, SB_WIDTH, wide, wide)]
    return pl.pallas_call(
        _proj_kernel,
        grid=(t // rows,),
        in_specs=[tok(1), tok(D_MODEL), full(g_attn), full(w1), full(g_cq), full(wq), full(wqr), full(g_ckv),
                  full(wk), full(wv), full(invf)],
        out_specs=[tok(wide), tok(wide), tok(wide), tok(SB_WIDTH), tok(wide), tok(wide)],
        out_shape=outs,
        compiler_params=pltpu.CompilerParams(dimension_semantics=("parallel",), vmem_limit_bytes=VMEM_LIMIT),
        name="proj",
    )(pos, x2, g_attn, w1, g_cq, wq, wqr, g_ckv, wk, wv, invf)


def _mla_kernel(q_ref, k_ref, v_ref, o_ref, s_a, s_b, m_ref, acc_ref):
    tile, half = MLA_TILE, MLA_TILE // 2
    qi = pl.program_id(2)
    heads = [slice(h * LANES, (h + 1) * LANES) for h in range(GROUP_HEADS)]
    everything = slice(0, tile)

    def key_rows(kt, n=tile):
        return pl.ds(pl.multiple_of(kt * tile, tile), n)

    def score(dst, kt):
        for h, sl in enumerate(heads):
            dst[h] = _dot_nt(q_ref[0, :, sl], k_ref[0, key_rows(kt), sl])

    def score_diagonal(dst):
        upper_mask = (lax.broadcasted_iota(jnp.int32, (half, half), 1)
                      <= lax.broadcasted_iota(jnp.int32, (half, half), 0))
        lower_mask = (lax.broadcasted_iota(jnp.int32, (half, tile), 1)
                      <= lax.broadcasted_iota(jnp.int32, (half, tile), 0) + half)
        for h, sl in enumerate(heads):
            upper = _dot_nt(q_ref[0, :half, sl], k_ref[0, key_rows(qi, half), sl])
            dst[h, :half, :half] = jnp.where(upper_mask, upper, -jnp.inf)
            lower = _dot_nt(q_ref[0, half:, sl], k_ref[0, key_rows(qi), sl])
            dst[h, half:, :] = jnp.where(lower_mask, lower, -jnp.inf)

    def absorb_block(src, h, sl, part, cols, keys):
        s = src[h, part * half:(part + 1) * half, cols]
        m = m_ref[h, part]
        row_max = jnp.max(s, axis=-1, keepdims=True)
        m_new = jnp.maximum(m, row_max)
        alpha = jnp.exp2(m - m_new)
        p = jnp.exp2(s - jnp.maximum(m[:, :1], row_max))
        m_ref[h, part] = m_new
        acc_ref[h, part] = alpha * acc_ref[h, part] + _dot(p.astype(_BF16), v_ref[0, keys, sl])

    def absorb(src, kt):
        for h, sl in enumerate(heads):
            for part in range(2):
                absorb_block(src, h, sl, part, everything, key_rows(kt))

    def absorb_diagonal(src):
        for h, sl in enumerate(heads):
            absorb_block(src, h, sl, 0, slice(0, half), key_rows(qi, half))
            absorb_block(src, h, sl, 1, everything, key_rows(qi))

    m_ref[...] = jnp.full(m_ref.shape, -jnp.inf, _F32)
    acc_ref[...] = jnp.zeros(acc_ref.shape, _F32)

    @pl.when(qi == 0)
    def _():
        score_diagonal(s_a)
        absorb_diagonal(s_a)

    @pl.when(qi > 0)
    def _():
        score(s_a, 0)

        def pair(i, c):
            score(s_b, 2 * i + 1)
            absorb(s_a, 2 * i)
            score(s_a, 2 * i + 2)
            absorb(s_b, 2 * i + 1)
            return c
        lax.fori_loop(0, (qi - 1) // 2, pair, 0)

        @pl.when(qi % 2 == 1)
        def _():
            score_diagonal(s_b)
            absorb(s_a, qi - 1)
            absorb_diagonal(s_b)

        @pl.when(qi % 2 == 0)
        def _():
            score(s_b, qi - 1)
            absorb(s_a, qi - 2)
            score_diagonal(s_a)
            absorb(s_b, qi - 1)
            absorb_diagonal(s_a)

    low = lax.broadcasted_iota(jnp.int32, (half, LANES), 1) < MLA_V_DIM
    for hp in range(GROUP_HEADS // 2):
        for part in range(2):
            even, odd = acc_ref[2 * hp, part], acc_ref[2 * hp + 1, part]
            o_ref[0, part * half:(part + 1) * half, hp * LANES:(hp + 1) * LANES] = jnp.where(
                low, even / even[:, MLA_V_DIM:MLA_V_DIM + 1], odd / odd[:, :1])


def _mla(qm, km, vm):
    b, s, _ = qm.shape
    tq = MLA_TILE
    gw = GROUP_HEADS * LANES
    return pl.pallas_call(
        _mla_kernel,
        grid=(b, MLA_HEADS // GROUP_HEADS, s // tq),
        in_specs=[pl.BlockSpec((1, tq, gw), lambda bi, g, qi: (bi, qi, g)),
                  pl.BlockSpec((1, s, gw), lambda bi, g, qi: (bi, 0, g)),
                  pl.BlockSpec((1, s, gw), lambda bi, g, qi: (bi, 0, g))],
        out_specs=pl.BlockSpec((1, tq, gw // 2), lambda bi, g, qi: (bi, qi, g)),
        out_shape=jax.ShapeDtypeStruct((b, s, HEAD_PAIRS * LANES), _F32),
        scratch_shapes=[pltpu.VMEM((GROUP_HEADS, tq, tq), _F32), pltpu.VMEM((GROUP_HEADS, tq, tq), _F32),
                        pltpu.VMEM((GROUP_HEADS, 2, tq // 2, LANES), _F32),
                        pltpu.VMEM((GROUP_HEADS, 2, tq // 2, LANES), _F32)],
        compiler_params=pltpu.CompilerParams(dimension_semantics=("parallel", "parallel", "arbitrary"),
                                             vmem_limit_bytes=VMEM_LIMIT),
        name="mla",
    )(qm, km, vm)


def _sb_kernel(q_ref, k_ref, v_ref, tri_ref, o_ref, rem_ref, acc_ref):
    tile = SB_TILE
    qi = pl.program_id(2)
    row = lax.broadcasted_iota(jnp.int32, (tile, tile), 0)
    col = lax.broadcasted_iota(jnp.int32, (tile, tile), 1)
    strict = col < row
    heads = [slice(h * LANES, (h + 1) * LANES) for h in range(SB_GROUP_HEADS)]

    def key_rows(kt):
        return pl.ds(pl.multiple_of(kt * tile, tile), tile)

    def tile_terms(kt, diagonal):
        tri = tri_ref[...]
        out = []
        for h, sl in enumerate(heads):
            q = q_ref[0, :, (h // 2) * LANES:(h // 2 + 1) * LANES]
            z = _dot_nt(q, k_ref[0, key_rows(kt), sl])
            sp = jnp.maximum(z, 0.0) + jnp.log(1.0 + jnp.exp(-jnp.abs(z)))
            if diagonal:
                sp = jnp.where(strict, sp, 0.0)
            hi = sp.astype(_BF16)
            lo = (sp - hi.astype(_F32)).astype(_BF16)
            out.append((z, _dot(jnp.concatenate([hi, lo], axis=-1), tri)))
        return out

    def absorb(kt, terms, rems, accs, diagonal, live=None):
        new_rems, new_accs = [], []
        for sl, (z, incl), rem, acc in zip(heads, terms, rems, accs):
            p = jnp.exp(z + (rem + incl))
            if diagonal:
                p = jnp.where(strict, p, 0.0)
            step = incl[:, :1]
            if live is not None:
                p = jnp.where(live, p, 0.0)
                step = jnp.where(live, step, 0.0)
            new_accs.append(acc + _dot(p.astype(_BF16), v_ref[0, key_rows(kt), sl]))
            new_rems.append(rem + step)
        return tuple(new_rems), tuple(new_accs)

    def rem_max(rems):
        return functools.reduce(jnp.maximum, [jnp.max(r) for r in rems])

    terms = [tile_terms(qi, True)] + [tile_terms(jnp.maximum(qi - d, 0), False) for d in range(1, SB_FIRST_TILES)]
    rems = tuple(jnp.zeros((tile, 1), _F32) for _ in range(SB_GROUP_HEADS))
    accs = tuple(jnp.zeros((tile, LANES), _F32) for _ in range(SB_GROUP_HEADS))
    rems, accs = absorb(qi, terms[0], rems, accs, True)
    for d in range(1, SB_FIRST_TILES):
        rems, accs = absorb(jnp.maximum(qi - d, 0), terms[d], rems, accs, False, live=qi >= d)

    def save(rems, accs):
        for h in range(SB_GROUP_HEADS):
            rem_ref[h] = rems[h]
            acc_ref[h] = accs[h]
    save(rems, accs)

    def cond(state):
        kt, worst = state
        return jnp.logical_and(kt >= 0, worst > SB_LOG_FLOOR)

    def body(state):
        kt, _ = state
        rems = tuple(rem_ref[h] for h in range(SB_GROUP_HEADS))
        accs = tuple(acc_ref[h] for h in range(SB_GROUP_HEADS))
        rems, accs = absorb(kt, tile_terms(kt, False), rems, accs, False)
        save(rems, accs)
        return kt - 1, rem_max(rems)

    lax.while_loop(cond, body, (qi - SB_FIRST_TILES, rem_max(rems)))
    for hp in range(SB_GROUP_HEADS // 2):
        o_ref[0, :, hp * LANES:(hp + 1) * LANES] = acc_ref[2 * hp] + acc_ref[2 * hp + 1]


def _sb(qs, ks, vs, tri):
    b, s, _ = qs.shape
    tile = SB_TILE
    gw = SB_GROUP_HEADS * LANES
    return pl.pallas_call(
        _sb_kernel,
        grid=(b, SB_HEADS // SB_GROUP_HEADS, s // tile),
        in_specs=[pl.BlockSpec((1, tile, gw // 2), lambda bi, g, qi: (bi, qi, g)),
                  pl.BlockSpec((1, s, gw), lambda bi, g, qi: (bi, 0, g)),
                  pl.BlockSpec((1, s, gw), lambda bi, g, qi: (bi, 0, g)),
                  pl.BlockSpec((2 * tile, tile), lambda bi, g, qi: (0, 0))],
        out_specs=pl.BlockSpec((1, tile, gw // 2), lambda bi, g, qi: (bi, qi, g)),
        out_shape=jax.ShapeDtypeStruct((b, s, HEAD_PAIRS * LANES), _F32),
        scratch_shapes=[pltpu.VMEM((SB_GROUP_HEADS, tile, 1), _F32), pltpu.VMEM((SB_GROUP_HEADS, tile, LANES), _F32)],
        compiler_params=pltpu.CompilerParams(dimension_semantics=("parallel", "parallel", "arbitrary"),
                                             vmem_limit_bytes=VMEM_LIMIT),
        name="sb",
    )(qs, ks, vs, tri)


def _post_kernel(om_ref, os_ref, x_ref, g_om_ref, g_os_ref, wo_ref, g_moe_ref, wr_ref, br_ref, ltri_ref,
                 h1_ref, u2_ref, route_ref, counts_ref, seen_ref):
    rows = POST_ROWS

    @pl.when(pl.program_id(0) == 0)
    def _():
        seen_ref[...] = jnp.zeros_like(seen_ref)

    mixed = jnp.concatenate([_rms(om_ref[...], g_om_ref[...]), _rms(os_ref[...], g_os_ref[...])], axis=-1)
    h1 = x_ref[...] + _dot(mixed.astype(_BF16), wo_ref[...])
    h1_ref[...] = h1
    u2 = _rms(h1, g_moe_ref[...])
    _store_row_tiles(u2_ref, u2)
    w_hi = wr_ref[...].astype(_BF16)
    w_lo = (wr_ref[...] - w_hi.astype(_F32)).astype(_BF16)
    u_hi = u2.astype(_BF16)
    u_lo = (u2 - u_hi.astype(_F32)).astype(_BF16)
    logits = _dot(u_hi, w_hi) + _dot(u_hi, w_lo) + _dot(u_lo, w_hi) + br_ref[...]

    part = rows // POST_PARTS
    lane = lax.broadcasted_iota(jnp.int32, (part, N_EXPERTS), 1).astype(_F32)
    out_lane = lax.broadcasted_iota(jnp.int32, (part, LANES), 1)
    work = [logits[r0:r0 + part, :] for r0 in range(0, rows, part)]
    onehot = [jnp.zeros((part, N_EXPERTS), _F32) for _ in work]
    ids = [[] for _ in work]
    tops = [[] for _ in work]
    for _ in range(TOP_K):
        for n in range(POST_PARTS):
            top = jnp.max(work[n], axis=-1, keepdims=True)
            idx = jnp.min(jnp.where(work[n] == top, lane, float(N_EXPERTS)), axis=-1, keepdims=True)
            hit = lane == idx
            onehot[n] = jnp.where(hit, 1.0, onehot[n])
            work[n] = jnp.where(hit, -jnp.inf, work[n])
            ids[n].append(idx)
            tops[n].append(top)

    seen = seen_ref[...]
    for n in range(POST_PARTS):
        exps = [jnp.exp(tp - tops[n][0]) for tp in tops[n]]
        denom = exps[0] + exps[1] + exps[2] + exps[3]
        gates = [e / denom for e in exps]
        before = seen + _dot(ltri_ref[...], onehot[n].astype(_BF16))
        ranks = [jnp.sum(jnp.where(lane == idx, before, 0.0), axis=-1, keepdims=True) for idx in ids[n]]
        seen = seen + jnp.sum(onehot[n], axis=0, keepdims=True)
        route = jnp.zeros((part, LANES), _F32)
        for j, val in enumerate(ids[n] + gates + ranks):
            route = jnp.where(out_lane == j, val, route)
        route_ref[n * part:(n + 1) * part, :] = route
    seen_ref[...] = seen
    counts_ref[...] = seen


def _post(om, os_, x2, g_om, g_os, wo, g_moe, wr, br, ltri):
    t = x2.shape[0]
    rows = POST_ROWS

    def full(a):
        return pl.BlockSpec(a.shape, lambda i: (0,) * a.ndim)

    def tok(n):
        return pl.BlockSpec((rows, n), lambda i: (i, 0))

    return pl.pallas_call(
        _post_kernel,
        grid=(t // rows,),
        in_specs=[tok(om.shape[1]), tok(os_.shape[1]), tok(D_MODEL), full(g_om), full(g_os), full(wo), full(g_moe),
                  full(wr), full(br), full(ltri)],
        out_specs=[tok(D_MODEL), pl.BlockSpec((rows * ROW_TILE, LANES), lambda i: (i, 0)), tok(LANES),
                   pl.BlockSpec((1, N_EXPERTS), lambda i: (0, 0))],
        out_shape=[jax.ShapeDtypeStruct((t, D_MODEL), _F32), jax.ShapeDtypeStruct((t * ROW_TILE, LANES), _F32),
                   jax.ShapeDtypeStruct((t, LANES), _F32), jax.ShapeDtypeStruct((1, N_EXPERTS), _F32)],
        scratch_shapes=[pltpu.VMEM((1, N_EXPERTS), _F32)],
        compiler_params=pltpu.CompilerParams(dimension_semantics=("arbitrary",), vmem_limit_bytes=VMEM_LIMIT),
        name="post",
    )(om, os_, x2, g_om, g_os, wo, g_moe, wr, br, ltri)


def _padded_count(cnt_ref, e):
    shift = EXPERT_ROWS.bit_length() - 1
    return lax.shift_left(lax.shift_right_logical(cnt_ref[e] + (EXPERT_ROWS - 1), shift), shift)


def _layout_kernel(cnt_ref, route_ref, dest_ref, start_ref, be_ref, used_ref):
    shift = EXPERT_ROWS.bit_length() - 1
    rows = MOE_ROWS
    nb = be_ref.shape[0]

    def place(e, off):
        padded = _padded_count(cnt_ref, e)
        start_ref[e] = off

        def mark(j, c):
            be_ref[lax.shift_right_logical(off, shift) + j] = e
            return c
        lax.fori_loop(0, lax.shift_right_logical(padded, shift), mark, 0)
        return off + padded
    total = lax.fori_loop(0, N_EXPERTS, place, 0)
    used_ref[0] = lax.shift_right_logical(total, shift)

    def tail(b, c):
        be_ref[b] = N_EXPERTS - 1
        return c
    lax.fori_loop(lax.shift_right_logical(total, shift), nb, tail, 0)

    lane = lax.broadcasted_iota(jnp.int32, (1, LANES), 1)
    rank_lanes = jnp.logical_and(lane >= 2 * TOP_K, lane < 3 * TOP_K)
    pick = (lax.broadcasted_iota(jnp.int32, (2 * SUBLANES, LANES), 0)
            == lax.broadcasted_iota(jnp.int32, (2 * SUBLANES, LANES), 1)).astype(_BF16)

    def tile(i, c):
        r = route_ref[pl.ds(pl.multiple_of(i * rows, rows), rows), :]
        high = jnp.floor(r * (1.0 / 256.0))
        low_digits = jnp.where(rank_lanes, r - 256.0 * high, r)
        high_digits = jnp.where(rank_lanes, high, 0.0)
        low_t = _dot_nt(pick, low_digits.astype(_BF16))
        high_t = _dot_nt(pick, high_digits.astype(_BF16))
        ids = low_t[:TOP_K, :].astype(jnp.int32)
        dest = (256.0 * high_t[2 * TOP_K:3 * TOP_K, :] + low_t[2 * TOP_K:3 * TOP_K, :]).astype(jnp.int32)
        for e in range(N_EXPERTS):
            dest = dest + jnp.where(ids == e, start_ref[e], 0)
        lines = [dest[k:k + 1, half * LANES:(half + 1) * LANES] for k in range(TOP_K) for half in range(rows // LANES)]
        dest_ref[pl.ds(pl.multiple_of(i * SUBLANES, SUBLANES), SUBLANES), :] = jnp.concatenate(lines, axis=0)
        return c
    lax.fori_loop(0, route_ref.shape[0] // rows, tile, 0)


def _layout(counts, route, n_blocks):
    t = route.shape[0]
    pair_shape = (t * TOP_K // LANES, LANES)
    grid_spec = pltpu.PrefetchScalarGridSpec(
        num_scalar_prefetch=1,
        grid=(1,),
        in_specs=[pl.BlockSpec(route.shape, lambda i, c: (0, 0))],
        out_specs=[pl.BlockSpec(pair_shape, lambda i, c: (0, 0)), pl.BlockSpec(memory_space=pltpu.SMEM),
                   pl.BlockSpec(memory_space=pltpu.SMEM), pl.BlockSpec(memory_space=pltpu.SMEM)],
    )
    return pl.pallas_call(
        _layout_kernel,
        grid_spec=grid_spec,
        out_shape=[jax.ShapeDtypeStruct(pair_shape, jnp.int32), jax.ShapeDtypeStruct((N_EXPERTS,), jnp.int32),
                   jax.ShapeDtypeStruct((n_blocks,), jnp.int32), jax.ShapeDtypeStruct((1,), jnp.int32)],
        compiler_params=pltpu.CompilerParams(dimension_semantics=("arbitrary",), vmem_limit_bytes=VMEM_LIMIT),
        name="layout",
    )(counts, route)


def _tile_rows(row):
    return pl.ds(pl.multiple_of(row * ROW_TILE, ROW_TILE), ROW_TILE)


def _for_each_pair(idx, s, body):
    halves = MOE_ROWS // LANES

    for g in range(SUBLANES):
        choice = g // halves
        first = (g % halves) * LANES
        for t0 in range(0, LANES, DMA_GROUP):
            vals = [idx[s, g, t0 + j] for j in range(DMA_GROUP)]
            for j, row in enumerate(vals):
                body(first + t0 + j, choice, row, j % DMA_THREADS)


def _dispatch_kernel(cnt_ref, start_ref, dest_hbm, u_hbm, xs_hbm, idx, stage, zblk, sem_d, sem_l, sem_i, sem_z):
    rows = MOE_ROWS
    shift = EXPERT_ROWS.bit_length() - 1
    i = pl.program_id(0)
    nt = pl.num_programs(0)
    nb = xs_hbm.shape[0] // (EXPERT_ROWS * ROW_TILE)
    slot = lax.rem(i, 2)
    stage_slot = lax.rem(i, STAGE_SLOTS)

    def idx_copy(tile, s):
        return pltpu.make_async_copy(dest_hbm.at[pl.ds(pl.multiple_of(tile * SUBLANES, SUBLANES), SUBLANES)],
                                     idx.at[s], sem_i.at[s])

    def load(tile):
        s = lax.rem(tile, STAGE_SLOTS)
        src = pl.ds(pl.multiple_of(tile * (rows * ROW_TILE), rows * ROW_TILE), rows * ROW_TILE)
        return pltpu.make_async_copy(u_hbm.at[src], stage.at[s], sem_l.at[s])

    def wait_rows(s):
        chunk = pl.ds(0, rows * ROW_TILE)
        for _ in range(TOP_K):
            pltpu.make_async_copy(stage.at[s], xs_hbm.at[chunk], sem_d.at[s]).wait()

    def zero_block(b):
        dst = pl.ds(pl.multiple_of(b * (EXPERT_ROWS * ROW_TILE), EXPERT_ROWS * ROW_TILE), EXPERT_ROWS * ROW_TILE)
        return pltpu.make_async_copy(zblk, xs_hbm.at[dst], sem_z)

    def pad_rows(e, fn):
        pad = _padded_count(cnt_ref, e) - cnt_ref[e]
        first = start_ref[e] + cnt_ref[e]
        for bit in reversed(range(shift)):
            size = 1 << bit
            before = lax.shift_left(lax.shift_right_logical(pad, bit + 1), bit + 1)

            @pl.when(lax.shift_right_logical(pad, bit) & 1 == 1)
            def _(size=size, before=before):
                dst = pl.ds(pl.multiple_of((first + before) * ROW_TILE, ROW_TILE), size * ROW_TILE)
                fn(pltpu.make_async_copy(zblk.at[pl.ds(0, size * ROW_TILE)], xs_hbm.at[dst], sem_z))

    def tail_blocks(fn):
        used = start_ref[N_EXPERTS - 1] + _padded_count(cnt_ref, N_EXPERTS - 1)

        def one(b, c):
            fn(zero_block(b))
            return c
        lax.fori_loop(lax.shift_right_logical(used, shift), nb, one, 0)

    @pl.when(i == 0)
    def _():
        idx_copy(0, 0).start()
        load(0).start()

        @pl.when(nt > 1)
        def _():
            load(1).start()
        zblk[...] = jnp.zeros_like(zblk)
        for fn in (lambda c: c.start(), lambda c: c.wait()):
            def per_expert(e, c, fn=fn):
                pad_rows(e, fn)
                return c
            lax.fori_loop(0, N_EXPERTS, per_expert, 0)
            tail_blocks(fn)

    idx_copy(i, slot).wait()

    @pl.when(i + 1 < nt)
    def _():
        idx_copy(i + 1, 1 - slot).start()

    @pl.when(i >= 2)
    def _():
        wait_rows(lax.rem(i + 2, STAGE_SLOTS))

    @pl.when(i + 2 < nt)
    def _():
        load(i + 2).start()

    load(i).wait()

    def copy_row(t, k, row, thread):
        del k
        pltpu.make_async_copy(stage.at[stage_slot, _tile_rows(t)], xs_hbm.at[_tile_rows(row)],
                              sem_d.at[stage_slot]).start(priority=thread)
    _for_each_pair(idx, slot, copy_row)

    @pl.when(i == nt - 1)
    def _():
        @pl.when(nt > 1)
        def _():
            wait_rows(lax.rem(i + STAGE_SLOTS - 1, STAGE_SLOTS))
        wait_rows(stage_slot)


def _dispatch(counts, start, dest, u2_tiles, n_blocks):
    nt = dest.shape[0] // SUBLANES
    grid_spec = pltpu.PrefetchScalarGridSpec(
        num_scalar_prefetch=2,
        grid=(nt,),
        in_specs=[pl.BlockSpec(memory_space=pl.ANY), pl.BlockSpec(memory_space=pl.ANY)],
        out_specs=pl.BlockSpec(memory_space=pl.ANY),
        scratch_shapes=[pltpu.SMEM((2, SUBLANES, LANES), jnp.int32),
                        pltpu.VMEM((STAGE_SLOTS, MOE_ROWS * ROW_TILE, LANES), _F32),
                        pltpu.VMEM((EXPERT_ROWS * ROW_TILE, LANES), _F32),
                        pltpu.SemaphoreType.DMA((STAGE_SLOTS,)), pltpu.SemaphoreType.DMA((STAGE_SLOTS,)),
                        pltpu.SemaphoreType.DMA((2,)), pltpu.SemaphoreType.DMA(())],
    )
    return pl.pallas_call(
        _dispatch_kernel,
        grid_spec=grid_spec,
        out_shape=jax.ShapeDtypeStruct((n_blocks * EXPERT_ROWS * ROW_TILE, LANES), _F32),
        compiler_params=pltpu.CompilerParams(dimension_semantics=("arbitrary",), vmem_limit_bytes=VMEM_LIMIT),
        name="dispatch",
    )(counts, start, dest, u2_tiles)


def _expert_kernel(be_ref, used_ref, xs_ref, wgu_hbm, bgu_ref, wdn_hbm, bdn_ref, ys_ref,
                   wgu_f32, wdn_f32, wgu_bf, wdn_bf, sem_w):
    rows = EXPERT_ROWS
    i = pl.program_id(0)
    nb = pl.num_programs(0)
    used = used_ref[0]
    e = be_ref[i]
    prev = be_ref[jnp.maximum(i - 1, 0)]
    in_use = i < used

    def fetch(expert):
        return (pltpu.make_async_copy(wgu_hbm.at[expert], wgu_f32, sem_w.at[0]),
                pltpu.make_async_copy(wdn_hbm.at[expert], wdn_f32, sem_w.at[1]))

    @pl.when(i == 0)
    def _():
        for c in fetch(e):
            c.start()

    @pl.when(jnp.logical_and(in_use, jnp.logical_or(i == 0, e != prev)))
    def _():
        for c in fetch(e):
            c.wait()
        chunk = 128

        def cast(c, carry):
            rs = pl.ds(pl.multiple_of(c * chunk, chunk), chunk)
            wgu_bf[rs, :] = wgu_f32[rs, :].astype(_BF16)
            wdn_bf[rs, :] = wdn_f32[rs, :].astype(_BF16)
            return carry
        lax.fori_loop(0, D_MODEL // chunk, cast, 0)

        def same_expert(j):
            return jnp.logical_and(j < used, be_ref[jnp.minimum(j, nb - 1)] == e)
        nxt = lax.while_loop(same_expert, lambda j: j + 1, i)

        @pl.when(nxt < used)
        def _():
            for c in fetch(be_ref[jnp.minimum(nxt, nb - 1)]):
                c.start()

    @pl.when(in_use)
    def _():
        x = _load_row_tiles(xs_ref, 0, rows).astype(_BF16)
        hh = _dot(x, wgu_bf[...]) + bgu_ref[0]
        glu = jnp.minimum(hh[:, :D_FF], SWIGLU_LIMIT)
        lin = jnp.clip(hh[:, D_FF:], -SWIGLU_LIMIT, SWIGLU_LIMIT)
        act = glu * jax.nn.sigmoid(SWIGLU_ALPHA * glu) * (lin + 1.0)
        _store_row_tiles(ys_ref, _dot(act.astype(_BF16), wdn_bf[...]) + bdn_ref[0])

    @pl.when(jnp.logical_not(in_use))
    def _():
        ys_ref[...] = jnp.zeros_like(ys_ref)


def _experts(block_e, used, xs, w_gu, b_gu, w_dn, b_dn):
    nb = block_e.shape[0]
    rows = EXPERT_ROWS
    grid_spec = pltpu.PrefetchScalarGridSpec(
        num_scalar_prefetch=2,
        grid=(nb,),
        in_specs=[pl.BlockSpec((rows * ROW_TILE, LANES), lambda i, be, u: (i, 0)),
                  pl.BlockSpec(memory_space=pl.ANY),
                  pl.BlockSpec((1, 1, 2 * D_FF), lambda i, be, u: (be[i], 0, 0)),
                  pl.BlockSpec(memory_space=pl.ANY),
                  pl.BlockSpec((1, 1, D_MODEL), lambda i, be, u: (be[i], 0, 0))],
        out_specs=pl.BlockSpec((rows * ROW_TILE, LANES), lambda i, be, u: (i, 0)),
        scratch_shapes=[pltpu.VMEM((D_MODEL, 2 * D_FF), _F32), pltpu.VMEM((D_FF, D_MODEL), _F32),
                        pltpu.VMEM((D_MODEL, 2 * D_FF), _BF16), pltpu.VMEM((D_FF, D_MODEL), _BF16),
                        pltpu.SemaphoreType.DMA((2,))],
    )
    return pl.pallas_call(
        _expert_kernel,
        grid_spec=grid_spec,
        out_shape=jax.ShapeDtypeStruct(xs.shape, _F32),
        compiler_params=pltpu.CompilerParams(dimension_semantics=("arbitrary",), vmem_limit_bytes=VMEM_LIMIT),
        name="experts",
    )(block_e, used, xs, w_gu, b_gu.reshape(N_EXPERTS, 1, 2 * D_FF), w_dn, b_dn.reshape(N_EXPERTS, 1, D_MODEL))


def _final_kernel(dest_hbm, ys_hbm, h1_ref, route_ref, p_ref, g_ple_ref, wpg_ref, wpp_ref, g_fin_ref,
                  out_ref, ybuf, idx, sem_y, sem_i, *, last_layer):
    rows = MOE_ROWS
    i = pl.program_id(0)
    nt = pl.num_programs(0)
    slot = lax.rem(i, 2)

    def idx_copy(tile, s):
        return pltpu.make_async_copy(dest_hbm.at[pl.ds(pl.multiple_of(tile * SUBLANES, SUBLANES), SUBLANES)],
                                     idx.at[s], sem_i.at[s])

    def gather(s):
        def copy_row(t, k, row, thread):
            pltpu.make_async_copy(ys_hbm.at[_tile_rows(row)], ybuf.at[s, _tile_rows(k * rows + t)],
                                  sem_y.at[s]).start(priority=thread)
        _for_each_pair(idx, s, copy_row)

    @pl.when(i == 0)
    def _():
        idx_copy(0, 0).start()
        idx_copy(0, 0).wait()
        gather(0)

        @pl.when(nt > 1)
        def _():
            idx_copy(1, 1).start()

    @pl.when(i + 1 < nt)
    def _():
        idx_copy(i + 1, 1 - slot).wait()
        gather(1 - slot)

        @pl.when(i + 2 < nt)
        def _():
            idx_copy(i + 2, slot).start()

    pltpu.make_async_copy(ys_hbm.at[pl.ds(0, TOP_K * rows * ROW_TILE)], ybuf.at[slot], sem_y.at[slot]).wait()
    part = rows // ROW_PARTS
    for r0 in range(0, rows, part):
        rs = slice(r0, r0 + part)
        route = route_ref[rs, :]
        y = jnp.zeros((part, D_MODEL), _F32)
        for k in range(TOP_K):
            y = y + _load_row_tiles(ybuf.at[slot], k * rows + r0, part) * route[:, TOP_K + k:TOP_K + k + 1]
        h2 = h1_ref[rs, :] + y
        u3 = _rms(h2, g_ple_ref[...]).astype(_BF16)
        gate = jax.nn.sigmoid(_dot(u3, wpg_ref[...]))
        h3 = h2 + gate * _dot(p_ref[rs, :].astype(_BF16), wpp_ref[...])
        out_ref[rs, :] = _rms(h3, g_fin_ref[...]) if last_layer else h3


def _final(dest, ys, h1, route, p2, g_ple, wpg, wpp, g_fin, last_layer):
    t = h1.shape[0]
    rows = MOE_ROWS

    def full(a):
        return pl.BlockSpec(a.shape, lambda i: (0,) * a.ndim)

    def tok(n):
        return pl.BlockSpec((rows, n), lambda i: (i, 0))

    return pl.pallas_call(
        functools.partial(_final_kernel, last_layer=last_layer),
        grid=(t // rows,),
        in_specs=[pl.BlockSpec(memory_space=pl.ANY), pl.BlockSpec(memory_space=pl.ANY), tok(D_MODEL), tok(LANES),
                  tok(PLE_DIM), full(g_ple), full(wpg), full(wpp), full(g_fin)],
        out_specs=tok(D_MODEL),
        out_shape=jax.ShapeDtypeStruct((t, D_MODEL), _F32),
        scratch_shapes=[pltpu.VMEM((2, TOP_K * rows * ROW_TILE, LANES), _F32),
                        pltpu.SMEM((2, SUBLANES, LANES), jnp.int32),
                        pltpu.SemaphoreType.DMA((2,)), pltpu.SemaphoreType.DMA((2,))],
        compiler_params=pltpu.CompilerParams(dimension_semantics=("arbitrary",), vmem_limit_bytes=VMEM_LIMIT),
        name="final",
    )(dest, ys, h1, route, p2, g_ple, wpg, wpp, g_fin)


def _rope_pad(w, rot):
    half = MLA_ROPE_DIM // 2
    body = jnp.concatenate([-w[:, half:], w[:, :half]], axis=1) if rot else w
    z = jnp.zeros((w.shape[0], MLA_NOPE_DIM), w.dtype)
    return jnp.concatenate([z, body, jnp.zeros((w.shape[0], LANES - MLA_NOPE_DIM - MLA_ROPE_DIM), w.dtype)], axis=1)


def _layer_weights(w_in, w_uq, w_ukv):
    o = Q_LORA_RANK + KV_LORA_RANK
    w_kr = w_in[:, o:o + MLA_ROPE_DIM]
    w1 = jnp.concatenate([w_in[:, :o], _rope_pad(w_kr, False), _rope_pad(w_kr, True), w_in[:, o + MLA_ROPE_DIM:]],
                         axis=1).astype(_BF16)
    uq = w_uq.reshape(Q_LORA_RANK, MLA_HEADS, MLA_NOPE_DIM + MLA_ROPE_DIM)
    zq = jnp.zeros((Q_LORA_RANK, MLA_HEADS, LANES - MLA_NOPE_DIM - MLA_ROPE_DIM), w_uq.dtype)
    wq = jnp.concatenate([uq, zq], axis=2).reshape(Q_LORA_RANK, MLA_HEADS * LANES).astype(_BF16)
    rope = uq[:, :, MLA_NOPE_DIM:]
    half = MLA_ROPE_DIM // 2
    rot = jnp.concatenate([jnp.zeros_like(uq[:, :, :MLA_NOPE_DIM]), -rope[:, :, half:], rope[:, :, :half], zq], axis=2)
    wqr = rot.reshape(Q_LORA_RANK, MLA_HEADS * LANES).astype(_BF16)
    ukv = w_ukv.reshape(KV_LORA_RANK, MLA_HEADS, MLA_NOPE_DIM + MLA_V_DIM)
    zk = jnp.zeros((KV_LORA_RANK, MLA_HEADS, LANES - MLA_NOPE_DIM), w_ukv.dtype)
    wk = jnp.concatenate([ukv[:, :, :MLA_NOPE_DIM], zk], axis=2).reshape(KV_LORA_RANK, MLA_HEADS * LANES).astype(_BF16)
    v = ukv[:, :, MLA_NOPE_DIM:].reshape(KV_LORA_RANK, HEAD_PAIRS, 2, MLA_V_DIM)
    zv = jnp.zeros((KV_LORA_RANK, HEAD_PAIRS, MLA_V_DIM), w_ukv.dtype)
    wv = jnp.stack([jnp.concatenate([v[:, :, 0], zv], axis=2), jnp.concatenate([zv, v[:, :, 1]], axis=2)], axis=2)
    wv = wv.reshape(KV_LORA_RANK, MLA_HEADS * LANES).astype(_BF16)
    return w1, wq, wqr, wk, wv


def kernel(x, p, positions, w_in, g_attn, g_cq, w_uq, g_ckv, w_ukv, g_out_mla, g_out_sb, w_o, g_moe, w_router,
           b_router, w_gu, b_gu, w_dn, b_dn, g_ple, w_ple_gate, w_ple_proj, g_final):
    b, s, d = x.shape
    t = b * s
    depth = w_in.shape[0]
    assert d == D_MODEL and ROW_TILE == SUBLANES and s % MLA_TILE == 0 and t % PROJ_ROWS == 0 and t % POST_ROWS == 0
    assert t % MOE_ROWS == 0 and LANES % DMA_GROUP == 0 and MOE_ROWS % LANES == 0 and (t * TOP_K) % EXPERT_ROWS == 0
    assert EXPERT_ROWS & (EXPERT_ROWS - 1) == 0

    freq = ROPE_THETA ** (-jnp.arange(0, MLA_ROPE_DIM, 2, dtype=_F32) / MLA_ROPE_DIM)
    invf = jnp.concatenate([jnp.zeros((MLA_NOPE_DIM,), _F32), freq, freq,
                            jnp.zeros((LANES - MLA_NOPE_DIM - MLA_ROPE_DIM,), _F32)]).reshape(1, LANES)
    pos = positions.reshape(t, 1)
    idx = jnp.arange(SB_TILE)
    tri = -(idx[:, None] >= idx[None, :]).astype(_BF16)
    tri = jnp.concatenate([tri, tri], axis=0)
    idx = jnp.arange(POST_ROWS // POST_PARTS)
    ltri = (idx[None, :] < idx[:, None]).astype(_BF16)
    n_blocks = t * TOP_K // EXPERT_ROWS + N_EXPERTS

    h = x.reshape(t, d)
    for i in range(depth):
        w1, wq, wqr, wk, wv = _layer_weights(w_in[i], w_uq[i], w_ukv[i])
        qm, km, vm, qs, ks, vs = _proj(pos, h, g_attn[i].reshape(1, d), w1, g_cq[i].reshape(1, -1), wq, wqr,
                                       g_ckv[i].reshape(1, -1), wk, wv, invf)

        def seq(a):
            return a.reshape(b, s, a.shape[1])

        om = _mla(seq(qm), seq(km), seq(vm)).reshape(t, -1)
        os_ = _sb(seq(qs), seq(ks), seq(vs), tri).reshape(t, -1)
        h1, u2_tiles, route, counts = _post(om, os_, h, g_out_mla[i].reshape(1, -1), g_out_sb[i].reshape(1, -1),
                                            w_o[i].astype(_BF16), g_moe[i].reshape(1, d), w_router[i],
                                            b_router[i].reshape(1, -1), ltri)
        counts = counts.reshape(-1).astype(jnp.int32)
        dest, start, block_e, used = _layout(counts, route, n_blocks)
        xs = _dispatch(counts, start, dest, u2_tiles, n_blocks)
        ys = _experts(block_e, used, xs, w_gu[i], b_gu[i], w_dn[i], b_dn[i])
        h = _final(dest, ys, h1, route, p[i].reshape(t, -1), g_ple[i].reshape(1, d), w_ple_gate[i].astype(_BF16),
                   w_ple_proj[i].astype(_BF16), g_final.reshape(1, d), i == depth - 1)
    return h.reshape(b, s, d)
```
